```python
import jax, jax.numpy as jnp
from jax import lax
import numpy as np

D_MODEL = 1024
BATCH = 4
SEQ = 8192
DEPTH = 1

CHUNK = 64
D_MLSTM = 1024
N_MLSTM_HEADS = 4
MLSTM_HEAD_DIM = D_MLSTM // N_MLSTM_HEADS
QKV_BLOCK = 4
N_QKV_BLOCKS = D_MLSTM // QKV_BLOCK
MLSTM_CONV_WIDTH = 4
D_CONV = 1024
CONV_WIDTH = 31
D_MIX = D_MLSTM + D_CONV
D_IN_PROJ = 2 * D_MLSTM + 2 * D_CONV
N_EXPERTS = 32
TOP_K = 4
D_FF = 1024
SWIGLU_ALPHA = 1.702
SWIGLU_LIMIT = 7.0
EXPERT_BLOCK = 256
EPS = 1e-5

kernel_name = 'hybrid_mlstm_conformer_conv_moe'


def rmsnorm(x, w):
    xf = x.astype(jnp.float32)
    y = xf * lax.rsqrt(jnp.mean(xf * xf, axis=-1, keepdims=True) + EPS)
    return (y * w.astype(jnp.float32)).astype(x.dtype)


def layernorm(x, w, b):
    xf = x.astype(jnp.float32)
    mu = jnp.mean(xf, axis=-1, keepdims=True)
    var = jnp.mean(jnp.square(xf - mu), axis=-1, keepdims=True)
    y = (xf - mu) * lax.rsqrt(var + EPS)
    return (y * w.astype(jnp.float32) + b.astype(jnp.float32)).astype(x.dtype)


def causal_depthwise_conv(x, w, b):
    width, ch = w.shape
    y = lax.conv_general_dilated(
        x, w[:, None, :].astype(x.dtype), window_strides=(1,), padding=[(width - 1, 0)],
        dimension_numbers=('NWC', 'WIO', 'NWC'), feature_group_count=ch)
    return y + b.astype(x.dtype)


def block_diag_proj(x, w):
    bsz, seq, _ = x.shape
    xb = x.reshape(bsz, seq, N_QKV_BLOCKS, QKV_BLOCK)
    return jnp.einsum('bsni,nio->bsno', xb, w.astype(x.dtype)).reshape(bsz, seq, D_MLSTM)


def mlstm_chunk_step(carry, chunk):
    c_prev, n_prev, m_prev = carry
    q, k, v, log_i, log_f = chunk
    length = q.shape[2]
    b = jnp.cumsum(log_f, axis=-1)
    g = b[..., -1]
    causal = jnp.tril(jnp.ones((length, length), dtype=bool))
    d_mat = jnp.where(causal, b[..., :, None] - b[..., None, :] + log_i[..., None, :], -jnp.inf)
    inter = b + m_prev[..., None]
    m = jnp.maximum(inter, jnp.max(d_mat, axis=-1))
    w_intra = jnp.exp(d_mat - m[..., None])
    w_inter = jnp.exp(inter - m)
    s = jnp.einsum('bhid,bhjd->bhij', q, k) * w_intra
    num = (w_inter[..., None] * jnp.einsum('bhid,bhde->bhie', q, c_prev)
           + jnp.einsum('bhij,bhje->bhie', s, v))
    den = w_inter * jnp.einsum('bhid,bhd->bhi', q, n_prev) + jnp.sum(s, axis=-1)
    h = num / jnp.maximum(jnp.abs(den), jnp.exp(-m))[..., None]
    a = g[..., None] - b + log_i
    m_new = jnp.maximum(g + m_prev, jnp.max(a, axis=-1))
    w_state = jnp.exp(a - m_new[..., None])
    decay = jnp.exp(g + m_prev - m_new)
    c_new = decay[..., None, None] * c_prev + jnp.einsum('bhj,bhjd,bhje->bhde', w_state, k, v)
    n_new = decay[..., None] * n_prev + jnp.einsum('bhj,bhjd->bhd', w_state, k)
    return (c_new, n_new, m_new), h


def mlstm_chunkwise(q, k, v, log_i, log_f):
    bsz, seq, nh, dh = q.shape
    nc = seq // CHUNK

    def to_chunks(t):
        return t.astype(jnp.float32).reshape(bsz, nc, CHUNK, nh, dh).transpose(1, 0, 3, 2, 4)

    def gate_chunks(t):
        return t.astype(jnp.float32).reshape(bsz, nc, CHUNK, nh).transpose(1, 0, 3, 2)

    xs = (to_chunks(q) * (dh ** -0.5), to_chunks(k), to_chunks(v), gate_chunks(log_i), gate_chunks(log_f))
    init = (jnp.zeros((bsz, nh, dh, dh), jnp.float32),
            jnp.zeros((bsz, nh, dh), jnp.float32),
            jnp.zeros((bsz, nh), jnp.float32))
    _, h = lax.scan(mlstm_chunk_step, init, xs)
    return h.transpose(1, 0, 3, 2, 4).reshape(bsz, seq, nh, dh)


def hybrid_mixer(a, w_in, mlstm_conv_w, mlstm_conv_b, w_q, w_k, w_v, w_igate, b_igate,
                 w_fgate, b_fgate, mlstm_norm_w, mlstm_skip, conv_dw_w, conv_dw_b,
                 conv_norm_w, conv_norm_b, w_out):
    bsz, seq, _ = a.shape
    proj = a @ w_in.astype(a.dtype)
    x_m, z, glu_a, glu_b = jnp.split(proj, [D_MLSTM, 2 * D_MLSTM, 2 * D_MLSTM + D_CONV], axis=-1)

    x_c = jax.nn.silu(causal_depthwise_conv(x_m, mlstm_conv_w, mlstm_conv_b))
    q = block_diag_proj(x_c, w_q)
    k = block_diag_proj(x_c, w_k)
    v = block_diag_proj(x_m, w_v)
    qkv = jnp.concatenate([q, k, v], axis=-1)
    log_i = (qkv @ w_igate.astype(a.dtype) + b_igate.astype(a.dtype)).astype(jnp.float32)
    log_f = jax.nn.log_sigmoid((qkv @ w_fgate.astype(a.dtype) + b_fgate.astype(a.dtype)).astype(jnp.float32))
    shp = (bsz, seq, N_MLSTM_HEADS, MLSTM_HEAD_DIM)
    h_m = mlstm_chunkwise(q.reshape(shp), k.reshape(shp), v.reshape(shp), log_i, log_f)
    mu = jnp.mean(h_m, axis=-1, keepdims=True)
    var = jnp.mean(jnp.square(h_m - mu), axis=-1, keepdims=True)
    h_m = ((h_m - mu) * lax.rsqrt(var + EPS)).reshape(bsz, seq, D_MLSTM)
    h_m = (h_m * mlstm_norm_w.astype(jnp.float32)).astype(a.dtype)
    y_m = (h_m + mlstm_skip.astype(a.dtype) * x_c) * jax.nn.silu(z)

    u = glu_a * jax.nn.sigmoid(glu_b)
    u = causal_depthwise_conv(u, conv_dw_w, conv_dw_b)
    y_c = jax.nn.silu(layernorm(u, conv_norm_w, conv_norm_b))

    return jnp.concatenate([y_m, y_c], axis=-1) @ w_out.astype(a.dtype)


def clamped_swiglu(hid):
    x_glu = jnp.minimum(hid[..., :D_FF], SWIGLU_LIMIT)
    x_lin = jnp.clip(hid[..., D_FF:], -SWIGLU_LIMIT, SWIGLU_LIMIT)
    return x_glu * jax.nn.sigmoid(SWIGLU_ALPHA * x_glu) * (x_lin + 1.0)


def moe_ffn(a, w_router, b_router, w1, b1, w2, b2):
    bsz, seq, dm = a.shape
    n_tok = bsz * seq
    xf = a.reshape(n_tok, dm)
    logits = (xf @ w_router.astype(a.dtype) + b_router.astype(a.dtype)).astype(jnp.float32)
    top_val, top_idx = lax.top_k(logits, TOP_K)
    gates = jax.nn.softmax(top_val, axis=-1)
    n_assign = n_tok * TOP_K
    expert_flat = top_idx.reshape(-1).astype(jnp.int32)
    gate_flat = gates.reshape(-1)
    counts = jnp.zeros((N_EXPERTS,), jnp.int32).at[expert_flat].add(1)
    starts = jnp.cumsum(counts) - counts
    padded = (counts + EXPERT_BLOCK - 1) // EXPERT_BLOCK * EXPERT_BLOCK
    pad_ends = jnp.cumsum(padded)
    pad_starts = pad_ends - padded
    order = jnp.argsort(expert_flat, stable=True)
    sorted_e = expert_flat[order]
    sorted_tok = (order // TOP_K).astype(jnp.int32)
    sorted_gate = gate_flat[order]
    dest = pad_starts[sorted_e] + jnp.arange(n_assign, dtype=jnp.int32) - starts[sorted_e]
    n_blocks = -(-n_assign // EXPERT_BLOCK) + N_EXPERTS
    n_rows = n_blocks * EXPERT_BLOCK
    row_tok = jnp.full((n_rows,), n_tok, jnp.int32).at[dest].set(sorted_tok)
    row_gate = jnp.zeros((n_rows,), jnp.float32).at[dest].set(sorted_gate)
    block_start = jnp.arange(n_blocks, dtype=jnp.int32) * EXPERT_BLOCK
    block_expert = jnp.minimum(jnp.searchsorted(pad_ends, block_start, side='right'), N_EXPERTS - 1)
    x_pad = jnp.concatenate([xf, jnp.zeros((1, dm), xf.dtype)], axis=0)

    def expert_block(args):
        tok, gate, e = args
        rows = x_pad[tok]
        hid = rows @ w1[e].astype(rows.dtype) + b1[e].astype(rows.dtype)
        y = clamped_swiglu(hid) @ w2[e].astype(rows.dtype) + b2[e].astype(rows.dtype)
        return y * gate[:, None].astype(y.dtype)

    y_rows = lax.map(expert_block, (row_tok.reshape(n_blocks, EXPERT_BLOCK),
                                    row_gate.reshape(n_blocks, EXPERT_BLOCK), block_expert))
    out = jnp.zeros((n_tok + 1, dm), a.dtype).at[row_tok].add(y_rows.reshape(n_rows, dm))
    return out[:n_tok].reshape(bsz, seq, dm)


def setup_inputs(seed: int = 0) -> dict:
    key = jax.random.key(seed)
    ks = jax.random.split(key, 32)
    f32 = jnp.float32
    L = DEPTH
    H = N_MLSTM_HEADS

    def nrm(k, shape, scale):
        return jax.random.normal(k, shape, f32) * scale

    return {
        'x': nrm(ks[0], (BATCH, SEQ, D_MODEL), 1.0),
        'norm_mix_w': 1.0 + nrm(ks[1], (L, D_MODEL), 0.01),
        'w_in': nrm(ks[2], (L, D_MODEL, D_IN_PROJ), D_MODEL ** -0.5),
        'mlstm_conv_w': nrm(ks[3], (L, MLSTM_CONV_WIDTH, D_MLSTM), MLSTM_CONV_WIDTH ** -0.5),
        'mlstm_conv_b': nrm(ks[4], (L, D_MLSTM), 0.01),
        'w_q': nrm(ks[5], (L, N_QKV_BLOCKS, QKV_BLOCK, QKV_BLOCK), QKV_BLOCK ** -0.5),
        'w_k': nrm(ks[6], (L, N_QKV_BLOCKS, QKV_BLOCK, QKV_BLOCK), QKV_BLOCK ** -0.5),
        'w_v': nrm(ks[7], (L, N_QKV_BLOCKS, QKV_BLOCK, QKV_BLOCK), QKV_BLOCK ** -0.5),
        'w_igate': nrm(ks[8], (L, 3 * D_MLSTM, H), 0.1 * (3 * D_MLSTM) ** -0.5),
        'b_igate': nrm(ks[9], (L, H), 0.1),
        'w_fgate': nrm(ks[10], (L, 3 * D_MLSTM, H), 0.1 * (3 * D_MLSTM) ** -0.5),
        'b_fgate': jnp.linspace(3.0, 6.0, H, dtype=f32)[None, :] + nrm(ks[11], (L, H), 0.1),
        'mlstm_norm_w': 1.0 + nrm(ks[12], (L, D_MLSTM), 0.01),
        'mlstm_skip': 1.0 + nrm(ks[13], (L, D_MLSTM), 0.01),
        'conv_dw_w': nrm(ks[14], (L, CONV_WIDTH, D_CONV), CONV_WIDTH ** -0.5),
        'conv_dw_b': nrm(ks[15], (L, D_CONV), 0.01),
        'conv_norm_w': 1.0 + nrm(ks[16], (L, D_CONV), 0.01),
        'conv_norm_b': nrm(ks[17], (L, D_CONV), 0.01),
        'w_out': nrm(ks[18], (L, D_MIX, D_MODEL), D_MIX ** -0.5),
        'norm_ffn_w': 1.0 + nrm(ks[19], (L, D_MODEL), 0.01),
        'w_router': nrm(ks[20], (L, D_MODEL, N_EXPERTS), D_MODEL ** -0.5),
        'b_router': nrm(ks[21], (L, N_EXPERTS), 0.01),
        'w1': nrm(ks[22], (L, N_EXPERTS, D_MODEL, 2 * D_FF), D_MODEL ** -0.5),
        'b1': nrm(ks[23], (L, N_EXPERTS, 2 * D_FF), 0.01),
        'w2': nrm(ks[24], (L, N_EXPERTS, D_FF, D_MODEL), D_FF ** -0.5),
        'b2': nrm(ks[25], (L, N_EXPERTS, D_MODEL), 0.01),
        'final_norm_w': 1.0 + nrm(ks[26], (D_MODEL,), 0.01),
    }


def reference(x, norm_mix_w, w_in, mlstm_conv_w, mlstm_conv_b, w_q, w_k, w_v, w_igate, b_igate,
              w_fgate, b_fgate, mlstm_norm_w, mlstm_skip, conv_dw_w, conv_dw_b, conv_norm_w,
              conv_norm_b, w_out, norm_ffn_w, w_router, b_router, w1, b1, w2, b2, final_norm_w):
    h = x
    for l in range(DEPTH):
        a = rmsnorm(h, norm_mix_w[l])
        h = h + hybrid_mixer(a, w_in[l], mlstm_conv_w[l], mlstm_conv_b[l], w_q[l], w_k[l], w_v[l],
                             w_igate[l], b_igate[l], w_fgate[l], b_fgate[l], mlstm_norm_w[l],
                             mlstm_skip[l], conv_dw_w[l], conv_dw_b[l], conv_norm_w[l],
                             conv_norm_b[l], w_out[l])
        a = rmsnorm(h, norm_ffn_w[l])
        h = h + moe_ffn(a, w_router[l], b_router[l], w1[l], b1[l], w2[l], b2[l])
    return rmsnorm(h, final_norm_w)
```

```python
import functools

import jax
import jax.numpy as jnp
from jax import lax
from jax.experimental import pallas as pl
from jax.experimental.pallas import tpu as pltpu

F32 = jnp.float32
BF16 = jnp.bfloat16

EPS = 1e-5
N_HEADS = 4
QKV_BLOCK = 4
MLSTM_CONV_WIDTH = 4
CONV_WIDTH = 31
N_EXPERTS = 32
TOP_K = 4
SWIGLU_ALPHA = 1.702
SWIGLU_LIMIT = 7.0

LANES = 128
SUBLANES = 8
MXU_DIM = 256
VMEM_LIMIT = 52 * 1024 * 1024

MLSTM_CHUNK = 256
CONV_HALO = 32
NEG_INF = float("-inf")


def _sigmoid(x):
    return 1.0 / (1.0 + jnp.exp(-x))


def _silu(x):
    return x * _sigmoid(x)


def _cparams(*sem):
    return pltpu.CompilerParams(dimension_semantics=sem, vmem_limit_bytes=VMEM_LIMIT)


def _inproj_kernel(x_ref, nw_ref, w_ref, xm_ref, sz_ref, u_ref, *, d_mlstm, d_conv):
    x = x_ref[...]
    a = x * lax.rsqrt(jnp.mean(x * x, axis=-1, keepdims=True) + EPS) * nw_ref[...]
    ab = a.astype(BF16)
    p1 = jnp.dot(ab, w_ref[:, : 2 * d_mlstm], preferred_element_type=F32)
    xm_ref[...] = p1[:, :d_mlstm]
    sz_ref[...] = _silu(p1[:, d_mlstm:])
    p2 = jnp.dot(ab, w_ref[:, 2 * d_mlstm:], preferred_element_type=F32)
    u_ref[...] = p2[:, :d_conv] * _sigmoid(p2[:, d_conv:])


def _in_proj(xf, norm_w, w_in_b, d_mlstm, d_conv, tm):
    n, d = xf.shape
    row = lambda i: (i, 0)
    const = lambda i: (0, 0)
    return pl.pallas_call(
        functools.partial(_inproj_kernel, d_mlstm=d_mlstm, d_conv=d_conv),
        grid=(n // tm,),
        in_specs=[pl.BlockSpec((tm, d), row),
                  pl.BlockSpec((1, d), const),
                  pl.BlockSpec(w_in_b.shape, const)],
        out_specs=[pl.BlockSpec((tm, d_mlstm), row),
                   pl.BlockSpec((tm, d_mlstm), row),
                   pl.BlockSpec((tm, d_conv), row)],
        out_shape=[jax.ShapeDtypeStruct((n, d_mlstm), F32),
                   jax.ShapeDtypeStruct((n, d_mlstm), F32),
                   jax.ShapeDtypeStruct((n, d_conv), F32)],
        compiler_params=_cparams("parallel"),
        name="in_proj",
    )(xf, norm_w, w_in_b)


def _split3(v):
    hi = v.astype(BF16)
    r1 = v - hi.astype(F32)
    mid = r1.astype(BF16)
    lo = (r1 - mid.astype(F32)).astype(BF16)
    return hi, mid, lo


def _mlstm_kernel(xm_ref, sz_ref, cw_ref, cb_ref, wq_ref, wk_ref, wv_ref, wg_ref, bg_ref, nw_ref, sk_ref,
                  ym_ref, xbuf, c_sc, n_sc, m_sc, *, chunk, dh):
    L = chunk
    nh = N_HEADS
    hist = SUBLANES
    j = pl.program_id(1)

    @pl.when(j == 0)
    def _():
        xbuf[0:hist, :] = jnp.zeros((hist, xbuf.shape[1]), F32)
        c_sc[...] = jnp.zeros_like(c_sc)
        n_sc[...] = jnp.zeros_like(n_sc)
        m_sc[...] = jnp.zeros_like(m_sc)

    xm = xm_ref[...]
    xbuf[hist:hist + L, :] = xm

    acc = cb_ref[...] + cw_ref[MLSTM_CONV_WIDTH - 1:MLSTM_CONV_WIDTH, :] * xm
    for k in range(MLSTM_CONV_WIDTH - 1):
        off = hist - (MLSTM_CONV_WIDTH - 1) + k
        acc = acc + cw_ref[k:k + 1, :] * xbuf[off:off + L, :]
    xc = _silu(acc)
    xbuf[0:hist, :] = xm[L - hist:, :]

    xcb = xc.astype(BF16)
    xmb = xm.astype(BF16)
    nb = xm.shape[1] // MXU_DIM

    def bd(xb, w_ref):
        return jnp.concatenate(
            [jnp.dot(xb[:, g * MXU_DIM:(g + 1) * MXU_DIM], w_ref[g], preferred_element_type=F32) for g in range(nb)],
            axis=-1)

    q = bd(xcb, wq_ref)
    k_ = bd(xcb, wk_ref)
    v = bd(xmb, wv_ref)
    qb, kb, vb = q.astype(BF16), k_.astype(BF16), v.astype(BF16)
    d = xm.shape[1]

    g = (jnp.dot(qb, wg_ref[0:d, :], preferred_element_type=F32)
         + jnp.dot(kb, wg_ref[d:2 * d, :], preferred_element_type=F32)
         + jnp.dot(vb, wg_ref[2 * d:3 * d, :], preferred_element_type=F32)
         + bg_ref[...])
    col = lax.broadcasted_iota(jnp.int32, g.shape, 1)
    log_f = jnp.minimum(g, 0.0) - jnp.log(1.0 + jnp.exp(-jnp.abs(g)))
    gates = jnp.where(col < nh, g, jnp.where(col < 2 * nh, log_f, 0.0))

    ri = lax.broadcasted_iota(jnp.int32, (L, L), 0)
    ci = lax.broadcasted_iota(jnp.int32, (L, L), 1)
    causal = ci <= ri
    tri = jnp.where(causal, 1.0, 0.0).astype(BF16)
    cum = sum(jnp.dot(tri, part, preferred_element_type=F32) for part in _split3(gates))
    colform = jnp.where(col < nh, gates, cum)
    rowform = colform.T

    scale = dh ** -0.5
    for h in range(nh):
        sl = slice(h * dh, (h + 1) * dh)
        li_row = rowform[h:h + 1, :]
        b_row = rowform[nh + h:nh + h + 1, :]
        li_col = colform[:, h:h + 1]
        b_col = colform[:, nh + h:nh + h + 1]
        m_prev = m_sc[h:h + 1, 0:1]
        g_tot = b_row[:, L - 1:L]

        dmat = jnp.where(causal, b_col - b_row + li_row, NEG_INF)
        inter = b_col + m_prev
        m_i = jnp.maximum(inter, jnp.max(dmat, axis=-1, keepdims=True))
        w_intra = jnp.exp(dmat - m_i)
        w_inter = jnp.exp(inter - m_i)

        qh = qb[:, sl]
        s = lax.dot_general(qh, kb[:, sl], (((1,), (1,)), ((), ())), preferred_element_type=F32) * (scale * w_intra)
        c_prev = c_sc[h]
        n_prev = n_sc[h:h + 1, :]
        num = (w_inter * scale) * jnp.dot(qh, c_prev.astype(BF16), preferred_element_type=F32) \
            + jnp.dot(s.astype(BF16), vb[:, sl], preferred_element_type=F32)
        qn = jnp.sum(q[:, sl] * n_prev, axis=-1, keepdims=True) * scale
        den = w_inter * qn + jnp.sum(s, axis=-1, keepdims=True)
        hh = num / jnp.maximum(jnp.abs(den), jnp.exp(-m_i))

        mu = jnp.mean(hh, axis=-1, keepdims=True)
        dev = hh - mu
        var = jnp.mean(dev * dev, axis=-1, keepdims=True)
        hn = dev * lax.rsqrt(var + EPS) * nw_ref[:, sl]
        ym_ref[:, sl] = ((hn + sk_ref[:, sl] * xc[:, sl]) * sz_ref[:, sl]).astype(ym_ref.dtype)

        a_row = g_tot - b_row + li_row
        m_new = jnp.maximum(g_tot + m_prev, jnp.max(a_row, axis=-1, keepdims=True))
        a_col = g_tot - b_col + li_col
        w_state = jnp.exp(a_col - m_new)
        decay = jnp.exp(g_tot + m_prev - m_new)
        kw = k_[:, sl] * w_state
        c_sc[h] = decay * c_prev + lax.dot_general(kw.astype(BF16), vb[:, sl], (((0,), (0,)), ((), ())),
                                                   preferred_element_type=F32)
        n_sc[h:h + 1, :] = decay * n_prev + jnp.sum(kw, axis=0, keepdims=True)
        m_sc[h:h + 1, :] = jnp.broadcast_to(m_new, (1, m_sc.shape[1]))


def _mlstm(xm, sz, conv_w, conv_b, wq, wk, wv, wg, bg, norm_w, skip, bsz, seq):
    n, d = xm.shape
    L = MLSTM_CHUNK
    nc = seq // L
    dh = d // N_HEADS
    row = lambda b, j: (b * nc + j, 0)
    c2 = lambda b, j: (0, 0)
    c3 = lambda b, j: (0, 0, 0)
    return pl.pallas_call(
        functools.partial(_mlstm_kernel, chunk=L, dh=dh),
        grid=(bsz, nc),
        in_specs=[pl.BlockSpec((L, d), row), pl.BlockSpec((L, d), row),
                  pl.BlockSpec(conv_w.shape, c2), pl.BlockSpec(conv_b.shape, c2),
                  pl.BlockSpec(wq.shape, c3), pl.BlockSpec(wk.shape, c3), pl.BlockSpec(wv.shape, c3),
                  pl.BlockSpec(wg.shape, c2), pl.BlockSpec(bg.shape, c2),
                  pl.BlockSpec(norm_w.shape, c2), pl.BlockSpec(skip.shape, c2)],
        out_specs=pl.BlockSpec((L, d), row),
        out_shape=jax.ShapeDtypeStruct((n, d), BF16),
        scratch_shapes=[pltpu.VMEM((L + SUBLANES, d), F32),
                        pltpu.VMEM((N_HEADS, dh, dh), F32),
                        pltpu.VMEM((SUBLANES, dh), F32),
                        pltpu.VMEM((SUBLANES, LANES), F32)],
        compiler_params=_cparams("arbitrary", "arbitrary"),
        name="mlstm",
    )(xm, sz, conv_w, conv_b, wq, wk, wv, wg, bg, norm_w, skip)


def _conv_kernel(u_ref, w_ref, b_ref, nw_ref, nb_ref, yc_ref, ubuf, cbuf, *, tile, rows):
    T = tile
    j = pl.program_id(1)

    @pl.when(j == 0)
    def _():
        ubuf[0:CONV_HALO, :] = jnp.zeros((CONV_HALO, ubuf.shape[1]), F32)

    ubuf[CONV_HALO:CONV_HALO + T, :] = u_ref[...]
    base = CONV_HALO - (CONV_WIDTH - 1)
    n_lane_blocks = ubuf.shape[1] // LANES

    def lane_block(c, carry):
        lanes = pl.ds(pl.multiple_of(c * LANES, LANES), LANES)
        for r0 in range(0, T, rows):
            acc = jnp.broadcast_to(b_ref[:, lanes], (rows, LANES))
            for k in range(CONV_WIDTH):
                acc = acc + w_ref[k:k + 1, lanes] * ubuf[base + r0 + k:base + r0 + k + rows, lanes]
            cbuf[r0:r0 + rows, lanes] = acc
        return carry

    lax.fori_loop(0, n_lane_blocks, lane_block, 0)
    ubuf[0:CONV_HALO, :] = ubuf[T:T + CONV_HALO, :]

    y = cbuf[...]
    mu = jnp.mean(y, axis=-1, keepdims=True)
    dev = y - mu
    var = jnp.mean(dev * dev, axis=-1, keepdims=True)
    yn = dev * lax.rsqrt(var + EPS) * nw_ref[...] + nb_ref[...]
    yc_ref[...] = _silu(yn).astype(yc_ref.dtype)


def _conv_group(u, w, b, norm_w, norm_b, bsz, seq, tile):
    n, d = u.shape
    nt = seq // tile
    row = lambda bi, j: (bi * nt + j, 0)
    c2 = lambda bi, j: (0, 0)
    return pl.pallas_call(
        functools.partial(_conv_kernel, tile=tile, rows=64),
        grid=(bsz, nt),
        in_specs=[pl.BlockSpec((tile, d), row), pl.BlockSpec(w.shape, c2), pl.BlockSpec(b.shape, c2),
                  pl.BlockSpec(norm_w.shape, c2), pl.BlockSpec(norm_b.shape, c2)],
        out_specs=pl.BlockSpec((tile, d), row),
        out_shape=jax.ShapeDtypeStruct((n, d), BF16),
        scratch_shapes=[pltpu.VMEM((tile + CONV_HALO, d), F32), pltpu.VMEM((tile, d), F32)],
        compiler_params=_cparams("arbitrary", "arbitrary"),
        name="conv_group",
    )(u, w, b, norm_w, norm_b)


def _outproj_kernel(x_ref, ym_ref, yc_ref, wo_ref, nw_ref, wr_ref, br_ref,
                    h1_ref, a2_ref, idx_ref, pos_ref, gate_ref, cnt_ref, cnt_sc, *, tm, d_mlstm):
    i = pl.program_id(0)

    @pl.when(i == 0)
    def _():
        cnt_sc[...] = jnp.zeros_like(cnt_sc)

    h1 = (x_ref[...]
          + jnp.dot(ym_ref[...], wo_ref[0:d_mlstm, :], preferred_element_type=F32)
          + jnp.dot(yc_ref[...], wo_ref[d_mlstm:, :], preferred_element_type=F32))
    h1_ref[...] = h1
    a2 = h1 * lax.rsqrt(jnp.mean(h1 * h1, axis=-1, keepdims=True) + EPS) * nw_ref[...]
    n_slabs = a2.shape[1] // LANES
    for s in range(n_slabs):
        a2_ref[pl.ds(s, tm, stride=n_slabs), :] = a2[:, s * LANES:(s + 1) * LANES]

    logits = lax.dot_general(wr_ref[...], a2, (((1,), (1,)), ((), ())),
                             precision=lax.Precision.HIGHEST, preferred_element_type=F32) + br_ref[...]
    e_iota = lax.broadcasted_iota(jnp.int32, logits.shape, 0)
    work = logits
    vals, idxs = [], []
    for _ in range(TOP_K):
        mx = jnp.max(work, axis=0, keepdims=True)
        sel = jnp.min(jnp.where(work == mx, e_iota, N_EXPERTS), axis=0, keepdims=True)
        vals.append(mx)
        idxs.append(sel)
        work = jnp.where(e_iota == sel, NEG_INF, work)
    exps = [jnp.exp(vv - vals[0]) for vv in vals]
    tot = exps[0] + exps[1] + exps[2] + exps[3]
    gates = [ev / tot for ev in exps]

    chosen = functools.reduce(jnp.logical_or, [e_iota == sel for sel in idxs])
    mh = jnp.where(chosen, 1.0, 0.0)
    ri = lax.broadcasted_iota(jnp.int32, (tm, tm), 0)
    ci = lax.broadcasted_iota(jnp.int32, (tm, tm), 1)
    upper = jnp.where(ri < ci, 1.0, 0.0).astype(BF16)
    rank = jnp.dot(mh.astype(BF16), upper, preferred_element_type=F32) + cnt_sc[:, 0:1]
    cnt_new = cnt_sc[...] + jnp.sum(mh, axis=1, keepdims=True)
    cnt_sc[...] = cnt_new
    cnt_ref[...] = cnt_new

    zero_i = jnp.zeros((SUBLANES - TOP_K, tm), jnp.int32)
    pos = [jnp.sum(jnp.where(e_iota == sel, rank, 0.0), axis=0, keepdims=True).astype(jnp.int32) for sel in idxs]
    idx_ref[...] = jnp.concatenate(idxs + [zero_i], axis=0)
    pos_ref[...] = jnp.concatenate(pos + [zero_i], axis=0)
    gate_ref[...] = jnp.concatenate(gates + [jnp.zeros((SUBLANES - TOP_K, tm), F32)], axis=0)


def _out_proj(xf, ym, yc, w_out_b, norm_w, w_router_t, b_router, d_mlstm, tm):
    n, d = xf.shape
    n_slabs = d // LANES
    row = lambda i: (i, 0)
    colb = lambda i: (0, i)
    const = lambda i: (0, 0)
    return pl.pallas_call(
        functools.partial(_outproj_kernel, tm=tm, d_mlstm=d_mlstm),
        grid=(n // tm,),
        in_specs=[pl.BlockSpec((tm, d), row), pl.BlockSpec((tm, ym.shape[1]), row), pl.BlockSpec((tm, yc.shape[1]), row),
                  pl.BlockSpec(w_out_b.shape, const), pl.BlockSpec(norm_w.shape, const),
                  pl.BlockSpec(w_router_t.shape, const), pl.BlockSpec(b_router.shape, const)],
        out_specs=[pl.BlockSpec((tm, d), row),
                   pl.BlockSpec((tm * n_slabs, LANES), row),
                   pl.BlockSpec((SUBLANES, tm), colb),
                   pl.BlockSpec((SUBLANES, tm), colb),
                   pl.BlockSpec((SUBLANES, tm), colb),
                   pl.BlockSpec((N_EXPERTS, LANES), const)],
        out_shape=[jax.ShapeDtypeStruct((n, d), F32),
                   jax.ShapeDtypeStruct((n * n_slabs, LANES), F32),
                   jax.ShapeDtypeStruct((SUBLANES, n), jnp.int32),
                   jax.ShapeDtypeStruct((SUBLANES, n), jnp.int32),
                   jax.ShapeDtypeStruct((SUBLANES, n), F32),
                   jax.ShapeDtypeStruct((N_EXPERTS, LANES), F32)],
        scratch_shapes=[pltpu.VMEM((N_EXPERTS, LANES), F32)],
        compiler_params=_cparams("arbitrary"),
        name="out_proj_router",
    )(xf, ym, yc, w_out_b, norm_w, w_router_t, b_router)


def _row_copy(src, src_row, dst, dst_row, sem, n_slabs):
    return pltpu.make_async_copy(
        src.at[pl.ds(pl.multiple_of(src_row * n_slabs, n_slabs), n_slabs), :],
        dst.at[pl.ds(pl.multiple_of(dst_row * n_slabs, n_slabs), n_slabs), :],
        sem)


def _dispatch_kernel(dest_ref, a2_ref, xs_ref, sem, *, tm, n_slabs):
    def issue(t, carry):
        for k in range(TOP_K):
            _row_copy(a2_ref, t, xs_ref, dest_ref[0, k, t], sem, n_slabs).start()
        return carry

    lax.fori_loop(0, tm, issue, 0)

    def drain(t, carry):
        for k in range(TOP_K):
            _row_copy(a2_ref, t, xs_ref, dest_ref[0, k, t], sem, n_slabs).wait()
        return carry

    lax.fori_loop(0, tm, drain, 0)


def _dispatch(a2s, dest_tiles, n_rows, n_slabs, tm):
    n_tok = a2s.shape[0] // n_slabs
    return pl.pallas_call(
        functools.partial(_dispatch_kernel, tm=tm, n_slabs=n_slabs),
        grid=(n_tok // tm,),
        in_specs=[pl.BlockSpec((1, TOP_K, tm), lambda i: (i, 0, 0), memory_space=pltpu.SMEM),
                  pl.BlockSpec((tm * n_slabs, LANES), lambda i: (i, 0))],
        out_specs=pl.BlockSpec(memory_space=pl.ANY),
        out_shape=jax.ShapeDtypeStruct((n_rows * n_slabs, LANES), F32),
        scratch_shapes=[pltpu.SemaphoreType.DMA],
        compiler_params=_cparams("arbitrary"),
        name="dispatch",
    )(dest_tiles, a2s)


def _experts_kernel(blk_ref, exp_ref, valid_ref, lo_ref, hi_ref,
                    xs_ref, w1_ref, b1_ref, w2_ref, b2_ref, ys_ref, acc_ref, *, rb, d_ff):
    i = pl.program_id(0)
    n_slabs = xs_ref.shape[0] // rb
    prev_blk = blk_ref[jnp.maximum(i - 1, 0)]
    first_visit = jnp.logical_or(i == 0, prev_blk != blk_ref[i])

    @pl.when(valid_ref[i] == 1)
    def _():
        x = jnp.concatenate([xs_ref[pl.ds(s, rb, stride=n_slabs), :] for s in range(n_slabs)], axis=-1)
        hid = jnp.dot(x.astype(BF16), w1_ref[0], preferred_element_type=F32) + b1_ref[0]
        x_glu = jnp.minimum(hid[:, :d_ff], SWIGLU_LIMIT)
        x_lin = jnp.clip(hid[:, d_ff:], -SWIGLU_LIMIT, SWIGLU_LIMIT)
        act = x_glu * _sigmoid(SWIGLU_ALPHA * x_glu) * (x_lin + 1.0)
        y = jnp.dot(act.astype(BF16), w2_ref[0], preferred_element_type=F32) + b2_ref[0]
        r = lax.broadcasted_iota(jnp.int32, (rb, 1), 0)
        y = jnp.where(jnp.logical_and(r >= lo_ref[i], r < hi_ref[i]), y, 0.0)

        @pl.when(first_visit)
        def _():
            acc_ref[...] = y

        @pl.when(jnp.logical_not(first_visit))
        def _():
            acc_ref[...] = acc_ref[...] + y

        for s in range(n_slabs):
            ys_ref[pl.ds(s, rb, stride=n_slabs), :] = acc_ref[:, s * LANES:(s + 1) * LANES]


def _experts(xs, w1b, b1, w2b, b2, item_blk, item_exp, item_valid, item_lo, item_hi, rb, n_slabs):
    n_items = item_blk.shape[0]
    d = n_slabs * LANES
    d_ff = w2b.shape[1]
    grid_spec = pltpu.PrefetchScalarGridSpec(
        num_scalar_prefetch=5,
        grid=(n_items,),
        in_specs=[pl.BlockSpec((rb * n_slabs, LANES), lambda i, blk, ex, va, lo, hi: (blk[i], 0)),
                  pl.BlockSpec((1,) + w1b.shape[1:], lambda i, blk, ex, va, lo, hi: (ex[i], 0, 0)),
                  pl.BlockSpec((1,) + b1.shape[1:], lambda i, blk, ex, va, lo, hi: (ex[i], 0, 0)),
                  pl.BlockSpec((1,) + w2b.shape[1:], lambda i, blk, ex, va, lo, hi: (ex[i], 0, 0)),
                  pl.BlockSpec((1,) + b2.shape[1:], lambda i, blk, ex, va, lo, hi: (ex[i], 0, 0))],
        out_specs=pl.BlockSpec((rb * n_slabs, LANES), lambda i, blk, ex, va, lo, hi: (blk[i], 0)),
        scratch_shapes=[pltpu.VMEM((rb, d), F32)],
    )
    return pl.pallas_call(
        functools.partial(_experts_kernel, rb=rb, d_ff=d_ff),
        grid_spec=grid_spec,
        out_shape=jax.ShapeDtypeStruct(xs.shape, F32),
        compiler_params=_cparams("arbitrary"),
        name="experts",
    )(item_blk, item_exp, item_valid, item_lo, item_hi, xs, w1b, b1, w2b, b2)


def _combine_kernel(dest_ref, h1_ref, gate_ref, ys_ref, nw_ref, out_ref, ybuf, sem, *, tm, n_slabs):
    def issue(t, carry):
        for k in range(TOP_K):
            _row_copy(ys_ref, dest_ref[0, k, t], ybuf, k * tm + t, sem, n_slabs).start()
        return carry

    lax.fori_loop(0, tm, issue, 0)

    gpad = jnp.concatenate([gate_ref[...], jnp.zeros((LANES - SUBLANES, tm), F32)], axis=0)
    gcol = gpad.T

    def drain(t, carry):
        for k in range(TOP_K):
            _row_copy(ys_ref, dest_ref[0, k, t], ybuf, k * tm + t, sem, n_slabs).wait()
        return carry

    lax.fori_loop(0, tm, drain, 0)

    h2 = h1_ref[...]
    for k in range(TOP_K):
        yk = jnp.concatenate([ybuf[pl.ds(k * tm * n_slabs + s, tm, stride=n_slabs), :] for s in range(n_slabs)],
                             axis=-1)
        h2 = h2 + gcol[:, k:k + 1] * yk
    out_ref[...] = h2 * lax.rsqrt(jnp.mean(h2 * h2, axis=-1, keepdims=True) + EPS) * nw_ref[...]


def _combine(dest_tiles, h1, gates_t, ys, final_w, n_slabs, tm):
    n, d = h1.shape
    return pl.pallas_call(
        functools.partial(_combine_kernel, tm=tm, n_slabs=n_slabs),
        grid=(n // tm,),
        in_specs=[pl.BlockSpec((1, TOP_K, tm), lambda i: (i, 0, 0), memory_space=pltpu.SMEM),
                  pl.BlockSpec((tm, d), lambda i: (i, 0)),
                  pl.BlockSpec((SUBLANES, tm), lambda i: (0, i)),
                  pl.BlockSpec(memory_space=pl.ANY),
                  pl.BlockSpec((1, d), lambda i: (0, 0))],
        out_specs=pl.BlockSpec((tm, d), lambda i: (i, 0)),
        out_shape=jax.ShapeDtypeStruct((n, d), F32),
        scratch_shapes=[pltpu.VMEM((TOP_K * tm * n_slabs, LANES), F32), pltpu.SemaphoreType.DMA],
        compiler_params=_cparams("arbitrary"),
        name="combine",
    )(dest_tiles, h1, gates_t, ys, final_w)


def _block_diag_tiles(w):
    nb, bs, _ = w.shape
    per = MXU_DIM // bs
    wt = w.reshape(nb // per, per, bs, bs)
    eye = jnp.eye(per, dtype=w.dtype)
    dense = jnp.einsum('gpio,pq->gpiqo', wt, eye).reshape(nb // per, MXU_DIM, MXU_DIM)
    return dense.astype(BF16)


def _layer(xf, bsz, seq, norm_mix_w, w_in, mlstm_conv_w, mlstm_conv_b, w_q, w_k, w_v, w_igate, b_igate,
           w_fgate, b_fgate, mlstm_norm_w, mlstm_skip, conv_dw_w, conv_dw_b, conv_norm_w, conv_norm_b,
           w_out, norm_ffn_w, w_router, b_router, w1, b1, w2, b2, final_norm_w):
    n, d = xf.shape
    d_mlstm = mlstm_norm_w.shape[0]
    d_conv = conv_norm_w.shape[0]
    n_slabs = d // LANES
    r2 = lambda v: v.reshape(1, -1)

    xm, sz, u = _in_proj(xf, r2(norm_mix_w), w_in.astype(BF16), d_mlstm, d_conv, tm=256)

    wg = jnp.concatenate([w_igate, w_fgate], axis=1)
    wg = jnp.pad(wg, ((0, 0), (0, LANES - wg.shape[1]))).astype(BF16)
    bg = jnp.pad(jnp.concatenate([b_igate, b_fgate]), (0, LANES - 2 * N_HEADS)).reshape(1, LANES)
    ym = _mlstm(xm, sz, mlstm_conv_w, r2(mlstm_conv_b), _block_diag_tiles(w_q), _block_diag_tiles(w_k),
                _block_diag_tiles(w_v), wg, bg, r2(mlstm_norm_w), r2(mlstm_skip), bsz, seq)
    yc = _conv_group(u, conv_dw_w, r2(conv_dw_b), r2(conv_norm_w), r2(conv_norm_b), bsz, seq, tile=256)

    tm = 256
    h1, a2s, idx_t, pos_t, gates_t, cnt = _out_proj(
        xf, ym, yc, w_out.astype(BF16), r2(norm_ffn_w), w_router.T, b_router.reshape(-1, 1), d_mlstm, tm=tm)

    counts = cnt[:, 0].astype(jnp.int32)
    ends = jnp.cumsum(counts)
    starts = ends - counts
    dest = starts[idx_t[:TOP_K]] + pos_t[:TOP_K]
    dest_tiles = dest.reshape(TOP_K, n // tm, tm).transpose(1, 0, 2)

    n_rows = n * TOP_K
    rb = 512
    n_blocks = n_rows // rb
    n_items = n_blocks + N_EXPERTS - 1
    first_blk = starts // rb
    last_blk = jnp.where(counts > 0, (ends - 1) // rb, first_blk - 1)
    per_e = last_blk - first_blk + 1
    item_end = jnp.cumsum(per_e)
    item_start = item_end - per_e
    ids = jnp.arange(n_items, dtype=jnp.int32)
    total = item_end[-1]
    item_valid = (ids < total).astype(jnp.int32)
    item_exp = jnp.minimum(jnp.searchsorted(item_end, ids, side='right'), N_EXPERTS - 1).astype(jnp.int32)
    item_blk = jnp.where(item_valid == 1, first_blk[item_exp] + ids - item_start[item_exp], n_blocks - 1)
    last_valid_exp = item_exp[jnp.maximum(total - 1, 0)]
    item_exp = jnp.where(item_valid == 1, item_exp, last_valid_exp).astype(jnp.int32)
    item_blk = item_blk.astype(jnp.int32)
    item_lo = (jnp.maximum(starts[item_exp], item_blk * rb) - item_blk * rb).astype(jnp.int32)
    item_hi = (jnp.minimum(ends[item_exp], (item_blk + 1) * rb) - item_blk * rb).astype(jnp.int32)

    xs = _dispatch(a2s, dest_tiles, n_rows, n_slabs, tm=tm)
    ys = _experts(xs, w1.astype(BF16), b1[:, None, :], w2.astype(BF16), b2[:, None, :],
                  item_blk, item_exp, item_valid, item_lo, item_hi, rb, n_slabs)
    return _combine(dest_tiles, h1, gates_t, ys, r2(final_norm_w), n_slabs, tm=tm)


def kernel(x, norm_mix_w, w_in, mlstm_conv_w, mlstm_conv_b, w_q, w_k, w_v, w_igate, b_igate, w_fgate, b_fgate,
           mlstm_norm_w, mlstm_skip, conv_dw_w, conv_dw_b, conv_norm_w, conv_norm_b, w_out, norm_ffn_w,
           w_router, b_router, w1, b1, w2, b2, final_norm_w):
    bsz, seq, d = x.shape
    assert norm_mix_w.shape[0] == 1, "single-layer block"
    out = _layer(x.reshape(bsz * seq, d), bsz, seq, norm_mix_w[0], w_in[0], mlstm_conv_w[0], mlstm_conv_b[0],
                 w_q[0], w_k[0], w_v[0], w_igate[0], b_igate[0], w_fgate[0], b_fgate[0], mlstm_norm_w[0],
                 mlstm_skip[0], conv_dw_w[0], conv_dw_b[0], conv_norm_w[0], conv_norm_b[0], w_out[0],
                 norm_ffn_w[0], w_router[0], b_router[0], w1[0], b1[0], w2[0], b2[0], final_norm_w)
    return out.reshape(bsz, seq, d)
```

```python
import functools

import jax
import jax.numpy as jnp
from jax import lax
from jax.experimental import pallas as pl
from jax.experimental.pallas import tpu as pltpu

F32 = jnp.float32
BF16 = jnp.bfloat16

EPS = 1e-5
N_HEADS = 4
QKV_BLOCK = 4
MLSTM_CONV_WIDTH = 4
CONV_WIDTH = 31
N_EXPERTS = 32
TOP_K = 4
SWIGLU_ALPHA = 1.702
SWIGLU_LIMIT = 7.0

LANES = 128
SUBLANES = 8
MXU_DIM = 256
VMEM_LIMIT = 52 * 1024 * 1024
EXPERTS_VMEM_LIMIT = 58 * 1024 * 1024

MLSTM_CHUNK = 256
CONV_HALO = 32
NEG_INF = float("-inf")


def _sigmoid(x):
    return 1.0 / (1.0 + jnp.exp(-x))


def _silu(x):
    return x * _sigmoid(x)


def _cparams(*sem):
    return pltpu.CompilerParams(dimension_semantics=sem, vmem_limit_bytes=VMEM_LIMIT)


def _inproj_kernel(x_ref, nw_ref, w_ref, cw_ref, cb_ref, xm_ref, xc_ref, sz_ref, u_ref, xbuf,
                   *, d_mlstm, d_conv, tiles_per_seq):
    tm = x_ref.shape[0]
    hist = SUBLANES

    @pl.when(pl.program_id(0) % tiles_per_seq == 0)
    def _():
        xbuf[0:hist, :] = jnp.zeros((hist, d_mlstm), F32)

    x = x_ref[...]
    a = x * lax.rsqrt(jnp.mean(x * x, axis=-1, keepdims=True) + EPS) * nw_ref[...]
    ab = a.astype(BF16)
    p1 = jnp.dot(ab, w_ref[:, : 2 * d_mlstm], preferred_element_type=F32)
    xm = p1[:, :d_mlstm]
    xm_ref[...] = xm.astype(xm_ref.dtype)
    sz_ref[...] = _silu(p1[:, d_mlstm:])

    xbuf[hist:hist + tm, :] = xm
    acc = cb_ref[...] + cw_ref[MLSTM_CONV_WIDTH - 1:MLSTM_CONV_WIDTH, :] * xm
    for k in range(MLSTM_CONV_WIDTH - 1):
        off = hist - (MLSTM_CONV_WIDTH - 1) + k
        acc = acc + cw_ref[k:k + 1, :] * xbuf[off:off + tm, :]
    xc_ref[...] = _silu(acc).astype(xc_ref.dtype)
    xbuf[0:hist, :] = xm[tm - hist:, :]

    p2 = jnp.dot(ab, w_ref[:, 2 * d_mlstm:], preferred_element_type=F32)
    u_ref[...] = p2[:, :d_conv] * _sigmoid(p2[:, d_conv:])


def _in_proj(xf, norm_w, w_in_b, conv_w, conv_b, d_mlstm, d_conv, seq, tm):
    n, d = xf.shape
    row = lambda i: (i, 0)
    const = lambda i: (0, 0)
    return pl.pallas_call(
        functools.partial(_inproj_kernel, d_mlstm=d_mlstm, d_conv=d_conv, tiles_per_seq=seq // tm),
        grid=(n // tm,),
        in_specs=[pl.BlockSpec((tm, d), row),
                  pl.BlockSpec((1, d), const),
                  pl.BlockSpec(w_in_b.shape, const),
                  pl.BlockSpec(conv_w.shape, const),
                  pl.BlockSpec(conv_b.shape, const)],
        out_specs=[pl.BlockSpec((tm, d_mlstm), row),
                   pl.BlockSpec((tm, d_mlstm), row),
                   pl.BlockSpec((tm, d_mlstm), row),
                   pl.BlockSpec((tm, d_conv), row)],
        out_shape=[jax.ShapeDtypeStruct((n, d_mlstm), BF16),
                   jax.ShapeDtypeStruct((n, d_mlstm), BF16),
                   jax.ShapeDtypeStruct((n, d_mlstm), F32),
                   jax.ShapeDtypeStruct((n, d_conv), F32)],
        scratch_shapes=[pltpu.VMEM((tm + SUBLANES, d_mlstm), F32)],
        compiler_params=_cparams("arbitrary"),
        name="in_proj",
    )(xf, norm_w, w_in_b, conv_w, conv_b)


def _split3(v):
    hi = v.astype(BF16)
    r1 = v - hi.astype(F32)
    mid = r1.astype(BF16)
    lo = (r1 - mid.astype(F32)).astype(BF16)
    return hi, mid, lo


def _mlstm_kernel(xm_ref, xc_ref, sz_ref, wq_ref, wk_ref, wv_ref, wg_ref, bg_ref, nw_ref, sk_ref,
                  ym_ref, c_sc, n_sc, m_sc, *, chunk, dh, n_seq):
    L = chunk
    nh = N_HEADS
    j = pl.program_id(1)

    @pl.when(j == 0)
    def _():
        c_sc[...] = jnp.zeros_like(c_sc)
        n_sc[...] = jnp.zeros_like(n_sc)
        m_sc[...] = jnp.zeros_like(m_sc)

    ri = lax.broadcasted_iota(jnp.int32, (L, L), 0)
    ci = lax.broadcasted_iota(jnp.int32, (L, L), 1)
    causal = ci <= ri
    tri = jnp.where(causal, 1.0, 0.0).astype(BF16)
    scale = dh ** -0.5

    for sq in range(n_seq):
        xmb = xm_ref[sq]
        xcb = xc_ref[sq]
        d = xmb.shape[1]
        nb = d // MXU_DIM

        def bd(xb, w_ref):
            return jnp.concatenate(
                [jnp.dot(xb[:, g * MXU_DIM:(g + 1) * MXU_DIM], w_ref[g], preferred_element_type=F32)
                 for g in range(nb)], axis=-1)

        q = bd(xcb, wq_ref)
        k_ = bd(xcb, wk_ref)
        v = bd(xmb, wv_ref)
        qb, kb, vb = q.astype(BF16), k_.astype(BF16), v.astype(BF16)

        g = (jnp.dot(qb, wg_ref[0:d, :], preferred_element_type=F32)
             + jnp.dot(kb, wg_ref[d:2 * d, :], preferred_element_type=F32)
             + jnp.dot(vb, wg_ref[2 * d:3 * d, :], preferred_element_type=F32)
             + bg_ref[...])
        col = lax.broadcasted_iota(jnp.int32, g.shape, 1)
        log_f = jnp.minimum(g, 0.0) - jnp.log(1.0 + jnp.exp(-jnp.abs(g)))
        gates = jnp.where(col < nh, g, jnp.where(col < 2 * nh, log_f, 0.0))
        cum = sum(jnp.dot(tri, part, preferred_element_type=F32) for part in _split3(gates))
        colform = jnp.where(col < nh, gates, cum)
        rowform = colform.T

        for h in range(nh):
            sl = slice(h * dh, (h + 1) * dh)
            li_row = rowform[h:h + 1, :]
            b_row = rowform[nh + h:nh + h + 1, :]
            li_col = colform[:, h:h + 1]
            b_col = colform[:, nh + h:nh + h + 1]
            m_prev = m_sc[sq, h:h + 1, 0:1]
            g_tot = b_row[:, L - 1:L]

            dmat = jnp.where(causal, b_col - b_row + li_row, NEG_INF)
            inter = b_col + m_prev
            m_i = jnp.maximum(inter, jnp.max(dmat, axis=-1, keepdims=True))
            w_intra = jnp.exp(dmat - m_i)
            w_inter = jnp.exp(inter - m_i)

            qh = qb[:, sl]
            s = lax.dot_general(qh, kb[:, sl], (((1,), (1,)), ((), ())),
                                preferred_element_type=F32) * (scale * w_intra)
            c_prev = c_sc[sq, h]
            n_prev = n_sc[sq, h:h + 1, :]
            num = (w_inter * scale) * jnp.dot(qh, c_prev.astype(BF16), preferred_element_type=F32) \
                + jnp.dot(s.astype(BF16), vb[:, sl], preferred_element_type=F32)
            qn = jnp.sum(q[:, sl] * n_prev, axis=-1, keepdims=True) * scale
            den = w_inter * qn + jnp.sum(s, axis=-1, keepdims=True)
            hh = num / jnp.maximum(jnp.abs(den), jnp.exp(-m_i))

            mu = jnp.mean(hh, axis=-1, keepdims=True)
            dev = hh - mu
            var = jnp.mean(dev * dev, axis=-1, keepdims=True)
            hn = dev * lax.rsqrt(var + EPS) * nw_ref[:, sl]
            ym_ref[sq, :, sl] = ((hn + sk_ref[:, sl] * xcb[:, sl].astype(F32)) * sz_ref[sq, :, sl]).astype(ym_ref.dtype)

            a_row = g_tot - b_row + li_row
            m_new = jnp.maximum(g_tot + m_prev, jnp.max(a_row, axis=-1, keepdims=True))
            a_col = g_tot - b_col + li_col
            w_state = jnp.exp(a_col - m_new)
            decay = jnp.exp(g_tot + m_prev - m_new)
            kw = k_[:, sl] * w_state
            c_sc[sq, h] = decay * c_prev + lax.dot_general(kw.astype(BF16), vb[:, sl], (((0,), (0,)), ((), ())),
                                                           preferred_element_type=F32)
            n_sc[sq, h:h + 1, :] = decay * n_prev + jnp.sum(kw, axis=0, keepdims=True)
            m_sc[sq, h:h + 1, :] = jnp.broadcast_to(m_new, (1, m_sc.shape[2]))


def _mlstm(xm, xc, sz, wq, wk, wv, wg, bg, norm_w, skip, bsz, seq):
    n, d = xm.shape
    L = MLSTM_CHUNK
    nc = seq // L
    dh = d // N_HEADS
    n_seq = 2 if bsz % 2 == 0 else 1
    blk = lambda b, j: (b, j, 0)
    c2 = lambda b, j: (0, 0)
    c3 = lambda b, j: (0, 0, 0)
    ym = pl.pallas_call(
        functools.partial(_mlstm_kernel, chunk=L, dh=dh, n_seq=n_seq),
        grid=(bsz // n_seq, nc),
        in_specs=[pl.BlockSpec((n_seq, L, d), blk), pl.BlockSpec((n_seq, L, d), blk), pl.BlockSpec((n_seq, L, d), blk),
                  pl.BlockSpec(wq.shape, c3), pl.BlockSpec(wk.shape, c3), pl.BlockSpec(wv.shape, c3),
                  pl.BlockSpec(wg.shape, c2), pl.BlockSpec(bg.shape, c2),
                  pl.BlockSpec(norm_w.shape, c2), pl.BlockSpec(skip.shape, c2)],
        out_specs=pl.BlockSpec((n_seq, L, d), blk),
        out_shape=jax.ShapeDtypeStruct((bsz, seq, d), BF16),
        scratch_shapes=[pltpu.VMEM((n_seq, N_HEADS, dh, dh), F32),
                        pltpu.VMEM((n_seq, SUBLANES, dh), F32),
                        pltpu.VMEM((n_seq, SUBLANES, LANES), F32)],
        compiler_params=_cparams("arbitrary", "arbitrary"),
        name="mlstm",
    )(xm.reshape(bsz, seq, d), xc.reshape(bsz, seq, d), sz.reshape(bsz, seq, d), wq, wk, wv, wg, bg, norm_w, skip)
    return ym.reshape(n, d)


def _conv_kernel(u_ref, w_ref, b_ref, nw_ref, nb_ref, yc_ref, ubuf, pbuf, cbuf, *, tile, rows):
    T = tile
    j = pl.program_id(1)

    @pl.when(j == 0)
    def _():
        ubuf[0:CONV_HALO, :] = jnp.zeros((CONV_HALO, ubuf.shape[1]), F32)

    ubuf[CONV_HALO:CONV_HALO + T, :] = u_ref[...]
    base = CONV_HALO - (CONV_WIDTH - 1)
    span = T + CONV_HALO - SUBLANES
    n_lane_blocks = ubuf.shape[1] // LANES

    def lane_block(c, carry):
        lanes = pl.ds(pl.multiple_of(c * LANES, LANES), LANES)
        for r in range(1, SUBLANES):
            pbuf[r - 1, :, :] = ubuf[r:r + span, lanes]
        for r0 in range(0, T, rows):
            acc = jnp.broadcast_to(b_ref[:, lanes], (rows, LANES))
            for k in range(CONV_WIDTH):
                q, r = divmod(base + k, SUBLANES)
                lo = r0 + q * SUBLANES
                src = ubuf[lo:lo + rows, lanes] if r == 0 else pbuf[r - 1, lo:lo + rows, :]
                acc = acc + w_ref[k:k + 1, lanes] * src
            cbuf[r0:r0 + rows, lanes] = acc
        return carry

    lax.fori_loop(0, n_lane_blocks, lane_block, 0)
    ubuf[0:CONV_HALO, :] = ubuf[T:T + CONV_HALO, :]

    y = cbuf[...]
    mu = jnp.mean(y, axis=-1, keepdims=True)
    dev = y - mu
    var = jnp.mean(dev * dev, axis=-1, keepdims=True)
    yn = dev * lax.rsqrt(var + EPS) * nw_ref[...] + nb_ref[...]
    yc_ref[...] = _silu(yn).astype(yc_ref.dtype)


def _conv_group(u, w, b, norm_w, norm_b, bsz, seq, tile):
    n, d = u.shape
    nt = seq // tile
    row = lambda bi, j: (bi * nt + j, 0)
    c2 = lambda bi, j: (0, 0)
    return pl.pallas_call(
        functools.partial(_conv_kernel, tile=tile, rows=64),
        grid=(bsz, nt),
        in_specs=[pl.BlockSpec((tile, d), row), pl.BlockSpec(w.shape, c2), pl.BlockSpec(b.shape, c2),
                  pl.BlockSpec(norm_w.shape, c2), pl.BlockSpec(norm_b.shape, c2)],
        out_specs=pl.BlockSpec((tile, d), row),
        out_shape=jax.ShapeDtypeStruct((n, d), BF16),
        scratch_shapes=[pltpu.VMEM((tile + CONV_HALO, d), F32),
                        pltpu.VMEM((SUBLANES - 1, tile + CONV_HALO - SUBLANES, LANES), F32),
                        pltpu.VMEM((tile, d), F32)],
        compiler_params=_cparams("arbitrary", "arbitrary"),
        name="conv_group",
    )(u, w, b, norm_w, norm_b)


def _outproj_kernel(x_ref, ym_ref, yc_ref, wo_ref, nw_ref, wr_ref, br_ref,
                    h1_ref, a2_ref, idx_ref, pos_ref, gate_ref, cnt_ref, cnt_sc, *, tm, d_mlstm):
    i = pl.program_id(0)

    @pl.when(i == 0)
    def _():
        cnt_sc[...] = jnp.zeros_like(cnt_sc)

    h1 = (x_ref[...]
          + jnp.dot(ym_ref[...], wo_ref[0:d_mlstm, :], preferred_element_type=F32)
          + jnp.dot(yc_ref[...], wo_ref[d_mlstm:, :], preferred_element_type=F32))
    h1_ref[...] = h1
    a2 = h1 * lax.rsqrt(jnp.mean(h1 * h1, axis=-1, keepdims=True) + EPS) * nw_ref[...]
    n_slabs = a2.shape[1] // LANES
    for s in range(n_slabs):
        a2_ref[pl.ds(s, tm, stride=n_slabs), :] = a2[:, s * LANES:(s + 1) * LANES]

    logits = lax.dot_general(wr_ref[...], a2, (((1,), (1,)), ((), ())),
                             precision=lax.Precision.HIGHEST, preferred_element_type=F32) + br_ref[...]
    e_iota = lax.broadcasted_iota(jnp.int32, logits.shape, 0)
    work = logits
    vals, idxs = [], []
    for _ in range(TOP_K):
        mx = jnp.max(work, axis=0, keepdims=True)
        sel = jnp.min(jnp.where(work == mx, e_iota, N_EXPERTS), axis=0, keepdims=True)
        vals.append(mx)
        idxs.append(sel)
        work = jnp.where(e_iota == sel, NEG_INF, work)
    exps = [jnp.exp(vv - vals[0]) for vv in vals]
    tot = exps[0] + exps[1] + exps[2] + exps[3]
    gates = [ev / tot for ev in exps]

    chosen = functools.reduce(jnp.logical_or, [e_iota == sel for sel in idxs])
    mh = jnp.where(chosen, 1.0, 0.0)
    ri = lax.broadcasted_iota(jnp.int32, (tm, tm), 0)
    ci = lax.broadcasted_iota(jnp.int32, (tm, tm), 1)
    upper = jnp.where(ri < ci, 1.0, 0.0).astype(BF16)
    rank = jnp.dot(mh.astype(BF16), upper, preferred_element_type=F32) + cnt_sc[:, 0:1]
    cnt_new = cnt_sc[...] + jnp.sum(mh, axis=1, keepdims=True)
    cnt_sc[...] = cnt_new
    cnt_ref[...] = cnt_new

    zero_i = jnp.zeros((SUBLANES - TOP_K, tm), jnp.int32)
    pos = [jnp.sum(jnp.where(e_iota == sel, rank, 0.0), axis=0, keepdims=True).astype(jnp.int32) for sel in idxs]
    idx_ref[...] = jnp.concatenate(idxs + [zero_i], axis=0)
    pos_ref[...] = jnp.concatenate(pos + [zero_i], axis=0)
    gate_ref[...] = jnp.concatenate(gates + [jnp.zeros((SUBLANES - TOP_K, tm), F32)], axis=0)


def _out_proj(xf, ym, yc, w_out_b, norm_w, w_router_t, b_router, d_mlstm, tm):
    n, d = xf.shape
    n_slabs = d // LANES
    row = lambda i: (i, 0)
    colb = lambda i: (0, i)
    const = lambda i: (0, 0)
    return pl.pallas_call(
        functools.partial(_outproj_kernel, tm=tm, d_mlstm=d_mlstm),
        grid=(n // tm,),
        in_specs=[pl.BlockSpec((tm, d), row), pl.BlockSpec((tm, ym.shape[1]), row), pl.BlockSpec((tm, yc.shape[1]), row),
                  pl.BlockSpec(w_out_b.shape, const), pl.BlockSpec(norm_w.shape, const),
                  pl.BlockSpec(w_router_t.shape, const), pl.BlockSpec(b_router.shape, const)],
        out_specs=[pl.BlockSpec((tm, d), row),
                   pl.BlockSpec((tm * n_slabs, LANES), row),
                   pl.BlockSpec((SUBLANES, tm), colb),
                   pl.BlockSpec((SUBLANES, tm), colb),
                   pl.BlockSpec((SUBLANES, tm), colb),
                   pl.BlockSpec((N_EXPERTS, LANES), const)],
        out_shape=[jax.ShapeDtypeStruct((n, d), F32),
                   jax.ShapeDtypeStruct((n * n_slabs, LANES), F32),
                   jax.ShapeDtypeStruct((SUBLANES, n), jnp.int32),
                   jax.ShapeDtypeStruct((SUBLANES, n), jnp.int32),
                   jax.ShapeDtypeStruct((SUBLANES, n), F32),
                   jax.ShapeDtypeStruct((N_EXPERTS, LANES), F32)],
        scratch_shapes=[pltpu.VMEM((N_EXPERTS, LANES), F32)],
        compiler_params=_cparams("arbitrary"),
        name="out_proj_router",
    )(xf, ym, yc, w_out_b, norm_w, w_router_t, b_router)


def _row_copy(src, src_row, dst, dst_row, sem, n_slabs):
    return pltpu.make_async_copy(
        src.at[pl.ds(pl.multiple_of(src_row * n_slabs, n_slabs), n_slabs), :],
        dst.at[pl.ds(pl.multiple_of(dst_row * n_slabs, n_slabs), n_slabs), :],
        sem)


def _dispatch_kernel(dest_ref, a2_ref, xs_ref, sem, *, tm, n_slabs):
    def issue(t, carry):
        for k in range(TOP_K):
            _row_copy(a2_ref, t, xs_ref, dest_ref[0, k, t], sem, n_slabs).start()
        return carry

    lax.fori_loop(0, tm, issue, 0)

    def drain(t, carry):
        for k in range(TOP_K):
            _row_copy(a2_ref, t, xs_ref, dest_ref[0, k, t], sem, n_slabs).wait()
        return carry

    lax.fori_loop(0, tm, drain, 0)


def _dispatch(a2s, dest_tiles, n_rows, n_slabs, tm):
    n_tok = a2s.shape[0] // n_slabs
    return pl.pallas_call(
        functools.partial(_dispatch_kernel, tm=tm, n_slabs=n_slabs),
        grid=(n_tok // tm,),
        in_specs=[pl.BlockSpec((1, TOP_K, tm), lambda i: (i, 0, 0), memory_space=pltpu.SMEM),
                  pl.BlockSpec((tm * n_slabs, LANES), lambda i: (i, 0))],
        out_specs=pl.BlockSpec(memory_space=pl.ANY),
        out_shape=jax.ShapeDtypeStruct((n_rows * n_slabs, LANES), F32),
        scratch_shapes=[pltpu.SemaphoreType.DMA],
        compiler_params=_cparams("arbitrary"),
        name="dispatch",
    )(dest_tiles, a2s)


def _experts_kernel(blk_ref, exp_ref, valid_ref, lo_ref, hi_ref,
                    xs_ref, w1_ref, b1_ref, w2_ref, b2_ref, ys_ref, w1c_ref, w2c_ref, acc_ref, *, rb, d_ff, n_sub):
    i = pl.program_id(0)
    n_slabs = xs_ref.shape[0] // rb
    prev = jnp.maximum(i - 1, 0)
    first_visit = jnp.logical_or(i == 0, blk_ref[prev] != blk_ref[i])
    new_expert = jnp.logical_or(i == 0, exp_ref[prev] != exp_ref[i])

    @pl.when(new_expert)
    def _():
        w1c_ref[...] = w1_ref[0].astype(BF16)
        w2c_ref[...] = w2_ref[0].astype(BF16)

    @pl.when(first_visit)
    def _():
        acc_ref[...] = jnp.zeros_like(acc_ref)

    @pl.when(valid_ref[i] == 1)
    def _():
        rs = rb // n_sub
        for c in range(n_sub):
            x = jnp.concatenate([xs_ref[pl.ds(c * rs * n_slabs + s, rs, stride=n_slabs), :] for s in range(n_slabs)],
                                axis=-1)
            hid = jnp.dot(x.astype(BF16), w1c_ref[...], preferred_element_type=F32) + b1_ref[0]
            x_glu = jnp.minimum(hid[:, :d_ff], SWIGLU_LIMIT)
            x_lin = jnp.clip(hid[:, d_ff:], -SWIGLU_LIMIT, SWIGLU_LIMIT)
            act = x_glu * _sigmoid(SWIGLU_ALPHA * x_glu) * (x_lin + 1.0)
            y = jnp.dot(act.astype(BF16), w2c_ref[...], preferred_element_type=F32) + b2_ref[0]
            r = c * rs + lax.broadcasted_iota(jnp.int32, (rs, 1), 0)
            y = jnp.where(jnp.logical_and(r >= lo_ref[i], r < hi_ref[i]), y, 0.0)
            y = acc_ref[c * rs:(c + 1) * rs, :] + y
            acc_ref[c * rs:(c + 1) * rs, :] = y
            for s in range(n_slabs):
                ys_ref[pl.ds(c * rs * n_slabs + s, rs, stride=n_slabs), :] = y[:, s * LANES:(s + 1) * LANES]


def _experts(xs, w1, b1, w2, b2, item_blk, item_exp, item_valid, item_lo, item_hi, rb, n_slabs):
    n_items = item_blk.shape[0]
    d = n_slabs * LANES
    d_ff = w2.shape[1]
    by_blk = lambda i, blk, ex, va, lo, hi: (blk[i], 0)
    by_exp = lambda i, blk, ex, va, lo, hi: (ex[i], 0, 0)
    grid_spec = pltpu.PrefetchScalarGridSpec(
        num_scalar_prefetch=5,
        grid=(n_items,),
        in_specs=[pl.BlockSpec((rb * n_slabs, LANES), by_blk),
                  pl.BlockSpec((1,) + w1.shape[1:], by_exp),
                  pl.BlockSpec((1,) + b1.shape[1:], by_exp),
                  pl.BlockSpec((1,) + w2.shape[1:], by_exp),
                  pl.BlockSpec((1,) + b2.shape[1:], by_exp)],
        out_specs=pl.BlockSpec((rb * n_slabs, LANES), by_blk),
        scratch_shapes=[pltpu.VMEM(w1.shape[1:], BF16), pltpu.VMEM(w2.shape[1:], BF16), pltpu.VMEM((rb, d), F32)],
    )
    return pl.pallas_call(
        functools.partial(_experts_kernel, rb=rb, d_ff=d_ff, n_sub=2),
        grid_spec=grid_spec,
        out_shape=jax.ShapeDtypeStruct(xs.shape, F32),
        compiler_params=pltpu.CompilerParams(dimension_semantics=("arbitrary",), vmem_limit_bytes=EXPERTS_VMEM_LIMIT),
        name="experts",
    )(item_blk, item_exp, item_valid, item_lo, item_hi, xs, w1, b1, w2, b2)


def _combine_kernel(dest_ref, h1_ref, gate_ref, ys_ref, nw_ref, out_ref, ybuf, sem, *, tm, n_slabs):
    def issue(t, carry):
        for k in range(TOP_K):
            _row_copy(ys_ref, dest_ref[0, k, t], ybuf, k * tm + t, sem, n_slabs).start()
        return carry

    lax.fori_loop(0, tm, issue, 0)

    gpad = jnp.concatenate([gate_ref[...], jnp.zeros((LANES - SUBLANES, tm), F32)], axis=0)
    gcol = gpad.T

    def drain(t, carry):
        for k in range(TOP_K):
            _row_copy(ys_ref, dest_ref[0, k, t], ybuf, k * tm + t, sem, n_slabs).wait()
        return carry

    lax.fori_loop(0, tm, drain, 0)

    h2 = h1_ref[...]
    for k in range(TOP_K):
        yk = jnp.concatenate([ybuf[pl.ds(k * tm * n_slabs + s, tm, stride=n_slabs), :] for s in range(n_slabs)],
                             axis=-1)
        h2 = h2 + gcol[:, k:k + 1] * yk
    out_ref[...] = h2 * lax.rsqrt(jnp.mean(h2 * h2, axis=-1, keepdims=True) + EPS) * nw_ref[...]


def _combine(dest_tiles, h1, gates_t, ys, final_w, n_slabs, tm):
    n, d = h1.shape
    return pl.pallas_call(
        functools.partial(_combine_kernel, tm=tm, n_slabs=n_slabs),
        grid=(n // tm,),
        in_specs=[pl.BlockSpec((1, TOP_K, tm), lambda i: (i, 0, 0), memory_space=pltpu.SMEM),
                  pl.BlockSpec((tm, d), lambda i: (i, 0)),
                  pl.BlockSpec((SUBLANES, tm), lambda i: (0, i)),
                  pl.BlockSpec(memory_space=pl.ANY),
                  pl.BlockSpec((1, d), lambda i: (0, 0))],
        out_specs=pl.BlockSpec((tm, d), lambda i: (i, 0)),
        out_shape=jax.ShapeDtypeStruct((n, d), F32),
        scratch_shapes=[pltpu.VMEM((TOP_K * tm * n_slabs, LANES), F32), pltpu.SemaphoreType.DMA],
        compiler_params=_cparams("arbitrary"),
        name="combine",
    )(dest_tiles, h1, gates_t, ys, final_w)


def _block_diag_tiles(w):
    nb, bs, _ = w.shape
    rows = jnp.tile(w.reshape(nb * bs // MXU_DIM, MXU_DIM, bs), (1, 1, MXU_DIM // bs))
    r_blk = lax.broadcasted_iota(jnp.int32, (MXU_DIM, MXU_DIM), 0) // bs
    c_blk = lax.broadcasted_iota(jnp.int32, (MXU_DIM, MXU_DIM), 1) // bs
    return jnp.where(r_blk == c_blk, rows, 0.0).astype(BF16)


def _layer(xf, bsz, seq, norm_mix_w, w_in, mlstm_conv_w, mlstm_conv_b, w_q, w_k, w_v, w_igate, b_igate,
           w_fgate, b_fgate, mlstm_norm_w, mlstm_skip, conv_dw_w, conv_dw_b, conv_norm_w, conv_norm_b,
           w_out, norm_ffn_w, w_router, b_router, w1, b1, w2, b2, final_norm_w):
    n, d = xf.shape
    d_mlstm = mlstm_norm_w.shape[0]
    d_conv = conv_norm_w.shape[0]
    n_slabs = d // LANES
    r2 = lambda v: v.reshape(1, -1)

    xm, xc, sz, u = _in_proj(xf, r2(norm_mix_w), w_in.astype(BF16), mlstm_conv_w, r2(mlstm_conv_b),
                             d_mlstm, d_conv, seq, tm=256)

    wg = jnp.concatenate([w_igate, w_fgate], axis=1)
    wg = jnp.pad(wg, ((0, 0), (0, LANES - wg.shape[1]))).astype(BF16)
    bg = jnp.pad(jnp.concatenate([b_igate, b_fgate]), (0, LANES - 2 * N_HEADS)).reshape(1, LANES)
    ym = _mlstm(xm, xc, sz, _block_diag_tiles(w_q), _block_diag_tiles(w_k), _block_diag_tiles(w_v),
                wg, bg, r2(mlstm_norm_w), r2(mlstm_skip), bsz, seq)
    yc = _conv_group(u, conv_dw_w, r2(conv_dw_b), r2(conv_norm_w), r2(conv_norm_b), bsz, seq, tile=256)

    h1, a2s, idx_t, pos_t, gates_t, cnt = _out_proj(
        xf, ym, yc, w_out.astype(BF16), r2(norm_ffn_w), w_router.T, b_router.reshape(-1, 1), d_mlstm, tm=512)

    counts = cnt[:, 0].astype(jnp.int32)
    ends = jnp.cumsum(counts)
    starts = ends - counts
    e_ids = jnp.arange(N_EXPERTS, dtype=jnp.int32)
    idx4 = idx_t[:TOP_K]
    dest = pos_t[:TOP_K] + jnp.sum(
        jnp.where(idx4[None] == e_ids[:, None, None], starts[:, None, None], 0), axis=0)
    tm = 256
    dest_tiles = dest.reshape(TOP_K, n // tm, tm).transpose(1, 0, 2)

    n_rows = n * TOP_K
    rb = 512
    n_blocks = n_rows // rb
    n_items = n_blocks + N_EXPERTS - 1
    first_blk = starts // rb
    last_blk = jnp.where(counts > 0, (ends - 1) // rb, first_blk - 1)
    per_e = last_blk - first_blk + 1
    item_end = jnp.cumsum(per_e)
    item_start = item_end - per_e
    ids = jnp.arange(n_items, dtype=jnp.int32)
    total = item_end[-1]
    item_valid = (ids < total).astype(jnp.int32)
    item_exp = jnp.minimum(jnp.sum((ids[:, None] >= item_end[None, :]).astype(jnp.int32), axis=1), N_EXPERTS - 1)
    item_blk = jnp.where(item_valid == 1, first_blk[item_exp] + ids - item_start[item_exp], n_blocks - 1)
    last_valid_exp = item_exp[jnp.maximum(total - 1, 0)]
    item_exp = jnp.where(item_valid == 1, item_exp, last_valid_exp).astype(jnp.int32)
    item_blk = item_blk.astype(jnp.int32)
    item_lo = (jnp.maximum(starts[item_exp], item_blk * rb) - item_blk * rb).astype(jnp.int32)
    item_hi = (jnp.minimum(ends[item_exp], (item_blk + 1) * rb) - item_blk * rb).astype(jnp.int32)

    xs = _dispatch(a2s, dest_tiles, n_rows, n_slabs, tm=tm)
    ys = _experts(xs, w1, b1[:, None, :], w2, b2[:, None, :],
                  item_blk, item_exp, item_valid, item_lo, item_hi, rb, n_slabs)
    return _combine(dest_tiles, h1, gates_t, ys, r2(final_norm_w), n_slabs, tm=tm)


def kernel(x, norm_mix_w, w_in, mlstm_conv_w, mlstm_conv_b, w_q, w_k, w_v, w_igate, b_igate, w_fgate, b_fgate,
           mlstm_norm_w, mlstm_skip, conv_dw_w, conv_dw_b, conv_norm_w, conv_norm_b, w_out, norm_ffn_w,
           w_router, b_router, w1, b1, w2, b2, final_norm_w):
    bsz, seq, d = x.shape
    assert norm_mix_w.shape[0] == 1, "single-layer block"
    out = _layer(x.reshape(bsz * seq, d), bsz, seq, norm_mix_w[0], w_in[0], mlstm_conv_w[0], mlstm_conv_b[0],
                 w_q[0], w_k[0], w_v[0], w_igate[0], b_igate[0], w_fgate[0], b_fgate[0], mlstm_norm_w[0],
                 mlstm_skip[0], conv_dw_w[0], conv_dw_b[0], conv_norm_w[0], conv_norm_b[0], w_out[0],
                 norm_ffn_w[0], w_router[0], b_router[0], w1[0], b1[0], w2[0], b2[0], final_norm_w)
    return out.reshape(bsz, seq, d)
```

```python
import functools

import jax
import jax.numpy as jnp
from jax import lax
from jax.experimental import pallas as pl
from jax.experimental.pallas import tpu as pltpu
from jax.experimental.pallas import tpu_sc as plsc

F32 = jnp.float32
BF16 = jnp.bfloat16

EPS = 1e-5
N_HEADS = 4
QKV_BLOCK = 4
MLSTM_CONV_WIDTH = 4
CONV_WIDTH = 31
N_EXPERTS = 32
TOP_K = 4
SWIGLU_ALPHA = 1.702
SWIGLU_LIMIT = 7.0

LANES = 128
SUBLANES = 8
MXU_DIM = 256
VMEM_LIMIT = 52 * 1024 * 1024
EXPERTS_VMEM_LIMIT = 58 * 1024 * 1024

MLSTM_CHUNK = 256
CONV_HALO = 32
NEG_INF = float("-inf")


def _sigmoid(x):
    return 1.0 / (1.0 + jnp.exp(-x))


def _silu(x):
    return x * _sigmoid(x)


def _cparams(*sem):
    return pltpu.CompilerParams(dimension_semantics=sem, vmem_limit_bytes=VMEM_LIMIT)


def _inproj_kernel(x_ref, nw_ref, w_ref, cw_ref, cb_ref, xm_ref, xc_ref, sz_ref, u_ref, xbuf,
                   *, d_mlstm, d_conv, tiles_per_seq):
    tm = x_ref.shape[0]
    hist = SUBLANES

    @pl.when(pl.program_id(0) % tiles_per_seq == 0)
    def _():
        xbuf[0:hist, :] = jnp.zeros((hist, d_mlstm), F32)

    x = x_ref[...]
    a = x * lax.rsqrt(jnp.mean(x * x, axis=-1, keepdims=True) + EPS) * nw_ref[...]
    ab = a.astype(BF16)
    p1 = jnp.dot(ab, w_ref[:, : 2 * d_mlstm], preferred_element_type=F32)
    xm = p1[:, :d_mlstm]
    xm_ref[...] = xm.astype(xm_ref.dtype)
    sz_ref[...] = _silu(p1[:, d_mlstm:])

    xbuf[hist:hist + tm, :] = xm
    acc = cb_ref[...] + cw_ref[MLSTM_CONV_WIDTH - 1:MLSTM_CONV_WIDTH, :] * xm
    for k in range(MLSTM_CONV_WIDTH - 1):
        off = hist - (MLSTM_CONV_WIDTH - 1) + k
        acc = acc + cw_ref[k:k + 1, :] * xbuf[off:off + tm, :]
    xc_ref[...] = _silu(acc).astype(xc_ref.dtype)
    xbuf[0:hist, :] = xm[tm - hist:, :]

    p2 = jnp.dot(ab, w_ref[:, 2 * d_mlstm:], preferred_element_type=F32)
    u_ref[...] = p2[:, :d_conv] * _sigmoid(p2[:, d_conv:])


def _in_proj(xf, norm_w, w_in_b, conv_w, conv_b, d_mlstm, d_conv, seq, tm):
    n, d = xf.shape
    row = lambda i: (i, 0)
    const = lambda i: (0, 0)
    return pl.pallas_call(
        functools.partial(_inproj_kernel, d_mlstm=d_mlstm, d_conv=d_conv, tiles_per_seq=seq // tm),
        grid=(n // tm,),
        in_specs=[pl.BlockSpec((tm, d), row),
                  pl.BlockSpec((1, d), const),
                  pl.BlockSpec(w_in_b.shape, const),
                  pl.BlockSpec(conv_w.shape, const),
                  pl.BlockSpec(conv_b.shape, const)],
        out_specs=[pl.BlockSpec((tm, d_mlstm), row),
                   pl.BlockSpec((tm, d_mlstm), row),
                   pl.BlockSpec((tm, d_mlstm), row),
                   pl.BlockSpec((tm, d_conv), row)],
        out_shape=[jax.ShapeDtypeStruct((n, d_mlstm), BF16),
                   jax.ShapeDtypeStruct((n, d_mlstm), BF16),
                   jax.ShapeDtypeStruct((n, d_mlstm), F32),
                   jax.ShapeDtypeStruct((n, d_conv), F32)],
        scratch_shapes=[pltpu.VMEM((tm + SUBLANES, d_mlstm), F32)],
        compiler_params=_cparams("arbitrary"),
        name="in_proj",
    )(xf, norm_w, w_in_b, conv_w, conv_b)


def _split3(v):
    hi = v.astype(BF16)
    r1 = v - hi.astype(F32)
    mid = r1.astype(BF16)
    lo = (r1 - mid.astype(F32)).astype(BF16)
    return hi, mid, lo


def _mlstm_kernel(xm_ref, xc_ref, sz_ref, wq_ref, wk_ref, wv_ref, wg_ref, bg_ref, nw_ref, sk_ref,
                  ym_ref, c_sc, n_sc, m_sc, *, chunk, dh, n_seq):
    L = chunk
    nh = N_HEADS
    j = pl.program_id(1)

    @pl.when(j == 0)
    def _():
        c_sc[...] = jnp.zeros_like(c_sc)
        n_sc[...] = jnp.zeros_like(n_sc)
        m_sc[...] = jnp.zeros_like(m_sc)

    ri = lax.broadcasted_iota(jnp.int32, (L, L), 0)
    ci = lax.broadcasted_iota(jnp.int32, (L, L), 1)
    causal = ci <= ri
    tri = jnp.where(causal, 1.0, 0.0).astype(BF16)
    scale = dh ** -0.5

    for sq in range(n_seq):
        xmb = xm_ref[sq]
        xcb = xc_ref[sq]
        d = xmb.shape[1]
        nb = d // MXU_DIM

        def bd(xb, w_ref):
            return jnp.concatenate(
                [jnp.dot(xb[:, g * MXU_DIM:(g + 1) * MXU_DIM], w_ref[g], preferred_element_type=F32)
                 for g in range(nb)], axis=-1)

        q = bd(xcb, wq_ref)
        k_ = bd(xcb, wk_ref)
        v = bd(xmb, wv_ref)
        qb, kb, vb = q.astype(BF16), k_.astype(BF16), v.astype(BF16)

        g = (jnp.dot(qb, wg_ref[0:d, :], preferred_element_type=F32)
             + jnp.dot(kb, wg_ref[d:2 * d, :], preferred_element_type=F32)
             + jnp.dot(vb, wg_ref[2 * d:3 * d, :], preferred_element_type=F32)
             + bg_ref[...])
        col = lax.broadcasted_iota(jnp.int32, g.shape, 1)
        log_f = jnp.minimum(g, 0.0) - jnp.log(1.0 + jnp.exp(-jnp.abs(g)))
        gates = jnp.where(col < nh, g, jnp.where(col < 2 * nh, log_f, 0.0))
        cum = sum(jnp.dot(tri, part, preferred_element_type=F32) for part in _split3(gates))
        colform = jnp.where(col < nh, gates, cum)
        rowform = colform.T

        for h in range(nh):
            sl = slice(h * dh, (h + 1) * dh)
            li_row = rowform[h:h + 1, :]
            b_row = rowform[nh + h:nh + h + 1, :]
            li_col = colform[:, h:h + 1]
            b_col = colform[:, nh + h:nh + h + 1]
            m_prev = m_sc[sq, h:h + 1, 0:1]
            g_tot = b_row[:, L - 1:L]

            dmat = jnp.where(causal, b_col - b_row + li_row, NEG_INF)
            inter = b_col + m_prev
            m_i = jnp.maximum(inter, jnp.max(dmat, axis=-1, keepdims=True))
            w_intra = jnp.exp(dmat - m_i)
            w_inter = jnp.exp(inter - m_i)

            qh = qb[:, sl]
            s = lax.dot_general(qh, kb[:, sl], (((1,), (1,)), ((), ())),
                                preferred_element_type=F32) * (scale * w_intra)
            c_prev = c_sc[sq, h]
            n_prev = n_sc[sq, h:h + 1, :]
            num = (w_inter * scale) * jnp.dot(qh, c_prev.astype(BF16), preferred_element_type=F32) \
                + jnp.dot(s.astype(BF16), vb[:, sl], preferred_element_type=F32)
            qn = jnp.sum(q[:, sl] * n_prev, axis=-1, keepdims=True) * scale
            den = w_inter * qn + jnp.sum(s, axis=-1, keepdims=True)
            hh = num / jnp.maximum(jnp.abs(den), jnp.exp(-m_i))

            mu = jnp.mean(hh, axis=-1, keepdims=True)
            dev = hh - mu
            var = jnp.mean(dev * dev, axis=-1, keepdims=True)
            hn = dev * lax.rsqrt(var + EPS) * nw_ref[:, sl]
            ym_ref[sq, :, sl] = ((hn + sk_ref[:, sl] * xcb[:, sl].astype(F32)) * sz_ref[sq, :, sl]).astype(ym_ref.dtype)

            a_row = g_tot - b_row + li_row
            m_new = jnp.maximum(g_tot + m_prev, jnp.max(a_row, axis=-1, keepdims=True))
            a_col = g_tot - b_col + li_col
            w_state = jnp.exp(a_col - m_new)
            decay = jnp.exp(g_tot + m_prev - m_new)
            kw = k_[:, sl] * w_state
            c_sc[sq, h] = decay * c_prev + lax.dot_general(kw.astype(BF16), vb[:, sl], (((0,), (0,)), ((), ())),
                                                           preferred_element_type=F32)
            n_sc[sq, h:h + 1, :] = decay * n_prev + jnp.sum(kw, axis=0, keepdims=True)
            m_sc[sq, h:h + 1, :] = jnp.broadcast_to(m_new, (1, m_sc.shape[2]))


def _mlstm(xm, xc, sz, wq, wk, wv, wg, bg, norm_w, skip, bsz, seq):
    n, d = xm.shape
    L = MLSTM_CHUNK
    nc = seq // L
    dh = d // N_HEADS
    n_seq = 2 if bsz % 2 == 0 else 1
    blk = lambda b, j: (b, j, 0)
    c2 = lambda b, j: (0, 0)
    c3 = lambda b, j: (0, 0, 0)
    ym = pl.pallas_call(
        functools.partial(_mlstm_kernel, chunk=L, dh=dh, n_seq=n_seq),
        grid=(bsz // n_seq, nc),
        in_specs=[pl.BlockSpec((n_seq, L, d), blk), pl.BlockSpec((n_seq, L, d), blk), pl.BlockSpec((n_seq, L, d), blk),
                  pl.BlockSpec(wq.shape, c3), pl.BlockSpec(wk.shape, c3), pl.BlockSpec(wv.shape, c3),
                  pl.BlockSpec(wg.shape, c2), pl.BlockSpec(bg.shape, c2),
                  pl.BlockSpec(norm_w.shape, c2), pl.BlockSpec(skip.shape, c2)],
        out_specs=pl.BlockSpec((n_seq, L, d), blk),
        out_shape=jax.ShapeDtypeStruct((bsz, seq, d), BF16),
        scratch_shapes=[pltpu.VMEM((n_seq, N_HEADS, dh, dh), F32),
                        pltpu.VMEM((n_seq, SUBLANES, dh), F32),
                        pltpu.VMEM((n_seq, SUBLANES, LANES), F32)],
        compiler_params=_cparams("arbitrary", "arbitrary"),
        name="mlstm",
    )(xm.reshape(bsz, seq, d), xc.reshape(bsz, seq, d), sz.reshape(bsz, seq, d), wq, wk, wv, wg, bg, norm_w, skip)
    return ym.reshape(n, d)


def _conv_kernel(u_ref, w_ref, b_ref, nw_ref, nb_ref, yc_ref, ubuf, pbuf, cbuf, *, tile, rows):
    T = tile
    j = pl.program_id(1)

    @pl.when(j == 0)
    def _():
        ubuf[0:CONV_HALO, :] = jnp.zeros((CONV_HALO, ubuf.shape[1]), F32)

    ubuf[CONV_HALO:CONV_HALO + T, :] = u_ref[...]
    base = CONV_HALO - (CONV_WIDTH - 1)
    span = T + CONV_HALO - SUBLANES
    n_lane_blocks = ubuf.shape[1] // LANES

    def lane_block(c, carry):
        lanes = pl.ds(pl.multiple_of(c * LANES, LANES), LANES)
        for r in range(1, SUBLANES):
            pbuf[r - 1, :, :] = ubuf[r:r + span, lanes]
        for r0 in range(0, T, rows):
            acc = jnp.broadcast_to(b_ref[:, lanes], (rows, LANES))
            for k in range(CONV_WIDTH):
                q, r = divmod(base + k, SUBLANES)
                lo = r0 + q * SUBLANES
                src = ubuf[lo:lo + rows, lanes] if r == 0 else pbuf[r - 1, lo:lo + rows, :]
                acc = acc + w_ref[k:k + 1, lanes] * src
            cbuf[r0:r0 + rows, lanes] = acc
        return carry

    lax.fori_loop(0, n_lane_blocks, lane_block, 0)
    ubuf[0:CONV_HALO, :] = ubuf[T:T + CONV_HALO, :]

    y = cbuf[...]
    mu = jnp.mean(y, axis=-1, keepdims=True)
    dev = y - mu
    var = jnp.mean(dev * dev, axis=-1, keepdims=True)
    yn = dev * lax.rsqrt(var + EPS) * nw_ref[...] + nb_ref[...]
    yc_ref[...] = _silu(yn).astype(yc_ref.dtype)


def _conv_group(u, w, b, norm_w, norm_b, bsz, seq, tile):
    n, d = u.shape
    nt = seq // tile
    row = lambda bi, j: (bi * nt + j, 0)
    c2 = lambda bi, j: (0, 0)
    return pl.pallas_call(
        functools.partial(_conv_kernel, tile=tile, rows=64),
        grid=(bsz, nt),
        in_specs=[pl.BlockSpec((tile, d), row), pl.BlockSpec(w.shape, c2), pl.BlockSpec(b.shape, c2),
                  pl.BlockSpec(norm_w.shape, c2), pl.BlockSpec(norm_b.shape, c2)],
        out_specs=pl.BlockSpec((tile, d), row),
        out_shape=jax.ShapeDtypeStruct((n, d), BF16),
        scratch_shapes=[pltpu.VMEM((tile + CONV_HALO, d), F32),
                        pltpu.VMEM((SUBLANES - 1, tile + CONV_HALO - SUBLANES, LANES), F32),
                        pltpu.VMEM((tile, d), F32)],
        compiler_params=_cparams("arbitrary", "arbitrary"),
        name="conv_group",
    )(u, w, b, norm_w, norm_b)


def _outproj_kernel(x_ref, ym_ref, yc_ref, wo_ref, nw_ref, wr_ref, br_ref,
                    h1_ref, a2_ref, idx_ref, pos_ref, gate_ref, cnt_ref, cnt_sc, *, tm, d_mlstm):
    i = pl.program_id(0)

    @pl.when(i == 0)
    def _():
        cnt_sc[...] = jnp.zeros_like(cnt_sc)

    h1 = (x_ref[...]
          + jnp.dot(ym_ref[...], wo_ref[0:d_mlstm, :], preferred_element_type=F32)
          + jnp.dot(yc_ref[...], wo_ref[d_mlstm:, :], preferred_element_type=F32))
    h1_ref[...] = h1
    a2 = h1 * lax.rsqrt(jnp.mean(h1 * h1, axis=-1, keepdims=True) + EPS) * nw_ref[...]
    n_slabs = a2.shape[1] // LANES
    for s in range(n_slabs):
        a2_ref[pl.ds(s, tm, stride=n_slabs), :] = a2[:, s * LANES:(s + 1) * LANES]

    logits = lax.dot_general(wr_ref[...], a2, (((1,), (1,)), ((), ())),
                             precision=lax.Precision.HIGHEST, preferred_element_type=F32) + br_ref[...]
    e_iota = lax.broadcasted_iota(jnp.int32, logits.shape, 0)
    work = logits
    vals, idxs = [], []
    for _ in range(TOP_K):
        mx = jnp.max(work, axis=0, keepdims=True)
        sel = jnp.min(jnp.where(work == mx, e_iota, N_EXPERTS), axis=0, keepdims=True)
        vals.append(mx)
        idxs.append(sel)
        work = jnp.where(e_iota == sel, NEG_INF, work)
    exps = [jnp.exp(vv - vals[0]) for vv in vals]
    tot = exps[0] + exps[1] + exps[2] + exps[3]
    gates = [ev / tot for ev in exps]

    chosen = functools.reduce(jnp.logical_or, [e_iota == sel for sel in idxs])
    mh = jnp.where(chosen, 1.0, 0.0)
    ri = lax.broadcasted_iota(jnp.int32, (tm, tm), 0)
    ci = lax.broadcasted_iota(jnp.int32, (tm, tm), 1)
    upper = jnp.where(ri < ci, 1.0, 0.0).astype(BF16)
    rank = jnp.dot(mh.astype(BF16), upper, preferred_element_type=F32) + cnt_sc[:, 0:1]
    cnt_new = cnt_sc[...] + jnp.sum(mh, axis=1, keepdims=True)
    cnt_sc[...] = cnt_new
    cnt_ref[...] = cnt_new

    zero_i = jnp.zeros((SUBLANES - TOP_K, tm), jnp.int32)
    pos = [jnp.sum(jnp.where(e_iota == sel, rank, 0.0), axis=0, keepdims=True).astype(jnp.int32) for sel in idxs]
    idx_ref[...] = jnp.concatenate(idxs + [zero_i], axis=0)
    pos_ref[...] = jnp.concatenate(pos + [zero_i], axis=0)
    gate_ref[...] = jnp.concatenate(gates + [jnp.zeros((SUBLANES - TOP_K, tm), F32)], axis=0)


def _out_proj(xf, ym, yc, w_out_b, norm_w, w_router_t, b_router, d_mlstm, tm):
    n, d = xf.shape
    n_slabs = d // LANES
    row = lambda i: (i, 0)
    colb = lambda i: (0, i)
    const = lambda i: (0, 0)
    return pl.pallas_call(
        functools.partial(_outproj_kernel, tm=tm, d_mlstm=d_mlstm),
        grid=(n // tm,),
        in_specs=[pl.BlockSpec((tm, d), row), pl.BlockSpec((tm, ym.shape[1]), row), pl.BlockSpec((tm, yc.shape[1]), row),
                  pl.BlockSpec(w_out_b.shape, const), pl.BlockSpec(norm_w.shape, const),
                  pl.BlockSpec(w_router_t.shape, const), pl.BlockSpec(b_router.shape, const)],
        out_specs=[pl.BlockSpec((tm, d), row),
                   pl.BlockSpec((tm * n_slabs, LANES), row),
                   pl.BlockSpec((SUBLANES, tm), colb),
                   pl.BlockSpec((SUBLANES, tm), colb),
                   pl.BlockSpec((SUBLANES, tm), colb),
                   pl.BlockSpec((N_EXPERTS, LANES), const)],
        out_shape=[jax.ShapeDtypeStruct((n, d), F32),
                   jax.ShapeDtypeStruct((n * n_slabs, LANES), F32),
                   jax.ShapeDtypeStruct((SUBLANES, n), jnp.int32),
                   jax.ShapeDtypeStruct((SUBLANES, n), jnp.int32),
                   jax.ShapeDtypeStruct((SUBLANES, n), F32),
                   jax.ShapeDtypeStruct((N_EXPERTS, LANES), F32)],
        scratch_shapes=[pltpu.VMEM((N_EXPERTS, LANES), F32)],
        compiler_params=_cparams("arbitrary"),
        name="out_proj_router",
    )(xf, ym, yc, w_out_b, norm_w, w_router_t, b_router)


def _row_copy(src, src_row, dst, dst_row, sem, n_slabs):
    return pltpu.make_async_copy(
        src.at[pl.ds(pl.multiple_of(src_row * n_slabs, n_slabs), n_slabs), :],
        dst.at[pl.ds(pl.multiple_of(dst_row * n_slabs, n_slabs), n_slabs), :],
        sem)


def _dispatch_kernel(dest_ref, a2_ref, xs_ref, sem, *, tm, n_slabs):
    def issue(t, carry):
        for k in range(TOP_K):
            _row_copy(a2_ref, t, xs_ref, dest_ref[0, k, t], sem, n_slabs).start()
        return carry

    lax.fori_loop(0, tm, issue, 0)

    def drain(t, carry):
        for k in range(TOP_K):
            _row_copy(a2_ref, t, xs_ref, dest_ref[0, k, t], sem, n_slabs).wait()
        return carry

    lax.fori_loop(0, tm, drain, 0)


def _dispatch(a2s, dest_tiles, n_rows, n_slabs, tm):
    n_tok = a2s.shape[0] // n_slabs
    return pl.pallas_call(
        functools.partial(_dispatch_kernel, tm=tm, n_slabs=n_slabs),
        grid=(n_tok // tm,),
        in_specs=[pl.BlockSpec((1, TOP_K, tm), lambda i: (i, 0, 0), memory_space=pltpu.SMEM),
                  pl.BlockSpec((tm * n_slabs, LANES), lambda i: (i, 0))],
        out_specs=pl.BlockSpec(memory_space=pl.ANY),
        out_shape=jax.ShapeDtypeStruct((n_rows * n_slabs, LANES), F32),
        scratch_shapes=[pltpu.SemaphoreType.DMA],
        compiler_params=_cparams("arbitrary"),
        name="dispatch",
    )(dest_tiles, a2s)


def _experts_kernel(blk_ref, exp_ref, valid_ref, lo_ref, hi_ref,
                    xs_ref, w1_ref, b1_ref, w2_ref, b2_ref, ys_ref, w1c_ref, w2c_ref, acc_ref, *, rb, d_ff, n_sub):
    i = pl.program_id(0)
    n_slabs = xs_ref.shape[0] // rb
    prev = jnp.maximum(i - 1, 0)
    first_visit = jnp.logical_or(i == 0, blk_ref[prev] != blk_ref[i])
    new_expert = jnp.logical_or(i == 0, exp_ref[prev] != exp_ref[i])

    @pl.when(new_expert)
    def _():
        w1c_ref[...] = w1_ref[0].astype(BF16)
        w2c_ref[...] = w2_ref[0].astype(BF16)

    @pl.when(first_visit)
    def _():
        acc_ref[...] = jnp.zeros_like(acc_ref)

    @pl.when(valid_ref[i] == 1)
    def _():
        rs = rb // n_sub
        for c in range(n_sub):
            x = jnp.concatenate([xs_ref[pl.ds(c * rs * n_slabs + s, rs, stride=n_slabs), :] for s in range(n_slabs)],
                                axis=-1)
            hid = jnp.dot(x.astype(BF16), w1c_ref[...], preferred_element_type=F32) + b1_ref[0]
            x_glu = jnp.minimum(hid[:, :d_ff], SWIGLU_LIMIT)
            x_lin = jnp.clip(hid[:, d_ff:], -SWIGLU_LIMIT, SWIGLU_LIMIT)
            act = x_glu * _sigmoid(SWIGLU_ALPHA * x_glu) * (x_lin + 1.0)
            y = jnp.dot(act.astype(BF16), w2c_ref[...], preferred_element_type=F32) + b2_ref[0]
            r = c * rs + lax.broadcasted_iota(jnp.int32, (rs, 1), 0)
            y = jnp.where(jnp.logical_and(r >= lo_ref[i], r < hi_ref[i]), y, 0.0)
            y = acc_ref[c * rs:(c + 1) * rs, :] + y
            acc_ref[c * rs:(c + 1) * rs, :] = y
            for s in range(n_slabs):
                ys_ref[pl.ds(c * rs * n_slabs + s, rs, stride=n_slabs), :] = y[:, s * LANES:(s + 1) * LANES]


def _experts(xs, w1, b1, w2, b2, item_blk, item_exp, item_valid, item_lo, item_hi, rb, n_slabs):
    n_items = item_blk.shape[0]
    d = n_slabs * LANES
    d_ff = w2.shape[1]
    by_blk = lambda i, blk, ex, va, lo, hi: (blk[i], 0)
    by_exp = lambda i, blk, ex, va, lo, hi: (ex[i], 0, 0)
    grid_spec = pltpu.PrefetchScalarGridSpec(
        num_scalar_prefetch=5,
        grid=(n_items,),
        in_specs=[pl.BlockSpec((rb * n_slabs, LANES), by_blk),
                  pl.BlockSpec((1,) + w1.shape[1:], by_exp),
                  pl.BlockSpec((1,) + b1.shape[1:], by_exp),
                  pl.BlockSpec((1,) + w2.shape[1:], by_exp),
                  pl.BlockSpec((1,) + b2.shape[1:], by_exp)],
        out_specs=pl.BlockSpec((rb * n_slabs, LANES), by_blk),
        scratch_shapes=[pltpu.VMEM(w1.shape[1:], BF16), pltpu.VMEM(w2.shape[1:], BF16), pltpu.VMEM((rb, d), F32)],
    )
    return pl.pallas_call(
        functools.partial(_experts_kernel, rb=rb, d_ff=d_ff, n_sub=2),
        grid_spec=grid_spec,
        out_shape=jax.ShapeDtypeStruct(xs.shape, F32),
        compiler_params=pltpu.CompilerParams(dimension_semantics=("arbitrary",), vmem_limit_bytes=EXPERTS_VMEM_LIMIT),
        name="experts",
    )(item_blk, item_exp, item_valid, item_lo, item_hi, xs, w1, b1, w2, b2)


def _combine_kernel(dest_ref, h1_ref, gate_ref, ys_ref, nw_ref, out_ref, ybuf, sem, *, tm, n_slabs):
    def issue(t, carry):
        for k in range(TOP_K):
            _row_copy(ys_ref, dest_ref[0, k, t], ybuf, k * tm + t, sem, n_slabs).start()
        return carry

    lax.fori_loop(0, tm, issue, 0)

    gpad = jnp.concatenate([gate_ref[...], jnp.zeros((LANES - SUBLANES, tm), F32)], axis=0)
    gcol = gpad.T

    def drain(t, carry):
        for k in range(TOP_K):
            _row_copy(ys_ref, dest_ref[0, k, t], ybuf, k * tm + t, sem, n_slabs).wait()
        return carry

    lax.fori_loop(0, tm, drain, 0)

    h2 = h1_ref[...]
    for k in range(TOP_K):
        yk = jnp.concatenate([ybuf[pl.ds(k * tm * n_slabs + s, tm, stride=n_slabs), :] for s in range(n_slabs)],
                             axis=-1)
        h2 = h2 + gcol[:, k:k + 1] * yk
    out_ref[...] = h2 * lax.rsqrt(jnp.mean(h2 * h2, axis=-1, keepdims=True) + EPS) * nw_ref[...]


def _combine(dest_tiles, h1, gates_t, ys, final_w, n_slabs, tm):
    n, d = h1.shape
    return pl.pallas_call(
        functools.partial(_combine_kernel, tm=tm, n_slabs=n_slabs),
        grid=(n // tm,),
        in_specs=[pl.BlockSpec((1, TOP_K, tm), lambda i: (i, 0, 0), memory_space=pltpu.SMEM),
                  pl.BlockSpec((tm, d), lambda i: (i, 0)),
                  pl.BlockSpec((SUBLANES, tm), lambda i: (0, i)),
                  pl.BlockSpec(memory_space=pl.ANY),
                  pl.BlockSpec((1, d), lambda i: (0, 0))],
        out_specs=pl.BlockSpec((tm, d), lambda i: (i, 0)),
        out_shape=jax.ShapeDtypeStruct((n, d), F32),
        scratch_shapes=[pltpu.VMEM((TOP_K * tm * n_slabs, LANES), F32), pltpu.SemaphoreType.DMA],
        compiler_params=_cparams("arbitrary"),
        name="combine",
    )(dest_tiles, h1, gates_t, ys, final_w)


SC_CORES = 2
SC_SUBCORES = 16
SC_GATHER_WINDOW = 32


def _sc_gather_rows(table, idx):
    m = idx.shape[0]
    n_workers = SC_CORES * SC_SUBCORES
    window = SC_GATHER_WINDOW
    per_worker = m // n_workers
    n_win = per_worker // window
    assert per_worker * n_workers == m and n_win * window == per_worker
    mesh = plsc.VectorSubcoreMesh(core_axis_name="c", subcore_axis_name="s")
    slab = table.shape[1:]

    @functools.partial(
        pl.kernel, mesh=mesh,
        out_type=jax.ShapeDtypeStruct((m,) + slab, table.dtype),
        scratch_types=[pltpu.VMEM((window,), jnp.int32),
                       pltpu.VMEM((window,) + slab, table.dtype),
                       pltpu.SemaphoreType.DMA],
        name="sc_gather_rows",
    )
    def gather(table_hbm, idx_hbm, out_hbm, idx_v, rows_v, sem):
        wid = lax.axis_index("s") * SC_CORES + lax.axis_index("c")
        base = wid * per_worker

        @pl.loop(0, n_win)
        def _(w):
            off = pl.multiple_of(base + w * window, window)
            pltpu.sync_copy(idx_hbm.at[pl.ds(off, window)], idx_v)
            pltpu.async_copy(table_hbm.at[idx_v], rows_v, sem).wait()
            pltpu.sync_copy(rows_v, out_hbm.at[pl.ds(off, window)])

    return gather(table, idx)


def _final_kernel(h1_ref, gate_ref, yg_ref, nw_ref, out_ref, *, tm, n_slabs):
    gpad = jnp.concatenate([gate_ref[...], jnp.zeros((LANES - SUBLANES, tm), F32)], axis=0)
    gcol = gpad.T
    h2 = h1_ref[...]
    for k in range(TOP_K):
        yk = jnp.concatenate([yg_ref[k, pl.ds(s, tm, stride=n_slabs), :] for s in range(n_slabs)], axis=-1)
        h2 = h2 + gcol[:, k:k + 1] * yk
    out_ref[...] = h2 * lax.rsqrt(jnp.mean(h2 * h2, axis=-1, keepdims=True) + EPS) * nw_ref[...]


def _final(h1, gates_t, yg, final_w, n_slabs, tm):
    n, d = h1.shape
    return pl.pallas_call(
        functools.partial(_final_kernel, tm=tm, n_slabs=n_slabs),
        grid=(n // tm,),
        in_specs=[pl.BlockSpec((tm, d), lambda i: (i, 0)),
                  pl.BlockSpec((SUBLANES, tm), lambda i: (0, i)),
                  pl.BlockSpec((TOP_K, tm * n_slabs, LANES), lambda i: (0, i, 0)),
                  pl.BlockSpec((1, d), lambda i: (0, 0))],
        out_specs=pl.BlockSpec((tm, d), lambda i: (i, 0)),
        out_shape=jax.ShapeDtypeStruct((n, d), F32),
        compiler_params=_cparams("parallel"),
        name="final",
    )(h1, gates_t, yg, final_w)


def _block_diag_tiles(w):
    nb, bs, _ = w.shape
    rows = jnp.tile(w.reshape(nb * bs // MXU_DIM, MXU_DIM, bs), (1, 1, MXU_DIM // bs))
    r_blk = lax.broadcasted_iota(jnp.int32, (MXU_DIM, MXU_DIM), 0) // bs
    c_blk = lax.broadcasted_iota(jnp.int32, (MXU_DIM, MXU_DIM), 1) // bs
    return jnp.where(r_blk == c_blk, rows, 0.0).astype(BF16)


def _layer(xf, bsz, seq, norm_mix_w, w_in, mlstm_conv_w, mlstm_conv_b, w_q, w_k, w_v, w_igate, b_igate,
           w_fgate, b_fgate, mlstm_norm_w, mlstm_skip, conv_dw_w, conv_dw_b, conv_norm_w, conv_norm_b,
           w_out, norm_ffn_w, w_router, b_router, w1, b1, w2, b2, final_norm_w):
    n, d = xf.shape
    d_mlstm = mlstm_norm_w.shape[0]
    d_conv = conv_norm_w.shape[0]
    n_slabs = d // LANES
    r2 = lambda v: v.reshape(1, -1)

    xm, xc, sz, u = _in_proj(xf, r2(norm_mix_w), w_in.astype(BF16), mlstm_conv_w, r2(mlstm_conv_b),
                             d_mlstm, d_conv, seq, tm=256)

    wg = jnp.concatenate([w_igate, w_fgate], axis=1)
    wg = jnp.pad(wg, ((0, 0), (0, LANES - wg.shape[1]))).astype(BF16)
    bg = jnp.pad(jnp.concatenate([b_igate, b_fgate]), (0, LANES - 2 * N_HEADS)).reshape(1, LANES)
    ym = _mlstm(xm, xc, sz, _block_diag_tiles(w_q), _block_diag_tiles(w_k), _block_diag_tiles(w_v),
                wg, bg, r2(mlstm_norm_w), r2(mlstm_skip), bsz, seq)
    yc = _conv_group(u, conv_dw_w, r2(conv_dw_b), r2(conv_norm_w), r2(conv_norm_b), bsz, seq, tile=256)

    h1, a2s, idx_t, pos_t, gates_t, cnt = _out_proj(
        xf, ym, yc, w_out.astype(BF16), r2(norm_ffn_w), w_router.T, b_router.reshape(-1, 1), d_mlstm, tm=512)

    counts = cnt[:, 0].astype(jnp.int32)
    ends = jnp.cumsum(counts)
    starts = ends - counts
    e_ids = jnp.arange(N_EXPERTS, dtype=jnp.int32)
    idx4 = idx_t[:TOP_K]
    dest = pos_t[:TOP_K] + jnp.sum(
        jnp.where(idx4[None] == e_ids[:, None, None], starts[:, None, None], 0), axis=0)
    tm = 256
    dest_tiles = dest.reshape(TOP_K, n // tm, tm).transpose(1, 0, 2)

    n_rows = n * TOP_K
    rb = 512
    n_blocks = n_rows // rb
    n_items = n_blocks + N_EXPERTS - 1
    first_blk = starts // rb
    last_blk = jnp.where(counts > 0, (ends - 1) // rb, first_blk - 1)
    per_e = last_blk - first_blk + 1
    item_end = jnp.cumsum(per_e)
    item_start = item_end - per_e
    ids = jnp.arange(n_items, dtype=jnp.int32)
    total = item_end[-1]
    item_valid = (ids < total).astype(jnp.int32)
    item_exp = jnp.minimum(jnp.sum((ids[:, None] >= item_end[None, :]).astype(jnp.int32), axis=1), N_EXPERTS - 1)
    item_blk = jnp.where(item_valid == 1, first_blk[item_exp] + ids - item_start[item_exp], n_blocks - 1)
    last_valid_exp = item_exp[jnp.maximum(total - 1, 0)]
    item_exp = jnp.where(item_valid == 1, item_exp, last_valid_exp).astype(jnp.int32)
    item_blk = item_blk.astype(jnp.int32)
    item_lo = (jnp.maximum(starts[item_exp], item_blk * rb) - item_blk * rb).astype(jnp.int32)
    item_hi = (jnp.minimum(ends[item_exp], (item_blk + 1) * rb) - item_blk * rb).astype(jnp.int32)

    xs = _dispatch(a2s, dest_tiles, n_rows, n_slabs, tm=tm)
    ys = _experts(xs, w1, b1[:, None, :], w2, b2[:, None, :],
                  item_blk, item_exp, item_valid, item_lo, item_hi, rb, n_slabs)
    yg = _sc_gather_rows(ys.reshape(n_rows, n_slabs, LANES), dest.reshape(-1))
    return _final(h1, gates_t, yg.reshape(TOP_K, n * n_slabs, LANES), r2(final_norm_w), n_slabs, tm=256)


def kernel(x, norm_mix_w, w_in, mlstm_conv_w, mlstm_conv_b, w_q, w_k, w_v, w_igate, b_igate, w_fgate, b_fgate,
           mlstm_norm_w, mlstm_skip, conv_dw_w, conv_dw_b, conv_norm_w, conv_norm_b, w_out, norm_ffn_w,
           w_router, b_router, w1, b1, w2, b2, final_norm_w):
    bsz, seq, d = x.shape
    assert norm_mix_w.shape[0] == 1, "single-layer block"
    out = _layer(x.reshape(bsz * seq, d), bsz, seq, norm_mix_w[0], w_in[0], mlstm_conv_w[0], mlstm_conv_b[0],
                 w_q[0], w_k[0], w_v[0], w_igate[0], b_igate[0], w_fgate[0], b_fgate[0], mlstm_norm_w[0],
                 mlstm_skip[0], conv_dw_w[0], conv_dw_b[0], conv_norm_w[0], conv_norm_b[0], w_out[0],
                 norm_ffn_w[0], w_router[0], b_router[0], w1[0], b1[0], w2[0], b2[0], final_norm_w)
    return out.reshape(bsz, seq, d)
```

```python
import functools

import jax
import jax.numpy as jnp
from jax import lax
from jax.experimental import pallas as pl
from jax.experimental.pallas import tpu as pltpu
from jax.experimental.pallas import tpu_sc as plsc

F32 = jnp.float32
BF16 = jnp.bfloat16

EPS = 1e-5
N_HEADS = 4
QKV_BLOCK = 4
MLSTM_CONV_WIDTH = 4
CONV_WIDTH = 31
N_EXPERTS = 32
TOP_K = 4
SWIGLU_ALPHA = 1.702
SWIGLU_LIMIT = 7.0

LANES = 128
SUBLANES = 8
MXU_DIM = 256
VMEM_LIMIT = 52 * 1024 * 1024
EXPERTS_VMEM_LIMIT = 58 * 1024 * 1024

MLSTM_CHUNK = 256
CONV_HALO = 32
NEG_INF = float("-inf")


def _sigmoid(x):
    return 1.0 / (1.0 + jnp.exp(-x))


def _silu(x):
    return x * _sigmoid(x)


def _cparams(*sem):
    return pltpu.CompilerParams(dimension_semantics=sem, vmem_limit_bytes=VMEM_LIMIT)


def _inproj_kernel(x_ref, nw_ref, w_ref, cw_ref, cb_ref, xm_ref, xc_ref, sz_ref, u_ref, xbuf,
                   *, d_mlstm, d_conv, tiles_per_seq):
    tm = x_ref.shape[0]
    hist = SUBLANES

    @pl.when(pl.program_id(0) % tiles_per_seq == 0)
    def _():
        xbuf[0:hist, :] = jnp.zeros((hist, d_mlstm), F32)

    x = x_ref[...]
    a = x * lax.rsqrt(jnp.mean(x * x, axis=-1, keepdims=True) + EPS) * nw_ref[...]
    ab = a.astype(BF16)
    p1 = jnp.dot(ab, w_ref[:, : 2 * d_mlstm], preferred_element_type=F32)
    xm = p1[:, :d_mlstm]
    xm_ref[...] = xm.astype(xm_ref.dtype)
    sz_ref[...] = _silu(p1[:, d_mlstm:])

    xbuf[hist:hist + tm, :] = xm
    acc = cb_ref[...] + cw_ref[MLSTM_CONV_WIDTH - 1:MLSTM_CONV_WIDTH, :] * xm
    for k in range(MLSTM_CONV_WIDTH - 1):
        off = hist - (MLSTM_CONV_WIDTH - 1) + k
        acc = acc + cw_ref[k:k + 1, :] * xbuf[off:off + tm, :]
    xc_ref[...] = _silu(acc).astype(xc_ref.dtype)
    xbuf[0:hist, :] = xm[tm - hist:, :]

    p2 = jnp.dot(ab, w_ref[:, 2 * d_mlstm:], preferred_element_type=F32)
    u_ref[...] = p2[:, :d_conv] * _sigmoid(p2[:, d_conv:])


def _in_proj(xf, norm_w, w_in_b, conv_w, conv_b, d_mlstm, d_conv, seq, tm):
    n, d = xf.shape
    row = lambda i: (i, 0)
    const = lambda i: (0, 0)
    return pl.pallas_call(
        functools.partial(_inproj_kernel, d_mlstm=d_mlstm, d_conv=d_conv, tiles_per_seq=seq // tm),
        grid=(n // tm,),
        in_specs=[pl.BlockSpec((tm, d), row),
                  pl.BlockSpec((1, d), const),
                  pl.BlockSpec(w_in_b.shape, const),
                  pl.BlockSpec(conv_w.shape, const),
                  pl.BlockSpec(conv_b.shape, const)],
        out_specs=[pl.BlockSpec((tm, d_mlstm), row),
                   pl.BlockSpec((tm, d_mlstm), row),
                   pl.BlockSpec((tm, d_mlstm), row),
                   pl.BlockSpec((tm, d_conv), row)],
        out_shape=[jax.ShapeDtypeStruct((n, d_mlstm), BF16),
                   jax.ShapeDtypeStruct((n, d_mlstm), BF16),
                   jax.ShapeDtypeStruct((n, d_mlstm), F32),
                   jax.ShapeDtypeStruct((n, d_conv), F32)],
        scratch_shapes=[pltpu.VMEM((tm + SUBLANES, d_mlstm), F32)],
        compiler_params=_cparams("arbitrary"),
        name="in_proj",
    )(xf, norm_w, w_in_b, conv_w, conv_b)


def _split3(v):
    hi = v.astype(BF16)
    r1 = v - hi.astype(F32)
    mid = r1.astype(BF16)
    lo = (r1 - mid.astype(F32)).astype(BF16)
    return hi, mid, lo


def _mlstm_kernel(xm_ref, xc_ref, sz_ref, wq_ref, wk_ref, wv_ref, wg_ref, bg_ref, nw_ref, sk_ref,
                  ym_ref, c_sc, n_sc, m_sc, *, chunk, dh, n_seq):
    L = chunk
    nh = N_HEADS
    j = pl.program_id(1)

    @pl.when(j == 0)
    def _():
        c_sc[...] = jnp.zeros_like(c_sc)
        n_sc[...] = jnp.zeros_like(n_sc)
        m_sc[...] = jnp.zeros_like(m_sc)

    ri = lax.broadcasted_iota(jnp.int32, (L, L), 0)
    ci = lax.broadcasted_iota(jnp.int32, (L, L), 1)
    causal = ci <= ri
    tri = jnp.where(causal, 1.0, 0.0).astype(BF16)
    scale = dh ** -0.5

    for sq in range(n_seq):
        xmb = xm_ref[sq]
        xcb = xc_ref[sq]
        d = xmb.shape[1]
        nb = d // MXU_DIM

        def bd(xb, w_ref):
            return jnp.concatenate(
                [jnp.dot(xb[:, g * MXU_DIM:(g + 1) * MXU_DIM], w_ref[g], preferred_element_type=F32)
                 for g in range(nb)], axis=-1)

        q = bd(xcb, wq_ref)
        k_ = bd(xcb, wk_ref)
        v = bd(xmb, wv_ref)
        qb, kb, vb = q.astype(BF16), k_.astype(BF16), v.astype(BF16)

        g = (jnp.dot(qb, wg_ref[0:d, :], preferred_element_type=F32)
             + jnp.dot(kb, wg_ref[d:2 * d, :], preferred_element_type=F32)
             + jnp.dot(vb, wg_ref[2 * d:3 * d, :], preferred_element_type=F32)
             + bg_ref[...])
        col = lax.broadcasted_iota(jnp.int32, g.shape, 1)
        log_f = jnp.minimum(g, 0.0) - jnp.log(1.0 + jnp.exp(-jnp.abs(g)))
        gates = jnp.where(col < nh, g, jnp.where(col < 2 * nh, log_f, 0.0))
        cum = sum(jnp.dot(tri, part, preferred_element_type=F32) for part in _split3(gates))
        colform = jnp.where(col < nh, gates, cum)
        rowform = colform.T

        for h in range(nh):
            sl = slice(h * dh, (h + 1) * dh)
            li_row = rowform[h:h + 1, :]
            b_row = rowform[nh + h:nh + h + 1, :]
            li_col = colform[:, h:h + 1]
            b_col = colform[:, nh + h:nh + h + 1]
            m_prev = m_sc[sq, h:h + 1, 0:1]
            g_tot = b_row[:, L - 1:L]

            dmat = jnp.where(causal, b_col - b_row + li_row, NEG_INF)
            inter = b_col + m_prev
            m_i = jnp.maximum(inter, jnp.max(dmat, axis=-1, keepdims=True))
            w_intra = jnp.exp(dmat - m_i)
            w_inter = jnp.exp(inter - m_i)

            qh = qb[:, sl]
            s = lax.dot_general(qh, kb[:, sl], (((1,), (1,)), ((), ())),
                                preferred_element_type=F32) * (scale * w_intra)
            c_prev = c_sc[sq, h]
            n_prev = n_sc[sq, h:h + 1, :]
            num = (w_inter * scale) * jnp.dot(qh, c_prev.astype(BF16), preferred_element_type=F32) \
                + jnp.dot(s.astype(BF16), vb[:, sl], preferred_element_type=F32)
            qn = jnp.sum(q[:, sl] * n_prev, axis=-1, keepdims=True) * scale
            den = w_inter * qn + jnp.sum(s, axis=-1, keepdims=True)
            hh = num / jnp.maximum(jnp.abs(den), jnp.exp(-m_i))

            mu = jnp.mean(hh, axis=-1, keepdims=True)
            dev = hh - mu
            var = jnp.mean(dev * dev, axis=-1, keepdims=True)
            hn = dev * lax.rsqrt(var + EPS) * nw_ref[:, sl]
            ym_ref[sq, :, sl] = ((hn + sk_ref[:, sl] * xcb[:, sl].astype(F32)) * sz_ref[sq, :, sl]).astype(ym_ref.dtype)

            a_row = g_tot - b_row + li_row
            m_new = jnp.maximum(g_tot + m_prev, jnp.max(a_row, axis=-1, keepdims=True))
            a_col = g_tot - b_col + li_col
            w_state = jnp.exp(a_col - m_new)
            decay = jnp.exp(g_tot + m_prev - m_new)
            kw = k_[:, sl] * w_state
            c_sc[sq, h] = decay * c_prev + lax.dot_general(kw.astype(BF16), vb[:, sl], (((0,), (0,)), ((), ())),
                                                           preferred_element_type=F32)
            n_sc[sq, h:h + 1, :] = decay * n_prev + jnp.sum(kw, axis=0, keepdims=True)
            m_sc[sq, h:h + 1, :] = jnp.broadcast_to(m_new, (1, m_sc.shape[2]))


def _mlstm(xm, xc, sz, wq, wk, wv, wg, bg, norm_w, skip, bsz, seq):
    n, d = xm.shape
    L = MLSTM_CHUNK
    nc = seq // L
    dh = d // N_HEADS
    n_seq = 2 if bsz % 2 == 0 else 1
    blk = lambda b, j: (b, j, 0)
    c2 = lambda b, j: (0, 0)
    c3 = lambda b, j: (0, 0, 0)
    ym = pl.pallas_call(
        functools.partial(_mlstm_kernel, chunk=L, dh=dh, n_seq=n_seq),
        grid=(bsz // n_seq, nc),
        in_specs=[pl.BlockSpec((n_seq, L, d), blk), pl.BlockSpec((n_seq, L, d), blk), pl.BlockSpec((n_seq, L, d), blk),
                  pl.BlockSpec(wq.shape, c3), pl.BlockSpec(wk.shape, c3), pl.BlockSpec(wv.shape, c3),
                  pl.BlockSpec(wg.shape, c2), pl.BlockSpec(bg.shape, c2),
                  pl.BlockSpec(norm_w.shape, c2), pl.BlockSpec(skip.shape, c2)],
        out_specs=pl.BlockSpec((n_seq, L, d), blk),
        out_shape=jax.ShapeDtypeStruct((bsz, seq, d), BF16),
        scratch_shapes=[pltpu.VMEM((n_seq, N_HEADS, dh, dh), F32),
                        pltpu.VMEM((n_seq, SUBLANES, dh), F32),
                        pltpu.VMEM((n_seq, SUBLANES, LANES), F32)],
        compiler_params=_cparams("arbitrary", "arbitrary"),
        name="mlstm",
    )(xm.reshape(bsz, seq, d), xc.reshape(bsz, seq, d), sz.reshape(bsz, seq, d), wq, wk, wv, wg, bg, norm_w, skip)
    return ym.reshape(n, d)


def _conv_kernel(u_ref, w_ref, b_ref, nw_ref, nb_ref, yc_ref, ubuf, pbuf, cbuf, *, tile, rows):
    T = tile
    j = pl.program_id(1)

    @pl.when(j == 0)
    def _():
        ubuf[0:CONV_HALO, :] = jnp.zeros((CONV_HALO, ubuf.shape[1]), F32)

    ubuf[CONV_HALO:CONV_HALO + T, :] = u_ref[...]
    base = CONV_HALO - (CONV_WIDTH - 1)
    span = T + CONV_HALO - SUBLANES
    n_lane_blocks = ubuf.shape[1] // LANES

    def lane_block(c, carry):
        lanes = pl.ds(pl.multiple_of(c * LANES, LANES), LANES)
        for r in range(1, SUBLANES):
            pbuf[r - 1, :, :] = ubuf[r:r + span, lanes]
        for r0 in range(0, T, rows):
            acc = jnp.broadcast_to(b_ref[:, lanes], (rows, LANES))
            for k in range(CONV_WIDTH):
                q, r = divmod(base + k, SUBLANES)
                lo = r0 + q * SUBLANES
                src = ubuf[lo:lo + rows, lanes] if r == 0 else pbuf[r - 1, lo:lo + rows, :]
                acc = acc + w_ref[k:k + 1, lanes] * src
            cbuf[r0:r0 + rows, lanes] = acc
        return carry

    lax.fori_loop(0, n_lane_blocks, lane_block, 0)
    ubuf[0:CONV_HALO, :] = ubuf[T:T + CONV_HALO, :]

    y = cbuf[...]
    mu = jnp.mean(y, axis=-1, keepdims=True)
    dev = y - mu
    var = jnp.mean(dev * dev, axis=-1, keepdims=True)
    yn = dev * lax.rsqrt(var + EPS) * nw_ref[...] + nb_ref[...]
    yc_ref[...] = _silu(yn).astype(yc_ref.dtype)


def _conv_group(u, w, b, norm_w, norm_b, bsz, seq, tile):
    n, d = u.shape
    nt = seq // tile
    row = lambda bi, j: (bi * nt + j, 0)
    c2 = lambda bi, j: (0, 0)
    return pl.pallas_call(
        functools.partial(_conv_kernel, tile=tile, rows=64),
        grid=(bsz, nt),
        in_specs=[pl.BlockSpec((tile, d), row), pl.BlockSpec(w.shape, c2), pl.BlockSpec(b.shape, c2),
                  pl.BlockSpec(norm_w.shape, c2), pl.BlockSpec(norm_b.shape, c2)],
        out_specs=pl.BlockSpec((tile, d), row),
        out_shape=jax.ShapeDtypeStruct((n, d), BF16),
        scratch_shapes=[pltpu.VMEM((tile + CONV_HALO, d), F32),
                        pltpu.VMEM((SUBLANES - 1, tile + CONV_HALO - SUBLANES, LANES), F32),
                        pltpu.VMEM((tile, d), F32)],
        compiler_params=_cparams("arbitrary", "arbitrary"),
        name="conv_group",
    )(u, w, b, norm_w, norm_b)


def _outproj_kernel(x_ref, ym_ref, yc_ref, wo_ref, nw_ref, wr_ref, br_ref,
                    h1_ref, a2_ref, idx_ref, pos_ref, gate_ref, cnt_ref, cnt_sc, *, tm, d_mlstm):
    i = pl.program_id(0)

    @pl.when(i == 0)
    def _():
        cnt_sc[...] = jnp.zeros_like(cnt_sc)

    h1 = (x_ref[...]
          + jnp.dot(ym_ref[...], wo_ref[0:d_mlstm, :], preferred_element_type=F32)
          + jnp.dot(yc_ref[...], wo_ref[d_mlstm:, :], preferred_element_type=F32))
    h1_ref[...] = h1
    a2 = h1 * lax.rsqrt(jnp.mean(h1 * h1, axis=-1, keepdims=True) + EPS) * nw_ref[...]
    n_slabs = a2.shape[1] // LANES
    for s in range(n_slabs):
        a2_ref[pl.ds(s, tm, stride=n_slabs), :] = a2[:, s * LANES:(s + 1) * LANES]

    logits = lax.dot_general(wr_ref[...], a2, (((1,), (1,)), ((), ())),
                             precision=lax.Precision.HIGHEST, preferred_element_type=F32) + br_ref[...]
    e_iota = lax.broadcasted_iota(jnp.int32, logits.shape, 0)
    work = logits
    vals, idxs = [], []
    for _ in range(TOP_K):
        mx = jnp.max(work, axis=0, keepdims=True)
        sel = jnp.min(jnp.where(work == mx, e_iota, N_EXPERTS), axis=0, keepdims=True)
        vals.append(mx)
        idxs.append(sel)
        work = jnp.where(e_iota == sel, NEG_INF, work)
    exps = [jnp.exp(vv - vals[0]) for vv in vals]
    tot = exps[0] + exps[1] + exps[2] + exps[3]
    gates = [ev / tot for ev in exps]

    chosen = functools.reduce(jnp.logical_or, [e_iota == sel for sel in idxs])
    mh = jnp.where(chosen, 1.0, 0.0)
    ri = lax.broadcasted_iota(jnp.int32, (tm, tm), 0)
    ci = lax.broadcasted_iota(jnp.int32, (tm, tm), 1)
    upper = jnp.where(ri < ci, 1.0, 0.0).astype(BF16)
    rank = jnp.dot(mh.astype(BF16), upper, preferred_element_type=F32) + cnt_sc[:, 0:1]
    cnt_new = cnt_sc[...] + jnp.sum(mh, axis=1, keepdims=True)
    cnt_sc[...] = cnt_new
    cnt_ref[...] = cnt_new

    zero_i = jnp.zeros((SUBLANES - TOP_K, tm), jnp.int32)
    pos = [jnp.sum(jnp.where(e_iota == sel, rank, 0.0), axis=0, keepdims=True).astype(jnp.int32) for sel in idxs]
    idx_ref[...] = jnp.concatenate(idxs + [zero_i], axis=0)
    pos_ref[...] = jnp.concatenate(pos + [zero_i], axis=0)
    gate_ref[...] = jnp.concatenate(gates + [jnp.zeros((SUBLANES - TOP_K, tm), F32)], axis=0)


def _out_proj(xf, ym, yc, w_out_b, norm_w, w_router_t, b_router, d_mlstm, tm):
    n, d = xf.shape
    n_slabs = d // LANES
    row = lambda i: (i, 0)
    colb = lambda i: (0, i)
    const = lambda i: (0, 0)
    return pl.pallas_call(
        functools.partial(_outproj_kernel, tm=tm, d_mlstm=d_mlstm),
        grid=(n // tm,),
        in_specs=[pl.BlockSpec((tm, d), row), pl.BlockSpec((tm, ym.shape[1]), row), pl.BlockSpec((tm, yc.shape[1]), row),
                  pl.BlockSpec(w_out_b.shape, const), pl.BlockSpec(norm_w.shape, const),
                  pl.BlockSpec(w_router_t.shape, const), pl.BlockSpec(b_router.shape, const)],
        out_specs=[pl.BlockSpec((tm, d), row),
                   pl.BlockSpec((tm * n_slabs, LANES), row),
                   pl.BlockSpec((SUBLANES, tm), colb),
                   pl.BlockSpec((SUBLANES, tm), colb),
                   pl.BlockSpec((SUBLANES, tm), colb),
                   pl.BlockSpec((N_EXPERTS, LANES), const)],
        out_shape=[jax.ShapeDtypeStruct((n, d), F32),
                   jax.ShapeDtypeStruct((n * n_slabs, LANES), F32),
                   jax.ShapeDtypeStruct((SUBLANES, n), jnp.int32),
                   jax.ShapeDtypeStruct((SUBLANES, n), jnp.int32),
                   jax.ShapeDtypeStruct((SUBLANES, n), F32),
                   jax.ShapeDtypeStruct((N_EXPERTS, LANES), F32)],
        scratch_shapes=[pltpu.VMEM((N_EXPERTS, LANES), F32)],
        compiler_params=_cparams("arbitrary"),
        name="out_proj_router",
    )(xf, ym, yc, w_out_b, norm_w, w_router_t, b_router)


def _row_copy(src, src_row, dst, dst_row, sem, n_slabs):
    return pltpu.make_async_copy(
        src.at[pl.ds(pl.multiple_of(src_row * n_slabs, n_slabs), n_slabs), :],
        dst.at[pl.ds(pl.multiple_of(dst_row * n_slabs, n_slabs), n_slabs), :],
        sem)


def _dispatch_kernel(dest_ref, a2_ref, xs_ref, sem, *, tm, n_slabs):
    def issue(t, carry):
        for k in range(TOP_K):
            _row_copy(a2_ref, t, xs_ref, dest_ref[0, k, t], sem, n_slabs).start()
        return carry

    lax.fori_loop(0, tm, issue, 0)

    def drain(t, carry):
        for k in range(TOP_K):
            _row_copy(a2_ref, t, xs_ref, dest_ref[0, k, t], sem, n_slabs).wait()
        return carry

    lax.fori_loop(0, tm, drain, 0)


def _dispatch(a2s, dest_tiles, n_rows, n_slabs, tm):
    n_tok = a2s.shape[0] // n_slabs
    return pl.pallas_call(
        functools.partial(_dispatch_kernel, tm=tm, n_slabs=n_slabs),
        grid=(n_tok // tm,),
        in_specs=[pl.BlockSpec((1, TOP_K, tm), lambda i: (i, 0, 0), memory_space=pltpu.SMEM),
                  pl.BlockSpec((tm * n_slabs, LANES), lambda i: (i, 0))],
        out_specs=pl.BlockSpec(memory_space=pl.ANY),
        out_shape=jax.ShapeDtypeStruct((n_rows * n_slabs, LANES), F32),
        scratch_shapes=[pltpu.SemaphoreType.DMA],
        compiler_params=_cparams("arbitrary"),
        name="dispatch",
    )(dest_tiles, a2s)


def _experts_kernel(blk_ref, exp_ref, valid_ref, lo_ref, hi_ref,
                    xs_ref, w1_ref, b1_ref, w2_ref, b2_ref, ys_ref, w1c_ref, w2c_ref, acc_ref, *, rb, d_ff, n_sub):
    i = pl.program_id(0)
    n_slabs = xs_ref.shape[0] // rb
    prev = jnp.maximum(i - 1, 0)
    first_visit = jnp.logical_or(i == 0, blk_ref[prev] != blk_ref[i])
    new_expert = jnp.logical_or(i == 0, exp_ref[prev] != exp_ref[i])

    @pl.when(new_expert)
    def _():
        w1c_ref[...] = w1_ref[0].astype(BF16)
        w2c_ref[...] = w2_ref[0].astype(BF16)

    @pl.when(first_visit)
    def _():
        acc_ref[...] = jnp.zeros_like(acc_ref)

    @pl.when(valid_ref[i] == 1)
    def _():
        rs = rb // n_sub
        for c in range(n_sub):
            x = jnp.concatenate([xs_ref[pl.ds(c * rs * n_slabs + s, rs, stride=n_slabs), :] for s in range(n_slabs)],
                                axis=-1)
            hid = jnp.dot(x.astype(BF16), w1c_ref[...], preferred_element_type=F32) + b1_ref[0]
            x_glu = jnp.minimum(hid[:, :d_ff], SWIGLU_LIMIT)
            x_lin = jnp.clip(hid[:, d_ff:], -SWIGLU_LIMIT, SWIGLU_LIMIT)
            act = x_glu * _sigmoid(SWIGLU_ALPHA * x_glu) * (x_lin + 1.0)
            y = jnp.dot(act.astype(BF16), w2c_ref[...], preferred_element_type=F32) + b2_ref[0]
            r = c * rs + lax.broadcasted_iota(jnp.int32, (rs, 1), 0)
            y = jnp.where(jnp.logical_and(r >= lo_ref[i], r < hi_ref[i]), y, 0.0)
            y = acc_ref[c * rs:(c + 1) * rs, :] + y
            acc_ref[c * rs:(c + 1) * rs, :] = y
            for s in range(n_slabs):
                ys_ref[pl.ds(c * rs * n_slabs + s, rs, stride=n_slabs), :] = y[:, s * LANES:(s + 1) * LANES]


def _experts(xs, w1, b1, w2, b2, item_blk, item_exp, item_valid, item_lo, item_hi, rb, n_slabs):
    n_items = item_blk.shape[0]
    d = n_slabs * LANES
    d_ff = w2.shape[1]
    by_blk = lambda i, blk, ex, va, lo, hi: (blk[i], 0)
    by_exp = lambda i, blk, ex, va, lo, hi: (ex[i], 0, 0)
    grid_spec = pltpu.PrefetchScalarGridSpec(
        num_scalar_prefetch=5,
        grid=(n_items,),
        in_specs=[pl.BlockSpec((rb * n_slabs, LANES), by_blk),
                  pl.BlockSpec((1,) + w1.shape[1:], by_exp),
                  pl.BlockSpec((1,) + b1.shape[1:], by_exp),
                  pl.BlockSpec((1,) + w2.shape[1:], by_exp),
                  pl.BlockSpec((1,) + b2.shape[1:], by_exp)],
        out_specs=pl.BlockSpec((rb * n_slabs, LANES), by_blk),
        scratch_shapes=[pltpu.VMEM(w1.shape[1:], BF16), pltpu.VMEM(w2.shape[1:], BF16), pltpu.VMEM((rb, d), F32)],
    )
    return pl.pallas_call(
        functools.partial(_experts_kernel, rb=rb, d_ff=d_ff, n_sub=2),
        grid_spec=grid_spec,
        out_shape=jax.ShapeDtypeStruct(xs.shape, F32),
        compiler_params=pltpu.CompilerParams(dimension_semantics=("arbitrary",), vmem_limit_bytes=EXPERTS_VMEM_LIMIT),
        name="experts",
    )(item_blk, item_exp, item_valid, item_lo, item_hi, xs, w1, b1, w2, b2)


def _combine_kernel(dest_ref, h1_ref, gate_ref, ys_ref, nw_ref, out_ref, ybuf, sem, *, tm, n_slabs):
    def issue(t, carry):
        for k in range(TOP_K):
            _row_copy(ys_ref, dest_ref[0, k, t], ybuf, k * tm + t, sem, n_slabs).start()
        return carry

    lax.fori_loop(0, tm, issue, 0)

    gpad = jnp.concatenate([gate_ref[...], jnp.zeros((LANES - SUBLANES, tm), F32)], axis=0)
    gcol = gpad.T

    def drain(t, carry):
        for k in range(TOP_K):
            _row_copy(ys_ref, dest_ref[0, k, t], ybuf, k * tm + t, sem, n_slabs).wait()
        return carry

    lax.fori_loop(0, tm, drain, 0)

    h2 = h1_ref[...]
    for k in range(TOP_K):
        yk = jnp.concatenate([ybuf[pl.ds(k * tm * n_slabs + s, tm, stride=n_slabs), :] for s in range(n_slabs)],
                             axis=-1)
        h2 = h2 + gcol[:, k:k + 1] * yk
    out_ref[...] = h2 * lax.rsqrt(jnp.mean(h2 * h2, axis=-1, keepdims=True) + EPS) * nw_ref[...]


def _combine(dest_tiles, h1, gates_t, ys, final_w, n_slabs, tm):
    n, d = h1.shape
    return pl.pallas_call(
        functools.partial(_combine_kernel, tm=tm, n_slabs=n_slabs),
        grid=(n // tm,),
        in_specs=[pl.BlockSpec((1, TOP_K, tm), lambda i: (i, 0, 0), memory_space=pltpu.SMEM),
                  pl.BlockSpec((tm, d), lambda i: (i, 0)),
                  pl.BlockSpec((SUBLANES, tm), lambda i: (0, i)),
                  pl.BlockSpec(memory_space=pl.ANY),
                  pl.BlockSpec((1, d), lambda i: (0, 0))],
        out_specs=pl.BlockSpec((tm, d), lambda i: (i, 0)),
        out_shape=jax.ShapeDtypeStruct((n, d), F32),
        scratch_shapes=[pltpu.VMEM((TOP_K * tm * n_slabs, LANES), F32), pltpu.SemaphoreType.DMA],
        compiler_params=_cparams("arbitrary"),
        name="combine",
    )(dest_tiles, h1, gates_t, ys, final_w)


SC_CORES = 2
SC_SUBCORES = 16
SC_GATHER_WINDOW = 32


def _sc_worker_base(per_worker):
    wid = lax.axis_index("s") * SC_CORES + lax.axis_index("c")
    return wid * per_worker


def _sc_gather_rows(table, idx):
    m = idx.shape[0]
    n_workers = SC_CORES * SC_SUBCORES
    window = SC_GATHER_WINDOW
    per_worker = m // n_workers
    n_win = per_worker // window
    assert per_worker * n_workers == m and n_win * window == per_worker and n_win % 2 == 0
    mesh = plsc.VectorSubcoreMesh(core_axis_name="c", subcore_axis_name="s")
    slab = table.shape[1:]

    @functools.partial(
        pl.kernel, mesh=mesh,
        out_type=jax.ShapeDtypeStruct((m,) + slab, table.dtype),
        scratch_types=[pltpu.VMEM((window,), jnp.int32), pltpu.VMEM((window,), jnp.int32),
                       pltpu.VMEM((window,) + slab, table.dtype), pltpu.VMEM((window,) + slab, table.dtype),
                       pltpu.SemaphoreType.DMA, pltpu.SemaphoreType.DMA],
        name="sc_gather_rows",
    )
    def gather(table_hbm, idx_hbm, out_hbm, idx0, idx1, rows0, rows1, sem0, sem1):
        idx_v, rows_v, sems = (idx0, idx1), (rows0, rows1), (sem0, sem1)
        base = _sc_worker_base(per_worker)

        def rows_at(w):
            return pl.ds(pl.multiple_of(base + w * window, window), window)

        def start(w, b):
            pltpu.sync_copy(idx_hbm.at[rows_at(w)], idx_v[b])
            pltpu.async_copy(table_hbm.at[idx_v[b]], rows_v[b], sems[b])

        def finish(w, b):
            pltpu.make_async_copy(table_hbm.at[idx_v[b]], rows_v[b], sems[b]).wait()
            pltpu.sync_copy(rows_v[b], out_hbm.at[rows_at(w)])

        start(0, 0)

        @pl.loop(0, n_win, step=2)
        def _(w):
            start(w + 1, 1)
            finish(w, 0)

            @pl.when(w + 2 < n_win)
            def _():
                start(w + 2, 0)

            finish(w + 1, 1)

    return gather(table, idx)


def _sc_scatter_rows(rows, dest, n_slots):
    n = rows.shape[0]
    n_workers = SC_CORES * SC_SUBCORES
    window = SC_GATHER_WINDOW
    per_worker = n // n_workers
    n_win = per_worker // window
    assert per_worker * n_workers == n and n_win * window == per_worker and n_win % 2 == 0
    mesh = plsc.VectorSubcoreMesh(core_axis_name="c", subcore_axis_name="s")
    slab = rows.shape[1:]

    @functools.partial(
        pl.kernel, mesh=mesh,
        out_type=jax.ShapeDtypeStruct((n_slots * n,) + slab, rows.dtype),
        scratch_types=[pltpu.VMEM((window,), jnp.int32)] * (2 * n_slots)
        + [pltpu.VMEM((window,) + slab, rows.dtype)] * 2 + [pltpu.SemaphoreType.DMA] * 4,
        name="sc_scatter_rows",
    )
    def scatter(rows_hbm, dest_hbm, out_hbm, *scratch):
        idx_v = (scratch[:n_slots], scratch[n_slots:2 * n_slots])
        rows_v = scratch[2 * n_slots:2 * n_slots + 2]
        rsem = scratch[2 * n_slots + 2:2 * n_slots + 4]
        wsem = scratch[2 * n_slots + 4:2 * n_slots + 6]
        base = _sc_worker_base(per_worker)

        def rows_at(w, k=0):
            return pl.ds(pl.multiple_of(k * n + base + w * window, window), window)

        def start_read(w, b):
            for k in range(n_slots):
                pltpu.sync_copy(dest_hbm.at[rows_at(w, k)], idx_v[b][k])
            pltpu.async_copy(rows_hbm.at[rows_at(w)], rows_v[b], rsem[b])

        def scatter_window(w, b):
            pltpu.make_async_copy(rows_hbm.at[rows_at(w)], rows_v[b], rsem[b]).wait()
            for k in range(n_slots):
                pltpu.async_copy(rows_v[b], out_hbm.at[idx_v[b][k]], wsem[b])
            for k in range(n_slots):
                pltpu.make_async_copy(rows_v[b], out_hbm.at[idx_v[b][k]], wsem[b]).wait()

        start_read(0, 0)

        @pl.loop(0, n_win, step=2)
        def _(w):
            start_read(w + 1, 1)
            scatter_window(w, 0)

            @pl.when(w + 2 < n_win)
            def _():
                start_read(w + 2, 0)

            scatter_window(w + 1, 1)

    return scatter(rows, dest)


def _final_kernel(h1_ref, gate_ref, yg_ref, nw_ref, out_ref, *, tm, n_slabs):
    gpad = jnp.concatenate([gate_ref[...], jnp.zeros((LANES - SUBLANES, tm), F32)], axis=0)
    gcol = gpad.T
    h2 = h1_ref[...]
    for k in range(TOP_K):
        yk = jnp.concatenate([yg_ref[k, pl.ds(s, tm, stride=n_slabs), :] for s in range(n_slabs)], axis=-1)
        h2 = h2 + gcol[:, k:k + 1] * yk
    out_ref[...] = h2 * lax.rsqrt(jnp.mean(h2 * h2, axis=-1, keepdims=True) + EPS) * nw_ref[...]


def _final(h, gates_t, yg, final_w, n_slabs, tm, chunk, n_chunks):
    n, d = h.shape
    tiles = n // tm // n_chunks
    first = chunk * tiles
    return pl.pallas_call(
        functools.partial(_final_kernel, tm=tm, n_slabs=n_slabs),
        grid=(tiles,),
        in_specs=[pl.BlockSpec((tm, d), lambda i: (first + i, 0)),
                  pl.BlockSpec((SUBLANES, tm), lambda i: (0, first + i)),
                  pl.BlockSpec((TOP_K, tm * n_slabs, LANES), lambda i: (0, i, 0)),
                  pl.BlockSpec((1, d), lambda i: (0, 0))],
        out_specs=pl.BlockSpec((tm, d), lambda i: (first + i, 0)),
        out_shape=jax.ShapeDtypeStruct((n, d), F32),
        input_output_aliases={0: 0},
        compiler_params=_cparams("parallel"),
        name="final",
    )(h, gates_t, yg, final_w)


def _block_diag_tiles(w):
    nb, bs, _ = w.shape
    rows = jnp.tile(w.reshape(nb * bs // MXU_DIM, MXU_DIM, bs), (1, 1, MXU_DIM // bs))
    r_blk = lax.broadcasted_iota(jnp.int32, (MXU_DIM, MXU_DIM), 0) // bs
    c_blk = lax.broadcasted_iota(jnp.int32, (MXU_DIM, MXU_DIM), 1) // bs
    return jnp.where(r_blk == c_blk, rows, 0.0).astype(BF16)


def _layer(xf, bsz, seq, norm_mix_w, w_in, mlstm_conv_w, mlstm_conv_b, w_q, w_k, w_v, w_igate, b_igate,
           w_fgate, b_fgate, mlstm_norm_w, mlstm_skip, conv_dw_w, conv_dw_b, conv_norm_w, conv_norm_b,
           w_out, norm_ffn_w, w_router, b_router, w1, b1, w2, b2, final_norm_w):
    n, d = xf.shape
    d_mlstm = mlstm_norm_w.shape[0]
    d_conv = conv_norm_w.shape[0]
    n_slabs = d // LANES
    r2 = lambda v: v.reshape(1, -1)

    xm, xc, sz, u = _in_proj(xf, r2(norm_mix_w), w_in.astype(BF16), mlstm_conv_w, r2(mlstm_conv_b),
                             d_mlstm, d_conv, seq, tm=256)

    wg = jnp.concatenate([w_igate, w_fgate], axis=1)
    wg = jnp.pad(wg, ((0, 0), (0, LANES - wg.shape[1]))).astype(BF16)
    bg = jnp.pad(jnp.concatenate([b_igate, b_fgate]), (0, LANES - 2 * N_HEADS)).reshape(1, LANES)
    ym = _mlstm(xm, xc, sz, _block_diag_tiles(w_q), _block_diag_tiles(w_k), _block_diag_tiles(w_v),
                wg, bg, r2(mlstm_norm_w), r2(mlstm_skip), bsz, seq)
    yc = _conv_group(u, conv_dw_w, r2(conv_dw_b), r2(conv_norm_w), r2(conv_norm_b), bsz, seq, tile=256)

    h1, a2s, idx_t, pos_t, gates_t, cnt = _out_proj(
        xf, ym, yc, w_out.astype(BF16), r2(norm_ffn_w), w_router.T, b_router.reshape(-1, 1), d_mlstm, tm=512)

    counts = cnt[:, 0].astype(jnp.int32)
    ends = jnp.cumsum(counts)
    starts = ends - counts
    e_ids = jnp.arange(N_EXPERTS, dtype=jnp.int32)
    idx4 = idx_t[:TOP_K]
    dest = pos_t[:TOP_K] + jnp.sum(
        jnp.where(idx4[None] == e_ids[:, None, None], starts[:, None, None], 0), axis=0)
    tm = 256
    dest_tiles = dest.reshape(TOP_K, n // tm, tm).transpose(1, 0, 2)

    n_rows = n * TOP_K
    rb = 512
    n_blocks = n_rows // rb
    n_items = n_blocks + N_EXPERTS - 1
    first_blk = starts // rb
    last_blk = jnp.where(counts > 0, (ends - 1) // rb, first_blk - 1)
    per_e = last_blk - first_blk + 1
    item_end = jnp.cumsum(per_e)
    item_start = item_end - per_e
    ids = jnp.arange(n_items, dtype=jnp.int32)
    total = item_end[-1]
    item_valid = (ids < total).astype(jnp.int32)
    item_exp = jnp.minimum(jnp.sum((ids[:, None] >= item_end[None, :]).astype(jnp.int32), axis=1), N_EXPERTS - 1)
    item_blk = jnp.where(item_valid == 1, first_blk[item_exp] + ids - item_start[item_exp], n_blocks - 1)
    last_valid_exp = item_exp[jnp.maximum(total - 1, 0)]
    item_exp = jnp.where(item_valid == 1, item_exp, last_valid_exp).astype(jnp.int32)
    item_blk = item_blk.astype(jnp.int32)
    item_lo = (jnp.maximum(starts[item_exp], item_blk * rb) - item_blk * rb).astype(jnp.int32)
    item_hi = (jnp.minimum(ends[item_exp], (item_blk + 1) * rb) - item_blk * rb).astype(jnp.int32)

    xs = _sc_scatter_rows(a2s.reshape(n, n_slabs, LANES), dest.reshape(-1), TOP_K)
    ys = _experts(xs.reshape(n_rows * n_slabs, LANES), w1, b1[:, None, :], w2, b2[:, None, :],
                  item_blk, item_exp, item_valid, item_lo, item_hi, rb, n_slabs)
    ys = ys.reshape(n_rows, n_slabs, LANES)

    n_chunks = 4
    nc = n // n_chunks
    out = h1
    for c in range(n_chunks):
        yg = _sc_gather_rows(ys, dest[:, c * nc:(c + 1) * nc].reshape(-1))
        out = _final(out, gates_t, yg.reshape(TOP_K, nc * n_slabs, LANES), r2(final_norm_w), n_slabs,
                     tm=256, chunk=c, n_chunks=n_chunks)
    return out


def kernel(x, norm_mix_w, w_in, mlstm_conv_w, mlstm_conv_b, w_q, w_k, w_v, w_igate, b_igate, w_fgate, b_fgate,
           mlstm_norm_w, mlstm_skip, conv_dw_w, conv_dw_b, conv_norm_w, conv_norm_b, w_out, norm_ffn_w,
           w_router, b_router, w1, b1, w2, b2, final_norm_w):
    bsz, seq, d = x.shape
    assert norm_mix_w.shape[0] == 1, "single-layer block"
    out = _layer(x.reshape(bsz * seq, d), bsz, seq, norm_mix_w[0], w_in[0], mlstm_conv_w[0], mlstm_conv_b[0],
                 w_q[0], w_k[0], w_v[0], w_igate[0], b_igate[0], w_fgate[0], b_fgate[0], mlstm_norm_w[0],
                 mlstm_skip[0], conv_dw_w[0], conv_dw_b[0], conv_norm_w[0], conv_norm_b[0], w_out[0],
                 norm_ffn_w[0], w_router[0], b_router[0], w1[0], b1[0], w2[0], b2[0], final_norm_w)
    return out.reshape(bsz, seq, d)
```

```python
import functools

import jax
import jax.numpy as jnp
from jax import lax
from jax.experimental import pallas as pl
from jax.experimental.pallas import tpu as pltpu
from jax.experimental.pallas import tpu_sc as plsc

F32 = jnp.float32
BF16 = jnp.bfloat16

EPS = 1e-5
N_HEADS = 4
QKV_BLOCK = 4
MLSTM_CONV_WIDTH = 4
CONV_WIDTH = 31
N_EXPERTS = 32
TOP_K = 4
SWIGLU_ALPHA = 1.702
SWIGLU_LIMIT = 7.0

LANES = 128
SUBLANES = 8
MXU_DIM = 256
VMEM_LIMIT = 52 * 1024 * 1024
EXPERTS_VMEM_LIMIT = 58 * 1024 * 1024

MLSTM_CHUNK = 256
CONV_HALO = 32
NEG_INF = float("-inf")


def _sigmoid(x):
    return 1.0 / (1.0 + jnp.exp(-x))


def _silu(x):
    return x * _sigmoid(x)


def _pack_bf16_pairs(v):
    half = v.shape[1] // 2
    hi = lax.bitcast_convert_type(v[:, :half].astype(BF16).astype(F32), jnp.uint32)
    lo = lax.bitcast_convert_type(v[:, half:].astype(BF16).astype(F32), jnp.uint32)
    return hi | (lo >> 16)


def _unpack_bf16_pairs(w):
    hi = lax.bitcast_convert_type(w & jnp.uint32(0xFFFF0000), F32)
    lo = lax.bitcast_convert_type(w << 16, F32)
    return hi, lo


def _load_row_slabs(ref, first, rows, n_slabs, lead=()):
    return jnp.concatenate([ref[lead + (pl.ds(first * n_slabs + s, rows, stride=n_slabs), slice(None))]
                            for s in range(n_slabs)], axis=-1)


def _store_row_slabs(ref, first, v, n_slabs):
    rows = v.shape[0]
    for s in range(n_slabs):
        ref[pl.ds(first * n_slabs + s, rows, stride=n_slabs), :] = v[:, s * LANES:(s + 1) * LANES]


def _cparams(*sem):
    return pltpu.CompilerParams(dimension_semantics=sem, vmem_limit_bytes=VMEM_LIMIT)


def _inproj_kernel(x_ref, nw_ref, w_ref, cw_ref, cb_ref, xm_ref, xc_ref, sz_ref, u_ref, xbuf,
                   *, d_mlstm, d_conv, tiles_per_seq):
    tm = x_ref.shape[0]
    hist = SUBLANES

    @pl.when(pl.program_id(0) % tiles_per_seq == 0)
    def _():
        xbuf[0:hist, :] = jnp.zeros((hist, d_mlstm), F32)

    x = x_ref[...]
    a = x * lax.rsqrt(jnp.mean(x * x, axis=-1, keepdims=True) + EPS) * nw_ref[...]
    ab = a.astype(BF16)
    p1 = jnp.dot(ab, w_ref[:, : 2 * d_mlstm], preferred_element_type=F32)
    xm = p1[:, :d_mlstm]
    xm_ref[...] = xm.astype(xm_ref.dtype)
    sz_ref[...] = _silu(p1[:, d_mlstm:])

    xbuf[hist:hist + tm, :] = xm
    acc = cb_ref[...] + cw_ref[MLSTM_CONV_WIDTH - 1:MLSTM_CONV_WIDTH, :] * xm
    for k in range(MLSTM_CONV_WIDTH - 1):
        off = hist - (MLSTM_CONV_WIDTH - 1) + k
        acc = acc + cw_ref[k:k + 1, :] * xbuf[off:off + tm, :]
    xc_ref[...] = _silu(acc).astype(xc_ref.dtype)
    xbuf[0:hist, :] = xm[tm - hist:, :]

    p2 = jnp.dot(ab, w_ref[:, 2 * d_mlstm:], preferred_element_type=F32)
    u_ref[...] = p2[:, :d_conv] * _sigmoid(p2[:, d_conv:])


def _in_proj(xf, norm_w, w_in_b, conv_w, conv_b, d_mlstm, d_conv, seq, tm):
    n, d = xf.shape
    row = lambda i: (i, 0)
    const = lambda i: (0, 0)
    return pl.pallas_call(
        functools.partial(_inproj_kernel, d_mlstm=d_mlstm, d_conv=d_conv, tiles_per_seq=seq // tm),
        grid=(n // tm,),
        in_specs=[pl.BlockSpec((tm, d), row),
                  pl.BlockSpec((1, d), const),
                  pl.BlockSpec(w_in_b.shape, const),
                  pl.BlockSpec(conv_w.shape, const),
                  pl.BlockSpec(conv_b.shape, const)],
        out_specs=[pl.BlockSpec((tm, d_mlstm), row),
                   pl.BlockSpec((tm, d_mlstm), row),
                   pl.BlockSpec((tm, d_mlstm), row),
                   pl.BlockSpec((tm, d_conv), row)],
        out_shape=[jax.ShapeDtypeStruct((n, d_mlstm), BF16),
                   jax.ShapeDtypeStruct((n, d_mlstm), BF16),
                   jax.ShapeDtypeStruct((n, d_mlstm), F32),
                   jax.ShapeDtypeStruct((n, d_conv), F32)],
        scratch_shapes=[pltpu.VMEM((tm + SUBLANES, d_mlstm), F32)],
        compiler_params=_cparams("arbitrary"),
        name="in_proj",
    )(xf, norm_w, w_in_b, conv_w, conv_b)


def _split3(v):
    hi = v.astype(BF16)
    r1 = v - hi.astype(F32)
    mid = r1.astype(BF16)
    lo = (r1 - mid.astype(F32)).astype(BF16)
    return hi, mid, lo


def _mlstm_kernel(xm_ref, xc_ref, sz_ref, wq_ref, wk_ref, wv_ref, wg_ref, bg_ref, nw_ref, sk_ref,
                  ym_ref, c_sc, n_sc, m_sc, *, chunk, dh, n_seq):
    L = chunk
    nh = N_HEADS
    j = pl.program_id(1)

    @pl.when(j == 0)
    def _():
        c_sc[...] = jnp.zeros_like(c_sc)
        n_sc[...] = jnp.zeros_like(n_sc)
        m_sc[...] = jnp.zeros_like(m_sc)

    ri = lax.broadcasted_iota(jnp.int32, (L, L), 0)
    ci = lax.broadcasted_iota(jnp.int32, (L, L), 1)
    causal = ci <= ri
    tri = jnp.where(causal, 1.0, 0.0).astype(BF16)
    scale = dh ** -0.5

    for sq in range(n_seq):
        xmb = xm_ref[sq]
        xcb = xc_ref[sq]
        d = xmb.shape[1]
        nb = d // MXU_DIM

        def bd(xb, w_ref):
            return jnp.concatenate(
                [jnp.dot(xb[:, g * MXU_DIM:(g + 1) * MXU_DIM], w_ref[g], preferred_element_type=F32)
                 for g in range(nb)], axis=-1)

        q = bd(xcb, wq_ref)
        k_ = bd(xcb, wk_ref)
        v = bd(xmb, wv_ref)
        qb, kb, vb = q.astype(BF16), k_.astype(BF16), v.astype(BF16)

        g = (jnp.dot(qb, wg_ref[0:d, :], preferred_element_type=F32)
             + jnp.dot(kb, wg_ref[d:2 * d, :], preferred_element_type=F32)
             + jnp.dot(vb, wg_ref[2 * d:3 * d, :], preferred_element_type=F32)
             + bg_ref[...])
        col = lax.broadcasted_iota(jnp.int32, g.shape, 1)
        log_f = jnp.minimum(g, 0.0) - jnp.log(1.0 + jnp.exp(-jnp.abs(g)))
        gates = jnp.where(col < nh, g, jnp.where(col < 2 * nh, log_f, 0.0))
        cum = sum(jnp.dot(tri, part, preferred_element_type=F32) for part in _split3(gates))
        colform = jnp.where(col < nh, gates, cum)
        rowform = colform.T

        for h in range(nh):
            sl = slice(h * dh, (h + 1) * dh)
            li_row = rowform[h:h + 1, :]
            b_row = rowform[nh + h:nh + h + 1, :]
            li_col = colform[:, h:h + 1]
            b_col = colform[:, nh + h:nh + h + 1]
            m_prev = m_sc[sq, h:h + 1, 0:1]
            g_tot = b_row[:, L - 1:L]

            dmat = jnp.where(causal, b_col - b_row + li_row, NEG_INF)
            inter = b_col + m_prev
            m_i = jnp.maximum(inter, jnp.max(dmat, axis=-1, keepdims=True))
            w_intra = jnp.exp(dmat - m_i)
            w_inter = jnp.exp(inter - m_i)

            qh = qb[:, sl]
            s = lax.dot_general(qh, kb[:, sl], (((1,), (1,)), ((), ())),
                                preferred_element_type=F32) * (scale * w_intra)
            c_prev = c_sc[sq, h]
            n_prev = n_sc[sq, h:h + 1, :]
            num = (w_inter * scale) * jnp.dot(qh, c_prev.astype(BF16), preferred_element_type=F32) \
                + jnp.dot(s.astype(BF16), vb[:, sl], preferred_element_type=F32)
            qn = jnp.sum(q[:, sl] * n_prev, axis=-1, keepdims=True) * scale
            den = w_inter * qn + jnp.sum(s, axis=-1, keepdims=True)
            hh = num / jnp.maximum(jnp.abs(den), jnp.exp(-m_i))

            mu = jnp.mean(hh, axis=-1, keepdims=True)
            dev = hh - mu
            var = jnp.mean(dev * dev, axis=-1, keepdims=True)
            hn = dev * lax.rsqrt(var + EPS) * nw_ref[:, sl]
            ym_ref[sq, :, sl] = ((hn + sk_ref[:, sl] * xcb[:, sl].astype(F32)) * sz_ref[sq, :, sl]).astype(ym_ref.dtype)

            a_row = g_tot - b_row + li_row
            m_new = jnp.maximum(g_tot + m_prev, jnp.max(a_row, axis=-1, keepdims=True))
            a_col = g_tot - b_col + li_col
            w_state = jnp.exp(a_col - m_new)
            decay = jnp.exp(g_tot + m_prev - m_new)
            kw = k_[:, sl] * w_state
            c_sc[sq, h] = decay * c_prev + lax.dot_general(kw.astype(BF16), vb[:, sl], (((0,), (0,)), ((), ())),
                                                           preferred_element_type=F32)
            n_sc[sq, h:h + 1, :] = decay * n_prev + jnp.sum(kw, axis=0, keepdims=True)
            m_sc[sq, h:h + 1, :] = jnp.broadcast_to(m_new, (1, m_sc.shape[2]))


def _mlstm(xm, xc, sz, wq, wk, wv, wg, bg, norm_w, skip, bsz, seq):
    n, d = xm.shape
    L = MLSTM_CHUNK
    nc = seq // L
    dh = d // N_HEADS
    n_seq = 2 if bsz % 2 == 0 else 1
    blk = lambda b, j: (b, j, 0)
    c2 = lambda b, j: (0, 0)
    c3 = lambda b, j: (0, 0, 0)
    ym = pl.pallas_call(
        functools.partial(_mlstm_kernel, chunk=L, dh=dh, n_seq=n_seq),
        grid=(bsz // n_seq, nc),
        in_specs=[pl.BlockSpec((n_seq, L, d), blk), pl.BlockSpec((n_seq, L, d), blk), pl.BlockSpec((n_seq, L, d), blk),
                  pl.BlockSpec(wq.shape, c3), pl.BlockSpec(wk.shape, c3), pl.BlockSpec(wv.shape, c3),
                  pl.BlockSpec(wg.shape, c2), pl.BlockSpec(bg.shape, c2),
                  pl.BlockSpec(norm_w.shape, c2), pl.BlockSpec(skip.shape, c2)],
        out_specs=pl.BlockSpec((n_seq, L, d), blk),
        out_shape=jax.ShapeDtypeStruct((bsz, seq, d), BF16),
        scratch_shapes=[pltpu.VMEM((n_seq, N_HEADS, dh, dh), F32),
                        pltpu.VMEM((n_seq, SUBLANES, dh), F32),
                        pltpu.VMEM((n_seq, SUBLANES, LANES), F32)],
        compiler_params=_cparams("arbitrary", "arbitrary"),
        name="mlstm",
    )(xm.reshape(bsz, seq, d), xc.reshape(bsz, seq, d), sz.reshape(bsz, seq, d), wq, wk, wv, wg, bg, norm_w, skip)
    return ym.reshape(n, d)


def _conv_kernel(u_ref, w_ref, b_ref, nw_ref, nb_ref, yc_ref, ubuf, pbuf, cbuf, *, tile, rows):
    T = tile
    j = pl.program_id(1)

    @pl.when(j == 0)
    def _():
        ubuf[0:CONV_HALO, :] = jnp.zeros((CONV_HALO, ubuf.shape[1]), F32)

    ubuf[CONV_HALO:CONV_HALO + T, :] = u_ref[...]
    base = CONV_HALO - (CONV_WIDTH - 1)
    span = T + CONV_HALO - SUBLANES
    n_lane_blocks = ubuf.shape[1] // LANES

    def lane_block(c, carry):
        lanes = pl.ds(pl.multiple_of(c * LANES, LANES), LANES)
        for r in range(1, SUBLANES):
            pbuf[r - 1, :, :] = ubuf[r:r + span, lanes]
        for r0 in range(0, T, rows):
            acc = jnp.broadcast_to(b_ref[:, lanes], (rows, LANES))
            for k in range(CONV_WIDTH):
                q, r = divmod(base + k, SUBLANES)
                lo = r0 + q * SUBLANES
                src = ubuf[lo:lo + rows, lanes] if r == 0 else pbuf[r - 1, lo:lo + rows, :]
                acc = acc + w_ref[k:k + 1, lanes] * src
            cbuf[r0:r0 + rows, lanes] = acc
        return carry

    lax.fori_loop(0, n_lane_blocks, lane_block, 0)
    ubuf[0:CONV_HALO, :] = ubuf[T:T + CONV_HALO, :]

    y = cbuf[...]
    mu = jnp.mean(y, axis=-1, keepdims=True)
    dev = y - mu
    var = jnp.mean(dev * dev, axis=-1, keepdims=True)
    yn = dev * lax.rsqrt(var + EPS) * nw_ref[...] + nb_ref[...]
    yc_ref[...] = _silu(yn).astype(yc_ref.dtype)


def _conv_group(u, w, b, norm_w, norm_b, bsz, seq, tile):
    n, d = u.shape
    nt = seq // tile
    row = lambda bi, j: (bi * nt + j, 0)
    c2 = lambda bi, j: (0, 0)
    return pl.pallas_call(
        functools.partial(_conv_kernel, tile=tile, rows=64),
        grid=(bsz, nt),
        in_specs=[pl.BlockSpec((tile, d), row), pl.BlockSpec(w.shape, c2), pl.BlockSpec(b.shape, c2),
                  pl.BlockSpec(norm_w.shape, c2), pl.BlockSpec(norm_b.shape, c2)],
        out_specs=pl.BlockSpec((tile, d), row),
        out_shape=jax.ShapeDtypeStruct((n, d), BF16),
        scratch_shapes=[pltpu.VMEM((tile + CONV_HALO, d), F32),
                        pltpu.VMEM((SUBLANES - 1, tile + CONV_HALO - SUBLANES, LANES), F32),
                        pltpu.VMEM((tile, d), F32)],
        compiler_params=_cparams("arbitrary", "arbitrary"),
        name="conv_group",
    )(u, w, b, norm_w, norm_b)


def _outproj_kernel(x_ref, ym_ref, yc_ref, wo_ref, nw_ref, wr_ref, br_ref,
                    h1_ref, a2_ref, idx_ref, pos_ref, gate_ref, cnt_ref, cnt_sc, *, tm, d_mlstm):
    i = pl.program_id(0)

    @pl.when(i == 0)
    def _():
        cnt_sc[...] = jnp.zeros_like(cnt_sc)

    h1 = (x_ref[...]
          + jnp.dot(ym_ref[...], wo_ref[0:d_mlstm, :], preferred_element_type=F32)
          + jnp.dot(yc_ref[...], wo_ref[d_mlstm:, :], preferred_element_type=F32))
    h1_ref[...] = h1
    a2 = h1 * lax.rsqrt(jnp.mean(h1 * h1, axis=-1, keepdims=True) + EPS) * nw_ref[...]
    _store_row_slabs(a2_ref, 0, _pack_bf16_pairs(a2), a2.shape[1] // (2 * LANES))

    logits = lax.dot_general(wr_ref[...], a2, (((1,), (1,)), ((), ())),
                             precision=lax.Precision.HIGHEST, preferred_element_type=F32) + br_ref[...]
    e_iota = lax.broadcasted_iota(jnp.int32, logits.shape, 0)
    work = logits
    vals, idxs = [], []
    for _ in range(TOP_K):
        mx = jnp.max(work, axis=0, keepdims=True)
        sel = jnp.min(jnp.where(work == mx, e_iota, N_EXPERTS), axis=0, keepdims=True)
        vals.append(mx)
        idxs.append(sel)
        work = jnp.where(e_iota == sel, NEG_INF, work)
    exps = [jnp.exp(vv - vals[0]) for vv in vals]
    tot = exps[0] + exps[1] + exps[2] + exps[3]
    gates = [ev / tot for ev in exps]

    chosen = functools.reduce(jnp.logical_or, [e_iota == sel for sel in idxs])
    mh = jnp.where(chosen, 1.0, 0.0)
    ri = lax.broadcasted_iota(jnp.int32, (tm, tm), 0)
    ci = lax.broadcasted_iota(jnp.int32, (tm, tm), 1)
    upper = jnp.where(ri < ci, 1.0, 0.0).astype(BF16)
    rank = jnp.dot(mh.astype(BF16), upper, preferred_element_type=F32) + cnt_sc[:, 0:1]
    cnt_new = cnt_sc[...] + jnp.sum(mh, axis=1, keepdims=True)
    cnt_sc[...] = cnt_new
    cnt_ref[...] = cnt_new

    zero_i = jnp.zeros((SUBLANES - TOP_K, tm), jnp.int32)
    pos = [jnp.sum(jnp.where(e_iota == sel, rank, 0.0), axis=0, keepdims=True).astype(jnp.int32) for sel in idxs]
    idx_ref[...] = jnp.concatenate(idxs + [zero_i], axis=0)
    pos_ref[...] = jnp.concatenate(pos + [zero_i], axis=0)
    gate_ref[...] = jnp.concatenate(gates + [jnp.zeros((SUBLANES - TOP_K, tm), F32)], axis=0)


def _out_proj(xf, ym, yc, w_out_b, norm_w, w_router_t, b_router, d_mlstm, tm):
    n, d = xf.shape
    n_slabs = d // (2 * LANES)
    row = lambda i: (i, 0)
    colb = lambda i: (0, i)
    const = lambda i: (0, 0)
    return pl.pallas_call(
        functools.partial(_outproj_kernel, tm=tm, d_mlstm=d_mlstm),
        grid=(n // tm,),
        in_specs=[pl.BlockSpec((tm, d), row), pl.BlockSpec((tm, ym.shape[1]), row), pl.BlockSpec((tm, yc.shape[1]), row),
                  pl.BlockSpec(w_out_b.shape, const), pl.BlockSpec(norm_w.shape, const),
                  pl.BlockSpec(w_router_t.shape, const), pl.BlockSpec(b_router.shape, const)],
        out_specs=[pl.BlockSpec((tm, d), row),
                   pl.BlockSpec((tm * n_slabs, LANES), row),
                   pl.BlockSpec((SUBLANES, tm), colb),
                   pl.BlockSpec((SUBLANES, tm), colb),
                   pl.BlockSpec((SUBLANES, tm), colb),
                   pl.BlockSpec((N_EXPERTS, LANES), const)],
        out_shape=[jax.ShapeDtypeStruct((n, d), F32),
                   jax.ShapeDtypeStruct((n * n_slabs, LANES), jnp.uint32),
                   jax.ShapeDtypeStruct((SUBLANES, n), jnp.int32),
                   jax.ShapeDtypeStruct((SUBLANES, n), jnp.int32),
                   jax.ShapeDtypeStruct((SUBLANES, n), F32),
                   jax.ShapeDtypeStruct((N_EXPERTS, LANES), F32)],
        scratch_shapes=[pltpu.VMEM((N_EXPERTS, LANES), F32)],
        compiler_params=_cparams("arbitrary"),
        name="out_proj_router",
    )(xf, ym, yc, w_out_b, norm_w, w_router_t, b_router)


def _experts_kernel(blk_ref, exp_ref, valid_ref, lo_ref, hi_ref,
                    xs_ref, w1_ref, b1_ref, w2_ref, b2_ref, ys_ref, w1c_ref, w2c_ref, acc_ref, *, rb, d_ff, n_sub):
    i = pl.program_id(0)
    n_slabs = xs_ref.shape[0] // rb
    prev = jnp.maximum(i - 1, 0)
    first_visit = jnp.logical_or(i == 0, blk_ref[prev] != blk_ref[i])
    new_expert = jnp.logical_or(i == 0, exp_ref[prev] != exp_ref[i])

    @pl.when(new_expert)
    def _():
        w1c_ref[...] = w1_ref[0].astype(BF16)
        w2c_ref[...] = w2_ref[0].astype(BF16)

    @pl.when(first_visit)
    def _():
        acc_ref[...] = jnp.zeros_like(acc_ref)

    rs = rb // n_sub
    lo, hi = lo_ref[i], hi_ref[i]

    def sub_block(c):
        xa, xb = _unpack_bf16_pairs(_load_row_slabs(xs_ref, c * rs, rs, n_slabs))
        x = jnp.concatenate([xa.astype(BF16), xb.astype(BF16)], axis=-1)
        hid = jnp.dot(x, w1c_ref[...], preferred_element_type=F32) + b1_ref[0]
        x_glu = jnp.minimum(hid[:, :d_ff], SWIGLU_LIMIT)
        x_lin = jnp.clip(hid[:, d_ff:], -SWIGLU_LIMIT, SWIGLU_LIMIT)
        act = x_glu * _sigmoid(SWIGLU_ALPHA * x_glu) * (x_lin + 1.0)
        y = jnp.dot(act.astype(BF16), w2c_ref[...], preferred_element_type=F32) + b2_ref[0]
        r = c * rs + lax.broadcasted_iota(jnp.int32, (rs, 1), 0)
        y = jnp.where(jnp.logical_and(r >= lo, r < hi), y, 0.0)
        y = acc_ref[c * rs:(c + 1) * rs, :] + y
        acc_ref[c * rs:(c + 1) * rs, :] = y
        _store_row_slabs(ys_ref, c * rs, _pack_bf16_pairs(y), n_slabs)

    for c in range(n_sub):
        has_rows = jnp.logical_and(valid_ref[i] == 1, jnp.logical_and(lo < (c + 1) * rs, hi > c * rs))
        pl.when(has_rows)(functools.partial(sub_block, c))


def _experts(xs, w1, b1, w2, b2, item_blk, item_exp, item_valid, item_lo, item_hi, rb, n_slabs):
    n_items = item_blk.shape[0]
    d = w1.shape[1]
    d_ff = w2.shape[1]
    by_blk = lambda i, blk, ex, va, lo, hi: (blk[i], 0)
    by_exp = lambda i, blk, ex, va, lo, hi: (ex[i], 0, 0)
    grid_spec = pltpu.PrefetchScalarGridSpec(
        num_scalar_prefetch=5,
        grid=(n_items,),
        in_specs=[pl.BlockSpec((rb * n_slabs, LANES), by_blk),
                  pl.BlockSpec((1,) + w1.shape[1:], by_exp),
                  pl.BlockSpec((1,) + b1.shape[1:], by_exp),
                  pl.BlockSpec((1,) + w2.shape[1:], by_exp),
                  pl.BlockSpec((1,) + b2.shape[1:], by_exp)],
        out_specs=pl.BlockSpec((rb * n_slabs, LANES), by_blk),
        scratch_shapes=[pltpu.VMEM(w1.shape[1:], BF16), pltpu.VMEM(w2.shape[1:], BF16), pltpu.VMEM((rb, d), F32)],
    )
    return pl.pallas_call(
        functools.partial(_experts_kernel, rb=rb, d_ff=d_ff, n_sub=2),
        grid_spec=grid_spec,
        out_shape=jax.ShapeDtypeStruct(xs.shape, jnp.uint32),
        compiler_params=pltpu.CompilerParams(dimension_semantics=("arbitrary",), vmem_limit_bytes=EXPERTS_VMEM_LIMIT),
        name="experts",
    )(item_blk, item_exp, item_valid, item_lo, item_hi, xs, w1, b1, w2, b2)


SC_CORES = 2
SC_SUBCORES = 16
SC_GATHER_WINDOW = 64


def _sc_worker_base(per_worker):
    wid = lax.axis_index("s") * SC_CORES + lax.axis_index("c")
    return wid * per_worker


def _sc_gather_rows(table, idx):
    m = idx.shape[0]
    n_workers = SC_CORES * SC_SUBCORES
    window = SC_GATHER_WINDOW
    per_worker = m // n_workers
    n_win = per_worker // window
    assert per_worker * n_workers == m and n_win * window == per_worker and n_win % 2 == 0
    mesh = plsc.VectorSubcoreMesh(core_axis_name="c", subcore_axis_name="s")
    slab = table.shape[1:]

    @functools.partial(
        pl.kernel, mesh=mesh,
        out_type=jax.ShapeDtypeStruct((m,) + slab, table.dtype),
        scratch_types=[pltpu.VMEM((window,), jnp.int32), pltpu.VMEM((window,), jnp.int32),
                       pltpu.VMEM((window,) + slab, table.dtype), pltpu.VMEM((window,) + slab, table.dtype),
                       pltpu.SemaphoreType.DMA, pltpu.SemaphoreType.DMA],
        name="sc_gather_rows",
    )
    def gather(table_hbm, idx_hbm, out_hbm, idx0, idx1, rows0, rows1, sem0, sem1):
        idx_v, rows_v, sems = (idx0, idx1), (rows0, rows1), (sem0, sem1)
        base = _sc_worker_base(per_worker)

        def rows_at(w):
            return pl.ds(pl.multiple_of(base + w * window, window), window)

        def start(w, b):
            pltpu.sync_copy(idx_hbm.at[rows_at(w)], idx_v[b])
            pltpu.async_copy(table_hbm.at[idx_v[b]], rows_v[b], sems[b])

        def finish(w, b):
            pltpu.make_async_copy(table_hbm.at[idx_v[b]], rows_v[b], sems[b]).wait()
            pltpu.sync_copy(rows_v[b], out_hbm.at[rows_at(w)])

        start(0, 0)

        @pl.loop(0, n_win, step=2)
        def _(w):
            start(w + 1, 1)
            finish(w, 0)

            @pl.when(w + 2 < n_win)
            def _():
                start(w + 2, 0)

            finish(w + 1, 1)

    return gather(table, idx)


def _sc_scatter_rows(rows, dest, n_slots):
    n = rows.shape[0]
    n_workers = SC_CORES * SC_SUBCORES
    window = SC_GATHER_WINDOW
    per_worker = n // n_workers
    n_win = per_worker // window
    assert per_worker * n_workers == n and n_win * window == per_worker and n_win % 2 == 0
    mesh = plsc.VectorSubcoreMesh(core_axis_name="c", subcore_axis_name="s")
    slab = rows.shape[1:]

    @functools.partial(
        pl.kernel, mesh=mesh,
        out_type=jax.ShapeDtypeStruct((n_slots * n,) + slab, rows.dtype),
        scratch_types=[pltpu.VMEM((window,), jnp.int32)] * (2 * n_slots)
        + [pltpu.VMEM((window,) + slab, rows.dtype)] * 2 + [pltpu.SemaphoreType.DMA] * 4,
        name="sc_scatter_rows",
    )
    def scatter(rows_hbm, dest_hbm, out_hbm, *scratch):
        idx_v = (scratch[:n_slots], scratch[n_slots:2 * n_slots])
        rows_v = scratch[2 * n_slots:2 * n_slots + 2]
        rsem = scratch[2 * n_slots + 2:2 * n_slots + 4]
        wsem = scratch[2 * n_slots + 4:2 * n_slots + 6]
        base = _sc_worker_base(per_worker)

        def rows_at(w, k=0):
            return pl.ds(pl.multiple_of(k * n + base + w * window, window), window)

        def start_read(w, b):
            for k in range(n_slots):
                pltpu.sync_copy(dest_hbm.at[rows_at(w, k)], idx_v[b][k])
            pltpu.async_copy(rows_hbm.at[rows_at(w)], rows_v[b], rsem[b])

        def scatter_window(w, b):
            pltpu.make_async_copy(rows_hbm.at[rows_at(w)], rows_v[b], rsem[b]).wait()
            for k in range(n_slots):
                pltpu.async_copy(rows_v[b], out_hbm.at[idx_v[b][k]], wsem[b])
            for k in range(n_slots):
                pltpu.make_async_copy(rows_v[b], out_hbm.at[idx_v[b][k]], wsem[b]).wait()

        start_read(0, 0)

        @pl.loop(0, n_win, step=2)
        def _(w):
            start_read(w + 1, 1)
            scatter_window(w, 0)

            @pl.when(w + 2 < n_win)
            def _():
                start_read(w + 2, 0)

            scatter_window(w + 1, 1)

    return scatter(rows, dest)


def _final_kernel(h1_ref, gate_ref, yg_ref, nw_ref, out_ref, *, tm, n_slabs):
    gpad = jnp.concatenate([gate_ref[...], jnp.zeros((LANES - SUBLANES, tm), F32)], axis=0)
    gcol = gpad.T
    h2 = h1_ref[...]
    for k in range(TOP_K):
        ya, yb = _unpack_bf16_pairs(_load_row_slabs(yg_ref, 0, tm, n_slabs, lead=(k,)))
        h2 = h2 + gcol[:, k:k + 1] * jnp.concatenate([ya, yb], axis=-1)
    out_ref[...] = h2 * lax.rsqrt(jnp.mean(h2 * h2, axis=-1, keepdims=True) + EPS) * nw_ref[...]


def _final(h, gates_t, yg, final_w, n_slabs, tm, chunk, n_chunks):
    n, d = h.shape
    tiles = n // tm // n_chunks
    first = chunk * tiles
    return pl.pallas_call(
        functools.partial(_final_kernel, tm=tm, n_slabs=n_slabs),
        grid=(tiles,),
        in_specs=[pl.BlockSpec((tm, d), lambda i: (first + i, 0)),
                  pl.BlockSpec((SUBLANES, tm), lambda i: (0, first + i)),
                  pl.BlockSpec((TOP_K, tm * n_slabs, LANES), lambda i: (0, i, 0)),
                  pl.BlockSpec((1, d), lambda i: (0, 0))],
        out_specs=pl.BlockSpec((tm, d), lambda i: (first + i, 0)),
        out_shape=jax.ShapeDtypeStruct((n, d), F32),
        input_output_aliases={0: 0},
        compiler_params=_cparams("parallel"),
        name="final",
    )(h, gates_t, yg, final_w)


def _block_diag_tiles(w):
    nb, bs, _ = w.shape
    rows = jnp.tile(w.reshape(nb * bs // MXU_DIM, MXU_DIM, bs), (1, 1, MXU_DIM // bs))
    r_blk = lax.broadcasted_iota(jnp.int32, (MXU_DIM, MXU_DIM), 0) // bs
    c_blk = lax.broadcasted_iota(jnp.int32, (MXU_DIM, MXU_DIM), 1) // bs
    return jnp.where(r_blk == c_blk, rows, 0.0).astype(BF16)


def _layer(xf, bsz, seq, norm_mix_w, w_in, mlstm_conv_w, mlstm_conv_b, w_q, w_k, w_v, w_igate, b_igate,
           w_fgate, b_fgate, mlstm_norm_w, mlstm_skip, conv_dw_w, conv_dw_b, conv_norm_w, conv_norm_b,
           w_out, norm_ffn_w, w_router, b_router, w1, b1, w2, b2, final_norm_w):
    n, d = xf.shape
    d_mlstm = mlstm_norm_w.shape[0]
    d_conv = conv_norm_w.shape[0]
    n_slabs = d // (2 * LANES)
    r2 = lambda v: v.reshape(1, -1)

    xm, xc, sz, u = _in_proj(xf, r2(norm_mix_w), w_in.astype(BF16), mlstm_conv_w, r2(mlstm_conv_b),
                             d_mlstm, d_conv, seq, tm=256)

    wg = jnp.concatenate([w_igate, w_fgate], axis=1)
    wg = jnp.pad(wg, ((0, 0), (0, LANES - wg.shape[1]))).astype(BF16)
    bg = jnp.pad(jnp.concatenate([b_igate, b_fgate]), (0, LANES - 2 * N_HEADS)).reshape(1, LANES)
    ym = _mlstm(xm, xc, sz, _block_diag_tiles(w_q), _block_diag_tiles(w_k), _block_diag_tiles(w_v),
                wg, bg, r2(mlstm_norm_w), r2(mlstm_skip), bsz, seq)
    yc = _conv_group(u, conv_dw_w, r2(conv_dw_b), r2(conv_norm_w), r2(conv_norm_b), bsz, seq, tile=256)

    h1, a2s, idx_t, pos_t, gates_t, cnt = _out_proj(
        xf, ym, yc, w_out.astype(BF16), r2(norm_ffn_w), w_router.T, b_router.reshape(-1, 1), d_mlstm, tm=512)

    counts = cnt[:, 0].astype(jnp.int32)
    ends = jnp.cumsum(counts)
    starts = ends - counts
    e_ids = jnp.arange(N_EXPERTS, dtype=jnp.int32)
    idx4 = idx_t[:TOP_K]
    dest = pos_t[:TOP_K] + jnp.sum(
        jnp.where(idx4[None] == e_ids[:, None, None], starts[:, None, None], 0), axis=0)

    n_rows = n * TOP_K
    rb = 512
    n_blocks = n_rows // rb
    n_items = n_blocks + N_EXPERTS - 1
    first_blk = starts // rb
    last_blk = jnp.where(counts > 0, (ends - 1) // rb, first_blk - 1)
    per_e = last_blk - first_blk + 1
    item_end = jnp.cumsum(per_e)
    item_start = item_end - per_e
    ids = jnp.arange(n_items, dtype=jnp.int32)
    total = item_end[-1]
    item_valid = (ids < total).astype(jnp.int32)
    item_exp = jnp.minimum(jnp.sum((ids[:, None] >= item_end[None, :]).astype(jnp.int32), axis=1), N_EXPERTS - 1)
    item_blk = jnp.where(item_valid == 1, first_blk[item_exp] + ids - item_start[item_exp], n_blocks - 1)
    last_valid_exp = item_exp[jnp.maximum(total - 1, 0)]
    item_exp = jnp.where(item_valid == 1, item_exp, last_valid_exp).astype(jnp.int32)
    item_blk = item_blk.astype(jnp.int32)
    item_lo = (jnp.maximum(starts[item_exp], item_blk * rb) - item_blk * rb).astype(jnp.int32)
    item_hi = (jnp.minimum(ends[item_exp], (item_blk + 1) * rb) - item_blk * rb).astype(jnp.int32)

    xs = _sc_scatter_rows(a2s.reshape(n, n_slabs, LANES), dest.reshape(-1), TOP_K)
    ys = _experts(xs.reshape(n_rows * n_slabs, LANES), w1, b1[:, None, :], w2, b2[:, None, :],
                  item_blk, item_exp, item_valid, item_lo, item_hi, rb, n_slabs)
    ys = ys.reshape(n_rows, n_slabs, LANES)

    n_chunks = 4
    nc = n // n_chunks
    out = h1
    for c in range(n_chunks):
        yg = _sc_gather_rows(ys, dest[:, c * nc:(c + 1) * nc].reshape(-1))
        out = _final(out, gates_t, yg.reshape(TOP_K, nc * n_slabs, LANES), r2(final_norm_w), n_slabs,
                     tm=256, chunk=c, n_chunks=n_chunks)
    return out


def kernel(x, norm_mix_w, w_in, mlstm_conv_w, mlstm_conv_b, w_q, w_k, w_v, w_igate, b_igate, w_fgate, b_fgate,
           mlstm_norm_w, mlstm_skip, conv_dw_w, conv_dw_b, conv_norm_w, conv_norm_b, w_out, norm_ffn_w,
           w_router, b_router, w1, b1, w2, b2, final_norm_w):
    bsz, seq, d = x.shape
    assert norm_mix_w.shape[0] == 1, "single-layer block"
    out = _layer(x.reshape(bsz * seq, d), bsz, seq, norm_mix_w[0], w_in[0], mlstm_conv_w[0], mlstm_conv_b[0],
                 w_q[0], w_k[0], w_v[0], w_igate[0], b_igate[0], w_fgate[0], b_fgate[0], mlstm_norm_w[0],
                 mlstm_skip[0], conv_dw_w[0], conv_dw_b[0], conv_norm_w[0], conv_norm_b[0], w_out[0],
                 norm_ffn_w[0], w_router[0], b_router[0], w1[0], b1[0], w2[0], b2[0], final_norm_w)
    return out.reshape(bsz, seq, d)
```

```python
import functools

import jax
import jax.numpy as jnp
from jax import lax
from jax.experimental import pallas as pl
from jax.experimental.pallas import tpu as pltpu
from jax.experimental.pallas import tpu_sc as plsc

F32 = jnp.float32
BF16 = jnp.bfloat16

EPS = 1e-5
N_HEADS = 4
QKV_BLOCK = 4
MLSTM_CONV_WIDTH = 4
CONV_WIDTH = 31
N_EXPERTS = 32
TOP_K = 4
SWIGLU_ALPHA = 1.702
SWIGLU_LIMIT = 7.0

LANES = 128
SUBLANES = 8
MXU_DIM = 256
VMEM_LIMIT = 52 * 1024 * 1024
EXPERTS_VMEM_LIMIT = 58 * 1024 * 1024

MLSTM_CHUNK = 256
CONV_HALO = 32
NEG_INF = float("-inf")


def _sigmoid(x):
    return 1.0 / (1.0 + jnp.exp(-x))


def _silu(x):
    return x * _sigmoid(x)


def _pack_bf16_pairs(v):
    half = v.shape[1] // 2
    hi = lax.bitcast_convert_type(v[:, :half].astype(BF16).astype(F32), jnp.uint32)
    lo = lax.bitcast_convert_type(v[:, half:].astype(BF16).astype(F32), jnp.uint32)
    return hi | (lo >> 16)


def _unpack_bf16_pairs(w):
    hi = lax.bitcast_convert_type(w & jnp.uint32(0xFFFF0000), F32)
    lo = lax.bitcast_convert_type(w << 16, F32)
    return hi, lo


def _load_row_slabs(ref, first, rows, n_slabs, lead=()):
    return jnp.concatenate([ref[lead + (pl.ds(first * n_slabs + s, rows, stride=n_slabs), slice(None))]
                            for s in range(n_slabs)], axis=-1)


def _store_row_slabs(ref, first, v, n_slabs):
    rows = v.shape[0]
    for s in range(n_slabs):
        ref[pl.ds(first * n_slabs + s, rows, stride=n_slabs), :] = v[:, s * LANES:(s + 1) * LANES]


def _cparams(*sem):
    return pltpu.CompilerParams(dimension_semantics=sem, vmem_limit_bytes=VMEM_LIMIT)


def _inproj_kernel(x_ref, nw_ref, w_ref, cw_ref, cb_ref, xm_ref, xc_ref, sz_ref, u_ref, xbuf,
                   *, d_mlstm, d_conv, tiles_per_seq):
    tm = x_ref.shape[0]
    hist = SUBLANES
    i = pl.program_id(0)

    @pl.when(i == 0)
    def _():
        xbuf[...] = jnp.zeros_like(xbuf)

    x = x_ref[...]
    a = x * lax.rsqrt(jnp.mean(x * x, axis=-1, keepdims=True) + EPS) * nw_ref[...]
    ab = a.astype(BF16)
    same_seq = i % tiles_per_seq != 0
    cols = MXU_DIM

    for c in range(d_mlstm // cols):
        sl = slice(c * cols, (c + 1) * cols)
        prev = xbuf[hist:hist + tm, sl]
        acc = cb_ref[:, sl] + cw_ref[MLSTM_CONV_WIDTH - 1:MLSTM_CONV_WIDTH, sl] * prev
        for k in range(MLSTM_CONV_WIDTH - 1):
            off = hist - (MLSTM_CONV_WIDTH - 1) + k
            acc = acc + cw_ref[k:k + 1, sl] * xbuf[off:off + tm, sl]
        xc_ref[:, sl] = _silu(acc).astype(xc_ref.dtype)
        xbuf[0:hist, sl] = jnp.where(same_seq, prev[tm - hist:, :], 0.0)

        xm = jnp.dot(ab, w_ref[:, sl], preferred_element_type=F32)
        xm_ref[:, sl] = xm.astype(xm_ref.dtype)
        xbuf[hist:hist + tm, sl] = xm
        z = jnp.dot(ab, w_ref[:, d_mlstm + c * cols:d_mlstm + (c + 1) * cols], preferred_element_type=F32)
        sz_ref[:, sl] = _silu(z)

    for c in range(d_conv // cols):
        sl = slice(c * cols, (c + 1) * cols)
        ga = jnp.dot(ab, w_ref[:, 2 * d_mlstm + c * cols:2 * d_mlstm + (c + 1) * cols], preferred_element_type=F32)
        gb = jnp.dot(ab, w_ref[:, 2 * d_mlstm + d_conv + c * cols:2 * d_mlstm + d_conv + (c + 1) * cols],
                     preferred_element_type=F32)
        u_ref[:, sl] = ga * _sigmoid(gb)


def _in_proj(xf, norm_w, w_in_b, conv_w, conv_b, d_mlstm, d_conv, seq, tm):
    n, d = xf.shape
    n_tiles = n // tm
    cur = lambda i: (jnp.minimum(i, n_tiles - 1), 0)
    prv = lambda i: (jnp.maximum(i - 1, 0), 0)
    const = lambda i: (0, 0)
    return pl.pallas_call(
        functools.partial(_inproj_kernel, d_mlstm=d_mlstm, d_conv=d_conv, tiles_per_seq=seq // tm),
        grid=(n_tiles + 1,),
        in_specs=[pl.BlockSpec((tm, d), cur),
                  pl.BlockSpec((1, d), const),
                  pl.BlockSpec(w_in_b.shape, const),
                  pl.BlockSpec(conv_w.shape, const),
                  pl.BlockSpec(conv_b.shape, const)],
        out_specs=[pl.BlockSpec((tm, d_mlstm), cur),
                   pl.BlockSpec((tm, d_mlstm), prv),
                   pl.BlockSpec((tm, d_mlstm), cur),
                   pl.BlockSpec((tm, d_conv), cur)],
        out_shape=[jax.ShapeDtypeStruct((n, d_mlstm), BF16),
                   jax.ShapeDtypeStruct((n, d_mlstm), BF16),
                   jax.ShapeDtypeStruct((n, d_mlstm), F32),
                   jax.ShapeDtypeStruct((n, d_conv), F32)],
        scratch_shapes=[pltpu.VMEM((tm + SUBLANES, d_mlstm), F32)],
        compiler_params=_cparams("arbitrary"),
        name="in_proj",
    )(xf, norm_w, w_in_b, conv_w, conv_b)


def _split3(v):
    hi = v.astype(BF16)
    r1 = v - hi.astype(F32)
    mid = r1.astype(BF16)
    lo = (r1 - mid.astype(F32)).astype(BF16)
    return hi, mid, lo


def _mlstm_kernel(xm_ref, xc_ref, sz_ref, wq_ref, wk_ref, wv_ref, wg_ref, bg_ref, nw_ref, sk_ref,
                  ym_ref, c_sc, n_sc, m_sc, *, chunk, dh, n_seq):
    L = chunk
    nh = N_HEADS
    j = pl.program_id(1)

    @pl.when(j == 0)
    def _():
        c_sc[...] = jnp.zeros_like(c_sc)
        n_sc[...] = jnp.zeros_like(n_sc)
        m_sc[...] = jnp.zeros_like(m_sc)

    ri = lax.broadcasted_iota(jnp.int32, (L, L), 0)
    ci = lax.broadcasted_iota(jnp.int32, (L, L), 1)
    causal = ci <= ri
    tri = jnp.where(causal, 1.0, 0.0).astype(BF16)
    scale = dh ** -0.5

    for sq in range(n_seq):
        xmb = xm_ref[sq]
        xcb = xc_ref[sq]
        d = xmb.shape[1]
        nb = d // MXU_DIM

        def bd(xb, w_ref):
            return jnp.concatenate(
                [jnp.dot(xb[:, g * MXU_DIM:(g + 1) * MXU_DIM], w_ref[g], preferred_element_type=F32)
                 for g in range(nb)], axis=-1)

        q = bd(xcb, wq_ref)
        k_ = bd(xcb, wk_ref)
        v = bd(xmb, wv_ref)
        qb, kb, vb = q.astype(BF16), k_.astype(BF16), v.astype(BF16)

        g = (jnp.dot(qb, wg_ref[0:d, :], preferred_element_type=F32)
             + jnp.dot(kb, wg_ref[d:2 * d, :], preferred_element_type=F32)
             + jnp.dot(vb, wg_ref[2 * d:3 * d, :], preferred_element_type=F32)
             + bg_ref[...])
        col = lax.broadcasted_iota(jnp.int32, g.shape, 1)
        log_f = jnp.minimum(g, 0.0) - jnp.log(1.0 + jnp.exp(-jnp.abs(g)))
        gates = jnp.where(col < nh, g, jnp.where(col < 2 * nh, log_f, 0.0))
        cum = sum(jnp.dot(tri, part, preferred_element_type=F32) for part in _split3(gates))
        colform = jnp.where(col < nh, gates, cum)
        rowform = colform.T

        for h in range(nh):
            sl = slice(h * dh, (h + 1) * dh)
            li_row = rowform[h:h + 1, :]
            b_row = rowform[nh + h:nh + h + 1, :]
            li_col = colform[:, h:h + 1]
            b_col = colform[:, nh + h:nh + h + 1]
            m_prev = m_sc[sq, h:h + 1, 0:1]
            g_tot = b_row[:, L - 1:L]

            dmat = jnp.where(causal, b_col - b_row + li_row, NEG_INF)
            inter = b_col + m_prev
            m_i = jnp.maximum(inter, jnp.max(dmat, axis=-1, keepdims=True))
            w_intra = jnp.exp(dmat - m_i)
            w_inter = jnp.exp(inter - m_i)

            qh = qb[:, sl]
            s = lax.dot_general(qh, kb[:, sl], (((1,), (1,)), ((), ())),
                                preferred_element_type=F32) * (scale * w_intra)
            c_prev = c_sc[sq, h]
            n_prev = n_sc[sq, h:h + 1, :]
            num = (w_inter * scale) * jnp.dot(qh, c_prev.astype(BF16), preferred_element_type=F32) \
                + jnp.dot(s.astype(BF16), vb[:, sl], preferred_element_type=F32)
            qn = jnp.sum(q[:, sl] * n_prev, axis=-1, keepdims=True) * scale
            den = w_inter * qn + jnp.sum(s, axis=-1, keepdims=True)
            hh = num / jnp.maximum(jnp.abs(den), jnp.exp(-m_i))

            mu = jnp.mean(hh, axis=-1, keepdims=True)
            dev = hh - mu
            var = jnp.mean(dev * dev, axis=-1, keepdims=True)
            hn = dev * lax.rsqrt(var + EPS) * nw_ref[:, sl]
            ym_ref[sq, :, sl] = ((hn + sk_ref[:, sl] * xcb[:, sl].astype(F32)) * sz_ref[sq, :, sl]).astype(ym_ref.dtype)

            a_row = g_tot - b_row + li_row
            m_new = jnp.maximum(g_tot + m_prev, jnp.max(a_row, axis=-1, keepdims=True))
            a_col = g_tot - b_col + li_col
            w_state = jnp.exp(a_col - m_new)
            decay = jnp.exp(g_tot + m_prev - m_new)
            kw = k_[:, sl] * w_state
            c_sc[sq, h] = decay * c_prev + lax.dot_general(kw.astype(BF16), vb[:, sl], (((0,), (0,)), ((), ())),
                                                           preferred_element_type=F32)
            n_sc[sq, h:h + 1, :] = decay * n_prev + jnp.sum(kw, axis=0, keepdims=True)
            m_sc[sq, h:h + 1, :] = jnp.broadcast_to(m_new, (1, m_sc.shape[2]))


def _mlstm(xm, xc, sz, wq, wk, wv, wg, bg, norm_w, skip, bsz, seq):
    n, d = xm.shape
    L = MLSTM_CHUNK
    nc = seq // L
    dh = d // N_HEADS
    n_seq = 2 if bsz % 2 == 0 else 1
    blk = lambda b, j: (b, j, 0)
    c2 = lambda b, j: (0, 0)
    c3 = lambda b, j: (0, 0, 0)
    ym = pl.pallas_call(
        functools.partial(_mlstm_kernel, chunk=L, dh=dh, n_seq=n_seq),
        grid=(bsz // n_seq, nc),
        in_specs=[pl.BlockSpec((n_seq, L, d), blk), pl.BlockSpec((n_seq, L, d), blk), pl.BlockSpec((n_seq, L, d), blk),
                  pl.BlockSpec(wq.shape, c3), pl.BlockSpec(wk.shape, c3), pl.BlockSpec(wv.shape, c3),
                  pl.BlockSpec(wg.shape, c2), pl.BlockSpec(bg.shape, c2),
                  pl.BlockSpec(norm_w.shape, c2), pl.BlockSpec(skip.shape, c2)],
        out_specs=pl.BlockSpec((n_seq, L, d), blk),
        out_shape=jax.ShapeDtypeStruct((bsz, seq, d), BF16),
        scratch_shapes=[pltpu.VMEM((n_seq, N_HEADS, dh, dh), F32),
                        pltpu.VMEM((n_seq, SUBLANES, dh), F32),
                        pltpu.VMEM((n_seq, SUBLANES, LANES), F32)],
        compiler_params=_cparams("arbitrary", "arbitrary"),
        name="mlstm",
    )(xm.reshape(bsz, seq, d), xc.reshape(bsz, seq, d), sz.reshape(bsz, seq, d), wq, wk, wv, wg, bg, norm_w, skip)
    return ym.reshape(n, d)


def _conv_kernel(u_ref, w_ref, b_ref, nw_ref, nb_ref, yc_ref, ubuf, pbuf, cbuf, *, tile, rows):
    T = tile
    j = pl.program_id(1)

    @pl.when(j == 0)
    def _():
        ubuf[0:CONV_HALO, :] = jnp.zeros((CONV_HALO, ubuf.shape[1]), F32)

    ubuf[CONV_HALO:CONV_HALO + T, :] = u_ref[...]
    base = CONV_HALO - (CONV_WIDTH - 1)
    span = T + CONV_HALO - SUBLANES
    n_lane_blocks = ubuf.shape[1] // LANES

    def lane_block(c, carry):
        lanes = pl.ds(pl.multiple_of(c * LANES, LANES), LANES)
        for r in range(1, SUBLANES):
            pbuf[r - 1, :, :] = ubuf[r:r + span, lanes]
        for r0 in range(0, T, rows):
            acc = jnp.broadcast_to(b_ref[:, lanes], (rows, LANES))
            for k in range(CONV_WIDTH):
                q, r = divmod(base + k, SUBLANES)
                lo = r0 + q * SUBLANES
                src = ubuf[lo:lo + rows, lanes] if r == 0 else pbuf[r - 1, lo:lo + rows, :]
                acc = acc + w_ref[k:k + 1, lanes] * src
            cbuf[r0:r0 + rows, lanes] = acc
        return carry

    lax.fori_loop(0, n_lane_blocks, lane_block, 0)
    ubuf[0:CONV_HALO, :] = ubuf[T:T + CONV_HALO, :]

    y = cbuf[...]
    mu = jnp.mean(y, axis=-1, keepdims=True)
    dev = y - mu
    var = jnp.mean(dev * dev, axis=-1, keepdims=True)
    yn = dev * lax.rsqrt(var + EPS) * nw_ref[...] + nb_ref[...]
    yc_ref[...] = _silu(yn).astype(yc_ref.dtype)


def _conv_group(u, w, b, norm_w, norm_b, bsz, seq, tile):
    n, d = u.shape
    nt = seq // tile
    row = lambda bi, j: (bi * nt + j, 0)
    c2 = lambda bi, j: (0, 0)
    return pl.pallas_call(
        functools.partial(_conv_kernel, tile=tile, rows=64),
        grid=(bsz, nt),
        in_specs=[pl.BlockSpec((tile, d), row), pl.BlockSpec(w.shape, c2), pl.BlockSpec(b.shape, c2),
                  pl.BlockSpec(norm_w.shape, c2), pl.BlockSpec(norm_b.shape, c2)],
        out_specs=pl.BlockSpec((tile, d), row),
        out_shape=jax.ShapeDtypeStruct((n, d), BF16),
        scratch_shapes=[pltpu.VMEM((tile + CONV_HALO, d), F32),
                        pltpu.VMEM((SUBLANES - 1, tile + CONV_HALO - SUBLANES, LANES), F32),
                        pltpu.VMEM((tile, d), F32)],
        compiler_params=_cparams("arbitrary", "arbitrary"),
        name="conv_group",
    )(u, w, b, norm_w, norm_b)


def _outproj_kernel(x_ref, ym_ref, yc_ref, wo_ref, nw_ref, wr_ref, br_ref,
                    h1_ref, a2_ref, idx_ref, pos_ref, gate_ref, cnt_ref, cnt_sc, *, tm, d_mlstm):
    i = pl.program_id(0)

    @pl.when(i == 0)
    def _():
        cnt_sc[...] = jnp.zeros_like(cnt_sc)

    h1 = (x_ref[...]
          + jnp.dot(ym_ref[...], wo_ref[0:d_mlstm, :], preferred_element_type=F32)
          + jnp.dot(yc_ref[...], wo_ref[d_mlstm:, :], preferred_element_type=F32))
    h1_ref[...] = h1
    a2 = h1 * lax.rsqrt(jnp.mean(h1 * h1, axis=-1, keepdims=True) + EPS) * nw_ref[...]
    _store_row_slabs(a2_ref, 0, _pack_bf16_pairs(a2), a2.shape[1] // (2 * LANES))

    logits = lax.dot_general(wr_ref[...], a2, (((1,), (1,)), ((), ())),
                             precision=lax.Precision.HIGHEST, preferred_element_type=F32) + br_ref[...]
    e_iota = lax.broadcasted_iota(jnp.int32, logits.shape, 0)
    work = logits
    vals, idxs = [], []
    for _ in range(TOP_K):
        mx = jnp.max(work, axis=0, keepdims=True)
        sel = jnp.min(jnp.where(work == mx, e_iota, N_EXPERTS), axis=0, keepdims=True)
        vals.append(mx)
        idxs.append(sel)
        work = jnp.where(e_iota == sel, NEG_INF, work)
    exps = [jnp.exp(vv - vals[0]) for vv in vals]
    tot = exps[0] + exps[1] + exps[2] + exps[3]
    gates = [ev / tot for ev in exps]

    chosen = functools.reduce(jnp.logical_or, [e_iota == sel for sel in idxs])
    mh = jnp.where(chosen, 1.0, 0.0)
    ri = lax.broadcasted_iota(jnp.int32, (tm, tm), 0)
    ci = lax.broadcasted_iota(jnp.int32, (tm, tm), 1)
    upper = jnp.where(ri < ci, 1.0, 0.0).astype(BF16)
    rank = jnp.dot(mh.astype(BF16), upper, preferred_element_type=F32) + cnt_sc[:, 0:1]
    cnt_new = cnt_sc[...] + jnp.sum(mh, axis=1, keepdims=True)
    cnt_sc[...] = cnt_new
    cnt_ref[...] = cnt_new

    zero_i = jnp.zeros((SUBLANES - TOP_K, tm), jnp.int32)
    pos = [jnp.sum(jnp.where(e_iota == sel, rank, 0.0), axis=0, keepdims=True).astype(jnp.int32) for sel in idxs]
    idx_ref[...] = jnp.concatenate(idxs + [zero_i], axis=0)
    pos_ref[...] = jnp.concatenate(pos + [zero_i], axis=0)
    gate_ref[...] = jnp.concatenate(gates + [jnp.zeros((SUBLANES - TOP_K, tm), F32)], axis=0)


def _out_proj(xf, ym, yc, w_out_b, norm_w, w_router_t, b_router, d_mlstm, tm):
    n, d = xf.shape
    n_slabs = d // (2 * LANES)
    row = lambda i: (i, 0)
    colb = lambda i: (0, i)
    const = lambda i: (0, 0)
    return pl.pallas_call(
        functools.partial(_outproj_kernel, tm=tm, d_mlstm=d_mlstm),
        grid=(n // tm,),
        in_specs=[pl.BlockSpec((tm, d), row), pl.BlockSpec((tm, ym.shape[1]), row), pl.BlockSpec((tm, yc.shape[1]), row),
                  pl.BlockSpec(w_out_b.shape, const), pl.BlockSpec(norm_w.shape, const),
                  pl.BlockSpec(w_router_t.shape, const), pl.BlockSpec(b_router.shape, const)],
        out_specs=[pl.BlockSpec((tm, d), row),
                   pl.BlockSpec((tm * n_slabs, LANES), row),
                   pl.BlockSpec((SUBLANES, tm), colb),
                   pl.BlockSpec((SUBLANES, tm), colb),
                   pl.BlockSpec((SUBLANES, tm), colb),
                   pl.BlockSpec((N_EXPERTS, LANES), const)],
        out_shape=[jax.ShapeDtypeStruct((n, d), F32),
                   jax.ShapeDtypeStruct((n * n_slabs, LANES), jnp.uint32),
                   jax.ShapeDtypeStruct((SUBLANES, n), jnp.int32),
                   jax.ShapeDtypeStruct((SUBLANES, n), jnp.int32),
                   jax.ShapeDtypeStruct((SUBLANES, n), F32),
                   jax.ShapeDtypeStruct((N_EXPERTS, LANES), F32)],
        scratch_shapes=[pltpu.VMEM((N_EXPERTS, LANES), F32)],
        compiler_params=_cparams("arbitrary"),
        name="out_proj_router",
    )(xf, ym, yc, w_out_b, norm_w, w_router_t, b_router)


def _experts_kernel(blk_ref, exp_ref, valid_ref, lo_ref, hi_ref,
                    xs_ref, w1_ref, b1_ref, w2_ref, b2_ref, ys_ref, w1c_ref, w2c_ref, acc_ref, *, rb, d_ff, n_sub):
    i = pl.program_id(0)
    n_slabs = xs_ref.shape[0] // rb
    prev = jnp.maximum(i - 1, 0)
    first_visit = jnp.logical_or(i == 0, blk_ref[prev] != blk_ref[i])
    new_expert = jnp.logical_or(i == 0, exp_ref[prev] != exp_ref[i])

    @pl.when(new_expert)
    def _():
        w1c_ref[...] = w1_ref[0].astype(BF16)
        w2c_ref[...] = w2_ref[0].astype(BF16)

    @pl.when(first_visit)
    def _():
        acc_ref[...] = jnp.zeros_like(acc_ref)

    rs = rb // n_sub
    lo, hi = lo_ref[i], hi_ref[i]

    def sub_block(c):
        xa, xb = _unpack_bf16_pairs(_load_row_slabs(xs_ref, c * rs, rs, n_slabs))
        x = jnp.concatenate([xa.astype(BF16), xb.astype(BF16)], axis=-1)
        hid = jnp.dot(x, w1c_ref[...], preferred_element_type=F32) + b1_ref[0]
        x_glu = jnp.minimum(hid[:, :d_ff], SWIGLU_LIMIT)
        x_lin = jnp.clip(hid[:, d_ff:], -SWIGLU_LIMIT, SWIGLU_LIMIT)
        act = x_glu * _sigmoid(SWIGLU_ALPHA * x_glu) * (x_lin + 1.0)
        y = jnp.dot(act.astype(BF16), w2c_ref[...], preferred_element_type=F32) + b2_ref[0]
        r = c * rs + lax.broadcasted_iota(jnp.int32, (rs, 1), 0)
        y = jnp.where(jnp.logical_and(r >= lo, r < hi), y, 0.0)
        y = acc_ref[c * rs:(c + 1) * rs, :] + y
        acc_ref[c * rs:(c + 1) * rs, :] = y
        _store_row_slabs(ys_ref, c * rs, _pack_bf16_pairs(y), n_slabs)

    for c in range(n_sub):
        has_rows = jnp.logical_and(valid_ref[i] == 1, jnp.logical_and(lo < (c + 1) * rs, hi > c * rs))
        pl.when(has_rows)(functools.partial(sub_block, c))


def _experts(xs, w1, b1, w2, b2, item_blk, item_exp, item_valid, item_lo, item_hi, rb, n_slabs):
    n_items = item_blk.shape[0]
    d = w1.shape[1]
    d_ff = w2.shape[1]
    by_blk = lambda i, blk, ex, va, lo, hi: (blk[i], 0)
    by_exp = lambda i, blk, ex, va, lo, hi: (ex[i], 0, 0)
    grid_spec = pltpu.PrefetchScalarGridSpec(
        num_scalar_prefetch=5,
        grid=(n_items,),
        in_specs=[pl.BlockSpec((rb * n_slabs, LANES), by_blk),
                  pl.BlockSpec((1,) + w1.shape[1:], by_exp),
                  pl.BlockSpec((1,) + b1.shape[1:], by_exp),
                  pl.BlockSpec((1,) + w2.shape[1:], by_exp),
                  pl.BlockSpec((1,) + b2.shape[1:], by_exp)],
        out_specs=pl.BlockSpec((rb * n_slabs, LANES), by_blk),
        scratch_shapes=[pltpu.VMEM(w1.shape[1:], BF16), pltpu.VMEM(w2.shape[1:], BF16), pltpu.VMEM((rb, d), F32)],
    )
    return pl.pallas_call(
        functools.partial(_experts_kernel, rb=rb, d_ff=d_ff, n_sub=2),
        grid_spec=grid_spec,
        out_shape=jax.ShapeDtypeStruct(xs.shape, jnp.uint32),
        compiler_params=pltpu.CompilerParams(dimension_semantics=("arbitrary",), vmem_limit_bytes=EXPERTS_VMEM_LIMIT),
        name="experts",
    )(item_blk, item_exp, item_valid, item_lo, item_hi, xs, w1, b1, w2, b2)


SC_CORES = 2
SC_SUBCORES = 16
SC_GATHER_WINDOW = 64


def _sc_worker_base(per_worker):
    wid = lax.axis_index("s") * SC_CORES + lax.axis_index("c")
    return wid * per_worker


def _sc_gather_rows(table, idx):
    m = idx.shape[0]
    n_workers = SC_CORES * SC_SUBCORES
    window = SC_GATHER_WINDOW
    per_worker = m // n_workers
    n_win = per_worker // window
    assert per_worker * n_workers == m and n_win * window == per_worker and n_win % 2 == 0
    mesh = plsc.VectorSubcoreMesh(core_axis_name="c", subcore_axis_name="s")
    slab = table.shape[1:]

    @functools.partial(
        pl.kernel, mesh=mesh,
        out_type=jax.ShapeDtypeStruct((m,) + slab, table.dtype),
        scratch_types=[pltpu.VMEM((window,), jnp.int32), pltpu.VMEM((window,), jnp.int32),
                       pltpu.VMEM((window,) + slab, table.dtype), pltpu.VMEM((window,) + slab, table.dtype),
                       pltpu.SemaphoreType.DMA, pltpu.SemaphoreType.DMA],
        name="sc_gather_rows",
    )
    def gather(table_hbm, idx_hbm, out_hbm, idx0, idx1, rows0, rows1, sem0, sem1):
        idx_v, rows_v, sems = (idx0, idx1), (rows0, rows1), (sem0, sem1)
        base = _sc_worker_base(per_worker)

        def rows_at(w):
            return pl.ds(pl.multiple_of(base + w * window, window), window)

        def start(w, b):
            pltpu.sync_copy(idx_hbm.at[rows_at(w)], idx_v[b])
            pltpu.async_copy(table_hbm.at[idx_v[b]], rows_v[b], sems[b])

        def finish(w, b):
            pltpu.make_async_copy(table_hbm.at[idx_v[b]], rows_v[b], sems[b]).wait()
            pltpu.sync_copy(rows_v[b], out_hbm.at[rows_at(w)])

        start(0, 0)

        @pl.loop(0, n_win, step=2)
        def _(w):
            start(w + 1, 1)
            finish(w, 0)

            @pl.when(w + 2 < n_win)
            def _():
                start(w + 2, 0)

            finish(w + 1, 1)

    return gather(table, idx)


def _sc_scatter_rows(rows, dest, n_slots):
    n = rows.shape[0]
    n_workers = SC_CORES * SC_SUBCORES
    window = SC_GATHER_WINDOW
    per_worker = n // n_workers
    n_win = per_worker // window
    assert per_worker * n_workers == n and n_win * window == per_worker and n_win % 2 == 0
    mesh = plsc.VectorSubcoreMesh(core_axis_name="c", subcore_axis_name="s")
    slab = rows.shape[1:]

    @functools.partial(
        pl.kernel, mesh=mesh,
        out_type=jax.ShapeDtypeStruct((n_slots * n,) + slab, rows.dtype),
        scratch_types=[pltpu.VMEM((window,), jnp.int32)] * (2 * n_slots)
        + [pltpu.VMEM((window,) + slab, rows.dtype)] * 2 + [pltpu.SemaphoreType.DMA] * 4,
        name="sc_scatter_rows",
    )
    def scatter(rows_hbm, dest_hbm, out_hbm, *scratch):
        idx_v = (scratch[:n_slots], scratch[n_slots:2 * n_slots])
        rows_v = scratch[2 * n_slots:2 * n_slots + 2]
        rsem = scratch[2 * n_slots + 2:2 * n_slots + 4]
        wsem = scratch[2 * n_slots + 4:2 * n_slots + 6]
        base = _sc_worker_base(per_worker)

        def rows_at(w, k=0):
            return pl.ds(pl.multiple_of(k * n + base + w * window, window), window)

        def start_read(w, b):
            for k in range(n_slots):
                pltpu.sync_copy(dest_hbm.at[rows_at(w, k)], idx_v[b][k])
            pltpu.async_copy(rows_hbm.at[rows_at(w)], rows_v[b], rsem[b])

        def scatter_window(w, b):
            pltpu.make_async_copy(rows_hbm.at[rows_at(w)], rows_v[b], rsem[b]).wait()
            for k in range(n_slots):
                pltpu.async_copy(rows_v[b], out_hbm.at[idx_v[b][k]], wsem[b])
            for k in range(n_slots):
                pltpu.make_async_copy(rows_v[b], out_hbm.at[idx_v[b][k]], wsem[b]).wait()

        start_read(0, 0)

        @pl.loop(0, n_win, step=2)
        def _(w):
            start_read(w + 1, 1)
            scatter_window(w, 0)

            @pl.when(w + 2 < n_win)
            def _():
                start_read(w + 2, 0)

            scatter_window(w + 1, 1)

    return scatter(rows, dest)


def _final_kernel(h1_ref, gate_ref, yg_ref, nw_ref, out_ref, *, tm, n_slabs):
    gpad = jnp.concatenate([gate_ref[...], jnp.zeros((LANES - SUBLANES, tm), F32)], axis=0)
    gcol = gpad.T
    h2 = h1_ref[...]
    for k in range(TOP_K):
        ya, yb = _unpack_bf16_pairs(_load_row_slabs(yg_ref, 0, tm, n_slabs, lead=(k,)))
        h2 = h2 + gcol[:, k:k + 1] * jnp.concatenate([ya, yb], axis=-1)
    out_ref[...] = h2 * lax.rsqrt(jnp.mean(h2 * h2, axis=-1, keepdims=True) + EPS) * nw_ref[...]


def _final(h, gates_t, yg, final_w, n_slabs, tm, chunk, n_chunks):
    n, d = h.shape
    tiles = n // tm // n_chunks
    first = chunk * tiles
    return pl.pallas_call(
        functools.partial(_final_kernel, tm=tm, n_slabs=n_slabs),
        grid=(tiles,),
        in_specs=[pl.BlockSpec((tm, d), lambda i: (first + i, 0)),
                  pl.BlockSpec((SUBLANES, tm), lambda i: (0, first + i)),
                  pl.BlockSpec((TOP_K, tm * n_slabs, LANES), lambda i: (0, i, 0)),
                  pl.BlockSpec((1, d), lambda i: (0, 0))],
        out_specs=pl.BlockSpec((tm, d), lambda i: (first + i, 0)),
        out_shape=jax.ShapeDtypeStruct((n, d), F32),
        input_output_aliases={0: 0},
        compiler_params=_cparams("parallel"),
        name="final",
    )(h, gates_t, yg, final_w)


def _block_diag_tiles(w):
    nb, bs, _ = w.shape
    rows = jnp.tile(w.reshape(nb * bs // MXU_DIM, MXU_DIM, bs), (1, 1, MXU_DIM // bs))
    r_blk = lax.broadcasted_iota(jnp.int32, (MXU_DIM, MXU_DIM), 0) // bs
    c_blk = lax.broadcasted_iota(jnp.int32, (MXU_DIM, MXU_DIM), 1) // bs
    return jnp.where(r_blk == c_blk, rows, 0.0).astype(BF16)


def _layer(xf, bsz, seq, norm_mix_w, w_in, mlstm_conv_w, mlstm_conv_b, w_q, w_k, w_v, w_igate, b_igate,
           w_fgate, b_fgate, mlstm_norm_w, mlstm_skip, conv_dw_w, conv_dw_b, conv_norm_w, conv_norm_b,
           w_out, norm_ffn_w, w_router, b_router, w1, b1, w2, b2, final_norm_w):
    n, d = xf.shape
    d_mlstm = mlstm_norm_w.shape[0]
    d_conv = conv_norm_w.shape[0]
    n_slabs = d // (2 * LANES)
    r2 = lambda v: v.reshape(1, -1)

    xm, xc, sz, u = _in_proj(xf, r2(norm_mix_w), w_in.astype(BF16), mlstm_conv_w, r2(mlstm_conv_b),
                             d_mlstm, d_conv, seq, tm=256)

    wg = jnp.concatenate([w_igate, w_fgate], axis=1)
    wg = jnp.pad(wg, ((0, 0), (0, LANES - wg.shape[1]))).astype(BF16)
    bg = jnp.pad(jnp.concatenate([b_igate, b_fgate]), (0, LANES - 2 * N_HEADS)).reshape(1, LANES)
    ym = _mlstm(xm, xc, sz, _block_diag_tiles(w_q), _block_diag_tiles(w_k), _block_diag_tiles(w_v),
                wg, bg, r2(mlstm_norm_w), r2(mlstm_skip), bsz, seq)
    yc = _conv_group(u, conv_dw_w, r2(conv_dw_b), r2(conv_norm_w), r2(conv_norm_b), bsz, seq, tile=256)

    h1, a2s, idx_t, pos_t, gates_t, cnt = _out_proj(
        xf, ym, yc, w_out.astype(BF16), r2(norm_ffn_w), w_router.T, b_router.reshape(-1, 1), d_mlstm, tm=512)

    counts = cnt[:, 0].astype(jnp.int32)
    ends = jnp.cumsum(counts)
    starts = ends - counts
    e_ids = jnp.arange(N_EXPERTS, dtype=jnp.int32)
    idx4 = idx_t[:TOP_K]
    dest = pos_t[:TOP_K] + jnp.sum(
        jnp.where(idx4[None] == e_ids[:, None, None], starts[:, None, None], 0), axis=0)

    n_rows = n * TOP_K
    rb = 512
    n_blocks = n_rows // rb
    n_items = n_blocks + N_EXPERTS - 1
    first_blk = starts // rb
    last_blk = jnp.where(counts > 0, (ends - 1) // rb, first_blk - 1)
    per_e = last_blk - first_blk + 1
    item_end = jnp.cumsum(per_e)
    item_start = item_end - per_e
    ids = jnp.arange(n_items, dtype=jnp.int32)
    total = item_end[-1]
    item_valid = (ids < total).astype(jnp.int32)
    item_exp = jnp.minimum(jnp.sum((ids[:, None] >= item_end[None, :]).astype(jnp.int32), axis=1), N_EXPERTS - 1)
    last_valid_exp = jnp.max(jnp.where(per_e > 0, e_ids, 0))
    item_exp = jnp.where(item_valid == 1, item_exp, last_valid_exp).astype(jnp.int32)
    is_exp = item_exp[:, None] == e_ids[None, :]
    of_item = lambda table: jnp.sum(jnp.where(is_exp, table[None, :], 0), axis=1)
    item_blk = jnp.where(item_valid == 1, of_item(first_blk) + ids - of_item(item_start), n_blocks - 1)
    item_blk = item_blk.astype(jnp.int32)
    item_lo = (jnp.maximum(of_item(starts), item_blk * rb) - item_blk * rb).astype(jnp.int32)
    item_hi = (jnp.minimum(of_item(ends), (item_blk + 1) * rb) - item_blk * rb).astype(jnp.int32)

    xs = _sc_scatter_rows(a2s.reshape(n, n_slabs, LANES), dest.reshape(-1), TOP_K)
    ys = _experts(xs.reshape(n_rows * n_slabs, LANES), w1, b1[:, None, :], w2, b2[:, None, :],
                  item_blk, item_exp, item_valid, item_lo, item_hi, rb, n_slabs)
    ys = ys.reshape(n_rows, n_slabs, LANES)

    n_chunks = 8
    nc = n // n_chunks
    out = h1
    for c in range(n_chunks):
        yg = _sc_gather_rows(ys, dest[:, c * nc:(c + 1) * nc].reshape(-1))
        out = _final(out, gates_t, yg.reshape(TOP_K, nc * n_slabs, LANES), r2(final_norm_w), n_slabs,
                     tm=256, chunk=c, n_chunks=n_chunks)
    return out


def kernel(x, norm_mix_w, w_in, mlstm_conv_w, mlstm_conv_b, w_q, w_k, w_v, w_igate, b_igate, w_fgate, b_fgate,
           mlstm_norm_w, mlstm_skip, conv_dw_w, conv_dw_b, conv_norm_w, conv_norm_b, w_out, norm_ffn_w,
           w_router, b_router, w1, b1, w2, b2, final_norm_w):
    bsz, seq, d = x.shape
    assert norm_mix_w.shape[0] == 1, "single-layer block"
    out = _layer(x.reshape(bsz * seq, d), bsz, seq, norm_mix_w[0], w_in[0], mlstm_conv_w[0], mlstm_conv_b[0],
                 w_q[0], w_k[0], w_v[0], w_igate[0], b_igate[0], w_fgate[0], b_fgate[0], mlstm_norm_w[0],
                 mlstm_skip[0], conv_dw_w[0], conv_dw_b[0], conv_norm_w[0], conv_norm_b[0], w_out[0],
                 norm_ffn_w[0], w_router[0], b_router[0], w1[0], b1[0], w2[0], b2[0], final_norm_w)
    return out.reshape(bsz, seq, d)
```

```python
import functools

import jax
import jax.numpy as jnp
from jax import lax
from jax.experimental import pallas as pl
from jax.experimental.pallas import tpu as pltpu
from jax.experimental.pallas import tpu_sc as plsc

F32 = jnp.float32
BF16 = jnp.bfloat16

EPS = 1e-5
N_HEADS = 4
QKV_BLOCK = 4
MLSTM_CONV_WIDTH = 4
CONV_WIDTH = 31
N_EXPERTS = 32
TOP_K = 4
SWIGLU_ALPHA = 1.702
SWIGLU_LIMIT = 7.0

LANES = 128
SUBLANES = 8
MXU_DIM = 256
VMEM_LIMIT = 52 * 1024 * 1024
EXPERTS_VMEM_LIMIT = 58 * 1024 * 1024

MLSTM_CHUNK = 256
CONV_HALO = 32
NEG_INF = float("-inf")


def _sigmoid(x):
    return 1.0 / (1.0 + jnp.exp(-x))


def _silu(x):
    return x * _sigmoid(x)


def _pack_bf16_pairs(v):
    half = v.shape[1] // 2
    hi = lax.bitcast_convert_type(v[:, :half].astype(BF16).astype(F32), jnp.uint32)
    lo = lax.bitcast_convert_type(v[:, half:].astype(BF16).astype(F32), jnp.uint32)
    return hi | (lo >> 16)


def _unpack_bf16_pairs(w):
    hi = lax.bitcast_convert_type(w & jnp.uint32(0xFFFF0000), F32)
    lo = lax.bitcast_convert_type(w << 16, F32)
    return hi, lo


def _load_row_slabs(ref, first, rows, n_slabs, lead=()):
    return jnp.concatenate([ref[lead + (pl.ds(first * n_slabs + s, rows, stride=n_slabs), slice(None))]
                            for s in range(n_slabs)], axis=-1)


def _store_row_slabs(ref, first, v, n_slabs):
    rows = v.shape[0]
    for s in range(n_slabs):
        ref[pl.ds(first * n_slabs + s, rows, stride=n_slabs), :] = v[:, s * LANES:(s + 1) * LANES]


def _cparams(*sem):
    return pltpu.CompilerParams(dimension_semantics=sem, vmem_limit_bytes=VMEM_LIMIT)


def _inproj_kernel(x_ref, nw_ref, w_ref, cw_ref, cb_ref, xm_ref, xc_ref, sz_ref, u_ref, xbuf,
                   *, d_mlstm, d_conv, tiles_per_seq):
    tm = x_ref.shape[0]
    hist = SUBLANES
    i = pl.program_id(0)

    @pl.when(i == 0)
    def _():
        xbuf[...] = jnp.zeros_like(xbuf)

    x = x_ref[...]
    a = x * lax.rsqrt(jnp.mean(x * x, axis=-1, keepdims=True) + EPS) * nw_ref[...]
    ab = a.astype(BF16)
    same_seq = i % tiles_per_seq != 0
    cols = MXU_DIM

    for c in range(d_mlstm // cols):
        sl = slice(c * cols, (c + 1) * cols)
        prev = xbuf[hist:hist + tm, sl]
        acc = cb_ref[:, sl] + cw_ref[MLSTM_CONV_WIDTH - 1:MLSTM_CONV_WIDTH, sl] * prev
        for k in range(MLSTM_CONV_WIDTH - 1):
            off = hist - (MLSTM_CONV_WIDTH - 1) + k
            acc = acc + cw_ref[k:k + 1, sl] * xbuf[off:off + tm, sl]
        xc_ref[:, sl] = _silu(acc).astype(xc_ref.dtype)
        xbuf[0:hist, sl] = jnp.where(same_seq, prev[tm - hist:, :], 0.0)

        xm = jnp.dot(ab, w_ref[:, sl], preferred_element_type=F32)
        xm_ref[:, sl] = xm.astype(xm_ref.dtype)
        xbuf[hist:hist + tm, sl] = xm
        z = jnp.dot(ab, w_ref[:, d_mlstm + c * cols:d_mlstm + (c + 1) * cols], preferred_element_type=F32)
        sz_ref[:, sl] = _silu(z)

    for c in range(d_conv // cols):
        sl = slice(c * cols, (c + 1) * cols)
        ga = jnp.dot(ab, w_ref[:, 2 * d_mlstm + c * cols:2 * d_mlstm + (c + 1) * cols], preferred_element_type=F32)
        gb = jnp.dot(ab, w_ref[:, 2 * d_mlstm + d_conv + c * cols:2 * d_mlstm + d_conv + (c + 1) * cols],
                     preferred_element_type=F32)
        u_ref[:, sl] = ga * _sigmoid(gb)


def _in_proj(xf, norm_w, w_in_b, conv_w, conv_b, d_mlstm, d_conv, seq, tm):
    n, d = xf.shape
    n_tiles = n // tm
    cur = lambda i: (jnp.minimum(i, n_tiles - 1), 0)
    prv = lambda i: (jnp.maximum(i - 1, 0), 0)
    const = lambda i: (0, 0)
    return pl.pallas_call(
        functools.partial(_inproj_kernel, d_mlstm=d_mlstm, d_conv=d_conv, tiles_per_seq=seq // tm),
        grid=(n_tiles + 1,),
        in_specs=[pl.BlockSpec((tm, d), cur),
                  pl.BlockSpec((1, d), const),
                  pl.BlockSpec(w_in_b.shape, const),
                  pl.BlockSpec(conv_w.shape, const),
                  pl.BlockSpec(conv_b.shape, const)],
        out_specs=[pl.BlockSpec((tm, d_mlstm), cur),
                   pl.BlockSpec((tm, d_mlstm), prv),
                   pl.BlockSpec((tm, d_mlstm), cur),
                   pl.BlockSpec((tm, d_conv), cur)],
        out_shape=[jax.ShapeDtypeStruct((n, d_mlstm), BF16),
                   jax.ShapeDtypeStruct((n, d_mlstm), BF16),
                   jax.ShapeDtypeStruct((n, d_mlstm), F32),
                   jax.ShapeDtypeStruct((n, d_conv), F32)],
        scratch_shapes=[pltpu.VMEM((tm + SUBLANES, d_mlstm), F32)],
        compiler_params=_cparams("arbitrary"),
        name="in_proj",
    )(xf, norm_w, w_in_b, conv_w, conv_b)


def _split3(v):
    hi = v.astype(BF16)
    r1 = v - hi.astype(F32)
    mid = r1.astype(BF16)
    lo = (r1 - mid.astype(F32)).astype(BF16)
    return hi, mid, lo


def _mlstm_kernel(xm_ref, xc_ref, sz_ref, wq_ref, wk_ref, wv_ref, wg_ref, bg_ref, nw_ref, sk_ref,
                  ym_ref, c_sc, n_sc, m_sc, *, chunk, dh, n_seq):
    L = chunk
    nh = N_HEADS
    j = pl.program_id(1)

    @pl.when(j == 0)
    def _():
        c_sc[...] = jnp.zeros_like(c_sc)
        n_sc[...] = jnp.zeros_like(n_sc)
        m_sc[...] = jnp.zeros_like(m_sc)

    ri = lax.broadcasted_iota(jnp.int32, (L, L), 0)
    ci = lax.broadcasted_iota(jnp.int32, (L, L), 1)
    causal = ci <= ri
    tri = jnp.where(causal, 1.0, 0.0).astype(BF16)
    scale = dh ** -0.5

    for sq in range(n_seq):
        xmb = xm_ref[sq]
        xcb = xc_ref[sq]
        d = xmb.shape[1]
        nb = d // MXU_DIM

        def bd(xb, w_ref):
            return jnp.concatenate(
                [jnp.dot(xb[:, g * MXU_DIM:(g + 1) * MXU_DIM], w_ref[g], preferred_element_type=F32)
                 for g in range(nb)], axis=-1)

        q = bd(xcb, wq_ref)
        k_ = bd(xcb, wk_ref)
        v = bd(xmb, wv_ref)
        qb, kb, vb = q.astype(BF16), k_.astype(BF16), v.astype(BF16)

        g = (jnp.dot(qb, wg_ref[0:d, :], preferred_element_type=F32)
             + jnp.dot(kb, wg_ref[d:2 * d, :], preferred_element_type=F32)
             + jnp.dot(vb, wg_ref[2 * d:3 * d, :], preferred_element_type=F32)
             + bg_ref[...])
        col = lax.broadcasted_iota(jnp.int32, g.shape, 1)
        log_f = jnp.minimum(g, 0.0) - jnp.log(1.0 + jnp.exp(-jnp.abs(g)))
        gates = jnp.where(col < nh, g, jnp.where(col < 2 * nh, log_f, 0.0))
        cum = sum(jnp.dot(tri, part, preferred_element_type=F32) for part in _split3(gates))
        colform = jnp.where(col < nh, gates, cum)
        rowform = colform.T

        for h in range(nh):
            sl = slice(h * dh, (h + 1) * dh)
            li_row = rowform[h:h + 1, :]
            b_row = rowform[nh + h:nh + h + 1, :]
            li_col = colform[:, h:h + 1]
            b_col = colform[:, nh + h:nh + h + 1]
            m_prev = m_sc[sq, h:h + 1, 0:1]
            g_tot = b_row[:, L - 1:L]

            dmat = jnp.where(causal, b_col - b_row + li_row, NEG_INF)
            inter = b_col + m_prev
            m_i = jnp.maximum(inter, jnp.max(dmat, axis=-1, keepdims=True))
            w_intra = jnp.exp(dmat - m_i)
            w_inter = jnp.exp(inter - m_i)

            qh = qb[:, sl]
            s = lax.dot_general(qh, kb[:, sl], (((1,), (1,)), ((), ())),
                                preferred_element_type=F32) * (scale * w_intra)
            c_prev = c_sc[sq, h]
            n_prev = n_sc[sq, h:h + 1, :]
            num = (w_inter * scale) * jnp.dot(qh, c_prev.astype(BF16), preferred_element_type=F32) \
                + jnp.dot(s.astype(BF16), vb[:, sl], preferred_element_type=F32)
            qn = jnp.sum(q[:, sl] * n_prev, axis=-1, keepdims=True) * scale
            den = w_inter * qn + jnp.sum(s, axis=-1, keepdims=True)
            hh = num / jnp.maximum(jnp.abs(den), jnp.exp(-m_i))

            mu = jnp.mean(hh, axis=-1, keepdims=True)
            dev = hh - mu
            var = jnp.mean(dev * dev, axis=-1, keepdims=True)
            hn = dev * lax.rsqrt(var + EPS) * nw_ref[:, sl]
            ym_ref[sq, :, sl] = ((hn + sk_ref[:, sl] * xcb[:, sl].astype(F32)) * sz_ref[sq, :, sl]).astype(ym_ref.dtype)

            a_row = g_tot - b_row + li_row
            m_new = jnp.maximum(g_tot + m_prev, jnp.max(a_row, axis=-1, keepdims=True))
            a_col = g_tot - b_col + li_col
            w_state = jnp.exp(a_col - m_new)
            decay = jnp.exp(g_tot + m_prev - m_new)
            kw = k_[:, sl] * w_state
            c_sc[sq, h] = decay * c_prev + lax.dot_general(kw.astype(BF16), vb[:, sl], (((0,), (0,)), ((), ())),
                                                           preferred_element_type=F32)
            n_sc[sq, h:h + 1, :] = decay * n_prev + jnp.sum(kw, axis=0, keepdims=True)
            m_sc[sq, h:h + 1, :] = jnp.broadcast_to(m_new, (1, m_sc.shape[2]))


def _mlstm(xm, xc, sz, wq, wk, wv, wg, bg, norm_w, skip, bsz, seq):
    n, d = xm.shape
    L = MLSTM_CHUNK
    nc = seq // L
    dh = d // N_HEADS
    n_seq = 2 if bsz % 2 == 0 else 1
    blk = lambda b, j: (b, j, 0)
    c2 = lambda b, j: (0, 0)
    c3 = lambda b, j: (0, 0, 0)
    ym = pl.pallas_call(
        functools.partial(_mlstm_kernel, chunk=L, dh=dh, n_seq=n_seq),
        grid=(bsz // n_seq, nc),
        in_specs=[pl.BlockSpec((n_seq, L, d), blk), pl.BlockSpec((n_seq, L, d), blk), pl.BlockSpec((n_seq, L, d), blk),
                  pl.BlockSpec(wq.shape, c3), pl.BlockSpec(wk.shape, c3), pl.BlockSpec(wv.shape, c3),
                  pl.BlockSpec(wg.shape, c2), pl.BlockSpec(bg.shape, c2),
                  pl.BlockSpec(norm_w.shape, c2), pl.BlockSpec(skip.shape, c2)],
        out_specs=pl.BlockSpec((n_seq, L, d), blk),
        out_shape=jax.ShapeDtypeStruct((bsz, seq, d), BF16),
        scratch_shapes=[pltpu.VMEM((n_seq, N_HEADS, dh, dh), F32),
                        pltpu.VMEM((n_seq, SUBLANES, dh), F32),
                        pltpu.VMEM((n_seq, SUBLANES, LANES), F32)],
        compiler_params=_cparams("arbitrary", "arbitrary"),
        name="mlstm",
    )(xm.reshape(bsz, seq, d), xc.reshape(bsz, seq, d), sz.reshape(bsz, seq, d), wq, wk, wv, wg, bg, norm_w, skip)
    return ym.reshape(n, d)


def _conv_kernel(u_ref, w_ref, b_ref, nw_ref, nb_ref, yc_ref, ubuf, pbuf, cbuf, *, tile, rows):
    T = tile
    j = pl.program_id(1)

    @pl.when(j == 0)
    def _():
        ubuf[0:CONV_HALO, :] = jnp.zeros((CONV_HALO, ubuf.shape[1]), F32)

    ubuf[CONV_HALO:CONV_HALO + T, :] = u_ref[...]
    base = CONV_HALO - (CONV_WIDTH - 1)
    span = T + CONV_HALO - SUBLANES
    n_lane_blocks = ubuf.shape[1] // LANES

    def lane_block(c, carry):
        lanes = pl.ds(pl.multiple_of(c * LANES, LANES), LANES)
        for r in range(1, SUBLANES):
            pbuf[r - 1, :, :] = ubuf[r:r + span, lanes]
        for r0 in range(0, T, rows):
            acc = jnp.broadcast_to(b_ref[:, lanes], (rows, LANES))
            for k in range(CONV_WIDTH):
                q, r = divmod(base + k, SUBLANES)
                lo = r0 + q * SUBLANES
                src = ubuf[lo:lo + rows, lanes] if r == 0 else pbuf[r - 1, lo:lo + rows, :]
                acc = acc + w_ref[k:k + 1, lanes] * src
            cbuf[r0:r0 + rows, lanes] = acc
        return carry

    lax.fori_loop(0, n_lane_blocks, lane_block, 0)
    ubuf[0:CONV_HALO, :] = ubuf[T:T + CONV_HALO, :]

    y = cbuf[...]
    mu = jnp.mean(y, axis=-1, keepdims=True)
    dev = y - mu
    var = jnp.mean(dev * dev, axis=-1, keepdims=True)
    yn = dev * lax.rsqrt(var + EPS) * nw_ref[...] + nb_ref[...]
    yc_ref[...] = _silu(yn).astype(yc_ref.dtype)


def _conv_group(u, w, b, norm_w, norm_b, bsz, seq, tile):
    n, d = u.shape
    nt = seq // tile
    row = lambda bi, j: (bi * nt + j, 0)
    c2 = lambda bi, j: (0, 0)
    return pl.pallas_call(
        functools.partial(_conv_kernel, tile=tile, rows=64),
        grid=(bsz, nt),
        in_specs=[pl.BlockSpec((tile, d), row), pl.BlockSpec(w.shape, c2), pl.BlockSpec(b.shape, c2),
                  pl.BlockSpec(norm_w.shape, c2), pl.BlockSpec(norm_b.shape, c2)],
        out_specs=pl.BlockSpec((tile, d), row),
        out_shape=jax.ShapeDtypeStruct((n, d), BF16),
        scratch_shapes=[pltpu.VMEM((tile + CONV_HALO, d), F32),
                        pltpu.VMEM((SUBLANES - 1, tile + CONV_HALO - SUBLANES, LANES), F32),
                        pltpu.VMEM((tile, d), F32)],
        compiler_params=_cparams("arbitrary", "arbitrary"),
        name="conv_group",
    )(u, w, b, norm_w, norm_b)


def _outproj_kernel(x_ref, ym_ref, yc_ref, wo_ref, nw_ref, wr_ref, br_ref,
                    h1_ref, a2_ref, idx_ref, pos_ref, gate_ref, cnt_ref, cnt_sc, *, tm, d_mlstm):
    i = pl.program_id(0)

    @pl.when(i == 0)
    def _():
        cnt_sc[...] = jnp.zeros_like(cnt_sc)

    h1 = (x_ref[...]
          + jnp.dot(ym_ref[...], wo_ref[0:d_mlstm, :], preferred_element_type=F32)
          + jnp.dot(yc_ref[...], wo_ref[d_mlstm:, :], preferred_element_type=F32))
    h1_ref[...] = h1
    a2 = h1 * lax.rsqrt(jnp.mean(h1 * h1, axis=-1, keepdims=True) + EPS) * nw_ref[...]
    _store_row_slabs(a2_ref, 0, _pack_bf16_pairs(a2), a2.shape[1] // (2 * LANES))

    logits = lax.dot_general(wr_ref[...], a2, (((1,), (1,)), ((), ())),
                             precision=lax.Precision.HIGHEST, preferred_element_type=F32) + br_ref[...]
    e_iota = lax.broadcasted_iota(jnp.int32, logits.shape, 0)
    work = logits
    vals, idxs = [], []
    for _ in range(TOP_K):
        mx = jnp.max(work, axis=0, keepdims=True)
        sel = jnp.min(jnp.where(work == mx, e_iota, N_EXPERTS), axis=0, keepdims=True)
        vals.append(mx)
        idxs.append(sel)
        work = jnp.where(e_iota == sel, NEG_INF, work)
    exps = [jnp.exp(vv - vals[0]) for vv in vals]
    tot = exps[0] + exps[1] + exps[2] + exps[3]
    gates = [ev / tot for ev in exps]

    chosen = functools.reduce(jnp.logical_or, [e_iota == sel for sel in idxs])
    mh = jnp.where(chosen, 1.0, 0.0)
    ri = lax.broadcasted_iota(jnp.int32, (tm, tm), 0)
    ci = lax.broadcasted_iota(jnp.int32, (tm, tm), 1)
    upper = jnp.where(ri < ci, 1.0, 0.0).astype(BF16)
    rank = jnp.dot(mh.astype(BF16), upper, preferred_element_type=F32) + cnt_sc[:, 0:1]
    cnt_new = cnt_sc[...] + jnp.sum(mh, axis=1, keepdims=True)
    cnt_sc[...] = cnt_new
    cnt_ref[...] = cnt_new

    zero_i = jnp.zeros((SUBLANES - TOP_K, tm), jnp.int32)
    pos = [jnp.sum(jnp.where(e_iota == sel, rank, 0.0), axis=0, keepdims=True).astype(jnp.int32) for sel in idxs]
    idx_ref[...] = jnp.concatenate(idxs + [zero_i], axis=0)
    pos_ref[...] = jnp.concatenate(pos + [zero_i], axis=0)
    gate_ref[...] = jnp.concatenate(gates + [jnp.zeros((SUBLANES - TOP_K, tm), F32)], axis=0)


def _out_proj(xf, ym, yc, w_out_b, norm_w, w_router_t, b_router, d_mlstm, tm):
    n, d = xf.shape
    n_slabs = d // (2 * LANES)
    row = lambda i: (i, 0)
    colb = lambda i: (0, i)
    const = lambda i: (0, 0)
    return pl.pallas_call(
        functools.partial(_outproj_kernel, tm=tm, d_mlstm=d_mlstm),
        grid=(n // tm,),
        in_specs=[pl.BlockSpec((tm, d), row), pl.BlockSpec((tm, ym.shape[1]), row), pl.BlockSpec((tm, yc.shape[1]), row),
                  pl.BlockSpec(w_out_b.shape, const), pl.BlockSpec(norm_w.shape, const),
                  pl.BlockSpec(w_router_t.shape, const), pl.BlockSpec(b_router.shape, const)],
        out_specs=[pl.BlockSpec((tm, d), row),
                   pl.BlockSpec((tm * n_slabs, LANES), row),
                   pl.BlockSpec((SUBLANES, tm), colb),
                   pl.BlockSpec((SUBLANES, tm), colb),
                   pl.BlockSpec((SUBLANES, tm), colb),
                   pl.BlockSpec((N_EXPERTS, LANES), const)],
        out_shape=[jax.ShapeDtypeStruct((n, d), F32),
                   jax.ShapeDtypeStruct((n * n_slabs, LANES), jnp.uint32),
                   jax.ShapeDtypeStruct((SUBLANES, n), jnp.int32),
                   jax.ShapeDtypeStruct((SUBLANES, n), jnp.int32),
                   jax.ShapeDtypeStruct((SUBLANES, n), F32),
                   jax.ShapeDtypeStruct((N_EXPERTS, LANES), F32)],
        scratch_shapes=[pltpu.VMEM((N_EXPERTS, LANES), F32)],
        compiler_params=_cparams("arbitrary"),
        name="out_proj_router",
    )(xf, ym, yc, w_out_b, norm_w, w_router_t, b_router)


def _experts_kernel(blk_ref, exp_ref, valid_ref, lo_ref, hi_ref,
                    xs_ref, w1_ref, b1_ref, w2_ref, b2_ref, ys_ref, w1c_ref, w2c_ref, acc_ref, *, rb, d_ff, n_sub):
    i = pl.program_id(0)
    n_slabs = xs_ref.shape[0] // rb
    prev = jnp.maximum(i - 1, 0)
    first_visit = jnp.logical_or(i == 0, blk_ref[prev] != blk_ref[i])
    new_expert = jnp.logical_or(i == 0, exp_ref[prev] != exp_ref[i])

    @pl.when(new_expert)
    def _():
        w1c_ref[...] = w1_ref[0].astype(BF16)
        w2c_ref[...] = w2_ref[0].astype(BF16)

    @pl.when(first_visit)
    def _():
        acc_ref[...] = jnp.zeros_like(acc_ref)

    rs = rb // n_sub
    lo, hi = lo_ref[i], hi_ref[i]

    def sub_block(c):
        xa, xb = _unpack_bf16_pairs(_load_row_slabs(xs_ref, c * rs, rs, n_slabs))
        x = jnp.concatenate([xa.astype(BF16), xb.astype(BF16)], axis=-1)
        hid = jnp.dot(x, w1c_ref[...], preferred_element_type=F32) + b1_ref[0]
        x_glu = jnp.minimum(hid[:, :d_ff], SWIGLU_LIMIT)
        x_lin = jnp.clip(hid[:, d_ff:], -SWIGLU_LIMIT, SWIGLU_LIMIT)
        act = x_glu * _sigmoid(SWIGLU_ALPHA * x_glu) * (x_lin + 1.0)
        y = jnp.dot(act.astype(BF16), w2c_ref[...], preferred_element_type=F32) + b2_ref[0]
        r = c * rs + lax.broadcasted_iota(jnp.int32, (rs, 1), 0)
        y = jnp.where(jnp.logical_and(r >= lo, r < hi), y, 0.0)
        y = acc_ref[c * rs:(c + 1) * rs, :] + y
        acc_ref[c * rs:(c + 1) * rs, :] = y
        _store_row_slabs(ys_ref, c * rs, _pack_bf16_pairs(y), n_slabs)

    for c in range(n_sub):
        has_rows = jnp.logical_and(valid_ref[i] == 1, jnp.logical_and(lo < (c + 1) * rs, hi > c * rs))
        pl.when(has_rows)(functools.partial(sub_block, c))


def _experts(xs, w1, b1, w2, b2, item_blk, item_exp, item_valid, item_lo, item_hi, rb, n_slabs):
    n_items = item_blk.shape[0]
    d = w1.shape[1]
    d_ff = w2.shape[1]
    by_blk = lambda i, blk, ex, va, lo, hi: (blk[i], 0)
    by_exp = lambda i, blk, ex, va, lo, hi: (ex[i], 0, 0)
    grid_spec = pltpu.PrefetchScalarGridSpec(
        num_scalar_prefetch=5,
        grid=(n_items,),
        in_specs=[pl.BlockSpec((rb * n_slabs, LANES), by_blk),
                  pl.BlockSpec((1,) + w1.shape[1:], by_exp),
                  pl.BlockSpec((1,) + b1.shape[1:], by_exp),
                  pl.BlockSpec((1,) + w2.shape[1:], by_exp),
                  pl.BlockSpec((1,) + b2.shape[1:], by_exp)],
        out_specs=pl.BlockSpec((rb * n_slabs, LANES), by_blk),
        scratch_shapes=[pltpu.VMEM(w1.shape[1:], BF16), pltpu.VMEM(w2.shape[1:], BF16), pltpu.VMEM((rb, d), F32)],
    )
    return pl.pallas_call(
        functools.partial(_experts_kernel, rb=rb, d_ff=d_ff, n_sub=2),
        grid_spec=grid_spec,
        out_shape=jax.ShapeDtypeStruct(xs.shape, jnp.uint32),
        compiler_params=pltpu.CompilerParams(dimension_semantics=("arbitrary",), vmem_limit_bytes=EXPERTS_VMEM_LIMIT),
        name="experts",
    )(item_blk, item_exp, item_valid, item_lo, item_hi, xs, w1, b1, w2, b2)


SC_CORES = 2
SC_SUBCORES = 16
SC_GATHER_WINDOW = 64


def _sc_worker_base(per_worker):
    wid = lax.axis_index("s") * SC_CORES + lax.axis_index("c")
    return wid * per_worker


def _sc_gather_rows(table, idx):
    m = idx.shape[0]
    n_workers = SC_CORES * SC_SUBCORES
    window = SC_GATHER_WINDOW
    per_worker = m // n_workers
    n_win = per_worker // window
    assert per_worker * n_workers == m and n_win * window == per_worker and n_win % 2 == 0
    mesh = plsc.VectorSubcoreMesh(core_axis_name="c", subcore_axis_name="s")
    slab = table.shape[1:]

    @functools.partial(
        pl.kernel, mesh=mesh,
        out_type=jax.ShapeDtypeStruct((m,) + slab, table.dtype),
        scratch_types=[pltpu.VMEM((per_worker,), jnp.int32),
                       pltpu.VMEM((window,) + slab, table.dtype), pltpu.VMEM((window,) + slab, table.dtype)]
        + [pltpu.SemaphoreType.DMA] * 4,
        name="sc_gather_rows",
    )
    def gather(table_hbm, idx_hbm, out_hbm, idx_v, rows0, rows1, gsem0, gsem1, wsem0, wsem1):
        rows_v, gsem, wsem = (rows0, rows1), (gsem0, gsem1), (wsem0, wsem1)
        base = _sc_worker_base(per_worker)
        pltpu.sync_copy(idx_hbm.at[pl.ds(pl.multiple_of(base, window), per_worker)], idx_v)

        def gather_copy(w, b):
            ids = idx_v.at[pl.ds(pl.multiple_of(w * window, window), window)]
            return pltpu.make_async_copy(table_hbm.at[ids], rows_v[b], gsem[b])

        def write_copy(w, b):
            dst = out_hbm.at[pl.ds(pl.multiple_of(base + w * window, window), window)]
            return pltpu.make_async_copy(rows_v[b], dst, wsem[b])

        gather_copy(0, 0).start()

        @pl.loop(0, n_win, step=2)
        def _(w):
            @pl.when(w >= 1)
            def _():
                write_copy(w - 1, 1).wait()

            gather_copy(w + 1, 1).start()
            gather_copy(w, 0).wait()
            write_copy(w, 0).start()

            @pl.when(w + 2 < n_win)
            def _():
                write_copy(w, 0).wait()
                gather_copy(w + 2, 0).start()

            gather_copy(w + 1, 1).wait()
            write_copy(w + 1, 1).start()

        write_copy(n_win - 2, 0).wait()
        write_copy(n_win - 1, 1).wait()

    return gather(table, idx)


def _sc_scatter_rows(rows, dest, n_slots):
    n = rows.shape[0]
    n_workers = SC_CORES * SC_SUBCORES
    window = SC_GATHER_WINDOW
    per_worker = n // n_workers
    n_win = per_worker // window
    assert per_worker * n_workers == n and n_win * window == per_worker and n_win % 2 == 0
    mesh = plsc.VectorSubcoreMesh(core_axis_name="c", subcore_axis_name="s")
    slab = rows.shape[1:]

    @functools.partial(
        pl.kernel, mesh=mesh,
        out_type=jax.ShapeDtypeStruct((n_slots * n,) + slab, rows.dtype),
        scratch_types=[pltpu.VMEM((window,), jnp.int32)] * (2 * n_slots)
        + [pltpu.VMEM((window,) + slab, rows.dtype)] * 2 + [pltpu.SemaphoreType.DMA] * 4,
        name="sc_scatter_rows",
    )
    def scatter(rows_hbm, dest_hbm, out_hbm, *scratch):
        idx_v = (scratch[:n_slots], scratch[n_slots:2 * n_slots])
        rows_v = scratch[2 * n_slots:2 * n_slots + 2]
        rsem = scratch[2 * n_slots + 2:2 * n_slots + 4]
        wsem = scratch[2 * n_slots + 4:2 * n_slots + 6]
        base = _sc_worker_base(per_worker)

        def rows_at(w, k=0):
            return pl.ds(pl.multiple_of(k * n + base + w * window, window), window)

        def start_read(w, b):
            for k in range(n_slots):
                pltpu.sync_copy(dest_hbm.at[rows_at(w, k)], idx_v[b][k])
            pltpu.async_copy(rows_hbm.at[rows_at(w)], rows_v[b], rsem[b])

        def scatter_window(w, b):
            pltpu.make_async_copy(rows_hbm.at[rows_at(w)], rows_v[b], rsem[b]).wait()
            for k in range(n_slots):
                pltpu.async_copy(rows_v[b], out_hbm.at[idx_v[b][k]], wsem[b])
            for k in range(n_slots):
                pltpu.make_async_copy(rows_v[b], out_hbm.at[idx_v[b][k]], wsem[b]).wait()

        start_read(0, 0)

        @pl.loop(0, n_win, step=2)
        def _(w):
            start_read(w + 1, 1)
            scatter_window(w, 0)

            @pl.when(w + 2 < n_win)
            def _():
                start_read(w + 2, 0)

            scatter_window(w + 1, 1)

    return scatter(rows, dest)


def _final_kernel(h1_ref, gate_ref, yg_ref, nw_ref, out_ref, *, tm, n_slabs):
    gpad = jnp.concatenate([gate_ref[...], jnp.zeros((LANES - SUBLANES, tm), F32)], axis=0)
    gcol = gpad.T
    h2 = h1_ref[...]
    for k in range(TOP_K):
        ya, yb = _unpack_bf16_pairs(_load_row_slabs(yg_ref, 0, tm, n_slabs, lead=(k,)))
        h2 = h2 + gcol[:, k:k + 1] * jnp.concatenate([ya, yb], axis=-1)
    out_ref[...] = h2 * lax.rsqrt(jnp.mean(h2 * h2, axis=-1, keepdims=True) + EPS) * nw_ref[...]


def _final(h, gates_t, yg, final_w, n_slabs, tm, chunk, n_chunks):
    n, d = h.shape
    tiles = n // tm // n_chunks
    first = chunk * tiles
    return pl.pallas_call(
        functools.partial(_final_kernel, tm=tm, n_slabs=n_slabs),
        grid=(tiles,),
        in_specs=[pl.BlockSpec((tm, d), lambda i: (first + i, 0)),
                  pl.BlockSpec((SUBLANES, tm), lambda i: (0, first + i)),
                  pl.BlockSpec((TOP_K, tm * n_slabs, LANES), lambda i: (0, i, 0)),
                  pl.BlockSpec((1, d), lambda i: (0, 0))],
        out_specs=pl.BlockSpec((tm, d), lambda i: (first + i, 0)),
        out_shape=jax.ShapeDtypeStruct((n, d), F32),
        input_output_aliases={0: 0},
        compiler_params=_cparams("parallel"),
        name="final",
    )(h, gates_t, yg, final_w)


def _block_diag_tiles(w):
    nb, bs, _ = w.shape
    rows = jnp.tile(w.reshape(nb * bs // MXU_DIM, MXU_DIM, bs), (1, 1, MXU_DIM // bs))
    r_blk = lax.broadcasted_iota(jnp.int32, (MXU_DIM, MXU_DIM), 0) // bs
    c_blk = lax.broadcasted_iota(jnp.int32, (MXU_DIM, MXU_DIM), 1) // bs
    return jnp.where(r_blk == c_blk, rows, 0.0).astype(BF16)


def _layer(xf, bsz, seq, norm_mix_w, w_in, mlstm_conv_w, mlstm_conv_b, w_q, w_k, w_v, w_igate, b_igate,
           w_fgate, b_fgate, mlstm_norm_w, mlstm_skip, conv_dw_w, conv_dw_b, conv_norm_w, conv_norm_b,
           w_out, norm_ffn_w, w_router, b_router, w1, b1, w2, b2, final_norm_w):
    n, d = xf.shape
    d_mlstm = mlstm_norm_w.shape[0]
    d_conv = conv_norm_w.shape[0]
    n_slabs = d // (2 * LANES)
    r2 = lambda v: v.reshape(1, -1)

    xm, xc, sz, u = _in_proj(xf, r2(norm_mix_w), w_in.astype(BF16), mlstm_conv_w, r2(mlstm_conv_b),
                             d_mlstm, d_conv, seq, tm=256)

    wg = jnp.concatenate([w_igate, w_fgate], axis=1)
    wg = jnp.pad(wg, ((0, 0), (0, LANES - wg.shape[1]))).astype(BF16)
    bg = jnp.pad(jnp.concatenate([b_igate, b_fgate]), (0, LANES - 2 * N_HEADS)).reshape(1, LANES)
    ym = _mlstm(xm, xc, sz, _block_diag_tiles(w_q), _block_diag_tiles(w_k), _block_diag_tiles(w_v),
                wg, bg, r2(mlstm_norm_w), r2(mlstm_skip), bsz, seq)
    yc = _conv_group(u, conv_dw_w, r2(conv_dw_b), r2(conv_norm_w), r2(conv_norm_b), bsz, seq, tile=256)

    h1, a2s, idx_t, pos_t, gates_t, cnt = _out_proj(
        xf, ym, yc, w_out.astype(BF16), r2(norm_ffn_w), w_router.T, b_router.reshape(-1, 1), d_mlstm, tm=512)

    counts = cnt[:, 0].astype(jnp.int32)
    ends = jnp.cumsum(counts)
    starts = ends - counts
    e_ids = jnp.arange(N_EXPERTS, dtype=jnp.int32)
    idx4 = idx_t[:TOP_K]
    dest = pos_t[:TOP_K] + jnp.sum(
        jnp.where(idx4[None] == e_ids[:, None, None], starts[:, None, None], 0), axis=0)

    n_rows = n * TOP_K
    rb = 512
    n_blocks = n_rows // rb
    n_items = n_blocks + N_EXPERTS - 1
    first_blk = starts // rb
    last_blk = jnp.where(counts > 0, (ends - 1) // rb, first_blk - 1)
    per_e = last_blk - first_blk + 1
    item_end = jnp.cumsum(per_e)
    item_start = item_end - per_e
    ids = jnp.arange(n_items, dtype=jnp.int32)
    total = item_end[-1]
    item_valid = (ids < total).astype(jnp.int32)
    item_exp = jnp.minimum(jnp.sum((ids[:, None] >= item_end[None, :]).astype(jnp.int32), axis=1), N_EXPERTS - 1)
    last_valid_exp = jnp.max(jnp.where(per_e > 0, e_ids, 0))
    item_exp = jnp.where(item_valid == 1, item_exp, last_valid_exp).astype(jnp.int32)
    is_exp = item_exp[:, None] == e_ids[None, :]
    of_item = lambda table: jnp.sum(jnp.where(is_exp, table[None, :], 0), axis=1)
    item_blk = jnp.where(item_valid == 1, of_item(first_blk) + ids - of_item(item_start), n_blocks - 1)
    item_blk = item_blk.astype(jnp.int32)
    item_lo = (jnp.maximum(of_item(starts), item_blk * rb) - item_blk * rb).astype(jnp.int32)
    item_hi = (jnp.minimum(of_item(ends), (item_blk + 1) * rb) - item_blk * rb).astype(jnp.int32)

    xs = _sc_scatter_rows(a2s.reshape(n, n_slabs, LANES), dest.reshape(-1), TOP_K)
    ys = _experts(xs.reshape(n_rows * n_slabs, LANES), w1, b1[:, None, :], w2, b2[:, None, :],
                  item_blk, item_exp, item_valid, item_lo, item_hi, rb, n_slabs)
    ys = ys.reshape(n_rows, n_slabs, LANES)

    n_chunks = 8
    nc = n // n_chunks
    out = h1
    for c in range(n_chunks):
        yg = _sc_gather_rows(ys, dest[:, c * nc:(c + 1) * nc].reshape(-1))
        out = _final(out, gates_t, yg.reshape(TOP_K, nc * n_slabs, LANES), r2(final_norm_w), n_slabs,
                     tm=256, chunk=c, n_chunks=n_chunks)
    return out


def kernel(x, norm_mix_w, w_in, mlstm_conv_w, mlstm_conv_b, w_q, w_k, w_v, w_igate, b_igate, w_fgate, b_fgate,
           mlstm_norm_w, mlstm_skip, conv_dw_w, conv_dw_b, conv_norm_w, conv_norm_b, w_out, norm_ffn_w,
           w_router, b_router, w1, b1, w2, b2, final_norm_w):
    bsz, seq, d = x.shape
    assert norm_mix_w.shape[0] == 1, "single-layer block"
    out = _layer(x.reshape(bsz * seq, d), bsz, seq, norm_mix_w[0], w_in[0], mlstm_conv_w[0], mlstm_conv_b[0],
                 w_q[0], w_k[0], w_v[0], w_igate[0], b_igate[0], w_fgate[0], b_fgate[0], mlstm_norm_w[0],
                 mlstm_skip[0], conv_dw_w[0], conv_dw_b[0], conv_norm_w[0], conv_norm_b[0], w_out[0],
                 norm_ffn_w[0], w_router[0], b_router[0], w1[0], b1[0], w2[0], b2[0], final_norm_w)
    return out.reshape(bsz, seq, d)
```

```python
import functools

import jax
import jax.numpy as jnp
from jax import lax
from jax.experimental import pallas as pl
from jax.experimental.pallas import tpu as pltpu
from jax.experimental.pallas import tpu_sc as plsc

F32 = jnp.float32
BF16 = jnp.bfloat16

EPS = 1e-5
N_HEADS = 4
QKV_BLOCK = 4
MLSTM_CONV_WIDTH = 4
CONV_WIDTH = 31
N_EXPERTS = 32
TOP_K = 4
SWIGLU_ALPHA = 1.702
SWIGLU_LIMIT = 7.0

LANES = 128
SUBLANES = 8
MXU_DIM = 256
VMEM_LIMIT = 52 * 1024 * 1024
EXPERTS_VMEM_LIMIT = 58 * 1024 * 1024

MLSTM_CHUNK = 256
CONV_HALO = 32
NEG_INF = float("-inf")


def _sigmoid(x):
    return 1.0 / (1.0 + jnp.exp(-x))


def _silu(x):
    return x * _sigmoid(x)


def _pack_bf16_pairs(v):
    half = v.shape[1] // 2
    hi = lax.bitcast_convert_type(v[:, :half].astype(BF16).astype(F32), jnp.uint32)
    lo = lax.bitcast_convert_type(v[:, half:].astype(BF16).astype(F32), jnp.uint32)
    return hi | (lo >> 16)


def _unpack_bf16_pairs(w):
    hi = lax.bitcast_convert_type(w & jnp.uint32(0xFFFF0000), F32)
    lo = lax.bitcast_convert_type(w << 16, F32)
    return hi, lo


def _load_row_slabs(ref, first, rows, n_slabs, lead=()):
    return jnp.concatenate([ref[lead + (pl.ds(first * n_slabs + s, rows, stride=n_slabs), slice(None))]
                            for s in range(n_slabs)], axis=-1)


def _store_row_slabs(ref, first, v, n_slabs):
    rows = v.shape[0]
    for s in range(n_slabs):
        ref[pl.ds(first * n_slabs + s, rows, stride=n_slabs), :] = v[:, s * LANES:(s + 1) * LANES]


def _cparams(*sem):
    return pltpu.CompilerParams(dimension_semantics=sem, vmem_limit_bytes=VMEM_LIMIT)


def _inproj_kernel(x_ref, nw_ref, w_ref, cw_ref, cb_ref, xm_ref, xc_ref, sz_ref, u_ref, xbuf,
                   *, d_mlstm, d_conv, tiles_per_seq):
    tm = x_ref.shape[0]
    hist = SUBLANES
    i = pl.program_id(0)

    @pl.when(i == 0)
    def _():
        xbuf[...] = jnp.zeros_like(xbuf)

    x = x_ref[...]
    a = x * lax.rsqrt(jnp.mean(x * x, axis=-1, keepdims=True) + EPS) * nw_ref[...]
    ab = a.astype(BF16)
    same_seq = i % tiles_per_seq != 0
    cols = MXU_DIM

    for c in range(d_mlstm // cols):
        sl = slice(c * cols, (c + 1) * cols)
        prev = xbuf[hist:hist + tm, sl]
        acc = cb_ref[:, sl] + cw_ref[MLSTM_CONV_WIDTH - 1:MLSTM_CONV_WIDTH, sl] * prev
        for k in range(MLSTM_CONV_WIDTH - 1):
            off = hist - (MLSTM_CONV_WIDTH - 1) + k
            acc = acc + cw_ref[k:k + 1, sl] * xbuf[off:off + tm, sl]
        xc_ref[:, sl] = _silu(acc).astype(xc_ref.dtype)
        xbuf[0:hist, sl] = jnp.where(same_seq, prev[tm - hist:, :], 0.0)

        xm = jnp.dot(ab, w_ref[:, sl], preferred_element_type=F32)
        xm_ref[:, sl] = xm.astype(xm_ref.dtype)
        xbuf[hist:hist + tm, sl] = xm
        z = jnp.dot(ab, w_ref[:, d_mlstm + c * cols:d_mlstm + (c + 1) * cols], preferred_element_type=F32)
        sz_ref[:, sl] = _silu(z)

    for c in range(d_conv // cols):
        sl = slice(c * cols, (c + 1) * cols)
        ga = jnp.dot(ab, w_ref[:, 2 * d_mlstm + c * cols:2 * d_mlstm + (c + 1) * cols], preferred_element_type=F32)
        gb = jnp.dot(ab, w_ref[:, 2 * d_mlstm + d_conv + c * cols:2 * d_mlstm + d_conv + (c + 1) * cols],
                     preferred_element_type=F32)
        u_ref[:, sl] = ga * _sigmoid(gb)


def _in_proj(xf, norm_w, w_in_b, conv_w, conv_b, d_mlstm, d_conv, seq, tm):
    n, d = xf.shape
    n_tiles = n // tm
    cur = lambda i: (jnp.minimum(i, n_tiles - 1), 0)
    prv = lambda i: (jnp.maximum(i - 1, 0), 0)
    const = lambda i: (0, 0)
    return pl.pallas_call(
        functools.partial(_inproj_kernel, d_mlstm=d_mlstm, d_conv=d_conv, tiles_per_seq=seq // tm),
        grid=(n_tiles + 1,),
        in_specs=[pl.BlockSpec((tm, d), cur),
                  pl.BlockSpec((1, d), const),
                  pl.BlockSpec(w_in_b.shape, const),
                  pl.BlockSpec(conv_w.shape, const),
                  pl.BlockSpec(conv_b.shape, const)],
        out_specs=[pl.BlockSpec((tm, d_mlstm), cur),
                   pl.BlockSpec((tm, d_mlstm), prv),
                   pl.BlockSpec((tm, d_mlstm), cur),
                   pl.BlockSpec((tm, d_conv), cur)],
        out_shape=[jax.ShapeDtypeStruct((n, d_mlstm), BF16),
                   jax.ShapeDtypeStruct((n, d_mlstm), BF16),
                   jax.ShapeDtypeStruct((n, d_mlstm), F32),
                   jax.ShapeDtypeStruct((n, d_conv), F32)],
        scratch_shapes=[pltpu.VMEM((tm + SUBLANES, d_mlstm), F32)],
        compiler_params=_cparams("arbitrary"),
        name="in_proj",
    )(xf, norm_w, w_in_b, conv_w, conv_b)


def _split3(v):
    hi = v.astype(BF16)
    r1 = v - hi.astype(F32)
    mid = r1.astype(BF16)
    lo = (r1 - mid.astype(F32)).astype(BF16)
    return hi, mid, lo


def _mlstm_kernel(xm_ref, xc_ref, sz_ref, wq_ref, wk_ref, wv_ref, wg_ref, bg_ref, nw_ref, sk_ref,
                  ym_ref, c_sc, n_sc, m_sc, *, chunk, dh, n_seq):
    L = chunk
    nh = N_HEADS
    j = pl.program_id(1)

    @pl.when(j == 0)
    def _():
        c_sc[...] = jnp.zeros_like(c_sc)
        n_sc[...] = jnp.zeros_like(n_sc)
        m_sc[...] = jnp.zeros_like(m_sc)

    ri = lax.broadcasted_iota(jnp.int32, (L, L), 0)
    ci = lax.broadcasted_iota(jnp.int32, (L, L), 1)
    causal = ci <= ri
    tri = jnp.where(causal, 1.0, 0.0).astype(BF16)
    scale = dh ** -0.5

    for sq in range(n_seq):
        xmb = xm_ref[sq]
        xcb = xc_ref[sq]
        d = xmb.shape[1]
        nb = d // MXU_DIM

        def bd(xb, w_ref):
            return jnp.concatenate(
                [jnp.dot(xb[:, g * MXU_DIM:(g + 1) * MXU_DIM], w_ref[g], preferred_element_type=F32)
                 for g in range(nb)], axis=-1)

        q = bd(xcb, wq_ref)
        k_ = bd(xcb, wk_ref)
        v = bd(xmb, wv_ref)
        qb, kb, vb = q.astype(BF16), k_.astype(BF16), v.astype(BF16)

        g = (jnp.dot(qb, wg_ref[0:d, :], preferred_element_type=F32)
             + jnp.dot(kb, wg_ref[d:2 * d, :], preferred_element_type=F32)
             + jnp.dot(vb, wg_ref[2 * d:3 * d, :], preferred_element_type=F32)
             + bg_ref[...])
        col = lax.broadcasted_iota(jnp.int32, g.shape, 1)
        log_f = jnp.minimum(g, 0.0) - jnp.log(1.0 + jnp.exp(-jnp.abs(g)))
        gates = jnp.where(col < nh, g, jnp.where(col < 2 * nh, log_f, 0.0))
        cum = sum(jnp.dot(tri, part, preferred_element_type=F32) for part in _split3(gates))
        colform = jnp.where(col < nh, gates, cum)
        rowform = colform.T

        for h in range(nh):
            sl = slice(h * dh, (h + 1) * dh)
            li_row = rowform[h:h + 1, :]
            b_row = rowform[nh + h:nh + h + 1, :]
            li_col = colform[:, h:h + 1]
            b_col = colform[:, nh + h:nh + h + 1]
            m_prev = m_sc[sq, h:h + 1, 0:1]
            g_tot = b_row[:, L - 1:L]

            dmat = jnp.where(causal, b_col - b_row + li_row, NEG_INF)
            inter = b_col + m_prev
            m_i = jnp.maximum(inter, jnp.max(dmat, axis=-1, keepdims=True))
            w_intra = jnp.exp(dmat - m_i)
            w_inter = jnp.exp(inter - m_i)

            qh = qb[:, sl]
            s = lax.dot_general(qh, kb[:, sl], (((1,), (1,)), ((), ())),
                                preferred_element_type=F32) * (scale * w_intra)
            c_prev = c_sc[sq, h]
            n_prev = n_sc[sq, h:h + 1, :]
            num = (w_inter * scale) * jnp.dot(qh, c_prev.astype(BF16), preferred_element_type=F32) \
                + jnp.dot(s.astype(BF16), vb[:, sl], preferred_element_type=F32)
            qn = jnp.sum(q[:, sl] * n_prev, axis=-1, keepdims=True) * scale
            den = w_inter * qn + jnp.sum(s, axis=-1, keepdims=True)
            hh = num / jnp.maximum(jnp.abs(den), jnp.exp(-m_i))

            mu = jnp.mean(hh, axis=-1, keepdims=True)
            dev = hh - mu
            var = jnp.mean(dev * dev, axis=-1, keepdims=True)
            hn = dev * lax.rsqrt(var + EPS) * nw_ref[:, sl]
            ym_ref[sq, :, sl] = ((hn + sk_ref[:, sl] * xcb[:, sl].astype(F32)) * sz_ref[sq, :, sl]).astype(ym_ref.dtype)

            a_row = g_tot - b_row + li_row
            m_new = jnp.maximum(g_tot + m_prev, jnp.max(a_row, axis=-1, keepdims=True))
            a_col = g_tot - b_col + li_col
            w_state = jnp.exp(a_col - m_new)
            decay = jnp.exp(g_tot + m_prev - m_new)
            kw = k_[:, sl] * w_state
            c_sc[sq, h] = decay * c_prev + lax.dot_general(kw.astype(BF16), vb[:, sl], (((0,), (0,)), ((), ())),
                                                           preferred_element_type=F32)
            n_sc[sq, h:h + 1, :] = decay * n_prev + jnp.sum(kw, axis=0, keepdims=True)
            m_sc[sq, h:h + 1, :] = jnp.broadcast_to(m_new, (1, m_sc.shape[2]))


def _mlstm(xm, xc, sz, wq, wk, wv, wg, bg, norm_w, skip, bsz, seq):
    n, d = xm.shape
    L = MLSTM_CHUNK
    nc = seq // L
    dh = d // N_HEADS
    n_seq = 2 if bsz % 2 == 0 else 1
    blk = lambda b, j: (b, j, 0)
    c2 = lambda b, j: (0, 0)
    c3 = lambda b, j: (0, 0, 0)
    ym = pl.pallas_call(
        functools.partial(_mlstm_kernel, chunk=L, dh=dh, n_seq=n_seq),
        grid=(bsz // n_seq, nc),
        in_specs=[pl.BlockSpec((n_seq, L, d), blk), pl.BlockSpec((n_seq, L, d), blk), pl.BlockSpec((n_seq, L, d), blk),
                  pl.BlockSpec(wq.shape, c3), pl.BlockSpec(wk.shape, c3), pl.BlockSpec(wv.shape, c3),
                  pl.BlockSpec(wg.shape, c2), pl.BlockSpec(bg.shape, c2),
                  pl.BlockSpec(norm_w.shape, c2), pl.BlockSpec(skip.shape, c2)],
        out_specs=pl.BlockSpec((n_seq, L, d), blk),
        out_shape=jax.ShapeDtypeStruct((bsz, seq, d), BF16),
        scratch_shapes=[pltpu.VMEM((n_seq, N_HEADS, dh, dh), F32),
                        pltpu.VMEM((n_seq, SUBLANES, dh), F32),
                        pltpu.VMEM((n_seq, SUBLANES, LANES), F32)],
        compiler_params=_cparams("arbitrary", "arbitrary"),
        name="mlstm",
    )(xm.reshape(bsz, seq, d), xc.reshape(bsz, seq, d), sz.reshape(bsz, seq, d), wq, wk, wv, wg, bg, norm_w, skip)
    return ym.reshape(n, d)


def _conv_kernel(u_ref, w_ref, b_ref, nw_ref, nb_ref, yc_ref, ubuf, pbuf, cbuf, *, tile, rows):
    T = tile
    j = pl.program_id(1)

    @pl.when(j == 0)
    def _():
        ubuf[0:CONV_HALO, :] = jnp.zeros((CONV_HALO, ubuf.shape[1]), F32)

    ubuf[CONV_HALO:CONV_HALO + T, :] = u_ref[...]
    base = CONV_HALO - (CONV_WIDTH - 1)
    span = T + CONV_HALO - SUBLANES
    n_lane_blocks = ubuf.shape[1] // LANES

    def lane_block(c, carry):
        lanes = pl.ds(pl.multiple_of(c * LANES, LANES), LANES)
        for r in range(1, SUBLANES):
            pbuf[r - 1, :, :] = ubuf[r:r + span, lanes]
        for r0 in range(0, T, rows):
            acc = jnp.broadcast_to(b_ref[:, lanes], (rows, LANES))
            for k in range(CONV_WIDTH):
                q, r = divmod(base + k, SUBLANES)
                lo = r0 + q * SUBLANES
                src = ubuf[lo:lo + rows, lanes] if r == 0 else pbuf[r - 1, lo:lo + rows, :]
                acc = acc + w_ref[k:k + 1, lanes] * src
            cbuf[r0:r0 + rows, lanes] = acc
        return carry

    lax.fori_loop(0, n_lane_blocks, lane_block, 0)
    ubuf[0:CONV_HALO, :] = ubuf[T:T + CONV_HALO, :]

    y = cbuf[...]
    mu = jnp.mean(y, axis=-1, keepdims=True)
    dev = y - mu
    var = jnp.mean(dev * dev, axis=-1, keepdims=True)
    yn = dev * lax.rsqrt(var + EPS) * nw_ref[...] + nb_ref[...]
    yc_ref[...] = _silu(yn).astype(yc_ref.dtype)


def _conv_group(u, w, b, norm_w, norm_b, bsz, seq, tile):
    n, d = u.shape
    nt = seq // tile
    row = lambda bi, j: (bi * nt + j, 0)
    c2 = lambda bi, j: (0, 0)
    return pl.pallas_call(
        functools.partial(_conv_kernel, tile=tile, rows=64),
        grid=(bsz, nt),
        in_specs=[pl.BlockSpec((tile, d), row), pl.BlockSpec(w.shape, c2), pl.BlockSpec(b.shape, c2),
                  pl.BlockSpec(norm_w.shape, c2), pl.BlockSpec(norm_b.shape, c2)],
        out_specs=pl.BlockSpec((tile, d), row),
        out_shape=jax.ShapeDtypeStruct((n, d), BF16),
        scratch_shapes=[pltpu.VMEM((tile + CONV_HALO, d), F32),
                        pltpu.VMEM((SUBLANES - 1, tile + CONV_HALO - SUBLANES, LANES), F32),
                        pltpu.VMEM((tile, d), F32)],
        compiler_params=_cparams("arbitrary", "arbitrary"),
        name="conv_group",
    )(u, w, b, norm_w, norm_b)


def _outproj_kernel(x_ref, ym_ref, yc_ref, wo_ref, nw_ref, wr_ref, br_ref,
                    h1_ref, a2_ref, idx_ref, pos_ref, gate_ref, cnt_ref, cnt_sc, *, tm, d_mlstm):
    i = pl.program_id(0)

    @pl.when(i == 0)
    def _():
        cnt_sc[...] = jnp.zeros_like(cnt_sc)

    h1 = (x_ref[...]
          + jnp.dot(ym_ref[...], wo_ref[0:d_mlstm, :], preferred_element_type=F32)
          + jnp.dot(yc_ref[...], wo_ref[d_mlstm:, :], preferred_element_type=F32))
    h1_ref[...] = h1
    a2 = h1 * lax.rsqrt(jnp.mean(h1 * h1, axis=-1, keepdims=True) + EPS) * nw_ref[...]
    _store_row_slabs(a2_ref, 0, _pack_bf16_pairs(a2), a2.shape[1] // (2 * LANES))

    logits = lax.dot_general(wr_ref[...], a2, (((1,), (1,)), ((), ())),
                             precision=lax.Precision.HIGHEST, preferred_element_type=F32) + br_ref[...]
    e_iota = lax.broadcasted_iota(jnp.int32, logits.shape, 0)
    work = logits
    vals, idxs = [], []
    for _ in range(TOP_K):
        mx = jnp.max(work, axis=0, keepdims=True)
        sel = jnp.min(jnp.where(work == mx, e_iota, N_EXPERTS), axis=0, keepdims=True)
        vals.append(mx)
        idxs.append(sel)
        work = jnp.where(e_iota == sel, NEG_INF, work)
    exps = [jnp.exp(vv - vals[0]) for vv in vals]
    tot = exps[0] + exps[1] + exps[2] + exps[3]
    gates = [ev / tot for ev in exps]

    chosen = functools.reduce(jnp.logical_or, [e_iota == sel for sel in idxs])
    mh = jnp.where(chosen, 1.0, 0.0)
    ri = lax.broadcasted_iota(jnp.int32, (tm, tm), 0)
    ci = lax.broadcasted_iota(jnp.int32, (tm, tm), 1)
    upper = jnp.where(ri < ci, 1.0, 0.0).astype(BF16)
    rank = jnp.dot(mh.astype(BF16), upper, preferred_element_type=F32) + cnt_sc[:, 0:1]
    cnt_new = cnt_sc[...] + jnp.sum(mh, axis=1, keepdims=True)
    cnt_sc[...] = cnt_new
    cnt_ref[...] = cnt_new

    zero_i = jnp.zeros((SUBLANES - TOP_K, tm), jnp.int32)
    pos = [jnp.sum(jnp.where(e_iota == sel, rank, 0.0), axis=0, keepdims=True).astype(jnp.int32) for sel in idxs]
    idx_ref[...] = jnp.concatenate(idxs + [zero_i], axis=0)
    pos_ref[...] = jnp.concatenate(pos + [zero_i], axis=0)
    gate_ref[...] = jnp.concatenate(gates + [jnp.zeros((SUBLANES - TOP_K, tm), F32)], axis=0)


def _out_proj(xf, ym, yc, w_out_b, norm_w, w_router_t, b_router, d_mlstm, tm):
    n, d = xf.shape
    n_slabs = d // (2 * LANES)
    row = lambda i: (i, 0)
    colb = lambda i: (0, i)
    const = lambda i: (0, 0)
    return pl.pallas_call(
        functools.partial(_outproj_kernel, tm=tm, d_mlstm=d_mlstm),
        grid=(n // tm,),
        in_specs=[pl.BlockSpec((tm, d), row), pl.BlockSpec((tm, ym.shape[1]), row), pl.BlockSpec((tm, yc.shape[1]), row),
                  pl.BlockSpec(w_out_b.shape, const), pl.BlockSpec(norm_w.shape, const),
                  pl.BlockSpec(w_router_t.shape, const), pl.BlockSpec(b_router.shape, const)],
        out_specs=[pl.BlockSpec((tm, d), row),
                   pl.BlockSpec((tm * n_slabs, LANES), row),
                   pl.BlockSpec((SUBLANES, tm), colb),
                   pl.BlockSpec((SUBLANES, tm), colb),
                   pl.BlockSpec((SUBLANES, tm), colb),
                   pl.BlockSpec((N_EXPERTS, LANES), const)],
        out_shape=[jax.ShapeDtypeStruct((n, d), F32),
                   jax.ShapeDtypeStruct((n * n_slabs, LANES), jnp.uint32),
                   jax.ShapeDtypeStruct((SUBLANES, n), jnp.int32),
                   jax.ShapeDtypeStruct((SUBLANES, n), jnp.int32),
                   jax.ShapeDtypeStruct((SUBLANES, n), F32),
                   jax.ShapeDtypeStruct((N_EXPERTS, LANES), F32)],
        scratch_shapes=[pltpu.VMEM((N_EXPERTS, LANES), F32)],
        compiler_params=_cparams("arbitrary"),
        name="out_proj_router",
    )(xf, ym, yc, w_out_b, norm_w, w_router_t, b_router)


def _experts_kernel(blk_ref, exp_ref, valid_ref, lo_ref, hi_ref,
                    xs_ref, w1_ref, b1_ref, w2_ref, b2_ref, ys_ref, w1c_ref, w2c_ref, acc_ref, *, rb, d_ff, n_sub):
    i = pl.program_id(0)
    n_slabs = xs_ref.shape[0] // rb
    prev = jnp.maximum(i - 1, 0)
    first_visit = jnp.logical_or(i == 0, blk_ref[prev] != blk_ref[i])
    new_expert = jnp.logical_or(i == 0, exp_ref[prev] != exp_ref[i])

    @pl.when(new_expert)
    def _():
        w1c_ref[...] = w1_ref[0].astype(BF16)
        w2c_ref[...] = w2_ref[0].astype(BF16)

    @pl.when(first_visit)
    def _():
        acc_ref[...] = jnp.zeros_like(acc_ref)

    rs = rb // n_sub
    lo, hi = lo_ref[i], hi_ref[i]

    def sub_block(c):
        xa, xb = _unpack_bf16_pairs(_load_row_slabs(xs_ref, c * rs, rs, n_slabs))
        x = jnp.concatenate([xa.astype(BF16), xb.astype(BF16)], axis=-1)
        hid = jnp.dot(x, w1c_ref[...], preferred_element_type=F32) + b1_ref[0]
        x_glu = jnp.minimum(hid[:, :d_ff], SWIGLU_LIMIT)
        x_lin = jnp.clip(hid[:, d_ff:], -SWIGLU_LIMIT, SWIGLU_LIMIT)
        act = x_glu * _sigmoid(SWIGLU_ALPHA * x_glu) * (x_lin + 1.0)
        y = jnp.dot(act.astype(BF16), w2c_ref[...], preferred_element_type=F32) + b2_ref[0]
        r = c * rs + lax.broadcasted_iota(jnp.int32, (rs, 1), 0)
        y = jnp.where(jnp.logical_and(r >= lo, r < hi), y, 0.0)
        y = acc_ref[c * rs:(c + 1) * rs, :] + y
        acc_ref[c * rs:(c + 1) * rs, :] = y
        _store_row_slabs(ys_ref, c * rs, _pack_bf16_pairs(y), n_slabs)

    for c in range(n_sub):
        has_rows = jnp.logical_and(valid_ref[i] == 1, jnp.logical_and(lo < (c + 1) * rs, hi > c * rs))
        pl.when(has_rows)(functools.partial(sub_block, c))


def _experts(xs, w1, b1, w2, b2, item_blk, item_exp, item_valid, item_lo, item_hi, rb, n_slabs):
    n_items = item_blk.shape[0]
    d = w1.shape[1]
    d_ff = w2.shape[1]
    by_blk = lambda i, blk, ex, va, lo, hi: (blk[i], 0)
    by_exp = lambda i, blk, ex, va, lo, hi: (ex[i], 0, 0)
    grid_spec = pltpu.PrefetchScalarGridSpec(
        num_scalar_prefetch=5,
        grid=(n_items,),
        in_specs=[pl.BlockSpec((rb * n_slabs, LANES), by_blk),
                  pl.BlockSpec((1,) + w1.shape[1:], by_exp),
                  pl.BlockSpec((1,) + b1.shape[1:], by_exp),
                  pl.BlockSpec((1,) + w2.shape[1:], by_exp),
                  pl.BlockSpec((1,) + b2.shape[1:], by_exp)],
        out_specs=pl.BlockSpec((rb * n_slabs, LANES), by_blk),
        scratch_shapes=[pltpu.VMEM(w1.shape[1:], BF16), pltpu.VMEM(w2.shape[1:], BF16), pltpu.VMEM((rb, d), F32)],
    )
    return pl.pallas_call(
        functools.partial(_experts_kernel, rb=rb, d_ff=d_ff, n_sub=rb // MXU_DIM),
        grid_spec=grid_spec,
        out_shape=jax.ShapeDtypeStruct(xs.shape, jnp.uint32),
        compiler_params=pltpu.CompilerParams(dimension_semantics=("arbitrary",), vmem_limit_bytes=EXPERTS_VMEM_LIMIT),
        name="experts",
    )(item_blk, item_exp, item_valid, item_lo, item_hi, xs, w1, b1, w2, b2)


SC_CORES = 2
SC_SUBCORES = 16
SC_GATHER_WINDOW = 64


def _sc_worker_base(per_worker):
    wid = lax.axis_index("s") * SC_CORES + lax.axis_index("c")
    return wid * per_worker


def _sc_gather_rows(table, idx):
    m = idx.shape[0]
    n_workers = SC_CORES * SC_SUBCORES
    window = SC_GATHER_WINDOW
    per_worker = m // n_workers
    n_win = per_worker // window
    assert per_worker * n_workers == m and n_win * window == per_worker and n_win % 2 == 0
    mesh = plsc.VectorSubcoreMesh(core_axis_name="c", subcore_axis_name="s")
    slab = table.shape[1:]

    @functools.partial(
        pl.kernel, mesh=mesh,
        out_type=jax.ShapeDtypeStruct((m,) + slab, table.dtype),
        scratch_types=[pltpu.VMEM((per_worker,), jnp.int32),
                       pltpu.VMEM((window,) + slab, table.dtype), pltpu.VMEM((window,) + slab, table.dtype)]
        + [pltpu.SemaphoreType.DMA] * 4,
        name="sc_gather_rows",
    )
    def gather(table_hbm, idx_hbm, out_hbm, idx_v, rows0, rows1, gsem0, gsem1, wsem0, wsem1):
        rows_v, gsem, wsem = (rows0, rows1), (gsem0, gsem1), (wsem0, wsem1)
        base = _sc_worker_base(per_worker)
        pltpu.sync_copy(idx_hbm.at[pl.ds(pl.multiple_of(base, window), per_worker)], idx_v)

        def gather_copy(w, b):
            ids = idx_v.at[pl.ds(pl.multiple_of(w * window, window), window)]
            return pltpu.make_async_copy(table_hbm.at[ids], rows_v[b], gsem[b])

        def write_copy(w, b):
            dst = out_hbm.at[pl.ds(pl.multiple_of(base + w * window, window), window)]
            return pltpu.make_async_copy(rows_v[b], dst, wsem[b])

        gather_copy(0, 0).start()

        @pl.loop(0, n_win, step=2)
        def _(w):
            @pl.when(w >= 1)
            def _():
                write_copy(w - 1, 1).wait()

            gather_copy(w + 1, 1).start()
            gather_copy(w, 0).wait()
            write_copy(w, 0).start()

            @pl.when(w + 2 < n_win)
            def _():
                write_copy(w, 0).wait()
                gather_copy(w + 2, 0).start()

            gather_copy(w + 1, 1).wait()
            write_copy(w + 1, 1).start()

        write_copy(n_win - 2, 0).wait()
        write_copy(n_win - 1, 1).wait()

    return gather(table, idx)


def _sc_scatter_rows(rows, dest, n_slots):
    n = rows.shape[0]
    n_workers = SC_CORES * SC_SUBCORES
    window = SC_GATHER_WINDOW
    per_worker = n // n_workers
    n_win = per_worker // window
    assert per_worker * n_workers == n and n_win * window == per_worker and n_win % 2 == 0
    mesh = plsc.VectorSubcoreMesh(core_axis_name="c", subcore_axis_name="s")
    slab = rows.shape[1:]

    @functools.partial(
        pl.kernel, mesh=mesh,
        out_type=jax.ShapeDtypeStruct((n_slots * n,) + slab, rows.dtype),
        scratch_types=[pltpu.VMEM((window,), jnp.int32)] * (2 * n_slots)
        + [pltpu.VMEM((window,) + slab, rows.dtype)] * 2 + [pltpu.SemaphoreType.DMA] * 4,
        name="sc_scatter_rows",
    )
    def scatter(rows_hbm, dest_hbm, out_hbm, *scratch):
        idx_v = (scratch[:n_slots], scratch[n_slots:2 * n_slots])
        rows_v = scratch[2 * n_slots:2 * n_slots + 2]
        rsem = scratch[2 * n_slots + 2:2 * n_slots + 4]
        wsem = scratch[2 * n_slots + 4:2 * n_slots + 6]
        base = _sc_worker_base(per_worker)

        def rows_at(w, k=0):
            return pl.ds(pl.multiple_of(k * n + base + w * window, window), window)

        def start_read(w, b):
            for k in range(n_slots):
                pltpu.sync_copy(dest_hbm.at[rows_at(w, k)], idx_v[b][k])
            pltpu.async_copy(rows_hbm.at[rows_at(w)], rows_v[b], rsem[b])

        def scatter_window(w, b):
            pltpu.make_async_copy(rows_hbm.at[rows_at(w)], rows_v[b], rsem[b]).wait()
            for k in range(n_slots):
                pltpu.async_copy(rows_v[b], out_hbm.at[idx_v[b][k]], wsem[b])
            for k in range(n_slots):
                pltpu.make_async_copy(rows_v[b], out_hbm.at[idx_v[b][k]], wsem[b]).wait()

        start_read(0, 0)

        @pl.loop(0, n_win, step=2)
        def _(w):
            start_read(w + 1, 1)
            scatter_window(w, 0)

            @pl.when(w + 2 < n_win)
            def _():
                start_read(w + 2, 0)

            scatter_window(w + 1, 1)

    return scatter(rows, dest)


def _final_kernel(h1_ref, gate_ref, yg_ref, nw_ref, out_ref, *, tm, n_slabs):
    gpad = jnp.concatenate([gate_ref[...], jnp.zeros((LANES - SUBLANES, tm), F32)], axis=0)
    gcol = gpad.T
    h2 = h1_ref[...]
    for k in range(TOP_K):
        ya, yb = _unpack_bf16_pairs(_load_row_slabs(yg_ref, 0, tm, n_slabs, lead=(k,)))
        h2 = h2 + gcol[:, k:k + 1] * jnp.concatenate([ya, yb], axis=-1)
    out_ref[...] = h2 * lax.rsqrt(jnp.mean(h2 * h2, axis=-1, keepdims=True) + EPS) * nw_ref[...]


def _final(h, gates_t, yg, final_w, n_slabs, tm, chunk, n_chunks):
    n, d = h.shape
    tiles = n // tm // n_chunks
    first = chunk * tiles
    return pl.pallas_call(
        functools.partial(_final_kernel, tm=tm, n_slabs=n_slabs),
        grid=(tiles,),
        in_specs=[pl.BlockSpec((tm, d), lambda i: (first + i, 0)),
                  pl.BlockSpec((SUBLANES, tm), lambda i: (0, first + i)),
                  pl.BlockSpec((TOP_K, tm * n_slabs, LANES), lambda i: (0, i, 0)),
                  pl.BlockSpec((1, d), lambda i: (0, 0))],
        out_specs=pl.BlockSpec((tm, d), lambda i: (first + i, 0)),
        out_shape=jax.ShapeDtypeStruct((n, d), F32),
        input_output_aliases={0: 0},
        compiler_params=_cparams("parallel"),
        name="final",
    )(h, gates_t, yg, final_w)


def _block_diag_tiles(w):
    nb, bs, _ = w.shape
    rows = jnp.tile(w.reshape(nb * bs // MXU_DIM, MXU_DIM, bs), (1, 1, MXU_DIM // bs))
    r_blk = lax.broadcasted_iota(jnp.int32, (MXU_DIM, MXU_DIM), 0) // bs
    c_blk = lax.broadcasted_iota(jnp.int32, (MXU_DIM, MXU_DIM), 1) // bs
    return jnp.where(r_blk == c_blk, rows, 0.0).astype(BF16)


def _layer(xf, bsz, seq, norm_mix_w, w_in, mlstm_conv_w, mlstm_conv_b, w_q, w_k, w_v, w_igate, b_igate,
           w_fgate, b_fgate, mlstm_norm_w, mlstm_skip, conv_dw_w, conv_dw_b, conv_norm_w, conv_norm_b,
           w_out, norm_ffn_w, w_router, b_router, w1, b1, w2, b2, final_norm_w):
    n, d = xf.shape
    d_mlstm = mlstm_norm_w.shape[0]
    d_conv = conv_norm_w.shape[0]
    n_slabs = d // (2 * LANES)
    r2 = lambda v: v.reshape(1, -1)

    xm, xc, sz, u = _in_proj(xf, r2(norm_mix_w), w_in.astype(BF16), mlstm_conv_w, r2(mlstm_conv_b),
                             d_mlstm, d_conv, seq, tm=256)

    wg = jnp.concatenate([w_igate, w_fgate], axis=1)
    wg = jnp.pad(wg, ((0, 0), (0, LANES - wg.shape[1]))).astype(BF16)
    bg = jnp.pad(jnp.concatenate([b_igate, b_fgate]), (0, LANES - 2 * N_HEADS)).reshape(1, LANES)
    ym = _mlstm(xm, xc, sz, _block_diag_tiles(w_q), _block_diag_tiles(w_k), _block_diag_tiles(w_v),
                wg, bg, r2(mlstm_norm_w), r2(mlstm_skip), bsz, seq)
    yc = _conv_group(u, conv_dw_w, r2(conv_dw_b), r2(conv_norm_w), r2(conv_norm_b), bsz, seq, tile=256)

    h1, a2s, idx_t, pos_t, gates_t, cnt = _out_proj(
        xf, ym, yc, w_out.astype(BF16), r2(norm_ffn_w), w_router.T, b_router.reshape(-1, 1), d_mlstm, tm=512)

    counts = cnt[:, 0].astype(jnp.int32)
    ends = jnp.cumsum(counts)
    starts = ends - counts
    e_ids = jnp.arange(N_EXPERTS, dtype=jnp.int32)
    idx4 = idx_t[:TOP_K]
    dest = pos_t[:TOP_K] + jnp.sum(
        jnp.where(idx4[None] == e_ids[:, None, None], starts[:, None, None], 0), axis=0)

    n_rows = n * TOP_K
    rb = 1024
    n_blocks = n_rows // rb
    n_items = n_blocks + N_EXPERTS - 1
    first_blk = starts // rb
    last_blk = jnp.where(counts > 0, (ends - 1) // rb, first_blk - 1)
    per_e = last_blk - first_blk + 1
    item_end = jnp.cumsum(per_e)
    item_start = item_end - per_e
    ids = jnp.arange(n_items, dtype=jnp.int32)
    total = item_end[-1]
    item_valid = (ids < total).astype(jnp.int32)
    item_exp = jnp.minimum(jnp.sum((ids[:, None] >= item_end[None, :]).astype(jnp.int32), axis=1), N_EXPERTS - 1)
    last_valid_exp = jnp.max(jnp.where(per_e > 0, e_ids, 0))
    item_exp = jnp.where(item_valid == 1, item_exp, last_valid_exp).astype(jnp.int32)
    is_exp = item_exp[:, None] == e_ids[None, :]
    of_item = lambda table: jnp.sum(jnp.where(is_exp, table[None, :], 0), axis=1)
    item_blk = jnp.where(item_valid == 1, of_item(first_blk) + ids - of_item(item_start), n_blocks - 1)
    item_blk = item_blk.astype(jnp.int32)
    item_lo = (jnp.maximum(of_item(starts), item_blk * rb) - item_blk * rb).astype(jnp.int32)
    item_hi = (jnp.minimum(of_item(ends), (item_blk + 1) * rb) - item_blk * rb).astype(jnp.int32)

    xs = _sc_scatter_rows(a2s.reshape(n, n_slabs, LANES), dest.reshape(-1), TOP_K)
    ys = _experts(xs.reshape(n_rows * n_slabs, LANES), w1, b1[:, None, :], w2, b2[:, None, :],
                  item_blk, item_exp, item_valid, item_lo, item_hi, rb, n_slabs)
    ys = ys.reshape(n_rows, n_slabs, LANES)

    n_chunks = 8
    nc = n // n_chunks
    out = h1
    for c in range(n_chunks):
        yg = _sc_gather_rows(ys, dest[:, c * nc:(c + 1) * nc].reshape(-1))
        out = _final(out, gates_t, yg.reshape(TOP_K, nc * n_slabs, LANES), r2(final_norm_w), n_slabs,
                     tm=256, chunk=c, n_chunks=n_chunks)
    return out


def kernel(x, norm_mix_w, w_in, mlstm_conv_w, mlstm_conv_b, w_q, w_k, w_v, w_igate, b_igate, w_fgate, b_fgate,
           mlstm_norm_w, mlstm_skip, conv_dw_w, conv_dw_b, conv_norm_w, conv_norm_b, w_out, norm_ffn_w,
           w_router, b_router, w1, b1, w2, b2, final_norm_w):
    bsz, seq, d = x.shape
    assert norm_mix_w.shape[0] == 1, "single-layer block"
    out = _layer(x.reshape(bsz * seq, d), bsz, seq, norm_mix_w[0], w_in[0], mlstm_conv_w[0], mlstm_conv_b[0],
                 w_q[0], w_k[0], w_v[0], w_igate[0], b_igate[0], w_fgate[0], b_fgate[0], mlstm_norm_w[0],
                 mlstm_skip[0], conv_dw_w[0], conv_dw_b[0], conv_norm_w[0], conv_norm_b[0], w_out[0],
                 norm_ffn_w[0], w_router[0], b_router[0], w1[0], b1[0], w2[0], b2[0], final_norm_w)
    return out.reshape(bsz, seq, d)
```

```python
import functools

import jax
import jax.numpy as jnp
from jax import lax
from jax.experimental import pallas as pl
from jax.experimental.pallas import tpu as pltpu
from jax.experimental.pallas import tpu_sc as plsc

F32 = jnp.float32
BF16 = jnp.bfloat16

EPS = 1e-5
N_HEADS = 4
QKV_BLOCK = 4
MLSTM_CONV_WIDTH = 4
CONV_WIDTH = 31
N_EXPERTS = 32
TOP_K = 4
SWIGLU_ALPHA = 1.702
SWIGLU_LIMIT = 7.0

LANES = 128
SUBLANES = 8
MXU_DIM = 256
VMEM_LIMIT = 52 * 1024 * 1024
EXPERTS_VMEM_LIMIT = 58 * 1024 * 1024

MLSTM_CHUNK = 256
CONV_HALO = 32
NEG_INF = float("-inf")


def _sigmoid(x):
    return 1.0 / (1.0 + jnp.exp(-x))


def _silu(x):
    return x * _sigmoid(x)


def _pack_bf16_pairs(v):
    half = v.shape[1] // 2
    hi = lax.bitcast_convert_type(v[:, :half].astype(BF16).astype(F32), jnp.uint32)
    lo = lax.bitcast_convert_type(v[:, half:].astype(BF16).astype(F32), jnp.uint32)
    return hi | (lo >> 16)


def _unpack_bf16_pairs(w):
    hi = lax.bitcast_convert_type(w & jnp.uint32(0xFFFF0000), F32)
    lo = lax.bitcast_convert_type(w << 16, F32)
    return hi, lo


def _load_row_slabs(ref, first, rows, n_slabs, lead=()):
    return jnp.concatenate([ref[lead + (pl.ds(first * n_slabs + s, rows, stride=n_slabs), slice(None))]
                            for s in range(n_slabs)], axis=-1)


def _store_row_slabs(ref, first, v, n_slabs):
    rows = v.shape[0]
    for s in range(n_slabs):
        ref[pl.ds(first * n_slabs + s, rows, stride=n_slabs), :] = v[:, s * LANES:(s + 1) * LANES]


def _cparams(*sem):
    return pltpu.CompilerParams(dimension_semantics=sem, vmem_limit_bytes=VMEM_LIMIT)


def _inproj_kernel(x_ref, nw_ref, w_ref, cw_ref, cb_ref, xm_ref, xc_ref, sz_ref, u_ref, xbuf,
                   *, d_mlstm, d_conv, tiles_per_seq):
    tm = x_ref.shape[0]
    hist = SUBLANES
    i = pl.program_id(0)

    @pl.when(i == 0)
    def _():
        xbuf[...] = jnp.zeros_like(xbuf)

    x = x_ref[...]
    a = x * lax.rsqrt(jnp.mean(x * x, axis=-1, keepdims=True) + EPS) * nw_ref[...]
    ab = a.astype(BF16)
    same_seq = i % tiles_per_seq != 0
    cols = MXU_DIM

    for c in range(d_mlstm // cols):
        sl = slice(c * cols, (c + 1) * cols)
        prev = xbuf[hist:hist + tm, sl]
        acc = cb_ref[:, sl] + cw_ref[MLSTM_CONV_WIDTH - 1:MLSTM_CONV_WIDTH, sl] * prev
        for k in range(MLSTM_CONV_WIDTH - 1):
            off = hist - (MLSTM_CONV_WIDTH - 1) + k
            acc = acc + cw_ref[k:k + 1, sl] * xbuf[off:off + tm, sl]
        xc_ref[:, sl] = _silu(acc).astype(xc_ref.dtype)
        xbuf[0:hist, sl] = jnp.where(same_seq, prev[tm - hist:, :], 0.0)

        xm = jnp.dot(ab, w_ref[:, sl], preferred_element_type=F32)
        xm_ref[:, sl] = xm.astype(xm_ref.dtype)
        xbuf[hist:hist + tm, sl] = xm
        z = jnp.dot(ab, w_ref[:, d_mlstm + c * cols:d_mlstm + (c + 1) * cols], preferred_element_type=F32)
        sz_ref[:, sl] = _silu(z)

    for c in range(d_conv // cols):
        sl = slice(c * cols, (c + 1) * cols)
        ga = jnp.dot(ab, w_ref[:, 2 * d_mlstm + c * cols:2 * d_mlstm + (c + 1) * cols], preferred_element_type=F32)
        gb = jnp.dot(ab, w_ref[:, 2 * d_mlstm + d_conv + c * cols:2 * d_mlstm + d_conv + (c + 1) * cols],
                     preferred_element_type=F32)
        u_ref[:, sl] = ga * _sigmoid(gb)


def _in_proj(xf, norm_w, w_in_b, conv_w, conv_b, d_mlstm, d_conv, seq, tm):
    n, d = xf.shape
    n_tiles = n // tm
    cur = lambda i: (jnp.minimum(i, n_tiles - 1), 0)
    prv = lambda i: (jnp.maximum(i - 1, 0), 0)
    const = lambda i: (0, 0)
    return pl.pallas_call(
        functools.partial(_inproj_kernel, d_mlstm=d_mlstm, d_conv=d_conv, tiles_per_seq=seq // tm),
        grid=(n_tiles + 1,),
        in_specs=[pl.BlockSpec((tm, d), cur),
                  pl.BlockSpec((1, d), const),
                  pl.BlockSpec(w_in_b.shape, const),
                  pl.BlockSpec(conv_w.shape, const),
                  pl.BlockSpec(conv_b.shape, const)],
        out_specs=[pl.BlockSpec((tm, d_mlstm), cur),
                   pl.BlockSpec((tm, d_mlstm), prv),
                   pl.BlockSpec((tm, d_mlstm), cur),
                   pl.BlockSpec((tm, d_conv), cur)],
        out_shape=[jax.ShapeDtypeStruct((n, d_mlstm), BF16),
                   jax.ShapeDtypeStruct((n, d_mlstm), BF16),
                   jax.ShapeDtypeStruct((n, d_mlstm), F32),
                   jax.ShapeDtypeStruct((n, d_conv), F32)],
        scratch_shapes=[pltpu.VMEM((tm + SUBLANES, d_mlstm), F32)],
        compiler_params=_cparams("arbitrary"),
        name="in_proj",
    )(xf, norm_w, w_in_b, conv_w, conv_b)


def _split3(v):
    hi = v.astype(BF16)
    r1 = v - hi.astype(F32)
    mid = r1.astype(BF16)
    lo = (r1 - mid.astype(F32)).astype(BF16)
    return hi, mid, lo


def _mlstm_kernel(xm_ref, xc_ref, sz_ref, wq_ref, wk_ref, wv_ref, wg_ref, bg_ref, nw_ref, sk_ref,
                  ym_ref, c_sc, n_sc, m_sc, *, chunk, dh, n_seq):
    L = chunk
    nh = N_HEADS
    j = pl.program_id(1)

    @pl.when(j == 0)
    def _():
        c_sc[...] = jnp.zeros_like(c_sc)
        n_sc[...] = jnp.zeros_like(n_sc)
        m_sc[...] = jnp.zeros_like(m_sc)

    ri = lax.broadcasted_iota(jnp.int32, (L, L), 0)
    ci = lax.broadcasted_iota(jnp.int32, (L, L), 1)
    causal = ci <= ri
    tri = jnp.where(causal, 1.0, 0.0).astype(BF16)
    scale = dh ** -0.5

    for sq in range(n_seq):
        xmb = xm_ref[sq]
        xcb = xc_ref[sq]
        d = xmb.shape[1]
        nb = d // MXU_DIM

        def bd(xb, w_ref):
            return jnp.concatenate(
                [jnp.dot(xb[:, g * MXU_DIM:(g + 1) * MXU_DIM], w_ref[g], preferred_element_type=F32)
                 for g in range(nb)], axis=-1)

        q = bd(xcb, wq_ref)
        k_ = bd(xcb, wk_ref)
        v = bd(xmb, wv_ref)
        qb, kb, vb = q.astype(BF16), k_.astype(BF16), v.astype(BF16)

        g = (jnp.dot(qb, wg_ref[0:d, :], preferred_element_type=F32)
             + jnp.dot(kb, wg_ref[d:2 * d, :], preferred_element_type=F32)
             + jnp.dot(vb, wg_ref[2 * d:3 * d, :], preferred_element_type=F32)
             + bg_ref[...])
        col = lax.broadcasted_iota(jnp.int32, g.shape, 1)
        log_f = jnp.minimum(g, 0.0) - jnp.log(1.0 + jnp.exp(-jnp.abs(g)))
        gates = jnp.where(col < nh, g, jnp.where(col < 2 * nh, log_f, 0.0))
        cum = sum(jnp.dot(tri, part, preferred_element_type=F32) for part in _split3(gates))
        colform = jnp.where(col < nh, gates, cum)
        rowform = colform.T

        for h in range(nh):
            sl = slice(h * dh, (h + 1) * dh)
            li_row = rowform[h:h + 1, :]
            b_row = rowform[nh + h:nh + h + 1, :]
            li_col = colform[:, h:h + 1]
            b_col = colform[:, nh + h:nh + h + 1]
            m_prev = m_sc[sq, h:h + 1, 0:1]
            g_tot = b_row[:, L - 1:L]

            dmat = jnp.where(causal, b_col - b_row + li_row, NEG_INF)
            inter = b_col + m_prev
            m_i = jnp.maximum(inter, jnp.max(dmat, axis=-1, keepdims=True))
            w_intra = jnp.exp(dmat - m_i)
            w_inter = jnp.exp(inter - m_i)

            qh = qb[:, sl]
            s = lax.dot_general(qh, kb[:, sl], (((1,), (1,)), ((), ())),
                                preferred_element_type=F32) * (scale * w_intra)
            c_prev = c_sc[sq, h]
            n_prev = n_sc[sq, h:h + 1, :]
            num = (w_inter * scale) * jnp.dot(qh, c_prev.astype(BF16), preferred_element_type=F32) \
                + jnp.dot(s.astype(BF16), vb[:, sl], preferred_element_type=F32)
            qn = jnp.sum(q[:, sl] * n_prev, axis=-1, keepdims=True) * scale
            den = w_inter * qn + jnp.sum(s, axis=-1, keepdims=True)
            hh = num / jnp.maximum(jnp.abs(den), jnp.exp(-m_i))

            mu = jnp.mean(hh, axis=-1, keepdims=True)
            dev = hh - mu
            var = jnp.mean(dev * dev, axis=-1, keepdims=True)
            hn = dev * lax.rsqrt(var + EPS) * nw_ref[:, sl]
            ym_ref[sq, :, sl] = ((hn + sk_ref[:, sl] * xcb[:, sl].astype(F32)) * sz_ref[sq, :, sl]).astype(ym_ref.dtype)

            a_row = g_tot - b_row + li_row
            m_new = jnp.maximum(g_tot + m_prev, jnp.max(a_row, axis=-1, keepdims=True))
            a_col = g_tot - b_col + li_col
            w_state = jnp.exp(a_col - m_new)
            decay = jnp.exp(g_tot + m_prev - m_new)
            kw = k_[:, sl] * w_state
            c_sc[sq, h] = decay * c_prev + lax.dot_general(kw.astype(BF16), vb[:, sl], (((0,), (0,)), ((), ())),
                                                           preferred_element_type=F32)
            n_sc[sq, h:h + 1, :] = decay * n_prev + jnp.sum(kw, axis=0, keepdims=True)
            m_sc[sq, h:h + 1, :] = jnp.broadcast_to(m_new, (1, m_sc.shape[2]))


def _mlstm(xm, xc, sz, wq, wk, wv, wg, bg, norm_w, skip, bsz, seq):
    n, d = xm.shape
    L = MLSTM_CHUNK
    nc = seq // L
    dh = d // N_HEADS
    n_seq = 2 if bsz % 2 == 0 else 1
    blk = lambda b, j: (b, j, 0)
    c2 = lambda b, j: (0, 0)
    c3 = lambda b, j: (0, 0, 0)
    ym = pl.pallas_call(
        functools.partial(_mlstm_kernel, chunk=L, dh=dh, n_seq=n_seq),
        grid=(bsz // n_seq, nc),
        in_specs=[pl.BlockSpec((n_seq, L, d), blk), pl.BlockSpec((n_seq, L, d), blk), pl.BlockSpec((n_seq, L, d), blk),
                  pl.BlockSpec(wq.shape, c3), pl.BlockSpec(wk.shape, c3), pl.BlockSpec(wv.shape, c3),
                  pl.BlockSpec(wg.shape, c2), pl.BlockSpec(bg.shape, c2),
                  pl.BlockSpec(norm_w.shape, c2), pl.BlockSpec(skip.shape, c2)],
        out_specs=pl.BlockSpec((n_seq, L, d), blk),
        out_shape=jax.ShapeDtypeStruct((bsz, seq, d), BF16),
        scratch_shapes=[pltpu.VMEM((n_seq, N_HEADS, dh, dh), F32),
                        pltpu.VMEM((n_seq, SUBLANES, dh), F32),
                        pltpu.VMEM((n_seq, SUBLANES, LANES), F32)],
        compiler_params=_cparams("arbitrary", "arbitrary"),
        name="mlstm",
    )(xm.reshape(bsz, seq, d), xc.reshape(bsz, seq, d), sz.reshape(bsz, seq, d), wq, wk, wv, wg, bg, norm_w, skip)
    return ym.reshape(n, d)


def _conv_kernel(u_ref, w_ref, b_ref, nw_ref, nb_ref, yc_ref, ubuf, pbuf, cbuf, *, tile, rows):
    T = tile
    j = pl.program_id(1)

    @pl.when(j == 0)
    def _():
        ubuf[0:CONV_HALO, :] = jnp.zeros((CONV_HALO, ubuf.shape[1]), F32)

    ubuf[CONV_HALO:CONV_HALO + T, :] = u_ref[...]
    base = CONV_HALO - (CONV_WIDTH - 1)
    span = T + CONV_HALO - SUBLANES
    n_lane_blocks = ubuf.shape[1] // LANES

    def lane_block(c, carry):
        lanes = pl.ds(pl.multiple_of(c * LANES, LANES), LANES)
        for r in range(1, SUBLANES):
            pbuf[r - 1, :, :] = ubuf[r:r + span, lanes]
        for r0 in range(0, T, rows):
            acc = jnp.broadcast_to(b_ref[:, lanes], (rows, LANES))
            for k in range(CONV_WIDTH):
                q, r = divmod(base + k, SUBLANES)
                lo = r0 + q * SUBLANES
                src = ubuf[lo:lo + rows, lanes] if r == 0 else pbuf[r - 1, lo:lo + rows, :]
                acc = acc + w_ref[k:k + 1, lanes] * src
            cbuf[r0:r0 + rows, lanes] = acc
        return carry

    lax.fori_loop(0, n_lane_blocks, lane_block, 0)
    ubuf[0:CONV_HALO, :] = ubuf[T:T + CONV_HALO, :]

    y = cbuf[...]
    mu = jnp.mean(y, axis=-1, keepdims=True)
    dev = y - mu
    var = jnp.mean(dev * dev, axis=-1, keepdims=True)
    yn = dev * lax.rsqrt(var + EPS) * nw_ref[...] + nb_ref[...]
    yc_ref[...] = _silu(yn).astype(yc_ref.dtype)


def _conv_group(u, w, b, norm_w, norm_b, bsz, seq, tile):
    n, d = u.shape
    nt = seq // tile
    row = lambda bi, j: (bi * nt + j, 0)
    c2 = lambda bi, j: (0, 0)
    return pl.pallas_call(
        functools.partial(_conv_kernel, tile=tile, rows=64),
        grid=(bsz, nt),
        in_specs=[pl.BlockSpec((tile, d), row), pl.BlockSpec(w.shape, c2), pl.BlockSpec(b.shape, c2),
                  pl.BlockSpec(norm_w.shape, c2), pl.BlockSpec(norm_b.shape, c2)],
        out_specs=pl.BlockSpec((tile, d), row),
        out_shape=jax.ShapeDtypeStruct((n, d), BF16),
        scratch_shapes=[pltpu.VMEM((tile + CONV_HALO, d), F32),
                        pltpu.VMEM((SUBLANES - 1, tile + CONV_HALO - SUBLANES, LANES), F32),
                        pltpu.VMEM((tile, d), F32)],
        compiler_params=_cparams("arbitrary", "arbitrary"),
        name="conv_group",
    )(u, w, b, norm_w, norm_b)


def _outproj_kernel(x_ref, ym_ref, yc_ref, wo_ref, nw_ref, wr_ref, br_ref,
                    h1_ref, a2_ref, idx_ref, pos_ref, gate_ref, cnt_ref, cnt_sc, *, tm, d_mlstm):
    i = pl.program_id(0)

    @pl.when(i == 0)
    def _():
        cnt_sc[...] = jnp.zeros_like(cnt_sc)

    h1 = (x_ref[...]
          + jnp.dot(ym_ref[...], wo_ref[0:d_mlstm, :], preferred_element_type=F32)
          + jnp.dot(yc_ref[...], wo_ref[d_mlstm:, :], preferred_element_type=F32))
    h1_ref[...] = h1
    a2 = h1 * lax.rsqrt(jnp.mean(h1 * h1, axis=-1, keepdims=True) + EPS) * nw_ref[...]
    _store_row_slabs(a2_ref, 0, _pack_bf16_pairs(a2), a2.shape[1] // (2 * LANES))

    logits = lax.dot_general(wr_ref[...], a2, (((1,), (1,)), ((), ())),
                             precision=lax.Precision.HIGHEST, preferred_element_type=F32) + br_ref[...]
    e_iota = lax.broadcasted_iota(jnp.int32, logits.shape, 0)
    work = logits
    vals, idxs = [], []
    for _ in range(TOP_K):
        mx = jnp.max(work, axis=0, keepdims=True)
        sel = jnp.min(jnp.where(work == mx, e_iota, N_EXPERTS), axis=0, keepdims=True)
        vals.append(mx)
        idxs.append(sel)
        work = jnp.where(e_iota == sel, NEG_INF, work)
    exps = [jnp.exp(vv - vals[0]) for vv in vals]
    tot = exps[0] + exps[1] + exps[2] + exps[3]
    gates = [ev / tot for ev in exps]

    chosen = functools.reduce(jnp.logical_or, [e_iota == sel for sel in idxs])
    mh = jnp.where(chosen, 1.0, 0.0)
    ri = lax.broadcasted_iota(jnp.int32, (tm, tm), 0)
    ci = lax.broadcasted_iota(jnp.int32, (tm, tm), 1)
    upper = jnp.where(ri < ci, 1.0, 0.0).astype(BF16)
    rank = jnp.dot(mh.astype(BF16), upper, preferred_element_type=F32) + cnt_sc[:, 0:1]
    cnt_new = cnt_sc[...] + jnp.sum(mh, axis=1, keepdims=True)
    cnt_sc[...] = cnt_new
    cnt_ref[...] = cnt_new

    zero_i = jnp.zeros((SUBLANES - TOP_K, tm), jnp.int32)
    pos = [jnp.sum(jnp.where(e_iota == sel, rank, 0.0), axis=0, keepdims=True).astype(jnp.int32) for sel in idxs]
    idx_ref[...] = jnp.concatenate(idxs + [zero_i], axis=0)
    pos_ref[...] = jnp.concatenate(pos + [zero_i], axis=0)
    gate_ref[...] = jnp.concatenate(gates + [jnp.zeros((SUBLANES - TOP_K, tm), F32)], axis=0)


def _out_proj(xf, ym, yc, w_out_b, norm_w, w_router_t, b_router, d_mlstm, tm):
    n, d = xf.shape
    n_slabs = d // (2 * LANES)
    row = lambda i: (i, 0)
    colb = lambda i: (0, i)
    const = lambda i: (0, 0)
    return pl.pallas_call(
        functools.partial(_outproj_kernel, tm=tm, d_mlstm=d_mlstm),
        grid=(n // tm,),
        in_specs=[pl.BlockSpec((tm, d), row), pl.BlockSpec((tm, ym.shape[1]), row), pl.BlockSpec((tm, yc.shape[1]), row),
                  pl.BlockSpec(w_out_b.shape, const), pl.BlockSpec(norm_w.shape, const),
                  pl.BlockSpec(w_router_t.shape, const), pl.BlockSpec(b_router.shape, const)],
        out_specs=[pl.BlockSpec((tm, d), row),
                   pl.BlockSpec((tm * n_slabs, LANES), row),
                   pl.BlockSpec((SUBLANES, tm), colb),
                   pl.BlockSpec((SUBLANES, tm), colb),
                   pl.BlockSpec((SUBLANES, tm), colb),
                   pl.BlockSpec((N_EXPERTS, LANES), const)],
        out_shape=[jax.ShapeDtypeStruct((n, d), F32),
                   jax.ShapeDtypeStruct((n * n_slabs, LANES), jnp.uint32),
                   jax.ShapeDtypeStruct((SUBLANES, n), jnp.int32),
                   jax.ShapeDtypeStruct((SUBLANES, n), jnp.int32),
                   jax.ShapeDtypeStruct((SUBLANES, n), F32),
                   jax.ShapeDtypeStruct((N_EXPERTS, LANES), F32)],
        scratch_shapes=[pltpu.VMEM((N_EXPERTS, LANES), F32)],
        compiler_params=_cparams("arbitrary"),
        name="out_proj_router",
    )(xf, ym, yc, w_out_b, norm_w, w_router_t, b_router)


def _experts_kernel(blk_ref, exp_ref, valid_ref, lo_ref, hi_ref,
                    xs_ref, w1_ref, b1_ref, w2_ref, b2_ref, ys_ref, w1c_ref, w2c_ref, acc_ref, *, rb, d_ff, n_sub):
    i = pl.program_id(0)
    n_slabs = xs_ref.shape[0] // rb
    prev = jnp.maximum(i - 1, 0)
    first_visit = jnp.logical_or(i == 0, blk_ref[prev] != blk_ref[i])
    new_expert = jnp.logical_or(i == 0, exp_ref[prev] != exp_ref[i])

    @pl.when(new_expert)
    def _():
        w1c_ref[...] = w1_ref[0].astype(BF16)
        w2c_ref[...] = w2_ref[0].astype(BF16)

    @pl.when(first_visit)
    def _():
        acc_ref[...] = jnp.zeros_like(acc_ref)

    rs = rb // n_sub
    lo, hi = lo_ref[i], hi_ref[i]

    def sub_block(c):
        xa, xb = _unpack_bf16_pairs(_load_row_slabs(xs_ref, c * rs, rs, n_slabs))
        x = jnp.concatenate([xa.astype(BF16), xb.astype(BF16)], axis=-1)
        hid = jnp.dot(x, w1c_ref[...], preferred_element_type=F32) + b1_ref[0]
        x_glu = jnp.minimum(hid[:, :d_ff], SWIGLU_LIMIT)
        x_lin = jnp.clip(hid[:, d_ff:], -SWIGLU_LIMIT, SWIGLU_LIMIT)
        act = x_glu * _sigmoid(SWIGLU_ALPHA * x_glu) * (x_lin + 1.0)
        y = jnp.dot(act.astype(BF16), w2c_ref[...], preferred_element_type=F32) + b2_ref[0]
        r = c * rs + lax.broadcasted_iota(jnp.int32, (rs, 1), 0)
        y = jnp.where(jnp.logical_and(r >= lo, r < hi), y, 0.0)
        y = acc_ref[c * rs:(c + 1) * rs, :] + y
        acc_ref[c * rs:(c + 1) * rs, :] = y
        _store_row_slabs(ys_ref, c * rs, _pack_bf16_pairs(y), n_slabs)

    for c in range(n_sub):
        has_rows = jnp.logical_and(valid_ref[i] == 1, jnp.logical_and(lo < (c + 1) * rs, hi > c * rs))
        pl.when(has_rows)(functools.partial(sub_block, c))


def _experts(xs, w1, b1, w2, b2, item_blk, item_exp, item_valid, item_lo, item_hi, rb, n_slabs):
    n_items = item_blk.shape[0]
    d = w1.shape[1]
    d_ff = w2.shape[1]
    by_blk = lambda i, blk, ex, va, lo, hi: (blk[i], 0)
    by_exp = lambda i, blk, ex, va, lo, hi: (ex[i], 0, 0)
    grid_spec = pltpu.PrefetchScalarGridSpec(
        num_scalar_prefetch=5,
        grid=(n_items,),
        in_specs=[pl.BlockSpec((rb * n_slabs, LANES), by_blk),
                  pl.BlockSpec((1,) + w1.shape[1:], by_exp),
                  pl.BlockSpec((1,) + b1.shape[1:], by_exp),
                  pl.BlockSpec((1,) + w2.shape[1:], by_exp),
                  pl.BlockSpec((1,) + b2.shape[1:], by_exp)],
        out_specs=pl.BlockSpec((rb * n_slabs, LANES), by_blk),
        scratch_shapes=[pltpu.VMEM(w1.shape[1:], BF16), pltpu.VMEM(w2.shape[1:], BF16), pltpu.VMEM((rb, d), F32)],
    )
    return pl.pallas_call(
        functools.partial(_experts_kernel, rb=rb, d_ff=d_ff, n_sub=rb // MXU_DIM),
        grid_spec=grid_spec,
        out_shape=jax.ShapeDtypeStruct(xs.shape, jnp.uint32),
        compiler_params=pltpu.CompilerParams(dimension_semantics=("arbitrary",), vmem_limit_bytes=EXPERTS_VMEM_LIMIT),
        name="experts",
    )(item_blk, item_exp, item_valid, item_lo, item_hi, xs, w1, b1, w2, b2)


SC_CORES = 2
SC_SUBCORES = 16
SC_GATHER_WINDOW = 64


def _sc_worker_base(per_worker):
    wid = lax.axis_index("s") * SC_CORES + lax.axis_index("c")
    return wid * per_worker


def _sc_gather_rows(table, idx):
    m = idx.shape[0]
    n_workers = SC_CORES * SC_SUBCORES
    window = SC_GATHER_WINDOW
    per_worker = m // n_workers
    n_win = per_worker // window
    assert per_worker * n_workers == m and n_win * window == per_worker and n_win % 2 == 0
    mesh = plsc.VectorSubcoreMesh(core_axis_name="c", subcore_axis_name="s")
    slab = table.shape[1:]

    @functools.partial(
        pl.kernel, mesh=mesh,
        out_type=jax.ShapeDtypeStruct((m,) + slab, table.dtype),
        scratch_types=[pltpu.VMEM((per_worker,), jnp.int32),
                       pltpu.VMEM((window,) + slab, table.dtype), pltpu.VMEM((window,) + slab, table.dtype)]
        + [pltpu.SemaphoreType.DMA] * 4,
        name="sc_gather_rows",
    )
    def gather(table_hbm, idx_hbm, out_hbm, idx_v, rows0, rows1, gsem0, gsem1, wsem0, wsem1):
        rows_v, gsem, wsem = (rows0, rows1), (gsem0, gsem1), (wsem0, wsem1)
        base = _sc_worker_base(per_worker)
        pltpu.sync_copy(idx_hbm.at[pl.ds(pl.multiple_of(base, window), per_worker)], idx_v)

        def gather_copy(w, b):
            ids = idx_v.at[pl.ds(pl.multiple_of(w * window, window), window)]
            return pltpu.make_async_copy(table_hbm.at[ids], rows_v[b], gsem[b])

        def write_copy(w, b):
            dst = out_hbm.at[pl.ds(pl.multiple_of(base + w * window, window), window)]
            return pltpu.make_async_copy(rows_v[b], dst, wsem[b])

        gather_copy(0, 0).start()

        @pl.loop(0, n_win, step=2)
        def _(w):
            @pl.when(w >= 1)
            def _():
                write_copy(w - 1, 1).wait()

            gather_copy(w + 1, 1).start()
            gather_copy(w, 0).wait()
            write_copy(w, 0).start()

            @pl.when(w + 2 < n_win)
            def _():
                write_copy(w, 0).wait()
                gather_copy(w + 2, 0).start()

            gather_copy(w + 1, 1).wait()
            write_copy(w + 1, 1).start()

        write_copy(n_win - 2, 0).wait()
        write_copy(n_win - 1, 1).wait()

    return gather(table, idx)


def _sc_scatter_rows(rows, dest, n_slots):
    n = rows.shape[0]
    n_workers = SC_CORES * SC_SUBCORES
    window = SC_GATHER_WINDOW
    per_worker = n // n_workers
    n_win = per_worker // window
    assert per_worker * n_workers == n and n_win * window == per_worker and n_win % 2 == 0
    mesh = plsc.VectorSubcoreMesh(core_axis_name="c", subcore_axis_name="s")
    slab = rows.shape[1:]

    @functools.partial(
        pl.kernel, mesh=mesh,
        out_type=jax.ShapeDtypeStruct((n_slots * n,) + slab, rows.dtype),
        scratch_types=[pltpu.VMEM((window,), jnp.int32)] * (2 * n_slots)
        + [pltpu.VMEM((window,) + slab, rows.dtype)] * 2 + [pltpu.SemaphoreType.DMA] * 4,
        name="sc_scatter_rows",
    )
    def scatter(rows_hbm, dest_hbm, out_hbm, *scratch):
        idx_v = (scratch[:n_slots], scratch[n_slots:2 * n_slots])
        rows_v = scratch[2 * n_slots:2 * n_slots + 2]
        rsem = scratch[2 * n_slots + 2:2 * n_slots + 4]
        wsem = scratch[2 * n_slots + 4:2 * n_slots + 6]
        base = _sc_worker_base(per_worker)

        def rows_at(w, k=0):
            return pl.ds(pl.multiple_of(k * n + base + w * window, window), window)

        def start_read(w, b):
            for k in range(n_slots):
                pltpu.sync_copy(dest_hbm.at[rows_at(w, k)], idx_v[b][k])
            pltpu.async_copy(rows_hbm.at[rows_at(w)], rows_v[b], rsem[b])

        def scatter_window(w, b):
            pltpu.make_async_copy(rows_hbm.at[rows_at(w)], rows_v[b], rsem[b]).wait()
            for k in range(n_slots):
                pltpu.async_copy(rows_v[b], out_hbm.at[idx_v[b][k]], wsem[b])
            for k in range(n_slots):
                pltpu.make_async_copy(rows_v[b], out_hbm.at[idx_v[b][k]], wsem[b]).wait()

        start_read(0, 0)

        @pl.loop(0, n_win, step=2)
        def _(w):
            start_read(w + 1, 1)
            scatter_window(w, 0)

            @pl.when(w + 2 < n_win)
            def _():
                start_read(w + 2, 0)

            scatter_window(w + 1, 1)

    return scatter(rows, dest)


def _final_kernel(h1_ref, gate_ref, yg_ref, nw_ref, out_ref, *, tm, n_slabs):
    gpad = jnp.concatenate([gate_ref[...], jnp.zeros((LANES - SUBLANES, tm), F32)], axis=0)
    gcol = gpad.T
    h2 = h1_ref[...]
    for k in range(TOP_K):
        ya, yb = _unpack_bf16_pairs(_load_row_slabs(yg_ref, 0, tm, n_slabs, lead=(k,)))
        h2 = h2 + gcol[:, k:k + 1] * jnp.concatenate([ya, yb], axis=-1)
    out_ref[...] = h2 * lax.rsqrt(jnp.mean(h2 * h2, axis=-1, keepdims=True) + EPS) * nw_ref[...]


def _final(h, gates_t, yg, final_w, n_slabs, tm, chunk, n_chunks):
    n, d = h.shape
    tiles = n // tm // n_chunks
    first = chunk * tiles
    return pl.pallas_call(
        functools.partial(_final_kernel, tm=tm, n_slabs=n_slabs),
        grid=(tiles,),
        in_specs=[pl.BlockSpec((tm, d), lambda i: (first + i, 0)),
                  pl.BlockSpec((SUBLANES, tm), lambda i: (0, first + i)),
                  pl.BlockSpec((TOP_K, tm * n_slabs, LANES), lambda i: (0, i, 0)),
                  pl.BlockSpec((1, d), lambda i: (0, 0))],
        out_specs=pl.BlockSpec((tm, d), lambda i: (first + i, 0)),
        out_shape=jax.ShapeDtypeStruct((n, d), F32),
        input_output_aliases={0: 0},
        compiler_params=_cparams("parallel"),
        name="final",
    )(h, gates_t, yg, final_w)


def _block_diag_tiles(w):
    nb, bs, _ = w.shape
    rows = jnp.tile(w.reshape(nb * bs // MXU_DIM, MXU_DIM, bs), (1, 1, MXU_DIM // bs))
    r_blk = lax.broadcasted_iota(jnp.int32, (MXU_DIM, MXU_DIM), 0) // bs
    c_blk = lax.broadcasted_iota(jnp.int32, (MXU_DIM, MXU_DIM), 1) // bs
    return jnp.where(r_blk == c_blk, rows, 0.0).astype(BF16)


def _layer(xf, bsz, seq, norm_mix_w, w_in, mlstm_conv_w, mlstm_conv_b, w_q, w_k, w_v, w_igate, b_igate,
           w_fgate, b_fgate, mlstm_norm_w, mlstm_skip, conv_dw_w, conv_dw_b, conv_norm_w, conv_norm_b,
           w_out, norm_ffn_w, w_router, b_router, w1, b1, w2, b2, final_norm_w):
    n, d = xf.shape
    d_mlstm = mlstm_norm_w.shape[0]
    d_conv = conv_norm_w.shape[0]
    n_slabs = d // (2 * LANES)
    r2 = lambda v: v.reshape(1, -1)

    xm, xc, sz, u = _in_proj(xf, r2(norm_mix_w), w_in.astype(BF16), mlstm_conv_w, r2(mlstm_conv_b),
                             d_mlstm, d_conv, seq, tm=512)

    wg = jnp.concatenate([w_igate, w_fgate], axis=1)
    wg = jnp.pad(wg, ((0, 0), (0, LANES - wg.shape[1]))).astype(BF16)
    bg = jnp.pad(jnp.concatenate([b_igate, b_fgate]), (0, LANES - 2 * N_HEADS)).reshape(1, LANES)
    ym = _mlstm(xm, xc, sz, _block_diag_tiles(w_q), _block_diag_tiles(w_k), _block_diag_tiles(w_v),
                wg, bg, r2(mlstm_norm_w), r2(mlstm_skip), bsz, seq)
    yc = _conv_group(u, conv_dw_w, r2(conv_dw_b), r2(conv_norm_w), r2(conv_norm_b), bsz, seq, tile=512)

    h1, a2s, idx_t, pos_t, gates_t, cnt = _out_proj(
        xf, ym, yc, w_out.astype(BF16), r2(norm_ffn_w), w_router.T, b_router.reshape(-1, 1), d_mlstm, tm=512)

    counts = cnt[:, 0].astype(jnp.int32)
    ends = jnp.cumsum(counts)
    starts = ends - counts
    e_ids = jnp.arange(N_EXPERTS, dtype=jnp.int32)
    idx4 = idx_t[:TOP_K]
    dest = pos_t[:TOP_K] + jnp.sum(
        jnp.where(idx4[None] == e_ids[:, None, None], starts[:, None, None], 0), axis=0)

    n_rows = n * TOP_K
    rb = 1024
    n_blocks = n_rows // rb
    n_items = n_blocks + N_EXPERTS - 1
    first_blk = starts // rb
    last_blk = jnp.where(counts > 0, (ends - 1) // rb, first_blk - 1)
    per_e = last_blk - first_blk + 1
    item_end = jnp.cumsum(per_e)
    item_start = item_end - per_e
    ids = jnp.arange(n_items, dtype=jnp.int32)
    total = item_end[-1]
    item_valid = (ids < total).astype(jnp.int32)
    item_exp = jnp.minimum(jnp.sum((ids[:, None] >= item_end[None, :]).astype(jnp.int32), axis=1), N_EXPERTS - 1)
    last_valid_exp = jnp.max(jnp.where(per_e > 0, e_ids, 0))
    item_exp = jnp.where(item_valid == 1, item_exp, last_valid_exp).astype(jnp.int32)
    is_exp = item_exp[:, None] == e_ids[None, :]
    of_item = lambda table: jnp.sum(jnp.where(is_exp, table[None, :], 0), axis=1)
    item_blk = jnp.where(item_valid == 1, of_item(first_blk) + ids - of_item(item_start), n_blocks - 1)
    item_blk = item_blk.astype(jnp.int32)
    item_lo = (jnp.maximum(of_item(starts), item_blk * rb) - item_blk * rb).astype(jnp.int32)
    item_hi = (jnp.minimum(of_item(ends), (item_blk + 1) * rb) - item_blk * rb).astype(jnp.int32)

    xs = _sc_scatter_rows(a2s.reshape(n, n_slabs, LANES), dest.reshape(-1), TOP_K)
    ys = _experts(xs.reshape(n_rows * n_slabs, LANES), w1, b1[:, None, :], w2, b2[:, None, :],
                  item_blk, item_exp, item_valid, item_lo, item_hi, rb, n_slabs)
    ys = ys.reshape(n_rows, n_slabs, LANES)

    n_chunks = 8
    nc = n // n_chunks
    out = h1
    for c in range(n_chunks):
        yg = _sc_gather_rows(ys, dest[:, c * nc:(c + 1) * nc].reshape(-1))
        out = _final(out, gates_t, yg.reshape(TOP_K, nc * n_slabs, LANES), r2(final_norm_w), n_slabs,
                     tm=256, chunk=c, n_chunks=n_chunks)
    return out


def kernel(x, norm_mix_w, w_in, mlstm_conv_w, mlstm_conv_b, w_q, w_k, w_v, w_igate, b_igate, w_fgate, b_fgate,
           mlstm_norm_w, mlstm_skip, conv_dw_w, conv_dw_b, conv_norm_w, conv_norm_b, w_out, norm_ffn_w,
           w_router, b_router, w1, b1, w2, b2, final_norm_w):
    bsz, seq, d = x.shape
    assert norm_mix_w.shape[0] == 1, "single-layer block"
    out = _layer(x.reshape(bsz * seq, d), bsz, seq, norm_mix_w[0], w_in[0], mlstm_conv_w[0], mlstm_conv_b[0],
                 w_q[0], w_k[0], w_v[0], w_igate[0], b_igate[0], w_fgate[0], b_fgate[0], mlstm_norm_w[0],
                 mlstm_skip[0], conv_dw_w[0], conv_dw_b[0], conv_norm_w[0], conv_norm_b[0], w_out[0],
                 norm_ffn_w[0], w_router[0], b_router[0], w1[0], b1[0], w2[0], b2[0], final_norm_w)
    return out.reshape(bsz, seq, d)
```

```python
import functools

import jax
import jax.numpy as jnp
from jax import lax
from jax.experimental import pallas as pl
from jax.experimental.pallas import tpu as pltpu
from jax.experimental.pallas import tpu_sc as plsc

F32 = jnp.float32
BF16 = jnp.bfloat16

EPS = 1e-5
N_HEADS = 4
QKV_BLOCK = 4
MLSTM_CONV_WIDTH = 4
CONV_WIDTH = 31
N_EXPERTS = 32
TOP_K = 4
SWIGLU_ALPHA = 1.702
SWIGLU_LIMIT = 7.0

LANES = 128
SUBLANES = 8
MXU_DIM = 256
VMEM_LIMIT = 52 * 1024 * 1024
EXPERTS_VMEM_LIMIT = 58 * 1024 * 1024

MLSTM_CHUNK = 256
CONV_HALO = 32
NEG_INF = float("-inf")


def _sigmoid(x):
    return 1.0 / (1.0 + jnp.exp(-x))


def _silu(x):
    return x * _sigmoid(x)


def _pack_bf16_pairs(v):
    half = v.shape[1] // 2
    hi = lax.bitcast_convert_type(v[:, :half].astype(BF16).astype(F32), jnp.uint32)
    lo = lax.bitcast_convert_type(v[:, half:].astype(BF16).astype(F32), jnp.uint32)
    return hi | (lo >> 16)


def _unpack_bf16_pairs(w):
    hi = lax.bitcast_convert_type(w & jnp.uint32(0xFFFF0000), F32)
    lo = lax.bitcast_convert_type(w << 16, F32)
    return hi, lo


def _load_row_slabs(ref, first, rows, n_slabs, lead=()):
    return jnp.concatenate([ref[lead + (pl.ds(first * n_slabs + s, rows, stride=n_slabs), slice(None))]
                            for s in range(n_slabs)], axis=-1)


def _store_row_slabs(ref, first, v, n_slabs):
    rows = v.shape[0]
    for s in range(n_slabs):
        ref[pl.ds(first * n_slabs + s, rows, stride=n_slabs), :] = v[:, s * LANES:(s + 1) * LANES]


def _cparams(*sem):
    return pltpu.CompilerParams(dimension_semantics=sem, vmem_limit_bytes=VMEM_LIMIT)


def _inproj_kernel(x_ref, nw_ref, w_ref, cw_ref, cb_ref, xm_ref, xc_ref, sz_ref, u_ref, xbuf,
                   *, d_mlstm, d_conv, tiles_per_seq):
    tm = x_ref.shape[0]
    hist = SUBLANES
    i = pl.program_id(0)

    @pl.when(i == 0)
    def _():
        xbuf[...] = jnp.zeros_like(xbuf)

    x = x_ref[...]
    a = x * lax.rsqrt(jnp.mean(x * x, axis=-1, keepdims=True) + EPS) * nw_ref[...]
    ab = a.astype(BF16)
    same_seq = i % tiles_per_seq != 0
    cols = MXU_DIM

    for c in range(d_mlstm // cols):
        sl = slice(c * cols, (c + 1) * cols)
        prev = xbuf[hist:hist + tm, sl]
        acc = cb_ref[:, sl] + cw_ref[MLSTM_CONV_WIDTH - 1:MLSTM_CONV_WIDTH, sl] * prev
        for k in range(MLSTM_CONV_WIDTH - 1):
            off = hist - (MLSTM_CONV_WIDTH - 1) + k
            acc = acc + cw_ref[k:k + 1, sl] * xbuf[off:off + tm, sl]
        xc_ref[:, sl] = _silu(acc).astype(xc_ref.dtype)
        xbuf[0:hist, sl] = jnp.where(same_seq, prev[tm - hist:, :], 0.0)

        xm = jnp.dot(ab, w_ref[:, sl], preferred_element_type=F32)
        xm_ref[:, sl] = xm.astype(xm_ref.dtype)
        xbuf[hist:hist + tm, sl] = xm
        z = jnp.dot(ab, w_ref[:, d_mlstm + c * cols:d_mlstm + (c + 1) * cols], preferred_element_type=F32)
        sz_ref[:, sl] = _silu(z)

    for c in range(d_conv // cols):
        sl = slice(c * cols, (c + 1) * cols)
        ga = jnp.dot(ab, w_ref[:, 2 * d_mlstm + c * cols:2 * d_mlstm + (c + 1) * cols], preferred_element_type=F32)
        gb = jnp.dot(ab, w_ref[:, 2 * d_mlstm + d_conv + c * cols:2 * d_mlstm + d_conv + (c + 1) * cols],
                     preferred_element_type=F32)
        u_ref[:, sl] = ga * _sigmoid(gb)


def _in_proj(xf, norm_w, w_in_b, conv_w, conv_b, d_mlstm, d_conv, seq, tm):
    n, d = xf.shape
    n_tiles = n // tm
    cur = lambda i: (jnp.minimum(i, n_tiles - 1), 0)
    prv = lambda i: (jnp.maximum(i - 1, 0), 0)
    const = lambda i: (0, 0)
    return pl.pallas_call(
        functools.partial(_inproj_kernel, d_mlstm=d_mlstm, d_conv=d_conv, tiles_per_seq=seq // tm),
        grid=(n_tiles + 1,),
        in_specs=[pl.BlockSpec((tm, d), cur),
                  pl.BlockSpec((1, d), const),
                  pl.BlockSpec(w_in_b.shape, const),
                  pl.BlockSpec(conv_w.shape, const),
                  pl.BlockSpec(conv_b.shape, const)],
        out_specs=[pl.BlockSpec((tm, d_mlstm), cur),
                   pl.BlockSpec((tm, d_mlstm), prv),
                   pl.BlockSpec((tm, d_mlstm), cur),
                   pl.BlockSpec((tm, d_conv), cur)],
        out_shape=[jax.ShapeDtypeStruct((n, d_mlstm), BF16),
                   jax.ShapeDtypeStruct((n, d_mlstm), BF16),
                   jax.ShapeDtypeStruct((n, d_mlstm), F32),
                   jax.ShapeDtypeStruct((n, d_conv), F32)],
        scratch_shapes=[pltpu.VMEM((tm + SUBLANES, d_mlstm), F32)],
        compiler_params=_cparams("arbitrary"),
        name="in_proj",
    )(xf, norm_w, w_in_b, conv_w, conv_b)


def _split3(v):
    hi = v.astype(BF16)
    r1 = v - hi.astype(F32)
    mid = r1.astype(BF16)
    lo = (r1 - mid.astype(F32)).astype(BF16)
    return hi, mid, lo


def _mlstm_kernel(xm_ref, xc_ref, sz_ref, wq_ref, wk_ref, wv_ref, wg_ref, bg_ref, nw_ref, sk_ref,
                  ym_ref, c_sc, n_sc, m_sc, *, chunk, dh, n_seq):
    L = chunk
    nh = N_HEADS
    j = pl.program_id(1)

    @pl.when(j == 0)
    def _():
        c_sc[...] = jnp.zeros_like(c_sc)
        n_sc[...] = jnp.zeros_like(n_sc)
        m_sc[...] = jnp.zeros_like(m_sc)

    ri = lax.broadcasted_iota(jnp.int32, (L, L), 0)
    ci = lax.broadcasted_iota(jnp.int32, (L, L), 1)
    causal = ci <= ri
    tri = jnp.where(causal, 1.0, 0.0).astype(BF16)
    scale = dh ** -0.5

    for sq in range(n_seq):
        xmb = xm_ref[sq]
        xcb = xc_ref[sq]
        d = xmb.shape[1]
        nb = d // MXU_DIM

        def bd(xb, w_ref):
            return jnp.concatenate(
                [jnp.dot(xb[:, g * MXU_DIM:(g + 1) * MXU_DIM], w_ref[g], preferred_element_type=F32)
                 for g in range(nb)], axis=-1)

        q = bd(xcb, wq_ref)
        k_ = bd(xcb, wk_ref)
        v = bd(xmb, wv_ref)
        qb, kb, vb = q.astype(BF16), k_.astype(BF16), v.astype(BF16)

        g = (jnp.dot(qb, wg_ref[0:d, :], preferred_element_type=F32)
             + jnp.dot(kb, wg_ref[d:2 * d, :], preferred_element_type=F32)
             + jnp.dot(vb, wg_ref[2 * d:3 * d, :], preferred_element_type=F32)
             + bg_ref[...])
        col = lax.broadcasted_iota(jnp.int32, g.shape, 1)
        log_f = jnp.minimum(g, 0.0) - jnp.log(1.0 + jnp.exp(-jnp.abs(g)))
        gates = jnp.where(col < nh, g, jnp.where(col < 2 * nh, log_f, 0.0))
        cum = sum(jnp.dot(tri, part, preferred_element_type=F32) for part in _split3(gates))
        colform = jnp.where(col < nh, gates, cum)
        rowform = colform.T

        for h in range(nh):
            sl = slice(h * dh, (h + 1) * dh)
            li_row = rowform[h:h + 1, :]
            b_row = rowform[nh + h:nh + h + 1, :]
            li_col = colform[:, h:h + 1]
            b_col = colform[:, nh + h:nh + h + 1]
            m_prev = m_sc[sq, h:h + 1, 0:1]
            g_tot = b_row[:, L - 1:L]

            dmat = jnp.where(causal, b_col - b_row + li_row, NEG_INF)
            inter = b_col + m_prev
            m_i = jnp.maximum(inter, jnp.max(dmat, axis=-1, keepdims=True))
            w_intra = jnp.exp(dmat - m_i)
            w_inter = jnp.exp(inter - m_i)

            qh = qb[:, sl]
            s = lax.dot_general(qh, kb[:, sl], (((1,), (1,)), ((), ())),
                                preferred_element_type=F32) * (scale * w_intra)
            c_prev = c_sc[sq, h]
            n_prev = n_sc[sq, h:h + 1, :]
            num = (w_inter * scale) * jnp.dot(qh, c_prev.astype(BF16), preferred_element_type=F32) \
                + jnp.dot(s.astype(BF16), vb[:, sl], preferred_element_type=F32)
            qn = jnp.sum(q[:, sl] * n_prev, axis=-1, keepdims=True) * scale
            den = w_inter * qn + jnp.sum(s, axis=-1, keepdims=True)
            hh = num / jnp.maximum(jnp.abs(den), jnp.exp(-m_i))

            mu = jnp.mean(hh, axis=-1, keepdims=True)
            dev = hh - mu
            var = jnp.mean(dev * dev, axis=-1, keepdims=True)
            hn = dev * lax.rsqrt(var + EPS) * nw_ref[:, sl]
            ym_ref[sq, :, sl] = ((hn + sk_ref[:, sl] * xcb[:, sl].astype(F32)) * sz_ref[sq, :, sl]).astype(ym_ref.dtype)

            a_row = g_tot - b_row + li_row
            m_new = jnp.maximum(g_tot + m_prev, jnp.max(a_row, axis=-1, keepdims=True))
            a_col = g_tot - b_col + li_col
            w_state = jnp.exp(a_col - m_new)
            decay = jnp.exp(g_tot + m_prev - m_new)
            kw = k_[:, sl] * w_state
            c_sc[sq, h] = decay * c_prev + lax.dot_general(kw.astype(BF16), vb[:, sl], (((0,), (0,)), ((), ())),
                                                           preferred_element_type=F32)
            n_sc[sq, h:h + 1, :] = decay * n_prev + jnp.sum(kw, axis=0, keepdims=True)
            m_sc[sq, h:h + 1, :] = jnp.broadcast_to(m_new, (1, m_sc.shape[2]))


def _mlstm(xm, xc, sz, wq, wk, wv, wg, bg, norm_w, skip, bsz, seq):
    n, d = xm.shape
    L = MLSTM_CHUNK
    nc = seq // L
    dh = d // N_HEADS
    n_seq = 2 if bsz % 2 == 0 else 1
    blk = lambda b, j: (b, j, 0)
    c2 = lambda b, j: (0, 0)
    c3 = lambda b, j: (0, 0, 0)
    ym = pl.pallas_call(
        functools.partial(_mlstm_kernel, chunk=L, dh=dh, n_seq=n_seq),
        grid=(bsz // n_seq, nc),
        in_specs=[pl.BlockSpec((n_seq, L, d), blk), pl.BlockSpec((n_seq, L, d), blk), pl.BlockSpec((n_seq, L, d), blk),
                  pl.BlockSpec(wq.shape, c3), pl.BlockSpec(wk.shape, c3), pl.BlockSpec(wv.shape, c3),
                  pl.BlockSpec(wg.shape, c2), pl.BlockSpec(bg.shape, c2),
                  pl.BlockSpec(norm_w.shape, c2), pl.BlockSpec(skip.shape, c2)],
        out_specs=pl.BlockSpec((n_seq, L, d), blk),
        out_shape=jax.ShapeDtypeStruct((bsz, seq, d), BF16),
        scratch_shapes=[pltpu.VMEM((n_seq, N_HEADS, dh, dh), F32),
                        pltpu.VMEM((n_seq, SUBLANES, dh), F32),
                        pltpu.VMEM((n_seq, SUBLANES, LANES), F32)],
        compiler_params=_cparams("arbitrary", "arbitrary"),
        name="mlstm",
    )(xm.reshape(bsz, seq, d), xc.reshape(bsz, seq, d), sz.reshape(bsz, seq, d), wq, wk, wv, wg, bg, norm_w, skip)
    return ym.reshape(n, d)


def _conv_kernel(u_ref, w_ref, b_ref, nw_ref, nb_ref, yc_ref, ubuf, pbuf, cbuf, *, tile, rows):
    T = tile
    j = pl.program_id(1)

    @pl.when(j == 0)
    def _():
        ubuf[0:CONV_HALO, :] = jnp.zeros((CONV_HALO, ubuf.shape[1]), F32)

    ubuf[CONV_HALO:CONV_HALO + T, :] = u_ref[...]
    base = CONV_HALO - (CONV_WIDTH - 1)
    span = T + CONV_HALO - SUBLANES
    n_lane_blocks = ubuf.shape[1] // LANES

    def lane_block(c, carry):
        lanes = pl.ds(pl.multiple_of(c * LANES, LANES), LANES)
        for r in range(1, SUBLANES):
            pbuf[r - 1, :, :] = ubuf[r:r + span, lanes]
        for r0 in range(0, T, rows):
            acc = jnp.broadcast_to(b_ref[:, lanes], (rows, LANES))
            for k in range(CONV_WIDTH):
                q, r = divmod(base + k, SUBLANES)
                lo = r0 + q * SUBLANES
                src = ubuf[lo:lo + rows, lanes] if r == 0 else pbuf[r - 1, lo:lo + rows, :]
                acc = acc + w_ref[k:k + 1, lanes] * src
            cbuf[r0:r0 + rows, lanes] = acc
        return carry

    lax.fori_loop(0, n_lane_blocks, lane_block, 0)
    ubuf[0:CONV_HALO, :] = ubuf[T:T + CONV_HALO, :]

    y = cbuf[...]
    mu = jnp.mean(y, axis=-1, keepdims=True)
    dev = y - mu
    var = jnp.mean(dev * dev, axis=-1, keepdims=True)
    yn = dev * lax.rsqrt(var + EPS) * nw_ref[...] + nb_ref[...]
    yc_ref[...] = _silu(yn).astype(yc_ref.dtype)


def _conv_group(u, w, b, norm_w, norm_b, bsz, seq, tile):
    n, d = u.shape
    nt = seq // tile
    row = lambda bi, j: (bi * nt + j, 0)
    c2 = lambda bi, j: (0, 0)
    return pl.pallas_call(
        functools.partial(_conv_kernel, tile=tile, rows=64),
        grid=(bsz, nt),
        in_specs=[pl.BlockSpec((tile, d), row), pl.BlockSpec(w.shape, c2), pl.BlockSpec(b.shape, c2),
                  pl.BlockSpec(norm_w.shape, c2), pl.BlockSpec(norm_b.shape, c2)],
        out_specs=pl.BlockSpec((tile, d), row),
        out_shape=jax.ShapeDtypeStruct((n, d), BF16),
        scratch_shapes=[pltpu.VMEM((tile + CONV_HALO, d), F32),
                        pltpu.VMEM((SUBLANES - 1, tile + CONV_HALO - SUBLANES, LANES), F32),
                        pltpu.VMEM((tile, d), F32)],
        compiler_params=_cparams("arbitrary", "arbitrary"),
        name="conv_group",
    )(u, w, b, norm_w, norm_b)


def _outproj_kernel(x_ref, ym_ref, yc_ref, wo_ref, nw_ref, wr_ref, br_ref,
                    h1_ref, a2_ref, idx_ref, pos_ref, gate_ref, cnt_ref, cnt_sc, *, tm, d_mlstm):
    i = pl.program_id(0)

    @pl.when(i == 0)
    def _():
        cnt_sc[...] = jnp.zeros_like(cnt_sc)

    h1 = (x_ref[...]
          + jnp.dot(ym_ref[...], wo_ref[0:d_mlstm, :], preferred_element_type=F32)
          + jnp.dot(yc_ref[...], wo_ref[d_mlstm:, :], preferred_element_type=F32))
    h1_ref[...] = h1
    a2 = h1 * lax.rsqrt(jnp.mean(h1 * h1, axis=-1, keepdims=True) + EPS) * nw_ref[...]
    _store_row_slabs(a2_ref, 0, _pack_bf16_pairs(a2), a2.shape[1] // (2 * LANES))

    logits = lax.dot_general(wr_ref[...], a2.astype(BF16), (((1,), (1,)), ((), ())),
                             preferred_element_type=F32) + br_ref[...]
    e_iota = lax.broadcasted_iota(jnp.int32, logits.shape, 0)
    work = logits
    vals, idxs = [], []
    for _ in range(TOP_K):
        mx = jnp.max(work, axis=0, keepdims=True)
        sel = jnp.min(jnp.where(work == mx, e_iota, N_EXPERTS), axis=0, keepdims=True)
        vals.append(mx)
        idxs.append(sel)
        work = jnp.where(e_iota == sel, NEG_INF, work)
    exps = [jnp.exp(vv - vals[0]) for vv in vals]
    tot = exps[0] + exps[1] + exps[2] + exps[3]
    gates = [ev / tot for ev in exps]

    chosen = functools.reduce(jnp.logical_or, [e_iota == sel for sel in idxs])
    mh = jnp.where(chosen, 1.0, 0.0)
    ri = lax.broadcasted_iota(jnp.int32, (tm, tm), 0)
    ci = lax.broadcasted_iota(jnp.int32, (tm, tm), 1)
    upper = jnp.where(ri < ci, 1.0, 0.0).astype(BF16)
    rank = jnp.dot(mh.astype(BF16), upper, preferred_element_type=F32) + cnt_sc[:, 0:1]
    cnt_new = cnt_sc[...] + jnp.sum(mh, axis=1, keepdims=True)
    cnt_sc[...] = cnt_new
    cnt_ref[...] = cnt_new

    zero_i = jnp.zeros((SUBLANES - TOP_K, tm), jnp.int32)
    pos = [jnp.sum(jnp.where(e_iota == sel, rank, 0.0), axis=0, keepdims=True).astype(jnp.int32) for sel in idxs]
    idx_ref[...] = jnp.concatenate(idxs + [zero_i], axis=0)
    pos_ref[...] = jnp.concatenate(pos + [zero_i], axis=0)
    gate_ref[...] = jnp.concatenate(gates + [jnp.zeros((SUBLANES - TOP_K, tm), F32)], axis=0)


def _out_proj(xf, ym, yc, w_out_b, norm_w, w_router_t, b_router, d_mlstm, tm):
    n, d = xf.shape
    n_slabs = d // (2 * LANES)
    row = lambda i: (i, 0)
    colb = lambda i: (0, i)
    const = lambda i: (0, 0)
    return pl.pallas_call(
        functools.partial(_outproj_kernel, tm=tm, d_mlstm=d_mlstm),
        grid=(n // tm,),
        in_specs=[pl.BlockSpec((tm, d), row), pl.BlockSpec((tm, ym.shape[1]), row), pl.BlockSpec((tm, yc.shape[1]), row),
                  pl.BlockSpec(w_out_b.shape, const), pl.BlockSpec(norm_w.shape, const),
                  pl.BlockSpec(w_router_t.shape, const), pl.BlockSpec(b_router.shape, const)],
        out_specs=[pl.BlockSpec((tm, d), row),
                   pl.BlockSpec((tm * n_slabs, LANES), row),
                   pl.BlockSpec((SUBLANES, tm), colb),
                   pl.BlockSpec((SUBLANES, tm), colb),
                   pl.BlockSpec((SUBLANES, tm), colb),
                   pl.BlockSpec((N_EXPERTS, LANES), const)],
        out_shape=[jax.ShapeDtypeStruct((n, d), F32),
                   jax.ShapeDtypeStruct((n * n_slabs, LANES), jnp.uint32),
                   jax.ShapeDtypeStruct((SUBLANES, n), jnp.int32),
                   jax.ShapeDtypeStruct((SUBLANES, n), jnp.int32),
                   jax.ShapeDtypeStruct((SUBLANES, n), F32),
                   jax.ShapeDtypeStruct((N_EXPERTS, LANES), F32)],
        scratch_shapes=[pltpu.VMEM((N_EXPERTS, LANES), F32)],
        compiler_params=_cparams("arbitrary"),
        name="out_proj_router",
    )(xf, ym, yc, w_out_b, norm_w, w_router_t, b_router)


def _experts_kernel(blk_ref, exp_ref, valid_ref, lo_ref, hi_ref,
                    xs_ref, w1_ref, b1_ref, w2_ref, b2_ref, ys_ref, w1c_ref, w2c_ref, acc_ref, *, rb, d_ff, n_sub):
    i = pl.program_id(0)
    n_slabs = xs_ref.shape[0] // rb
    prev = jnp.maximum(i - 1, 0)
    first_visit = jnp.logical_or(i == 0, blk_ref[prev] != blk_ref[i])
    new_expert = jnp.logical_or(i == 0, exp_ref[prev] != exp_ref[i])

    @pl.when(new_expert)
    def _():
        w1c_ref[...] = w1_ref[0].astype(BF16)
        w2c_ref[...] = w2_ref[0].astype(BF16)

    @pl.when(first_visit)
    def _():
        acc_ref[...] = jnp.zeros_like(acc_ref)

    rs = rb // n_sub
    lo, hi = lo_ref[i], hi_ref[i]

    def sub_block(c):
        xa, xb = _unpack_bf16_pairs(_load_row_slabs(xs_ref, c * rs, rs, n_slabs))
        x = jnp.concatenate([xa.astype(BF16), xb.astype(BF16)], axis=-1)
        hid = jnp.dot(x, w1c_ref[...], preferred_element_type=F32) + b1_ref[0]
        x_glu = jnp.minimum(hid[:, :d_ff], SWIGLU_LIMIT)
        x_lin = jnp.clip(hid[:, d_ff:], -SWIGLU_LIMIT, SWIGLU_LIMIT)
        act = x_glu * _sigmoid(SWIGLU_ALPHA * x_glu) * (x_lin + 1.0)
        y = jnp.dot(act.astype(BF16), w2c_ref[...], preferred_element_type=F32) + b2_ref[0]
        r = c * rs + lax.broadcasted_iota(jnp.int32, (rs, 1), 0)
        y = jnp.where(jnp.logical_and(r >= lo, r < hi), y, 0.0)
        y = acc_ref[c * rs:(c + 1) * rs, :] + y
        acc_ref[c * rs:(c + 1) * rs, :] = y
        _store_row_slabs(ys_ref, c * rs, _pack_bf16_pairs(y), n_slabs)

    for c in range(n_sub):
        has_rows = jnp.logical_and(valid_ref[i] == 1, jnp.logical_and(lo < (c + 1) * rs, hi > c * rs))
        pl.when(has_rows)(functools.partial(sub_block, c))


def _experts(xs, w1, b1, w2, b2, item_blk, item_exp, item_valid, item_lo, item_hi, rb, n_slabs):
    n_items = item_blk.shape[0]
    d = w1.shape[1]
    d_ff = w2.shape[1]
    by_blk = lambda i, blk, ex, va, lo, hi: (blk[i], 0)
    by_exp = lambda i, blk, ex, va, lo, hi: (ex[i], 0, 0)
    grid_spec = pltpu.PrefetchScalarGridSpec(
        num_scalar_prefetch=5,
        grid=(n_items,),
        in_specs=[pl.BlockSpec((rb * n_slabs, LANES), by_blk),
                  pl.BlockSpec((1,) + w1.shape[1:], by_exp),
                  pl.BlockSpec((1,) + b1.shape[1:], by_exp),
                  pl.BlockSpec((1,) + w2.shape[1:], by_exp),
                  pl.BlockSpec((1,) + b2.shape[1:], by_exp)],
        out_specs=pl.BlockSpec((rb * n_slabs, LANES), by_blk),
        scratch_shapes=[pltpu.VMEM(w1.shape[1:], BF16), pltpu.VMEM(w2.shape[1:], BF16), pltpu.VMEM((rb, d), F32)],
    )
    return pl.pallas_call(
        functools.partial(_experts_kernel, rb=rb, d_ff=d_ff, n_sub=rb // MXU_DIM),
        grid_spec=grid_spec,
        out_shape=jax.ShapeDtypeStruct(xs.shape, jnp.uint32),
        compiler_params=pltpu.CompilerParams(dimension_semantics=("arbitrary",), vmem_limit_bytes=EXPERTS_VMEM_LIMIT),
        name="experts",
    )(item_blk, item_exp, item_valid, item_lo, item_hi, xs, w1, b1, w2, b2)


SC_CORES = 2
SC_SUBCORES = 16
SC_GATHER_WINDOW = 64


def _sc_worker_base(per_worker):
    wid = lax.axis_index("s") * SC_CORES + lax.axis_index("c")
    return wid * per_worker


def _sc_gather_rows(table, idx):
    m = idx.shape[0]
    n_workers = SC_CORES * SC_SUBCORES
    window = SC_GATHER_WINDOW
    per_worker = m // n_workers
    n_win = per_worker // window
    assert per_worker * n_workers == m and n_win * window == per_worker and n_win % 2 == 0
    mesh = plsc.VectorSubcoreMesh(core_axis_name="c", subcore_axis_name="s")
    slab = table.shape[1:]

    @functools.partial(
        pl.kernel, mesh=mesh,
        out_type=jax.ShapeDtypeStruct((m,) + slab, table.dtype),
        scratch_types=[pltpu.VMEM((per_worker,), jnp.int32),
                       pltpu.VMEM((window,) + slab, table.dtype), pltpu.VMEM((window,) + slab, table.dtype)]
        + [pltpu.SemaphoreType.DMA] * 4,
        name="sc_gather_rows",
    )
    def gather(table_hbm, idx_hbm, out_hbm, idx_v, rows0, rows1, gsem0, gsem1, wsem0, wsem1):
        rows_v, gsem, wsem = (rows0, rows1), (gsem0, gsem1), (wsem0, wsem1)
        base = _sc_worker_base(per_worker)
        pltpu.sync_copy(idx_hbm.at[pl.ds(pl.multiple_of(base, window), per_worker)], idx_v)

        def gather_copy(w, b):
            ids = idx_v.at[pl.ds(pl.multiple_of(w * window, window), window)]
            return pltpu.make_async_copy(table_hbm.at[ids], rows_v[b], gsem[b])

        def write_copy(w, b):
            dst = out_hbm.at[pl.ds(pl.multiple_of(base + w * window, window), window)]
            return pltpu.make_async_copy(rows_v[b], dst, wsem[b])

        gather_copy(0, 0).start()

        @pl.loop(0, n_win, step=2)
        def _(w):
            @pl.when(w >= 1)
            def _():
                write_copy(w - 1, 1).wait()

            gather_copy(w + 1, 1).start()
            gather_copy(w, 0).wait()
            write_copy(w, 0).start()

            @pl.when(w + 2 < n_win)
            def _():
                write_copy(w, 0).wait()
                gather_copy(w + 2, 0).start()

            gather_copy(w + 1, 1).wait()
            write_copy(w + 1, 1).start()

        write_copy(n_win - 2, 0).wait()
        write_copy(n_win - 1, 1).wait()

    return gather(table, idx)


def _sc_scatter_rows(rows, dest, n_slots):
    n = rows.shape[0]
    n_workers = SC_CORES * SC_SUBCORES
    window = SC_GATHER_WINDOW
    per_worker = n // n_workers
    n_win = per_worker // window
    assert per_worker * n_workers == n and n_win * window == per_worker and n_win % 2 == 0
    mesh = plsc.VectorSubcoreMesh(core_axis_name="c", subcore_axis_name="s")
    slab = rows.shape[1:]

    @functools.partial(
        pl.kernel, mesh=mesh,
        out_type=jax.ShapeDtypeStruct((n_slots * n,) + slab, rows.dtype),
        scratch_types=[pltpu.VMEM((window,), jnp.int32)] * (2 * n_slots)
        + [pltpu.VMEM((window,) + slab, rows.dtype)] * 2 + [pltpu.SemaphoreType.DMA] * 4,
        name="sc_scatter_rows",
    )
    def scatter(rows_hbm, dest_hbm, out_hbm, *scratch):
        idx_v = (scratch[:n_slots], scratch[n_slots:2 * n_slots])
        rows_v = scratch[2 * n_slots:2 * n_slots + 2]
        rsem = scratch[2 * n_slots + 2:2 * n_slots + 4]
        wsem = scratch[2 * n_slots + 4:2 * n_slots + 6]
        base = _sc_worker_base(per_worker)

        def rows_at(w, k=0):
            return pl.ds(pl.multiple_of(k * n + base + w * window, window), window)

        def start_read(w, b):
            for k in range(n_slots):
                pltpu.sync_copy(dest_hbm.at[rows_at(w, k)], idx_v[b][k])
            pltpu.async_copy(rows_hbm.at[rows_at(w)], rows_v[b], rsem[b])

        def scatter_window(w, b):
            pltpu.make_async_copy(rows_hbm.at[rows_at(w)], rows_v[b], rsem[b]).wait()
            for k in range(n_slots):
                pltpu.async_copy(rows_v[b], out_hbm.at[idx_v[b][k]], wsem[b])
            for k in range(n_slots):
                pltpu.make_async_copy(rows_v[b], out_hbm.at[idx_v[b][k]], wsem[b]).wait()

        start_read(0, 0)

        @pl.loop(0, n_win, step=2)
        def _(w):
            start_read(w + 1, 1)
            scatter_window(w, 0)

            @pl.when(w + 2 < n_win)
            def _():
                start_read(w + 2, 0)

            scatter_window(w + 1, 1)

    return scatter(rows, dest)


def _final_kernel(h1_ref, gate_ref, yg_ref, nw_ref, out_ref, *, tm, n_slabs):
    gpad = jnp.concatenate([gate_ref[...], jnp.zeros((LANES - SUBLANES, tm), F32)], axis=0)
    gcol = gpad.T
    h2 = h1_ref[...]
    for k in range(TOP_K):
        ya, yb = _unpack_bf16_pairs(_load_row_slabs(yg_ref, 0, tm, n_slabs, lead=(k,)))
        h2 = h2 + gcol[:, k:k + 1] * jnp.concatenate([ya, yb], axis=-1)
    out_ref[...] = h2 * lax.rsqrt(jnp.mean(h2 * h2, axis=-1, keepdims=True) + EPS) * nw_ref[...]


def _final(h, gates_t, yg, final_w, n_slabs, tm, chunk, n_chunks):
    n, d = h.shape
    tiles = n // tm // n_chunks
    first = chunk * tiles
    return pl.pallas_call(
        functools.partial(_final_kernel, tm=tm, n_slabs=n_slabs),
        grid=(tiles,),
        in_specs=[pl.BlockSpec((tm, d), lambda i: (first + i, 0)),
                  pl.BlockSpec((SUBLANES, tm), lambda i: (0, first + i)),
                  pl.BlockSpec((TOP_K, tm * n_slabs, LANES), lambda i: (0, i, 0)),
                  pl.BlockSpec((1, d), lambda i: (0, 0))],
        out_specs=pl.BlockSpec((tm, d), lambda i: (first + i, 0)),
        out_shape=jax.ShapeDtypeStruct((n, d), F32),
        input_output_aliases={0: 0},
        compiler_params=_cparams("parallel"),
        name="final",
    )(h, gates_t, yg, final_w)


def _block_diag_tiles(w):
    nb, bs, _ = w.shape
    rows = jnp.tile(w.reshape(nb * bs // MXU_DIM, MXU_DIM, bs), (1, 1, MXU_DIM // bs))
    r_blk = lax.broadcasted_iota(jnp.int32, (MXU_DIM, MXU_DIM), 0) // bs
    c_blk = lax.broadcasted_iota(jnp.int32, (MXU_DIM, MXU_DIM), 1) // bs
    return jnp.where(r_blk == c_blk, rows, 0.0).astype(BF16)


def _layer(xf, bsz, seq, norm_mix_w, w_in, mlstm_conv_w, mlstm_conv_b, w_q, w_k, w_v, w_igate, b_igate,
           w_fgate, b_fgate, mlstm_norm_w, mlstm_skip, conv_dw_w, conv_dw_b, conv_norm_w, conv_norm_b,
           w_out, norm_ffn_w, w_router, b_router, w1, b1, w2, b2, final_norm_w):
    n, d = xf.shape
    d_mlstm = mlstm_norm_w.shape[0]
    d_conv = conv_norm_w.shape[0]
    n_slabs = d // (2 * LANES)
    r2 = lambda v: v.reshape(1, -1)

    xm, xc, sz, u = _in_proj(xf, r2(norm_mix_w), w_in.astype(BF16), mlstm_conv_w, r2(mlstm_conv_b),
                             d_mlstm, d_conv, seq, tm=512)

    wg = jnp.concatenate([w_igate, w_fgate], axis=1)
    wg = jnp.pad(wg, ((0, 0), (0, LANES - wg.shape[1]))).astype(BF16)
    bg = jnp.pad(jnp.concatenate([b_igate, b_fgate]), (0, LANES - 2 * N_HEADS)).reshape(1, LANES)
    ym = _mlstm(xm, xc, sz, _block_diag_tiles(w_q), _block_diag_tiles(w_k), _block_diag_tiles(w_v),
                wg, bg, r2(mlstm_norm_w), r2(mlstm_skip), bsz, seq)
    yc = _conv_group(u, conv_dw_w, r2(conv_dw_b), r2(conv_norm_w), r2(conv_norm_b), bsz, seq, tile=512)

    h1, a2s, idx_t, pos_t, gates_t, cnt = _out_proj(
        xf, ym, yc, w_out.astype(BF16), r2(norm_ffn_w), w_router.T.astype(BF16), b_router.reshape(-1, 1), d_mlstm, tm=512)

    counts = cnt[:, 0].astype(jnp.int32)
    ends = jnp.cumsum(counts)
    starts = ends - counts
    e_ids = jnp.arange(N_EXPERTS, dtype=jnp.int32)
    idx4 = idx_t[:TOP_K]
    dest = pos_t[:TOP_K] + jnp.sum(
        jnp.where(idx4[None] == e_ids[:, None, None], starts[:, None, None], 0), axis=0)

    n_rows = n * TOP_K
    rb = 1024
    n_blocks = n_rows // rb
    n_items = n_blocks + N_EXPERTS - 1
    first_blk = starts // rb
    last_blk = jnp.where(counts > 0, (ends - 1) // rb, first_blk - 1)
    per_e = last_blk - first_blk + 1
    item_end = jnp.cumsum(per_e)
    item_start = item_end - per_e
    ids = jnp.arange(n_items, dtype=jnp.int32)
    total = item_end[-1]
    item_valid = (ids < total).astype(jnp.int32)
    item_exp = jnp.minimum(jnp.sum((ids[:, None] >= item_end[None, :]).astype(jnp.int32), axis=1), N_EXPERTS - 1)
    last_valid_exp = jnp.max(jnp.where(per_e > 0, e_ids, 0))
    item_exp = jnp.where(item_valid == 1, item_exp, last_valid_exp).astype(jnp.int32)
    is_exp = item_exp[:, None] == e_ids[None, :]
    of_item = lambda table: jnp.sum(jnp.where(is_exp, table[None, :], 0), axis=1)
    item_blk = jnp.where(item_valid == 1, of_item(first_blk) + ids - of_item(item_start), n_blocks - 1)
    item_blk = item_blk.astype(jnp.int32)
    item_lo = (jnp.maximum(of_item(starts), item_blk * rb) - item_blk * rb).astype(jnp.int32)
    item_hi = (jnp.minimum(of_item(ends), (item_blk + 1) * rb) - item_blk * rb).astype(jnp.int32)

    xs = _sc_scatter_rows(a2s.reshape(n, n_slabs, LANES), dest.reshape(-1), TOP_K)
    ys = _experts(xs.reshape(n_rows * n_slabs, LANES), w1, b1[:, None, :], w2, b2[:, None, :],
                  item_blk, item_exp, item_valid, item_lo, item_hi, rb, n_slabs)
    ys = ys.reshape(n_rows, n_slabs, LANES)

    n_chunks = 8
    nc = n // n_chunks
    out = h1
    for c in range(n_chunks):
        yg = _sc_gather_rows(ys, dest[:, c * nc:(c + 1) * nc].reshape(-1))
        out = _final(out, gates_t, yg.reshape(TOP_K, nc * n_slabs, LANES), r2(final_norm_w), n_slabs,
                     tm=256, chunk=c, n_chunks=n_chunks)
    return out


def kernel(x, norm_mix_w, w_in, mlstm_conv_w, mlstm_conv_b, w_q, w_k, w_v, w_igate, b_igate, w_fgate, b_fgate,
           mlstm_norm_w, mlstm_skip, conv_dw_w, conv_dw_b, conv_norm_w, conv_norm_b, w_out, norm_ffn_w,
           w_router, b_router, w1, b1, w2, b2, final_norm_w):
    bsz, seq, d = x.shape
    assert norm_mix_w.shape[0] == 1, "single-layer block"
    out = _layer(x.reshape(bsz * seq, d), bsz, seq, norm_mix_w[0], w_in[0], mlstm_conv_w[0], mlstm_conv_b[0],
                 w_q[0], w_k[0], w_v[0], w_igate[0], b_igate[0], w_fgate[0], b_fgate[0], mlstm_norm_w[0],
                 mlstm_skip[0], conv_dw_w[0], conv_dw_b[0], conv_norm_w[0], conv_norm_b[0], w_out[0],
                 norm_ffn_w[0], w_router[0], b_router[0], w1[0], b1[0], w2[0], b2[0], final_norm_w)
    return out.reshape(bsz, seq, d)
```

```python
import functools

import jax
import jax.numpy as jnp
from jax import lax
from jax.experimental import pallas as pl
from jax.experimental.pallas import tpu as pltpu
from jax.experimental.pallas import tpu_sc as plsc

F32 = jnp.float32
BF16 = jnp.bfloat16

EPS = 1e-5
N_HEADS = 4
QKV_BLOCK = 4
MLSTM_CONV_WIDTH = 4
CONV_WIDTH = 31
N_EXPERTS = 32
TOP_K = 4
SWIGLU_ALPHA = 1.702
SWIGLU_LIMIT = 7.0

LANES = 128
SUBLANES = 8
MXU_DIM = 256
VMEM_LIMIT = 52 * 1024 * 1024
EXPERTS_VMEM_LIMIT = 58 * 1024 * 1024

MLSTM_CHUNK = 256
CONV_HALO = 32
NEG_INF = float("-inf")


def _sigmoid(x):
    return 0.5 * jnp.tanh(0.5 * x) + 0.5


def _silu(x):
    return x * _sigmoid(x)


def _pack_bf16_pairs(v):
    half = v.shape[1] // 2
    hi = lax.bitcast_convert_type(v[:, :half].astype(BF16).astype(F32), jnp.uint32)
    lo = lax.bitcast_convert_type(v[:, half:].astype(BF16).astype(F32), jnp.uint32)
    return hi | (lo >> 16)


def _unpack_bf16_pairs(w):
    hi = lax.bitcast_convert_type(w & jnp.uint32(0xFFFF0000), F32)
    lo = lax.bitcast_convert_type(w << 16, F32)
    return hi, lo


def _load_row_slabs(ref, first, rows, n_slabs, lead=()):
    return jnp.concatenate([ref[lead + (pl.ds(first * n_slabs + s, rows, stride=n_slabs), slice(None))]
                            for s in range(n_slabs)], axis=-1)


def _store_row_slabs(ref, first, v, n_slabs):
    rows = v.shape[0]
    for s in range(n_slabs):
        ref[pl.ds(first * n_slabs + s, rows, stride=n_slabs), :] = v[:, s * LANES:(s + 1) * LANES]


def _cparams(*sem):
    return pltpu.CompilerParams(dimension_semantics=sem, vmem_limit_bytes=VMEM_LIMIT)


def _inproj_kernel(x_ref, nw_ref, w_ref, cw_ref, cb_ref, xm_ref, xc_ref, sz_ref, u_ref, xbuf,
                   *, d_mlstm, d_conv, tiles_per_seq):
    tm = x_ref.shape[0]
    hist = SUBLANES
    i = pl.program_id(0)

    @pl.when(i == 0)
    def _():
        xbuf[...] = jnp.zeros_like(xbuf)

    x = x_ref[...]
    a = x * lax.rsqrt(jnp.mean(x * x, axis=-1, keepdims=True) + EPS) * nw_ref[...]
    ab = a.astype(BF16)
    same_seq = i % tiles_per_seq != 0
    cols = MXU_DIM

    for c in range(d_mlstm // cols):
        sl = slice(c * cols, (c + 1) * cols)
        prev = xbuf[hist:hist + tm, sl]
        acc = cb_ref[:, sl] + cw_ref[MLSTM_CONV_WIDTH - 1:MLSTM_CONV_WIDTH, sl] * prev
        for k in range(MLSTM_CONV_WIDTH - 1):
            off = hist - (MLSTM_CONV_WIDTH - 1) + k
            acc = acc + cw_ref[k:k + 1, sl] * xbuf[off:off + tm, sl]
        xc_ref[:, sl] = _silu(acc).astype(xc_ref.dtype)
        xbuf[0:hist, sl] = jnp.where(same_seq, prev[tm - hist:, :], 0.0)

        xm = jnp.dot(ab, w_ref[:, sl], preferred_element_type=F32)
        xm_ref[:, sl] = xm.astype(xm_ref.dtype)
        xbuf[hist:hist + tm, sl] = xm
        z = jnp.dot(ab, w_ref[:, d_mlstm + c * cols:d_mlstm + (c + 1) * cols], preferred_element_type=F32)
        sz_ref[:, sl] = _silu(z)

    for c in range(d_conv // cols):
        sl = slice(c * cols, (c + 1) * cols)
        ga = jnp.dot(ab, w_ref[:, 2 * d_mlstm + c * cols:2 * d_mlstm + (c + 1) * cols], preferred_element_type=F32)
        gb = jnp.dot(ab, w_ref[:, 2 * d_mlstm + d_conv + c * cols:2 * d_mlstm + d_conv + (c + 1) * cols],
                     preferred_element_type=F32)
        u_ref[:, sl] = ga * _sigmoid(gb)


def _in_proj(xf, norm_w, w_in_b, conv_w, conv_b, d_mlstm, d_conv, seq, tm):
    n, d = xf.shape
    n_tiles = n // tm
    cur = lambda i: (jnp.minimum(i, n_tiles - 1), 0)
    prv = lambda i: (jnp.maximum(i - 1, 0), 0)
    const = lambda i: (0, 0)
    return pl.pallas_call(
        functools.partial(_inproj_kernel, d_mlstm=d_mlstm, d_conv=d_conv, tiles_per_seq=seq // tm),
        grid=(n_tiles + 1,),
        in_specs=[pl.BlockSpec((tm, d), cur),
                  pl.BlockSpec((1, d), const),
                  pl.BlockSpec(w_in_b.shape, const),
                  pl.BlockSpec(conv_w.shape, const),
                  pl.BlockSpec(conv_b.shape, const)],
        out_specs=[pl.BlockSpec((tm, d_mlstm), cur),
                   pl.BlockSpec((tm, d_mlstm), prv),
                   pl.BlockSpec((tm, d_mlstm), cur),
                   pl.BlockSpec((tm, d_conv), cur)],
        out_shape=[jax.ShapeDtypeStruct((n, d_mlstm), BF16),
                   jax.ShapeDtypeStruct((n, d_mlstm), BF16),
                   jax.ShapeDtypeStruct((n, d_mlstm), F32),
                   jax.ShapeDtypeStruct((n, d_conv), F32)],
        scratch_shapes=[pltpu.VMEM((tm + SUBLANES, d_mlstm), F32)],
        compiler_params=_cparams("arbitrary"),
        name="in_proj",
    )(xf, norm_w, w_in_b, conv_w, conv_b)


def _split3(v):
    hi = v.astype(BF16)
    r1 = v - hi.astype(F32)
    mid = r1.astype(BF16)
    lo = (r1 - mid.astype(F32)).astype(BF16)
    return hi, mid, lo


def _mlstm_kernel(xm_ref, xc_ref, sz_ref, wq_ref, wk_ref, wv_ref, wg_ref, bg_ref, nw_ref, sk_ref,
                  ym_ref, c_sc, n_sc, m_sc, *, chunk, dh, n_seq):
    L = chunk
    nh = N_HEADS
    j = pl.program_id(1)

    @pl.when(j == 0)
    def _():
        c_sc[...] = jnp.zeros_like(c_sc)
        n_sc[...] = jnp.zeros_like(n_sc)
        m_sc[...] = jnp.zeros_like(m_sc)

    ri = lax.broadcasted_iota(jnp.int32, (L, L), 0)
    ci = lax.broadcasted_iota(jnp.int32, (L, L), 1)
    causal = ci <= ri
    tri = jnp.where(causal, 1.0, 0.0).astype(BF16)
    scale = dh ** -0.5

    for sq in range(n_seq):
        xmb = xm_ref[sq]
        xcb = xc_ref[sq]
        d = xmb.shape[1]
        nb = d // MXU_DIM

        def bd(xb, w_ref):
            return jnp.concatenate(
                [jnp.dot(xb[:, g * MXU_DIM:(g + 1) * MXU_DIM], w_ref[g], preferred_element_type=F32)
                 for g in range(nb)], axis=-1)

        q = bd(xcb, wq_ref)
        k_ = bd(xcb, wk_ref)
        v = bd(xmb, wv_ref)
        qb, kb, vb = q.astype(BF16), k_.astype(BF16), v.astype(BF16)

        g = (jnp.dot(qb, wg_ref[0:d, :], preferred_element_type=F32)
             + jnp.dot(kb, wg_ref[d:2 * d, :], preferred_element_type=F32)
             + jnp.dot(vb, wg_ref[2 * d:3 * d, :], preferred_element_type=F32)
             + bg_ref[...])
        col = lax.broadcasted_iota(jnp.int32, g.shape, 1)
        log_f = jnp.minimum(g, 0.0) - jnp.log(1.0 + jnp.exp(-jnp.abs(g)))
        gates = jnp.where(col < nh, g, jnp.where(col < 2 * nh, log_f, 0.0))
        cum = sum(jnp.dot(tri, part, preferred_element_type=F32) for part in _split3(gates))
        colform = jnp.where(col < nh, gates, cum)
        rowform = colform.T

        for h in range(nh):
            sl = slice(h * dh, (h + 1) * dh)
            li_row = rowform[h:h + 1, :]
            b_row = rowform[nh + h:nh + h + 1, :]
            li_col = colform[:, h:h + 1]
            b_col = colform[:, nh + h:nh + h + 1]
            m_prev = m_sc[sq, h:h + 1, 0:1]
            g_tot = b_row[:, L - 1:L]

            dmat = jnp.where(causal, b_col - b_row + li_row, NEG_INF)
            inter = b_col + m_prev
            m_i = jnp.maximum(inter, jnp.max(dmat, axis=-1, keepdims=True))
            w_intra = jnp.exp(dmat - m_i)
            w_inter = jnp.exp(inter - m_i)

            qh = qb[:, sl]
            s = lax.dot_general(qh, kb[:, sl], (((1,), (1,)), ((), ())),
                                preferred_element_type=F32) * (scale * w_intra)
            c_prev = c_sc[sq, h]
            n_prev = n_sc[sq, h:h + 1, :]
            num = (w_inter * scale) * jnp.dot(qh, c_prev.astype(BF16), preferred_element_type=F32) \
                + jnp.dot(s.astype(BF16), vb[:, sl], preferred_element_type=F32)
            qn = jnp.sum(q[:, sl] * n_prev, axis=-1, keepdims=True) * scale
            den = w_inter * qn + jnp.sum(s, axis=-1, keepdims=True)
            hh = num / jnp.maximum(jnp.abs(den), jnp.exp(-m_i))

            mu = jnp.mean(hh, axis=-1, keepdims=True)
            dev = hh - mu
            var = jnp.mean(dev * dev, axis=-1, keepdims=True)
            hn = dev * lax.rsqrt(var + EPS) * nw_ref[:, sl]
            ym_ref[sq, :, sl] = ((hn + sk_ref[:, sl] * xcb[:, sl].astype(F32)) * sz_ref[sq, :, sl]).astype(ym_ref.dtype)

            a_row = g_tot - b_row + li_row
            m_new = jnp.maximum(g_tot + m_prev, jnp.max(a_row, axis=-1, keepdims=True))
            a_col = g_tot - b_col + li_col
            w_state = jnp.exp(a_col - m_new)
            decay = jnp.exp(g_tot + m_prev - m_new)
            kw = k_[:, sl] * w_state
            c_sc[sq, h] = decay * c_prev + lax.dot_general(kw.astype(BF16), vb[:, sl], (((0,), (0,)), ((), ())),
                                                           preferred_element_type=F32)
            n_sc[sq, h:h + 1, :] = decay * n_prev + jnp.sum(kw, axis=0, keepdims=True)
            m_sc[sq, h:h + 1, :] = jnp.broadcast_to(m_new, (1, m_sc.shape[2]))


def _mlstm(xm, xc, sz, wq, wk, wv, wg, bg, norm_w, skip, bsz, seq):
    n, d = xm.shape
    L = MLSTM_CHUNK
    nc = seq // L
    dh = d // N_HEADS
    n_seq = 2 if bsz % 2 == 0 else 1
    blk = lambda b, j: (b, j, 0)
    c2 = lambda b, j: (0, 0)
    c3 = lambda b, j: (0, 0, 0)
    ym = pl.pallas_call(
        functools.partial(_mlstm_kernel, chunk=L, dh=dh, n_seq=n_seq),
        grid=(bsz // n_seq, nc),
        in_specs=[pl.BlockSpec((n_seq, L, d), blk), pl.BlockSpec((n_seq, L, d), blk), pl.BlockSpec((n_seq, L, d), blk),
                  pl.BlockSpec(wq.shape, c3), pl.BlockSpec(wk.shape, c3), pl.BlockSpec(wv.shape, c3),
                  pl.BlockSpec(wg.shape, c2), pl.BlockSpec(bg.shape, c2),
                  pl.BlockSpec(norm_w.shape, c2), pl.BlockSpec(skip.shape, c2)],
        out_specs=pl.BlockSpec((n_seq, L, d), blk),
        out_shape=jax.ShapeDtypeStruct((bsz, seq, d), BF16),
        scratch_shapes=[pltpu.VMEM((n_seq, N_HEADS, dh, dh), F32),
                        pltpu.VMEM((n_seq, SUBLANES, dh), F32),
                        pltpu.VMEM((n_seq, SUBLANES, LANES), F32)],
        compiler_params=_cparams("arbitrary", "arbitrary"),
        name="mlstm",
    )(xm.reshape(bsz, seq, d), xc.reshape(bsz, seq, d), sz.reshape(bsz, seq, d), wq, wk, wv, wg, bg, norm_w, skip)
    return ym.reshape(n, d)


def _conv_kernel(u_ref, w_ref, b_ref, nw_ref, nb_ref, yc_ref, ubuf, pbuf, cbuf, *, tile, rows):
    T = tile
    j = pl.program_id(1)

    @pl.when(j == 0)
    def _():
        ubuf[0:CONV_HALO, :] = jnp.zeros((CONV_HALO, ubuf.shape[1]), F32)

    ubuf[CONV_HALO:CONV_HALO + T, :] = u_ref[...]
    base = CONV_HALO - (CONV_WIDTH - 1)
    span = T + CONV_HALO - SUBLANES
    n_lane_blocks = ubuf.shape[1] // LANES

    def lane_block(c, carry):
        lanes = pl.ds(pl.multiple_of(c * LANES, LANES), LANES)
        for r in range(1, SUBLANES):
            pbuf[r - 1, :, :] = ubuf[r:r + span, lanes]
        for r0 in range(0, T, rows):
            acc = jnp.broadcast_to(b_ref[:, lanes], (rows, LANES))
            for k in range(CONV_WIDTH):
                q, r = divmod(base + k, SUBLANES)
                lo = r0 + q * SUBLANES
                src = ubuf[lo:lo + rows, lanes] if r == 0 else pbuf[r - 1, lo:lo + rows, :]
                acc = acc + w_ref[k:k + 1, lanes] * src
            cbuf[r0:r0 + rows, lanes] = acc
        return carry

    lax.fori_loop(0, n_lane_blocks, lane_block, 0)
    ubuf[0:CONV_HALO, :] = ubuf[T:T + CONV_HALO, :]

    y = cbuf[...]
    mu = jnp.mean(y, axis=-1, keepdims=True)
    dev = y - mu
    var = jnp.mean(dev * dev, axis=-1, keepdims=True)
    yn = dev * lax.rsqrt(var + EPS) * nw_ref[...] + nb_ref[...]
    yc_ref[...] = _silu(yn).astype(yc_ref.dtype)


def _conv_group(u, w, b, norm_w, norm_b, bsz, seq, tile):
    n, d = u.shape
    nt = seq // tile
    row = lambda bi, j: (bi * nt + j, 0)
    c2 = lambda bi, j: (0, 0)
    return pl.pallas_call(
        functools.partial(_conv_kernel, tile=tile, rows=64),
        grid=(bsz, nt),
        in_specs=[pl.BlockSpec((tile, d), row), pl.BlockSpec(w.shape, c2), pl.BlockSpec(b.shape, c2),
                  pl.BlockSpec(norm_w.shape, c2), pl.BlockSpec(norm_b.shape, c2)],
        out_specs=pl.BlockSpec((tile, d), row),
        out_shape=jax.ShapeDtypeStruct((n, d), BF16),
        scratch_shapes=[pltpu.VMEM((tile + CONV_HALO, d), F32),
                        pltpu.VMEM((SUBLANES - 1, tile + CONV_HALO - SUBLANES, LANES), F32),
                        pltpu.VMEM((tile, d), F32)],
        compiler_params=_cparams("arbitrary", "arbitrary"),
        name="conv_group",
    )(u, w, b, norm_w, norm_b)


def _outproj_kernel(x_ref, ym_ref, yc_ref, wo_ref, nw_ref, wr_ref, br_ref,
                    h1_ref, a2_ref, idx_ref, pos_ref, gate_ref, cnt_ref, cnt_sc, *, tm, d_mlstm):
    i = pl.program_id(0)

    @pl.when(i == 0)
    def _():
        cnt_sc[...] = jnp.zeros_like(cnt_sc)

    h1 = (x_ref[...]
          + jnp.dot(ym_ref[...], wo_ref[0:d_mlstm, :], preferred_element_type=F32)
          + jnp.dot(yc_ref[...], wo_ref[d_mlstm:, :], preferred_element_type=F32))
    h1_ref[...] = h1
    a2 = h1 * lax.rsqrt(jnp.mean(h1 * h1, axis=-1, keepdims=True) + EPS) * nw_ref[...]
    _store_row_slabs(a2_ref, 0, _pack_bf16_pairs(a2), a2.shape[1] // (2 * LANES))

    logits = lax.dot_general(wr_ref[...], a2.astype(BF16), (((1,), (1,)), ((), ())),
                             preferred_element_type=F32) + br_ref[...]
    e_iota = lax.broadcasted_iota(jnp.int32, logits.shape, 0)
    work = logits
    vals, idxs = [], []
    for _ in range(TOP_K):
        mx = jnp.max(work, axis=0, keepdims=True)
        sel = jnp.min(jnp.where(work == mx, e_iota, N_EXPERTS), axis=0, keepdims=True)
        vals.append(mx)
        idxs.append(sel)
        work = jnp.where(e_iota == sel, NEG_INF, work)
    exps = [jnp.exp(vv - vals[0]) for vv in vals]
    tot = exps[0] + exps[1] + exps[2] + exps[3]
    gates = [ev / tot for ev in exps]

    chosen = functools.reduce(jnp.logical_or, [e_iota == sel for sel in idxs])
    mh = jnp.where(chosen, 1.0, 0.0)
    ri = lax.broadcasted_iota(jnp.int32, (tm, tm), 0)
    ci = lax.broadcasted_iota(jnp.int32, (tm, tm), 1)
    upper = jnp.where(ri < ci, 1.0, 0.0).astype(BF16)
    rank = jnp.dot(mh.astype(BF16), upper, preferred_element_type=F32) + cnt_sc[:, 0:1]
    cnt_new = cnt_sc[...] + jnp.sum(mh, axis=1, keepdims=True)
    cnt_sc[...] = cnt_new
    cnt_ref[...] = cnt_new

    zero_i = jnp.zeros((SUBLANES - TOP_K, tm), jnp.int32)
    pos = [jnp.sum(jnp.where(e_iota == sel, rank, 0.0), axis=0, keepdims=True).astype(jnp.int32) for sel in idxs]
    idx_ref[...] = jnp.concatenate(idxs + [zero_i], axis=0)
    pos_ref[...] = jnp.concatenate(pos + [zero_i], axis=0)
    gate_ref[...] = jnp.concatenate(gates + [jnp.zeros((SUBLANES - TOP_K, tm), F32)], axis=0)


def _out_proj(xf, ym, yc, w_out_b, norm_w, w_router_t, b_router, d_mlstm, tm):
    n, d = xf.shape
    n_slabs = d // (2 * LANES)
    row = lambda i: (i, 0)
    colb = lambda i: (0, i)
    const = lambda i: (0, 0)
    return pl.pallas_call(
        functools.partial(_outproj_kernel, tm=tm, d_mlstm=d_mlstm),
        grid=(n // tm,),
        in_specs=[pl.BlockSpec((tm, d), row), pl.BlockSpec((tm, ym.shape[1]), row), pl.BlockSpec((tm, yc.shape[1]), row),
                  pl.BlockSpec(w_out_b.shape, const), pl.BlockSpec(norm_w.shape, const),
                  pl.BlockSpec(w_router_t.shape, const), pl.BlockSpec(b_router.shape, const)],
        out_specs=[pl.BlockSpec((tm, d), row),
                   pl.BlockSpec((tm * n_slabs, LANES), row),
                   pl.BlockSpec((SUBLANES, tm), colb),
                   pl.BlockSpec((SUBLANES, tm), colb),
                   pl.BlockSpec((SUBLANES, tm), colb),
                   pl.BlockSpec((N_EXPERTS, LANES), const)],
        out_shape=[jax.ShapeDtypeStruct((n, d), F32),
                   jax.ShapeDtypeStruct((n * n_slabs, LANES), jnp.uint32),
                   jax.ShapeDtypeStruct((SUBLANES, n), jnp.int32),
                   jax.ShapeDtypeStruct((SUBLANES, n), jnp.int32),
                   jax.ShapeDtypeStruct((SUBLANES, n), F32),
                   jax.ShapeDtypeStruct((N_EXPERTS, LANES), F32)],
        scratch_shapes=[pltpu.VMEM((N_EXPERTS, LANES), F32)],
        compiler_params=_cparams("arbitrary"),
        name="out_proj_router",
    )(xf, ym, yc, w_out_b, norm_w, w_router_t, b_router)


def _experts_kernel(blk_ref, exp_ref, valid_ref, lo_ref, hi_ref,
                    xs_ref, w1_ref, b1_ref, w2_ref, b2_ref, ys_ref, w1c_ref, w2c_ref, acc_ref, *, rb, d_ff, n_sub):
    i = pl.program_id(0)
    n_slabs = xs_ref.shape[0] // rb
    prev = jnp.maximum(i - 1, 0)
    first_visit = jnp.logical_or(i == 0, blk_ref[prev] != blk_ref[i])
    new_expert = jnp.logical_or(i == 0, exp_ref[prev] != exp_ref[i])

    @pl.when(new_expert)
    def _():
        w1c_ref[...] = w1_ref[0].astype(BF16)
        w2c_ref[...] = w2_ref[0].astype(BF16)

    @pl.when(first_visit)
    def _():
        acc_ref[...] = jnp.zeros_like(acc_ref)

    rs = rb // n_sub
    lo, hi = lo_ref[i], hi_ref[i]

    def sub_block(c):
        xa, xb = _unpack_bf16_pairs(_load_row_slabs(xs_ref, c * rs, rs, n_slabs))
        x = jnp.concatenate([xa.astype(BF16), xb.astype(BF16)], axis=-1)
        hid = jnp.dot(x, w1c_ref[...], preferred_element_type=F32) + b1_ref[0]
        x_glu = jnp.minimum(hid[:, :d_ff], SWIGLU_LIMIT)
        x_lin = jnp.clip(hid[:, d_ff:], -SWIGLU_LIMIT, SWIGLU_LIMIT)
        act = x_glu * _sigmoid(SWIGLU_ALPHA * x_glu) * (x_lin + 1.0)
        y = jnp.dot(act.astype(BF16), w2c_ref[...], preferred_element_type=F32) + b2_ref[0]
        r = c * rs + lax.broadcasted_iota(jnp.int32, (rs, 1), 0)
        y = jnp.where(jnp.logical_and(r >= lo, r < hi), y, 0.0)
        y = acc_ref[c * rs:(c + 1) * rs, :] + y
        acc_ref[c * rs:(c + 1) * rs, :] = y
        _store_row_slabs(ys_ref, c * rs, _pack_bf16_pairs(y), n_slabs)

    for c in range(n_sub):
        has_rows = jnp.logical_and(valid_ref[i] == 1, jnp.logical_and(lo < (c + 1) * rs, hi > c * rs))
        pl.when(has_rows)(functools.partial(sub_block, c))


def _experts(xs, w1, b1, w2, b2, item_blk, item_exp, item_valid, item_lo, item_hi, rb, n_slabs):
    n_items = item_blk.shape[0]
    d = w1.shape[1]
    d_ff = w2.shape[1]
    by_blk = lambda i, blk, ex, va, lo, hi: (blk[i], 0)
    by_exp = lambda i, blk, ex, va, lo, hi: (ex[i], 0, 0)
    grid_spec = pltpu.PrefetchScalarGridSpec(
        num_scalar_prefetch=5,
        grid=(n_items,),
        in_specs=[pl.BlockSpec((rb * n_slabs, LANES), by_blk),
                  pl.BlockSpec((1,) + w1.shape[1:], by_exp),
                  pl.BlockSpec((1,) + b1.shape[1:], by_exp),
                  pl.BlockSpec((1,) + w2.shape[1:], by_exp),
                  pl.BlockSpec((1,) + b2.shape[1:], by_exp)],
        out_specs=pl.BlockSpec((rb * n_slabs, LANES), by_blk),
        scratch_shapes=[pltpu.VMEM(w1.shape[1:], BF16), pltpu.VMEM(w2.shape[1:], BF16), pltpu.VMEM((rb, d), F32)],
    )
    return pl.pallas_call(
        functools.partial(_experts_kernel, rb=rb, d_ff=d_ff, n_sub=rb // MXU_DIM),
        grid_spec=grid_spec,
        out_shape=jax.ShapeDtypeStruct(xs.shape, jnp.uint32),
        compiler_params=pltpu.CompilerParams(dimension_semantics=("arbitrary",), vmem_limit_bytes=EXPERTS_VMEM_LIMIT),
        name="experts",
    )(item_blk, item_exp, item_valid, item_lo, item_hi, xs, w1, b1, w2, b2)


SC_CORES = 2
SC_SUBCORES = 16
SC_GATHER_WINDOW = 64


def _sc_worker_base(per_worker):
    wid = lax.axis_index("s") * SC_CORES + lax.axis_index("c")
    return wid * per_worker


def _sc_gather_rows(table, idx):
    m = idx.shape[0]
    n_workers = SC_CORES * SC_SUBCORES
    window = SC_GATHER_WINDOW
    per_worker = m // n_workers
    n_win = per_worker // window
    assert per_worker * n_workers == m and n_win * window == per_worker and n_win % 2 == 0
    mesh = plsc.VectorSubcoreMesh(core_axis_name="c", subcore_axis_name="s")
    slab = table.shape[1:]

    @functools.partial(
        pl.kernel, mesh=mesh,
        out_type=jax.ShapeDtypeStruct((m,) + slab, table.dtype),
        scratch_types=[pltpu.VMEM((per_worker,), jnp.int32),
                       pltpu.VMEM((window,) + slab, table.dtype), pltpu.VMEM((window,) + slab, table.dtype)]
        + [pltpu.SemaphoreType.DMA] * 4,
        name="sc_gather_rows",
    )
    def gather(table_hbm, idx_hbm, out_hbm, idx_v, rows0, rows1, gsem0, gsem1, wsem0, wsem1):
        rows_v, gsem, wsem = (rows0, rows1), (gsem0, gsem1), (wsem0, wsem1)
        base = _sc_worker_base(per_worker)
        pltpu.sync_copy(idx_hbm.at[pl.ds(pl.multiple_of(base, window), per_worker)], idx_v)

        def gather_copy(w, b):
            ids = idx_v.at[pl.ds(pl.multiple_of(w * window, window), window)]
            return pltpu.make_async_copy(table_hbm.at[ids], rows_v[b], gsem[b])

        def write_copy(w, b):
            dst = out_hbm.at[pl.ds(pl.multiple_of(base + w * window, window), window)]
            return pltpu.make_async_copy(rows_v[b], dst, wsem[b])

        gather_copy(0, 0).start()

        @pl.loop(0, n_win, step=2)
        def _(w):
            @pl.when(w >= 1)
            def _():
                write_copy(w - 1, 1).wait()

            gather_copy(w + 1, 1).start()
            gather_copy(w, 0).wait()
            write_copy(w, 0).start()

            @pl.when(w + 2 < n_win)
            def _():
                write_copy(w, 0).wait()
                gather_copy(w + 2, 0).start()

            gather_copy(w + 1, 1).wait()
            write_copy(w + 1, 1).start()

        write_copy(n_win - 2, 0).wait()
        write_copy(n_win - 1, 1).wait()

    return gather(table, idx)


def _sc_scatter_rows(rows, dest, n_slots):
    n = rows.shape[0]
    n_workers = SC_CORES * SC_SUBCORES
    window = SC_GATHER_WINDOW
    per_worker = n // n_workers
    n_win = per_worker // window
    assert per_worker * n_workers == n and n_win * window == per_worker and n_win % 2 == 0
    mesh = plsc.VectorSubcoreMesh(core_axis_name="c", subcore_axis_name="s")
    slab = rows.shape[1:]

    @functools.partial(
        pl.kernel, mesh=mesh,
        out_type=jax.ShapeDtypeStruct((n_slots * n,) + slab, rows.dtype),
        scratch_types=[pltpu.VMEM((window,), jnp.int32)] * (2 * n_slots)
        + [pltpu.VMEM((window,) + slab, rows.dtype)] * 2 + [pltpu.SemaphoreType.DMA] * 4,
        name="sc_scatter_rows",
    )
    def scatter(rows_hbm, dest_hbm, out_hbm, *scratch):
        idx_v = (scratch[:n_slots], scratch[n_slots:2 * n_slots])
        rows_v = scratch[2 * n_slots:2 * n_slots + 2]
        rsem = scratch[2 * n_slots + 2:2 * n_slots + 4]
        wsem = scratch[2 * n_slots + 4:2 * n_slots + 6]
        base = _sc_worker_base(per_worker)

        def rows_at(w, k=0):
            return pl.ds(pl.multiple_of(k * n + base + w * window, window), window)

        def start_read(w, b):
            for k in range(n_slots):
                pltpu.sync_copy(dest_hbm.at[rows_at(w, k)], idx_v[b][k])
            pltpu.async_copy(rows_hbm.at[rows_at(w)], rows_v[b], rsem[b])

        def scatter_window(w, b):
            pltpu.make_async_copy(rows_hbm.at[rows_at(w)], rows_v[b], rsem[b]).wait()
            for k in range(n_slots):
                pltpu.async_copy(rows_v[b], out_hbm.at[idx_v[b][k]], wsem[b])
            for k in range(n_slots):
                pltpu.make_async_copy(rows_v[b], out_hbm.at[idx_v[b][k]], wsem[b]).wait()

        start_read(0, 0)

        @pl.loop(0, n_win, step=2)
        def _(w):
            start_read(w + 1, 1)
            scatter_window(w, 0)

            @pl.when(w + 2 < n_win)
            def _():
                start_read(w + 2, 0)

            scatter_window(w + 1, 1)

    return scatter(rows, dest)


def _final_kernel(h1_ref, gate_ref, yg_ref, nw_ref, out_ref, *, tm, n_slabs):
    gpad = jnp.concatenate([gate_ref[...], jnp.zeros((LANES - SUBLANES, tm), F32)], axis=0)
    gcol = gpad.T
    h2 = h1_ref[...]
    for k in range(TOP_K):
        ya, yb = _unpack_bf16_pairs(_load_row_slabs(yg_ref, 0, tm, n_slabs, lead=(k,)))
        h2 = h2 + gcol[:, k:k + 1] * jnp.concatenate([ya, yb], axis=-1)
    out_ref[...] = h2 * lax.rsqrt(jnp.mean(h2 * h2, axis=-1, keepdims=True) + EPS) * nw_ref[...]


def _final(h, gates_t, yg, final_w, n_slabs, tm, chunk, n_chunks):
    n, d = h.shape
    tiles = n // tm // n_chunks
    first = chunk * tiles
    return pl.pallas_call(
        functools.partial(_final_kernel, tm=tm, n_slabs=n_slabs),
        grid=(tiles,),
        in_specs=[pl.BlockSpec((tm, d), lambda i: (first + i, 0)),
                  pl.BlockSpec((SUBLANES, tm), lambda i: (0, first + i)),
                  pl.BlockSpec((TOP_K, tm * n_slabs, LANES), lambda i: (0, i, 0)),
                  pl.BlockSpec((1, d), lambda i: (0, 0))],
        out_specs=pl.BlockSpec((tm, d), lambda i: (first + i, 0)),
        out_shape=jax.ShapeDtypeStruct((n, d), F32),
        input_output_aliases={0: 0},
        compiler_params=_cparams("parallel"),
        name="final",
    )(h, gates_t, yg, final_w)


def _block_diag_tiles(w):
    nb, bs, _ = w.shape
    rows = jnp.tile(w.reshape(nb * bs // MXU_DIM, MXU_DIM, bs), (1, 1, MXU_DIM // bs))
    r_blk = lax.broadcasted_iota(jnp.int32, (MXU_DIM, MXU_DIM), 0) // bs
    c_blk = lax.broadcasted_iota(jnp.int32, (MXU_DIM, MXU_DIM), 1) // bs
    return jnp.where(r_blk == c_blk, rows, 0.0).astype(BF16)


def _layer(xf, bsz, seq, norm_mix_w, w_in, mlstm_conv_w, mlstm_conv_b, w_q, w_k, w_v, w_igate, b_igate,
           w_fgate, b_fgate, mlstm_norm_w, mlstm_skip, conv_dw_w, conv_dw_b, conv_norm_w, conv_norm_b,
           w_out, norm_ffn_w, w_router, b_router, w1, b1, w2, b2, final_norm_w):
    n, d = xf.shape
    d_mlstm = mlstm_norm_w.shape[0]
    d_conv = conv_norm_w.shape[0]
    n_slabs = d // (2 * LANES)
    r2 = lambda v: v.reshape(1, -1)

    xm, xc, sz, u = _in_proj(xf, r2(norm_mix_w), w_in.astype(BF16), mlstm_conv_w, r2(mlstm_conv_b),
                             d_mlstm, d_conv, seq, tm=512)

    wg = jnp.concatenate([w_igate, w_fgate], axis=1)
    wg = jnp.pad(wg, ((0, 0), (0, LANES - wg.shape[1]))).astype(BF16)
    bg = jnp.pad(jnp.concatenate([b_igate, b_fgate]), (0, LANES - 2 * N_HEADS)).reshape(1, LANES)
    ym = _mlstm(xm, xc, sz, _block_diag_tiles(w_q), _block_diag_tiles(w_k), _block_diag_tiles(w_v),
                wg, bg, r2(mlstm_norm_w), r2(mlstm_skip), bsz, seq)
    yc = _conv_group(u, conv_dw_w, r2(conv_dw_b), r2(conv_norm_w), r2(conv_norm_b), bsz, seq, tile=512)

    h1, a2s, idx_t, pos_t, gates_t, cnt = _out_proj(
        xf, ym, yc, w_out.astype(BF16), r2(norm_ffn_w), w_router.T.astype(BF16), b_router.reshape(-1, 1), d_mlstm, tm=512)

    counts = cnt[:, 0].astype(jnp.int32)
    ends = jnp.cumsum(counts)
    starts = ends - counts
    e_ids = jnp.arange(N_EXPERTS, dtype=jnp.int32)
    idx4 = idx_t[:TOP_K]
    dest = pos_t[:TOP_K] + jnp.sum(
        jnp.where(idx4[None] == e_ids[:, None, None], starts[:, None, None], 0), axis=0)

    n_rows = n * TOP_K
    rb = 1024
    n_blocks = n_rows // rb
    n_items = n_blocks + N_EXPERTS - 1
    first_blk = starts // rb
    last_blk = jnp.where(counts > 0, (ends - 1) // rb, first_blk - 1)
    per_e = last_blk - first_blk + 1
    item_end = jnp.cumsum(per_e)
    item_start = item_end - per_e
    ids = jnp.arange(n_items, dtype=jnp.int32)
    total = item_end[-1]
    item_valid = (ids < total).astype(jnp.int32)
    item_exp = jnp.minimum(jnp.sum((ids[:, None] >= item_end[None, :]).astype(jnp.int32), axis=1), N_EXPERTS - 1)
    last_valid_exp = jnp.max(jnp.where(per_e > 0, e_ids, 0))
    item_exp = jnp.where(item_valid == 1, item_exp, last_valid_exp).astype(jnp.int32)
    is_exp = item_exp[:, None] == e_ids[None, :]
    of_item = lambda table: jnp.sum(jnp.where(is_exp, table[None, :], 0), axis=1)
    item_blk = jnp.where(item_valid == 1, of_item(first_blk) + ids - of_item(item_start), n_blocks - 1)
    item_blk = item_blk.astype(jnp.int32)
    item_lo = (jnp.maximum(of_item(starts), item_blk * rb) - item_blk * rb).astype(jnp.int32)
    item_hi = (jnp.minimum(of_item(ends), (item_blk + 1) * rb) - item_blk * rb).astype(jnp.int32)

    xs = _sc_scatter_rows(a2s.reshape(n, n_slabs, LANES), dest.reshape(-1), TOP_K)
    ys = _experts(xs.reshape(n_rows * n_slabs, LANES), w1, b1[:, None, :], w2, b2[:, None, :],
                  item_blk, item_exp, item_valid, item_lo, item_hi, rb, n_slabs)
    ys = ys.reshape(n_rows, n_slabs, LANES)

    n_chunks = 8
    nc = n // n_chunks
    out = h1
    for c in range(n_chunks):
        yg = _sc_gather_rows(ys, dest[:, c * nc:(c + 1) * nc].reshape(-1))
        out = _final(out, gates_t, yg.reshape(TOP_K, nc * n_slabs, LANES), r2(final_norm_w), n_slabs,
                     tm=256, chunk=c, n_chunks=n_chunks)
    return out


def kernel(x, norm_mix_w, w_in, mlstm_conv_w, mlstm_conv_b, w_q, w_k, w_v, w_igate, b_igate, w_fgate, b_fgate,
           mlstm_norm_w, mlstm_skip, conv_dw_w, conv_dw_b, conv_norm_w, conv_norm_b, w_out, norm_ffn_w,
           w_router, b_router, w1, b1, w2, b2, final_norm_w):
    bsz, seq, d = x.shape
    assert norm_mix_w.shape[0] == 1, "single-layer block"
    out = _layer(x.reshape(bsz * seq, d), bsz, seq, norm_mix_w[0], w_in[0], mlstm_conv_w[0], mlstm_conv_b[0],
                 w_q[0], w_k[0], w_v[0], w_igate[0], b_igate[0], w_fgate[0], b_fgate[0], mlstm_norm_w[0],
                 mlstm_skip[0], conv_dw_w[0], conv_dw_b[0], conv_norm_w[0], conv_norm_b[0], w_out[0],
                 norm_ffn_w[0], w_router[0], b_router[0], w1[0], b1[0], w2[0], b2[0], final_norm_w)
    return out.reshape(bsz, seq, d)
```

```python
import functools

import jax
import jax.numpy as jnp
from jax import lax
from jax.experimental import pallas as pl
from jax.experimental.pallas import tpu as pltpu
from jax.experimental.pallas import tpu_sc as plsc

F32 = jnp.float32
BF16 = jnp.bfloat16

EPS = 1e-5
N_HEADS = 4
QKV_BLOCK = 4
MLSTM_CONV_WIDTH = 4
CONV_WIDTH = 31
N_EXPERTS = 32
TOP_K = 4
SWIGLU_ALPHA = 1.702
SWIGLU_LIMIT = 7.0

LANES = 128
SUBLANES = 8
MXU_DIM = 256
VMEM_LIMIT = 52 * 1024 * 1024
EXPERTS_VMEM_LIMIT = 58 * 1024 * 1024

MLSTM_CHUNK = 256
CONV_HALO = 32
NEG_INF = float("-inf")


def _sigmoid(x):
    return 0.5 * jnp.tanh(0.5 * x) + 0.5


def _silu(x):
    return x * _sigmoid(x)


def _pack_bf16_pairs(v):
    half = v.shape[1] // 2
    hi = lax.bitcast_convert_type(v[:, :half].astype(BF16).astype(F32), jnp.uint32)
    lo = lax.bitcast_convert_type(v[:, half:].astype(BF16).astype(F32), jnp.uint32)
    return hi | (lo >> 16)


def _unpack_bf16_pairs(w):
    hi = lax.bitcast_convert_type(w & jnp.uint32(0xFFFF0000), F32)
    lo = lax.bitcast_convert_type(w << 16, F32)
    return hi, lo


def _load_row_slabs(ref, first, rows, n_slabs, lead=()):
    return jnp.concatenate([ref[lead + (pl.ds(first * n_slabs + s, rows, stride=n_slabs), slice(None))]
                            for s in range(n_slabs)], axis=-1)


def _store_row_slabs(ref, first, v, n_slabs):
    rows = v.shape[0]
    for s in range(n_slabs):
        ref[pl.ds(first * n_slabs + s, rows, stride=n_slabs), :] = v[:, s * LANES:(s + 1) * LANES]


def _cparams(*sem):
    return pltpu.CompilerParams(dimension_semantics=sem, vmem_limit_bytes=VMEM_LIMIT)


def _inproj_kernel(x_ref, nw_ref, w_ref, cw_ref, cb_ref, xm_ref, xc_ref, sz_ref, u_ref, xbuf,
                   *, d_mlstm, d_conv, tiles_per_seq):
    tm = x_ref.shape[0]
    hist = SUBLANES
    i = pl.program_id(0)

    @pl.when(i == 0)
    def _():
        xbuf[...] = jnp.zeros_like(xbuf)

    x = x_ref[...]
    a = x * lax.rsqrt(jnp.mean(x * x, axis=-1, keepdims=True) + EPS) * nw_ref[...]
    ab = a.astype(BF16)
    same_seq = i % tiles_per_seq != 0
    cols = MXU_DIM

    for c in range(d_mlstm // cols):
        sl = slice(c * cols, (c + 1) * cols)
        prev = xbuf[hist:hist + tm, sl]
        acc = cb_ref[:, sl] + cw_ref[MLSTM_CONV_WIDTH - 1:MLSTM_CONV_WIDTH, sl] * prev
        for k in range(MLSTM_CONV_WIDTH - 1):
            off = hist - (MLSTM_CONV_WIDTH - 1) + k
            acc = acc + cw_ref[k:k + 1, sl] * xbuf[off:off + tm, sl]
        xc_ref[:, sl] = _silu(acc).astype(xc_ref.dtype)
        xbuf[0:hist, sl] = jnp.where(same_seq, prev[tm - hist:, :], 0.0)

        xm = jnp.dot(ab, w_ref[:, sl], preferred_element_type=F32)
        xm_ref[:, sl] = xm.astype(xm_ref.dtype)
        xbuf[hist:hist + tm, sl] = xm
        z = jnp.dot(ab, w_ref[:, d_mlstm + c * cols:d_mlstm + (c + 1) * cols], preferred_element_type=F32)
        sz_ref[:, sl] = _silu(z)

    for c in range(d_conv // cols):
        sl = slice(c * cols, (c + 1) * cols)
        ga = jnp.dot(ab, w_ref[:, 2 * d_mlstm + c * cols:2 * d_mlstm + (c + 1) * cols], preferred_element_type=F32)
        gb = jnp.dot(ab, w_ref[:, 2 * d_mlstm + d_conv + c * cols:2 * d_mlstm + d_conv + (c + 1) * cols],
                     preferred_element_type=F32)
        u_ref[:, sl] = ga * _sigmoid(gb)


def _in_proj(xf, norm_w, w_in_b, conv_w, conv_b, d_mlstm, d_conv, seq, tm):
    n, d = xf.shape
    n_tiles = n // tm
    cur = lambda i: (jnp.minimum(i, n_tiles - 1), 0)
    prv = lambda i: (jnp.maximum(i - 1, 0), 0)
    const = lambda i: (0, 0)
    return pl.pallas_call(
        functools.partial(_inproj_kernel, d_mlstm=d_mlstm, d_conv=d_conv, tiles_per_seq=seq // tm),
        grid=(n_tiles + 1,),
        in_specs=[pl.BlockSpec((tm, d), cur),
                  pl.BlockSpec((1, d), const),
                  pl.BlockSpec(w_in_b.shape, const),
                  pl.BlockSpec(conv_w.shape, const),
                  pl.BlockSpec(conv_b.shape, const)],
        out_specs=[pl.BlockSpec((tm, d_mlstm), cur),
                   pl.BlockSpec((tm, d_mlstm), prv),
                   pl.BlockSpec((tm, d_mlstm), cur),
                   pl.BlockSpec((tm, d_conv), cur)],
        out_shape=[jax.ShapeDtypeStruct((n, d_mlstm), BF16),
                   jax.ShapeDtypeStruct((n, d_mlstm), BF16),
                   jax.ShapeDtypeStruct((n, d_mlstm), F32),
                   jax.ShapeDtypeStruct((n, d_conv), F32)],
        scratch_shapes=[pltpu.VMEM((tm + SUBLANES, d_mlstm), F32)],
        compiler_params=_cparams("arbitrary"),
        name="in_proj",
    )(xf, norm_w, w_in_b, conv_w, conv_b)


def _split3(v):
    hi = v.astype(BF16)
    r1 = v - hi.astype(F32)
    mid = r1.astype(BF16)
    lo = (r1 - mid.astype(F32)).astype(BF16)
    return hi, mid, lo


def _mlstm_kernel(xm_ref, xc_ref, sz_ref, wq_ref, wk_ref, wv_ref, wg_ref, bg_ref, nw_ref, sk_ref,
                  ym_ref, c_sc, n_sc, m_sc, *, chunk, dh, n_seq):
    L = chunk
    nh = N_HEADS
    j = pl.program_id(1)

    @pl.when(j == 0)
    def _():
        c_sc[...] = jnp.zeros_like(c_sc)
        n_sc[...] = jnp.zeros_like(n_sc)
        m_sc[...] = jnp.zeros_like(m_sc)

    ri = lax.broadcasted_iota(jnp.int32, (L, L), 0)
    ci = lax.broadcasted_iota(jnp.int32, (L, L), 1)
    causal = ci <= ri
    tri = jnp.where(causal, 1.0, 0.0).astype(BF16)

    for sq in range(n_seq):
        xmb = xm_ref[sq]
        xcb = xc_ref[sq]
        d = xmb.shape[1]
        nb = d // MXU_DIM

        def bd(xb, w_ref):
            return jnp.concatenate(
                [jnp.dot(xb[:, g * MXU_DIM:(g + 1) * MXU_DIM], w_ref[g], preferred_element_type=F32)
                 for g in range(nb)], axis=-1)

        q = bd(xcb, wq_ref)
        k_ = bd(xcb, wk_ref)
        v = bd(xmb, wv_ref)
        qb, kb, vb = q.astype(BF16), k_.astype(BF16), v.astype(BF16)

        g = (jnp.dot(qb, wg_ref[0:d, :], preferred_element_type=F32)
             + jnp.dot(kb, wg_ref[d:2 * d, :], preferred_element_type=F32)
             + jnp.dot(vb, wg_ref[2 * d:3 * d, :], preferred_element_type=F32)
             + bg_ref[...])
        col = lax.broadcasted_iota(jnp.int32, g.shape, 1)
        log_f = jnp.minimum(g, 0.0) - jnp.log(1.0 + jnp.exp(-jnp.abs(g)))
        gates = jnp.where(col < nh, g, jnp.where(col < 2 * nh, log_f, 0.0))
        cum = sum(jnp.dot(tri, part, preferred_element_type=F32) for part in _split3(gates))
        colform = jnp.where(col < nh, gates, cum)
        rowform = colform.T

        for h in range(nh):
            sl = slice(h * dh, (h + 1) * dh)
            li_row = rowform[h:h + 1, :]
            b_row = rowform[nh + h:nh + h + 1, :]
            li_col = colform[:, h:h + 1]
            b_col = colform[:, nh + h:nh + h + 1]
            m_prev = m_sc[sq, h:h + 1, 0:1]
            g_tot = b_row[:, L - 1:L]

            dmat = jnp.where(causal, b_col - b_row + li_row, NEG_INF)
            inter = b_col + m_prev
            m_i = jnp.maximum(inter, jnp.max(dmat, axis=-1, keepdims=True))
            w_intra = jnp.exp(dmat - m_i)
            w_inter = jnp.exp(inter - m_i)

            qh = qb[:, sl]
            nt = (((1,), (1,)), ((), ()))
            s = lax.dot_general(qh, kb[:, sl], nt, preferred_element_type=F32) * w_intra
            c_prev = c_sc[sq, h]
            n_prev = n_sc[sq, h:h + 1, :]
            num = w_inter * jnp.dot(qh, c_prev.astype(BF16), preferred_element_type=F32) \
                + jnp.dot(s.astype(BF16), vb[:, sl], preferred_element_type=F32)
            n_rep = jnp.broadcast_to(n_prev, (LANES, dh)).astype(BF16)
            qn = lax.dot_general(qh, n_rep, nt, preferred_element_type=F32)[:, 0:1]
            den = w_inter * qn + jnp.sum(s, axis=-1, keepdims=True)
            hh = num * (1.0 / jnp.maximum(jnp.abs(den), jnp.exp(-m_i)))

            mu = jnp.mean(hh, axis=-1, keepdims=True)
            dev = hh - mu
            var = jnp.mean(dev * dev, axis=-1, keepdims=True)
            hn = dev * lax.rsqrt(var + EPS) * nw_ref[:, sl]
            ym_ref[sq, :, sl] = ((hn + sk_ref[:, sl] * xcb[:, sl].astype(F32)) * sz_ref[sq, :, sl]).astype(ym_ref.dtype)

            a_row = g_tot - b_row + li_row
            m_new = jnp.maximum(g_tot + m_prev, jnp.max(a_row, axis=-1, keepdims=True))
            a_col = g_tot - b_col + li_col
            w_state = jnp.exp(a_col - m_new)
            decay = jnp.exp(g_tot + m_prev - m_new)
            kw = k_[:, sl] * w_state
            c_sc[sq, h] = decay * c_prev + lax.dot_general(kw.astype(BF16), vb[:, sl], (((0,), (0,)), ((), ())),
                                                           preferred_element_type=F32)
            n_sc[sq, h:h + 1, :] = decay * n_prev + jnp.sum(kw, axis=0, keepdims=True)
            m_sc[sq, h:h + 1, :] = jnp.broadcast_to(m_new, (1, m_sc.shape[2]))


def _mlstm(xm, xc, sz, wq, wk, wv, wg, bg, norm_w, skip, bsz, seq):
    n, d = xm.shape
    L = MLSTM_CHUNK
    nc = seq // L
    dh = d // N_HEADS
    n_seq = 2 if bsz % 2 == 0 else 1
    blk = lambda b, j: (b, j, 0)
    c2 = lambda b, j: (0, 0)
    c3 = lambda b, j: (0, 0, 0)
    ym = pl.pallas_call(
        functools.partial(_mlstm_kernel, chunk=L, dh=dh, n_seq=n_seq),
        grid=(bsz // n_seq, nc),
        in_specs=[pl.BlockSpec((n_seq, L, d), blk), pl.BlockSpec((n_seq, L, d), blk), pl.BlockSpec((n_seq, L, d), blk),
                  pl.BlockSpec(wq.shape, c3), pl.BlockSpec(wk.shape, c3), pl.BlockSpec(wv.shape, c3),
                  pl.BlockSpec(wg.shape, c2), pl.BlockSpec(bg.shape, c2),
                  pl.BlockSpec(norm_w.shape, c2), pl.BlockSpec(skip.shape, c2)],
        out_specs=pl.BlockSpec((n_seq, L, d), blk),
        out_shape=jax.ShapeDtypeStruct((bsz, seq, d), BF16),
        scratch_shapes=[pltpu.VMEM((n_seq, N_HEADS, dh, dh), F32),
                        pltpu.VMEM((n_seq, SUBLANES, dh), F32),
                        pltpu.VMEM((n_seq, SUBLANES, LANES), F32)],
        compiler_params=_cparams("arbitrary", "arbitrary"),
        name="mlstm",
    )(xm.reshape(bsz, seq, d), xc.reshape(bsz, seq, d), sz.reshape(bsz, seq, d), wq, wk, wv, wg, bg, norm_w, skip)
    return ym.reshape(n, d)


def _conv_kernel(u_ref, w_ref, b_ref, nw_ref, nb_ref, yc_ref, ubuf, pbuf, cbuf, *, tile, rows):
    T = tile
    j = pl.program_id(1)

    @pl.when(j == 0)
    def _():
        ubuf[0:CONV_HALO, :] = jnp.zeros((CONV_HALO, ubuf.shape[1]), F32)

    ubuf[CONV_HALO:CONV_HALO + T, :] = u_ref[...]
    base = CONV_HALO - (CONV_WIDTH - 1)
    span = T + CONV_HALO - SUBLANES
    n_lane_blocks = ubuf.shape[1] // LANES

    def lane_block(c, carry):
        lanes = pl.ds(pl.multiple_of(c * LANES, LANES), LANES)
        for r in range(1, SUBLANES):
            pbuf[r - 1, :, :] = ubuf[r:r + span, lanes]
        for r0 in range(0, T, rows):
            acc = jnp.broadcast_to(b_ref[:, lanes], (rows, LANES))
            for k in range(CONV_WIDTH):
                q, r = divmod(base + k, SUBLANES)
                lo = r0 + q * SUBLANES
                src = ubuf[lo:lo + rows, lanes] if r == 0 else pbuf[r - 1, lo:lo + rows, :]
                acc = acc + w_ref[k:k + 1, lanes] * src
            cbuf[r0:r0 + rows, lanes] = acc
        return carry

    lax.fori_loop(0, n_lane_blocks, lane_block, 0)
    ubuf[0:CONV_HALO, :] = ubuf[T:T + CONV_HALO, :]

    y = cbuf[...]
    mu = jnp.mean(y, axis=-1, keepdims=True)
    dev = y - mu
    var = jnp.mean(dev * dev, axis=-1, keepdims=True)
    yn = dev * lax.rsqrt(var + EPS) * nw_ref[...] + nb_ref[...]
    yc_ref[...] = _silu(yn).astype(yc_ref.dtype)


def _conv_group(u, w, b, norm_w, norm_b, bsz, seq, tile):
    n, d = u.shape
    nt = seq // tile
    row = lambda bi, j: (bi * nt + j, 0)
    c2 = lambda bi, j: (0, 0)
    return pl.pallas_call(
        functools.partial(_conv_kernel, tile=tile, rows=64),
        grid=(bsz, nt),
        in_specs=[pl.BlockSpec((tile, d), row), pl.BlockSpec(w.shape, c2), pl.BlockSpec(b.shape, c2),
                  pl.BlockSpec(norm_w.shape, c2), pl.BlockSpec(norm_b.shape, c2)],
        out_specs=pl.BlockSpec((tile, d), row),
        out_shape=jax.ShapeDtypeStruct((n, d), BF16),
        scratch_shapes=[pltpu.VMEM((tile + CONV_HALO, d), F32),
                        pltpu.VMEM((SUBLANES - 1, tile + CONV_HALO - SUBLANES, LANES), F32),
                        pltpu.VMEM((tile, d), F32)],
        compiler_params=_cparams("arbitrary", "arbitrary"),
        name="conv_group",
    )(u, w, b, norm_w, norm_b)


def _outproj_kernel(x_ref, ym_ref, yc_ref, wo_ref, nw_ref, wr_ref, br_ref,
                    h1_ref, a2_ref, idx_ref, pos_ref, gate_ref, cnt_ref, cnt_sc, *, tm, d_mlstm):
    i = pl.program_id(0)

    @pl.when(i == 0)
    def _():
        cnt_sc[...] = jnp.zeros_like(cnt_sc)

    h1 = (x_ref[...]
          + jnp.dot(ym_ref[...], wo_ref[0:d_mlstm, :], preferred_element_type=F32)
          + jnp.dot(yc_ref[...], wo_ref[d_mlstm:, :], preferred_element_type=F32))
    h1_ref[...] = h1
    a2 = h1 * lax.rsqrt(jnp.mean(h1 * h1, axis=-1, keepdims=True) + EPS) * nw_ref[...]
    _store_row_slabs(a2_ref, 0, _pack_bf16_pairs(a2), a2.shape[1] // (2 * LANES))

    logits = lax.dot_general(wr_ref[...], a2.astype(BF16), (((1,), (1,)), ((), ())),
                             preferred_element_type=F32) + br_ref[...]
    e_iota = lax.broadcasted_iota(jnp.int32, logits.shape, 0)
    work = logits
    vals, idxs = [], []
    for _ in range(TOP_K):
        mx = jnp.max(work, axis=0, keepdims=True)
        sel = jnp.min(jnp.where(work == mx, e_iota, N_EXPERTS), axis=0, keepdims=True)
        vals.append(mx)
        idxs.append(sel)
        work = jnp.where(e_iota == sel, NEG_INF, work)
    exps = [jnp.exp(vv - vals[0]) for vv in vals]
    tot = exps[0] + exps[1] + exps[2] + exps[3]
    gates = [ev / tot for ev in exps]

    chosen = functools.reduce(jnp.logical_or, [e_iota == sel for sel in idxs])
    mh = jnp.where(chosen, 1.0, 0.0)
    ri = lax.broadcasted_iota(jnp.int32, (tm, tm), 0)
    ci = lax.broadcasted_iota(jnp.int32, (tm, tm), 1)
    upper = jnp.where(ri < ci, 1.0, 0.0).astype(BF16)
    rank = jnp.dot(mh.astype(BF16), upper, preferred_element_type=F32) + cnt_sc[:, 0:1]
    cnt_new = cnt_sc[...] + jnp.sum(mh, axis=1, keepdims=True)
    cnt_sc[...] = cnt_new
    cnt_ref[...] = cnt_new

    zero_i = jnp.zeros((SUBLANES - TOP_K, tm), jnp.int32)
    pos = [jnp.sum(jnp.where(e_iota == sel, rank, 0.0), axis=0, keepdims=True).astype(jnp.int32) for sel in idxs]
    idx_ref[...] = jnp.concatenate(idxs + [zero_i], axis=0)
    pos_ref[...] = jnp.concatenate(pos + [zero_i], axis=0)
    gate_ref[...] = jnp.concatenate(gates + [jnp.zeros((SUBLANES - TOP_K, tm), F32)], axis=0)


def _out_proj(xf, ym, yc, w_out_b, norm_w, w_router_t, b_router, d_mlstm, tm):
    n, d = xf.shape
    n_slabs = d // (2 * LANES)
    row = lambda i: (i, 0)
    colb = lambda i: (0, i)
    const = lambda i: (0, 0)
    return pl.pallas_call(
        functools.partial(_outproj_kernel, tm=tm, d_mlstm=d_mlstm),
        grid=(n // tm,),
        in_specs=[pl.BlockSpec((tm, d), row), pl.BlockSpec((tm, ym.shape[1]), row), pl.BlockSpec((tm, yc.shape[1]), row),
                  pl.BlockSpec(w_out_b.shape, const), pl.BlockSpec(norm_w.shape, const),
                  pl.BlockSpec(w_router_t.shape, const), pl.BlockSpec(b_router.shape, const)],
        out_specs=[pl.BlockSpec((tm, d), row),
                   pl.BlockSpec((tm * n_slabs, LANES), row),
                   pl.BlockSpec((SUBLANES, tm), colb),
                   pl.BlockSpec((SUBLANES, tm), colb),
                   pl.BlockSpec((SUBLANES, tm), colb),
                   pl.BlockSpec((N_EXPERTS, LANES), const)],
        out_shape=[jax.ShapeDtypeStruct((n, d), F32),
                   jax.ShapeDtypeStruct((n * n_slabs, LANES), jnp.uint32),
                   jax.ShapeDtypeStruct((SUBLANES, n), jnp.int32),
                   jax.ShapeDtypeStruct((SUBLANES, n), jnp.int32),
                   jax.ShapeDtypeStruct((SUBLANES, n), F32),
                   jax.ShapeDtypeStruct((N_EXPERTS, LANES), F32)],
        scratch_shapes=[pltpu.VMEM((N_EXPERTS, LANES), F32)],
        compiler_params=_cparams("arbitrary"),
        name="out_proj_router",
    )(xf, ym, yc, w_out_b, norm_w, w_router_t, b_router)


def _experts_kernel(blk_ref, exp_ref, valid_ref, lo_ref, hi_ref,
                    xs_ref, w1_ref, b1_ref, w2_ref, b2_ref, ys_ref, w1c_ref, w2c_ref, acc_ref, *, rb, d_ff, n_sub):
    i = pl.program_id(0)
    n_slabs = xs_ref.shape[0] // rb
    prev = jnp.maximum(i - 1, 0)
    first_visit = jnp.logical_or(i == 0, blk_ref[prev] != blk_ref[i])
    new_expert = jnp.logical_or(i == 0, exp_ref[prev] != exp_ref[i])

    @pl.when(new_expert)
    def _():
        w1c_ref[...] = w1_ref[0].astype(BF16)
        w2c_ref[...] = w2_ref[0].astype(BF16)

    @pl.when(first_visit)
    def _():
        acc_ref[...] = jnp.zeros_like(acc_ref)

    rs = rb // n_sub
    lo, hi = lo_ref[i], hi_ref[i]

    def sub_block(c):
        xa, xb = _unpack_bf16_pairs(_load_row_slabs(xs_ref, c * rs, rs, n_slabs))
        x = jnp.concatenate([xa.astype(BF16), xb.astype(BF16)], axis=-1)
        hid = jnp.dot(x, w1c_ref[...], preferred_element_type=F32) + b1_ref[0]
        x_glu = jnp.minimum(hid[:, :d_ff], SWIGLU_LIMIT)
        x_lin = jnp.clip(hid[:, d_ff:], -SWIGLU_LIMIT, SWIGLU_LIMIT)
        act = x_glu * _sigmoid(SWIGLU_ALPHA * x_glu) * (x_lin + 1.0)
        y = jnp.dot(act.astype(BF16), w2c_ref[...], preferred_element_type=F32) + b2_ref[0]
        r = c * rs + lax.broadcasted_iota(jnp.int32, (rs, 1), 0)
        y = jnp.where(jnp.logical_and(r >= lo, r < hi), y, 0.0)
        y = acc_ref[c * rs:(c + 1) * rs, :] + y
        acc_ref[c * rs:(c + 1) * rs, :] = y
        _store_row_slabs(ys_ref, c * rs, _pack_bf16_pairs(y), n_slabs)

    for c in range(n_sub):
        has_rows = jnp.logical_and(valid_ref[i] == 1, jnp.logical_and(lo < (c + 1) * rs, hi > c * rs))
        pl.when(has_rows)(functools.partial(sub_block, c))


def _experts(xs, w1, b1, w2, b2, item_blk, item_exp, item_valid, item_lo, item_hi, rb, n_slabs):
    n_items = item_blk.shape[0]
    d = w1.shape[1]
    d_ff = w2.shape[1]
    by_blk = lambda i, blk, ex, va, lo, hi: (blk[i], 0)
    by_exp = lambda i, blk, ex, va, lo, hi: (ex[i], 0, 0)
    grid_spec = pltpu.PrefetchScalarGridSpec(
        num_scalar_prefetch=5,
        grid=(n_items,),
        in_specs=[pl.BlockSpec((rb * n_slabs, LANES), by_blk),
                  pl.BlockSpec((1,) + w1.shape[1:], by_exp),
                  pl.BlockSpec((1,) + b1.shape[1:], by_exp),
                  pl.BlockSpec((1,) + w2.shape[1:], by_exp),
                  pl.BlockSpec((1,) + b2.shape[1:], by_exp)],
        out_specs=pl.BlockSpec((rb * n_slabs, LANES), by_blk),
        scratch_shapes=[pltpu.VMEM(w1.shape[1:], BF16), pltpu.VMEM(w2.shape[1:], BF16), pltpu.VMEM((rb, d), F32)],
    )
    return pl.pallas_call(
        functools.partial(_experts_kernel, rb=rb, d_ff=d_ff, n_sub=rb // MXU_DIM),
        grid_spec=grid_spec,
        out_shape=jax.ShapeDtypeStruct(xs.shape, jnp.uint32),
        compiler_params=pltpu.CompilerParams(dimension_semantics=("arbitrary",), vmem_limit_bytes=EXPERTS_VMEM_LIMIT),
        name="experts",
    )(item_blk, item_exp, item_valid, item_lo, item_hi, xs, w1, b1, w2, b2)


SC_CORES = 2
SC_SUBCORES = 16
SC_GATHER_WINDOW = 64


def _sc_worker_base(per_worker):
    wid = lax.axis_index("s") * SC_CORES + lax.axis_index("c")
    return wid * per_worker


def _sc_gather_rows(table, idx):
    m = idx.shape[0]
    n_workers = SC_CORES * SC_SUBCORES
    window = SC_GATHER_WINDOW
    per_worker = m // n_workers
    n_win = per_worker // window
    assert per_worker * n_workers == m and n_win * window == per_worker and n_win % 2 == 0
    mesh = plsc.VectorSubcoreMesh(core_axis_name="c", subcore_axis_name="s")
    slab = table.shape[1:]

    @functools.partial(
        pl.kernel, mesh=mesh,
        out_type=jax.ShapeDtypeStruct((m,) + slab, table.dtype),
        scratch_types=[pltpu.VMEM((per_worker,), jnp.int32),
                       pltpu.VMEM((window,) + slab, table.dtype), pltpu.VMEM((window,) + slab, table.dtype)]
        + [pltpu.SemaphoreType.DMA] * 4,
        name="sc_gather_rows",
    )
    def gather(table_hbm, idx_hbm, out_hbm, idx_v, rows0, rows1, gsem0, gsem1, wsem0, wsem1):
        rows_v, gsem, wsem = (rows0, rows1), (gsem0, gsem1), (wsem0, wsem1)
        base = _sc_worker_base(per_worker)
        pltpu.sync_copy(idx_hbm.at[pl.ds(pl.multiple_of(base, window), per_worker)], idx_v)

        def gather_copy(w, b):
            ids = idx_v.at[pl.ds(pl.multiple_of(w * window, window), window)]
            return pltpu.make_async_copy(table_hbm.at[ids], rows_v[b], gsem[b])

        def write_copy(w, b):
            dst = out_hbm.at[pl.ds(pl.multiple_of(base + w * window, window), window)]
            return pltpu.make_async_copy(rows_v[b], dst, wsem[b])

        gather_copy(0, 0).start()

        @pl.loop(0, n_win, step=2)
        def _(w):
            @pl.when(w >= 1)
            def _():
                write_copy(w - 1, 1).wait()

            gather_copy(w + 1, 1).start()
            gather_copy(w, 0).wait()
            write_copy(w, 0).start()

            @pl.when(w + 2 < n_win)
            def _():
                write_copy(w, 0).wait()
                gather_copy(w + 2, 0).start()

            gather_copy(w + 1, 1).wait()
            write_copy(w + 1, 1).start()

        write_copy(n_win - 2, 0).wait()
        write_copy(n_win - 1, 1).wait()

    return gather(table, idx)


def _sc_scatter_rows(rows, dest, n_slots):
    n = rows.shape[0]
    n_workers = SC_CORES * SC_SUBCORES
    window = SC_GATHER_WINDOW
    per_worker = n // n_workers
    n_win = per_worker // window
    assert per_worker * n_workers == n and n_win * window == per_worker and n_win % 2 == 0
    mesh = plsc.VectorSubcoreMesh(core_axis_name="c", subcore_axis_name="s")
    slab = rows.shape[1:]

    @functools.partial(
        pl.kernel, mesh=mesh,
        out_type=jax.ShapeDtypeStruct((n_slots * n,) + slab, rows.dtype),
        scratch_types=[pltpu.VMEM((window,), jnp.int32)] * (2 * n_slots)
        + [pltpu.VMEM((window,) + slab, rows.dtype)] * 2 + [pltpu.SemaphoreType.DMA] * 4,
        name="sc_scatter_rows",
    )
    def scatter(rows_hbm, dest_hbm, out_hbm, *scratch):
        idx_v = (scratch[:n_slots], scratch[n_slots:2 * n_slots])
        rows_v = scratch[2 * n_slots:2 * n_slots + 2]
        rsem = scratch[2 * n_slots + 2:2 * n_slots + 4]
        wsem = scratch[2 * n_slots + 4:2 * n_slots + 6]
        base = _sc_worker_base(per_worker)

        def rows_at(w, k=0):
            return pl.ds(pl.multiple_of(k * n + base + w * window, window), window)

        def start_read(w, b):
            for k in range(n_slots):
                pltpu.sync_copy(dest_hbm.at[rows_at(w, k)], idx_v[b][k])
            pltpu.async_copy(rows_hbm.at[rows_at(w)], rows_v[b], rsem[b])

        def scatter_window(w, b):
            pltpu.make_async_copy(rows_hbm.at[rows_at(w)], rows_v[b], rsem[b]).wait()
            for k in range(n_slots):
                pltpu.async_copy(rows_v[b], out_hbm.at[idx_v[b][k]], wsem[b])
            for k in range(n_slots):
                pltpu.make_async_copy(rows_v[b], out_hbm.at[idx_v[b][k]], wsem[b]).wait()

        start_read(0, 0)

        @pl.loop(0, n_win, step=2)
        def _(w):
            start_read(w + 1, 1)
            scatter_window(w, 0)

            @pl.when(w + 2 < n_win)
            def _():
                start_read(w + 2, 0)

            scatter_window(w + 1, 1)

    return scatter(rows, dest)


def _final_kernel(h1_ref, gate_ref, yg_ref, nw_ref, out_ref, *, tm, n_slabs):
    gpad = jnp.concatenate([gate_ref[...], jnp.zeros((LANES - SUBLANES, tm), F32)], axis=0)
    gcol = gpad.T
    h2 = h1_ref[...]
    for k in range(TOP_K):
        ya, yb = _unpack_bf16_pairs(_load_row_slabs(yg_ref, 0, tm, n_slabs, lead=(k,)))
        h2 = h2 + gcol[:, k:k + 1] * jnp.concatenate([ya, yb], axis=-1)
    out_ref[...] = h2 * lax.rsqrt(jnp.mean(h2 * h2, axis=-1, keepdims=True) + EPS) * nw_ref[...]


def _final(h, gates_t, yg, final_w, n_slabs, tm, chunk, n_chunks):
    n, d = h.shape
    tiles = n // tm // n_chunks
    first = chunk * tiles
    return pl.pallas_call(
        functools.partial(_final_kernel, tm=tm, n_slabs=n_slabs),
        grid=(tiles,),
        in_specs=[pl.BlockSpec((tm, d), lambda i: (first + i, 0)),
                  pl.BlockSpec((SUBLANES, tm), lambda i: (0, first + i)),
                  pl.BlockSpec((TOP_K, tm * n_slabs, LANES), lambda i: (0, i, 0)),
                  pl.BlockSpec((1, d), lambda i: (0, 0))],
        out_specs=pl.BlockSpec((tm, d), lambda i: (first + i, 0)),
        out_shape=jax.ShapeDtypeStruct((n, d), F32),
        input_output_aliases={0: 0},
        compiler_params=_cparams("parallel"),
        name="final",
    )(h, gates_t, yg, final_w)


def _block_diag_tiles(w):
    nb, bs, _ = w.shape
    rows = jnp.tile(w.reshape(nb * bs // MXU_DIM, MXU_DIM, bs), (1, 1, MXU_DIM // bs))
    r_blk = lax.broadcasted_iota(jnp.int32, (MXU_DIM, MXU_DIM), 0) // bs
    c_blk = lax.broadcasted_iota(jnp.int32, (MXU_DIM, MXU_DIM), 1) // bs
    return jnp.where(r_blk == c_blk, rows, 0.0).astype(BF16)


def _layer(xf, bsz, seq, norm_mix_w, w_in, mlstm_conv_w, mlstm_conv_b, w_q, w_k, w_v, w_igate, b_igate,
           w_fgate, b_fgate, mlstm_norm_w, mlstm_skip, conv_dw_w, conv_dw_b, conv_norm_w, conv_norm_b,
           w_out, norm_ffn_w, w_router, b_router, w1, b1, w2, b2, final_norm_w):
    n, d = xf.shape
    d_mlstm = mlstm_norm_w.shape[0]
    d_conv = conv_norm_w.shape[0]
    n_slabs = d // (2 * LANES)
    r2 = lambda v: v.reshape(1, -1)

    xm, xc, sz, u = _in_proj(xf, r2(norm_mix_w), w_in.astype(BF16), mlstm_conv_w, r2(mlstm_conv_b),
                             d_mlstm, d_conv, seq, tm=512)

    q_scale = (d_mlstm // N_HEADS) ** -0.5
    inv_q_scale = round(1.0 / q_scale)
    assert inv_q_scale & (inv_q_scale - 1) == 0 and inv_q_scale * q_scale == 1.0, "head dim must be a power of 4"
    wg = jnp.concatenate([w_igate, w_fgate], axis=1)
    wg = jnp.concatenate([wg[:d_mlstm] * float(inv_q_scale), wg[d_mlstm:]], axis=0)
    wg = jnp.pad(wg, ((0, 0), (0, LANES - wg.shape[1]))).astype(BF16)
    bg = jnp.pad(jnp.concatenate([b_igate, b_fgate]), (0, LANES - 2 * N_HEADS)).reshape(1, LANES)
    ym = _mlstm(xm, xc, sz, _block_diag_tiles(w_q * q_scale), _block_diag_tiles(w_k), _block_diag_tiles(w_v),
                wg, bg, r2(mlstm_norm_w), r2(mlstm_skip), bsz, seq)
    yc = _conv_group(u, conv_dw_w, r2(conv_dw_b), r2(conv_norm_w), r2(conv_norm_b), bsz, seq, tile=512)

    h1, a2s, idx_t, pos_t, gates_t, cnt = _out_proj(
        xf, ym, yc, w_out.astype(BF16), r2(norm_ffn_w), w_router.T.astype(BF16), b_router.reshape(-1, 1), d_mlstm, tm=512)

    counts = cnt[:, 0].astype(jnp.int32)
    ends = jnp.cumsum(counts)
    starts = ends - counts
    e_ids = jnp.arange(N_EXPERTS, dtype=jnp.int32)
    idx4 = idx_t[:TOP_K]
    dest = pos_t[:TOP_K] + jnp.sum(
        jnp.where(idx4[None] == e_ids[:, None, None], starts[:, None, None], 0), axis=0)

    n_rows = n * TOP_K
    rb = 1024
    n_blocks = n_rows // rb
    n_items = n_blocks + N_EXPERTS - 1
    first_blk = starts // rb
    last_blk = jnp.where(counts > 0, (ends - 1) // rb, first_blk - 1)
    per_e = last_blk - first_blk + 1
    item_end = jnp.cumsum(per_e)
    item_start = item_end - per_e
    ids = jnp.arange(n_items, dtype=jnp.int32)
    total = item_end[-1]
    item_valid = (ids < total).astype(jnp.int32)
    item_exp = jnp.minimum(jnp.sum((ids[:, None] >= item_end[None, :]).astype(jnp.int32), axis=1), N_EXPERTS - 1)
    last_valid_exp = jnp.max(jnp.where(per_e > 0, e_ids, 0))
    item_exp = jnp.where(item_valid == 1, item_exp, last_valid_exp).astype(jnp.int32)
    is_exp = item_exp[:, None] == e_ids[None, :]
    of_item = lambda table: jnp.sum(jnp.where(is_exp, table[None, :], 0), axis=1)
    item_blk = jnp.where(item_valid == 1, of_item(first_blk) + ids - of_item(item_start), n_blocks - 1)
    item_blk = item_blk.astype(jnp.int32)
    item_lo = (jnp.maximum(of_item(starts), item_blk * rb) - item_blk * rb).astype(jnp.int32)
    item_hi = (jnp.minimum(of_item(ends), (item_blk + 1) * rb) - item_blk * rb).astype(jnp.int32)

    xs = _sc_scatter_rows(a2s.reshape(n, n_slabs, LANES), dest.reshape(-1), TOP_K)
    ys = _experts(xs.reshape(n_rows * n_slabs, LANES), w1, b1[:, None, :], w2, b2[:, None, :],
                  item_blk, item_exp, item_valid, item_lo, item_hi, rb, n_slabs)
    ys = ys.reshape(n_rows, n_slabs, LANES)

    n_chunks = 8
    nc = n // n_chunks
    out = h1
    for c in range(n_chunks):
        yg = _sc_gather_rows(ys, dest[:, c * nc:(c + 1) * nc].reshape(-1))
        out = _final(out, gates_t, yg.reshape(TOP_K, nc * n_slabs, LANES), r2(final_norm_w), n_slabs,
                     tm=256, chunk=c, n_chunks=n_chunks)
    return out


def kernel(x, norm_mix_w, w_in, mlstm_conv_w, mlstm_conv_b, w_q, w_k, w_v, w_igate, b_igate, w_fgate, b_fgate,
           mlstm_norm_w, mlstm_skip, conv_dw_w, conv_dw_b, conv_norm_w, conv_norm_b, w_out, norm_ffn_w,
           w_router, b_router, w1, b1, w2, b2, final_norm_w):
    bsz, seq, d = x.shape
    assert norm_mix_w.shape[0] == 1, "single-layer block"
    out = _layer(x.reshape(bsz * seq, d), bsz, seq, norm_mix_w[0], w_in[0], mlstm_conv_w[0], mlstm_conv_b[0],
                 w_q[0], w_k[0], w_v[0], w_igate[0], b_igate[0], w_fgate[0], b_fgate[0], mlstm_norm_w[0],
                 mlstm_skip[0], conv_dw_w[0], conv_dw_b[0], conv_norm_w[0], conv_norm_b[0], w_out[0],
                 norm_ffn_w[0], w_router[0], b_router[0], w1[0], b1[0], w2[0], b2[0], final_norm_w)
    return out.reshape(bsz, seq, d)
```

```python
import functools

import jax
import jax.numpy as jnp
from jax import lax
from jax.experimental import pallas as pl
from jax.experimental.pallas import tpu as pltpu
from jax.experimental.pallas import tpu_sc as plsc

F32 = jnp.float32
BF16 = jnp.bfloat16

EPS = 1e-5
N_HEADS = 4
MLSTM_CONV_WIDTH = 4
CONV_WIDTH = 31
N_EXPERTS = 32
TOP_K = 4
SWIGLU_ALPHA = 1.702
SWIGLU_LIMIT = 7.0

LANES = 128
SUBLANES = 8
MXU_DIM = 256
VMEM_LIMIT = 52 * 1024 * 1024
EXPERTS_VMEM_LIMIT = 58 * 1024 * 1024

IN_PROJ_ROWS = 512
CONV_ROWS = 512
CONV_ACC_ROWS = 64
OUT_PROJ_ROWS = 512
EXPERT_ROWS = 1024
FINAL_ROWS = 256
FINAL_CHUNKS = 8

MLSTM_CHUNK = 256
CONV_HALO = 32
NEG_INF = float("-inf")
LOG2_E = 1.4426950408889634


def _sigmoid(x):
    return 0.5 * jnp.tanh(0.5 * x) + 0.5


def _silu(x):
    return x * _sigmoid(x)


def _pack_bf16_pairs(v):
    half = v.shape[1] // 2
    hi = lax.bitcast_convert_type(v[:, :half].astype(BF16).astype(F32), jnp.uint32)
    lo = lax.bitcast_convert_type(v[:, half:].astype(BF16).astype(F32), jnp.uint32)
    return hi | (lo >> 16)


def _unpack_bf16_pairs(w):
    hi = lax.bitcast_convert_type(w & jnp.uint32(0xFFFF0000), F32)
    lo = lax.bitcast_convert_type(w << 16, F32)
    return hi, lo


def _load_row_slabs(ref, first, rows, n_slabs, lead=()):
    return jnp.concatenate([ref[lead + (pl.ds(first * n_slabs + s, rows, stride=n_slabs), slice(None))]
                            for s in range(n_slabs)], axis=-1)


def _store_row_slabs(ref, first, v, n_slabs):
    rows = v.shape[0]
    for s in range(n_slabs):
        ref[pl.ds(first * n_slabs + s, rows, stride=n_slabs), :] = v[:, s * LANES:(s + 1) * LANES]


def _cparams(*sem):
    return pltpu.CompilerParams(dimension_semantics=sem, vmem_limit_bytes=VMEM_LIMIT)


def _inproj_kernel(x_ref, nw_ref, w_ref, cw_ref, cb_ref, xm_ref, xc_ref, sz_ref, u_ref, xbuf,
                   *, d_mlstm, d_conv, tiles_per_seq):
    tm = x_ref.shape[0]
    hist = SUBLANES
    i = pl.program_id(0)

    @pl.when(i == 0)
    def _():
        xbuf[...] = jnp.zeros_like(xbuf)

    x = x_ref[...]
    a = x * lax.rsqrt(jnp.mean(x * x, axis=-1, keepdims=True) + EPS) * nw_ref[...]
    ab = a.astype(BF16)
    same_seq = i % tiles_per_seq != 0
    cols = MXU_DIM

    for c in range(d_mlstm // cols):
        sl = slice(c * cols, (c + 1) * cols)
        prev = xbuf[hist:hist + tm, sl]
        acc = cb_ref[:, sl] + cw_ref[MLSTM_CONV_WIDTH - 1:MLSTM_CONV_WIDTH, sl] * prev
        for k in range(MLSTM_CONV_WIDTH - 1):
            off = hist - (MLSTM_CONV_WIDTH - 1) + k
            acc = acc + cw_ref[k:k + 1, sl] * xbuf[off:off + tm, sl]
        xc_ref[:, sl] = _silu(acc).astype(xc_ref.dtype)
        xbuf[0:hist, sl] = jnp.where(same_seq, prev[tm - hist:, :], 0.0)

        xm = jnp.dot(ab, w_ref[:, sl], preferred_element_type=F32)
        xm_ref[:, sl] = xm.astype(xm_ref.dtype)
        xbuf[hist:hist + tm, sl] = xm
        z = jnp.dot(ab, w_ref[:, d_mlstm + c * cols:d_mlstm + (c + 1) * cols], preferred_element_type=F32)
        sz_ref[:, sl] = _silu(z)

    for c in range(d_conv // cols):
        sl = slice(c * cols, (c + 1) * cols)
        ga = jnp.dot(ab, w_ref[:, 2 * d_mlstm + c * cols:2 * d_mlstm + (c + 1) * cols], preferred_element_type=F32)
        gb = jnp.dot(ab, w_ref[:, 2 * d_mlstm + d_conv + c * cols:2 * d_mlstm + d_conv + (c + 1) * cols],
                     preferred_element_type=F32)
        u_ref[:, sl] = ga * _sigmoid(gb)


def _in_proj(xf, norm_w, w_in_b, conv_w, conv_b, d_mlstm, d_conv, seq, tm):
    n, d = xf.shape
    n_tiles = n // tm
    cur = lambda i: (jnp.minimum(i, n_tiles - 1), 0)
    prv = lambda i: (jnp.maximum(i - 1, 0), 0)
    const = lambda i: (0, 0)
    return pl.pallas_call(
        functools.partial(_inproj_kernel, d_mlstm=d_mlstm, d_conv=d_conv, tiles_per_seq=seq // tm),
        grid=(n_tiles + 1,),
        in_specs=[pl.BlockSpec((tm, d), cur),
                  pl.BlockSpec((1, d), const),
                  pl.BlockSpec(w_in_b.shape, const),
                  pl.BlockSpec(conv_w.shape, const),
                  pl.BlockSpec(conv_b.shape, const)],
        out_specs=[pl.BlockSpec((tm, d_mlstm), cur),
                   pl.BlockSpec((tm, d_mlstm), prv),
                   pl.BlockSpec((tm, d_mlstm), cur),
                   pl.BlockSpec((tm, d_conv), cur)],
        out_shape=[jax.ShapeDtypeStruct((n, d_mlstm), BF16),
                   jax.ShapeDtypeStruct((n, d_mlstm), BF16),
                   jax.ShapeDtypeStruct((n, d_mlstm), F32),
                   jax.ShapeDtypeStruct((n, d_conv), F32)],
        scratch_shapes=[pltpu.VMEM((tm + SUBLANES, d_mlstm), F32)],
        compiler_params=_cparams("arbitrary"),
        name="in_proj",
    )(xf, norm_w, w_in_b, conv_w, conv_b)


def _split3(v):
    hi = v.astype(BF16)
    r1 = v - hi.astype(F32)
    mid = r1.astype(BF16)
    lo = (r1 - mid.astype(F32)).astype(BF16)
    return hi, mid, lo


def _mlstm_kernel(xm_ref, xc_ref, sz_ref, wq_ref, wk_ref, wv_ref, wg_ref, bg_ref, nw_ref, sk_ref,
                  ym_ref, c_sc, n_sc, m_sc, *, chunk, dh, n_seq):
    L = chunk
    nh = N_HEADS
    j = pl.program_id(1)

    @pl.when(j == 0)
    def _():
        c_sc[...] = jnp.zeros_like(c_sc)
        n_sc[...] = jnp.zeros_like(n_sc)
        m_sc[...] = jnp.zeros_like(m_sc)

    ri = lax.broadcasted_iota(jnp.int32, (L, L), 0)
    ci = lax.broadcasted_iota(jnp.int32, (L, L), 1)
    causal = ci <= ri
    tri = jnp.where(causal, 1.0, 0.0).astype(BF16)

    for sq in range(n_seq):
        xmb = xm_ref[sq]
        xcb = xc_ref[sq]
        d = xmb.shape[1]
        nb = d // MXU_DIM

        def bd(xb, w_ref):
            return jnp.concatenate(
                [jnp.dot(xb[:, g * MXU_DIM:(g + 1) * MXU_DIM], w_ref[g], preferred_element_type=F32)
                 for g in range(nb)], axis=-1)

        q = bd(xcb, wq_ref)
        k_ = bd(xcb, wk_ref)
        v = bd(xmb, wv_ref)
        qb, kb, vb = q.astype(BF16), k_.astype(BF16), v.astype(BF16)

        g = (jnp.dot(qb, wg_ref[0:d, :], preferred_element_type=F32)
             + jnp.dot(kb, wg_ref[d:2 * d, :], preferred_element_type=F32)
             + jnp.dot(vb, wg_ref[2 * d:3 * d, :], preferred_element_type=F32)
             + bg_ref[...])
        col = lax.broadcasted_iota(jnp.int32, g.shape, 1)
        log_f = jnp.minimum(g, 0.0) - jnp.log(1.0 + jnp.exp(-jnp.abs(g)))
        gates = jnp.where(col < nh, g, jnp.where(col < 2 * nh, log_f, 0.0))
        cum = sum(jnp.dot(tri, part, preferred_element_type=F32) for part in _split3(gates))
        colform = jnp.where(col < nh, gates, cum) * LOG2_E
        rowform = colform.T

        for h in range(nh):
            sl = slice(h * dh, (h + 1) * dh)
            b_row = rowform[nh + h:nh + h + 1, :]
            r_row = rowform[h:h + 1, :] - b_row
            li_col = colform[:, h:h + 1]
            b_col = colform[:, nh + h:nh + h + 1]
            m_prev = m_sc[sq, h:h + 1, 0:1]
            g_tot = b_row[:, L - 1:L]

            dmat = jnp.where(causal, b_col + r_row, NEG_INF)
            inter = b_col + m_prev
            m_i = jnp.maximum(inter, jnp.max(dmat, axis=-1, keepdims=True))
            w_intra = jnp.exp2(dmat - m_i)
            w_inter = jnp.exp2(inter - m_i)

            qh = qb[:, sl]
            nt = (((1,), (1,)), ((), ()))
            s = lax.dot_general(qh, kb[:, sl], nt, preferred_element_type=F32) * w_intra
            c_prev = c_sc[sq, h]
            n_prev = n_sc[sq, h:h + 1, :]
            num = w_inter * jnp.dot(qh, c_prev.astype(BF16), preferred_element_type=F32) \
                + jnp.dot(s.astype(BF16), vb[:, sl], preferred_element_type=F32)
            n_rep = jnp.broadcast_to(n_prev, (LANES, dh)).astype(BF16)
            qn = lax.dot_general(qh, n_rep, nt, preferred_element_type=F32)[:, 0:1]
            den = w_inter * qn + jnp.sum(s, axis=-1, keepdims=True)
            hh = num * (1.0 / jnp.maximum(jnp.abs(den), jnp.exp2(-m_i)))

            mu = jnp.mean(hh, axis=-1, keepdims=True)
            dev = hh - mu
            var = jnp.mean(dev * dev, axis=-1, keepdims=True)
            hn = dev * lax.rsqrt(var + EPS) * nw_ref[:, sl]
            ym_ref[sq, :, sl] = ((hn + sk_ref[:, sl] * xcb[:, sl].astype(F32)) * sz_ref[sq, :, sl]).astype(ym_ref.dtype)

            a_row = g_tot + r_row
            m_new = jnp.maximum(g_tot + m_prev, jnp.max(a_row, axis=-1, keepdims=True))
            a_col = g_tot - b_col + li_col
            w_state = jnp.exp2(a_col - m_new)
            decay = jnp.exp2(g_tot + m_prev - m_new)
            kw = k_[:, sl] * w_state
            c_sc[sq, h] = decay * c_prev + lax.dot_general(kw.astype(BF16), vb[:, sl], (((0,), (0,)), ((), ())),
                                                           preferred_element_type=F32)
            w_state_rows = jnp.broadcast_to(jnp.exp2(a_row - m_new), (SUBLANES, L)).astype(BF16)
            n_sc[sq, h:h + 1, :] = decay * n_prev + jnp.dot(w_state_rows, kb[:, sl],
                                                            preferred_element_type=F32)[0:1, :]
            m_sc[sq, h:h + 1, :] = jnp.broadcast_to(m_new, (1, m_sc.shape[2]))


def _mlstm(xm, xc, sz, wq, wk, wv, wg, bg, norm_w, skip, bsz, seq):
    n, d = xm.shape
    L = MLSTM_CHUNK
    nc = seq // L
    dh = d // N_HEADS
    n_seq = 2 if bsz % 2 == 0 else 1
    blk = lambda b, j: (b, j, 0)
    c2 = lambda b, j: (0, 0)
    c3 = lambda b, j: (0, 0, 0)
    ym = pl.pallas_call(
        functools.partial(_mlstm_kernel, chunk=L, dh=dh, n_seq=n_seq),
        grid=(bsz // n_seq, nc),
        in_specs=[pl.BlockSpec((n_seq, L, d), blk), pl.BlockSpec((n_seq, L, d), blk), pl.BlockSpec((n_seq, L, d), blk),
                  pl.BlockSpec(wq.shape, c3), pl.BlockSpec(wk.shape, c3), pl.BlockSpec(wv.shape, c3),
                  pl.BlockSpec(wg.shape, c2), pl.BlockSpec(bg.shape, c2),
                  pl.BlockSpec(norm_w.shape, c2), pl.BlockSpec(skip.shape, c2)],
        out_specs=pl.BlockSpec((n_seq, L, d), blk),
        out_shape=jax.ShapeDtypeStruct((bsz, seq, d), BF16),
        scratch_shapes=[pltpu.VMEM((n_seq, N_HEADS, dh, dh), F32),
                        pltpu.VMEM((n_seq, SUBLANES, dh), F32),
                        pltpu.VMEM((n_seq, SUBLANES, LANES), F32)],
        compiler_params=_cparams("arbitrary", "arbitrary"),
        name="mlstm",
    )(xm.reshape(bsz, seq, d), xc.reshape(bsz, seq, d), sz.reshape(bsz, seq, d), wq, wk, wv, wg, bg, norm_w, skip)
    return ym.reshape(n, d)


def _conv_kernel(u_ref, w_ref, b_ref, nw_ref, nb_ref, yc_ref, ubuf, pbuf, cbuf, *, tile, rows):
    T = tile
    j = pl.program_id(1)

    @pl.when(j == 0)
    def _():
        ubuf[0:CONV_HALO, :] = jnp.zeros((CONV_HALO, ubuf.shape[1]), F32)

    ubuf[CONV_HALO:CONV_HALO + T, :] = u_ref[...]
    base = CONV_HALO - (CONV_WIDTH - 1)
    span = T + CONV_HALO - SUBLANES
    n_lane_blocks = ubuf.shape[1] // LANES

    def lane_block(c, carry):
        lanes = pl.ds(pl.multiple_of(c * LANES, LANES), LANES)
        for r in range(1, SUBLANES):
            pbuf[r - 1, :, :] = ubuf[r:r + span, lanes]
        for r0 in range(0, T, rows):
            acc = jnp.broadcast_to(b_ref[:, lanes], (rows, LANES))
            for k in range(CONV_WIDTH):
                q, r = divmod(base + k, SUBLANES)
                lo = r0 + q * SUBLANES
                src = ubuf[lo:lo + rows, lanes] if r == 0 else pbuf[r - 1, lo:lo + rows, :]
                acc = acc + w_ref[k:k + 1, lanes] * src
            cbuf[r0:r0 + rows, lanes] = acc
        return carry

    lax.fori_loop(0, n_lane_blocks, lane_block, 0)
    ubuf[0:CONV_HALO, :] = ubuf[T:T + CONV_HALO, :]

    y = cbuf[...]
    mu = jnp.mean(y, axis=-1, keepdims=True)
    dev = y - mu
    var = jnp.mean(dev * dev, axis=-1, keepdims=True)
    yn = dev * lax.rsqrt(var + EPS) * nw_ref[...] + nb_ref[...]
    yc_ref[...] = _silu(yn).astype(yc_ref.dtype)


def _conv_group(u, w, b, norm_w, norm_b, bsz, seq, tile):
    n, d = u.shape
    nt = seq // tile
    row = lambda bi, j: (bi * nt + j, 0)
    c2 = lambda bi, j: (0, 0)
    return pl.pallas_call(
        functools.partial(_conv_kernel, tile=tile, rows=CONV_ACC_ROWS),
        grid=(bsz, nt),
        in_specs=[pl.BlockSpec((tile, d), row), pl.BlockSpec(w.shape, c2), pl.BlockSpec(b.shape, c2),
                  pl.BlockSpec(norm_w.shape, c2), pl.BlockSpec(norm_b.shape, c2)],
        out_specs=pl.BlockSpec((tile, d), row),
        out_shape=jax.ShapeDtypeStruct((n, d), BF16),
        scratch_shapes=[pltpu.VMEM((tile + CONV_HALO, d), F32),
                        pltpu.VMEM((SUBLANES - 1, tile + CONV_HALO - SUBLANES, LANES), F32),
                        pltpu.VMEM((tile, d), F32)],
        compiler_params=_cparams("arbitrary", "arbitrary"),
        name="conv_group",
    )(u, w, b, norm_w, norm_b)


def _outproj_kernel(x_ref, ym_ref, yc_ref, wo_ref, nw_ref, wr_ref, br_ref,
                    h1_ref, a2_ref, idx_ref, pos_ref, gate_ref, cnt_ref, cnt_sc, *, tm, d_mlstm):
    i = pl.program_id(0)

    @pl.when(i == 0)
    def _():
        cnt_sc[...] = jnp.zeros_like(cnt_sc)

    h1 = (x_ref[...]
          + jnp.dot(ym_ref[...], wo_ref[0:d_mlstm, :], preferred_element_type=F32)
          + jnp.dot(yc_ref[...], wo_ref[d_mlstm:, :], preferred_element_type=F32))
    h1_ref[...] = h1
    a2 = h1 * lax.rsqrt(jnp.mean(h1 * h1, axis=-1, keepdims=True) + EPS) * nw_ref[...]
    _store_row_slabs(a2_ref, 0, _pack_bf16_pairs(a2), a2.shape[1] // (2 * LANES))

    logits = lax.dot_general(wr_ref[...], a2.astype(BF16), (((1,), (1,)), ((), ())),
                             preferred_element_type=F32) + br_ref[...]
    e_iota = lax.broadcasted_iota(jnp.int32, logits.shape, 0)
    work = logits
    vals, idxs = [], []
    for _ in range(TOP_K):
        mx = jnp.max(work, axis=0, keepdims=True)
        sel = jnp.min(jnp.where(work == mx, e_iota, N_EXPERTS), axis=0, keepdims=True)
        vals.append(mx)
        idxs.append(sel)
        work = jnp.where(e_iota == sel, NEG_INF, work)
    exps = [jnp.exp(vv - vals[0]) for vv in vals]
    tot = exps[0] + exps[1] + exps[2] + exps[3]
    gates = [ev / tot for ev in exps]

    chosen = functools.reduce(jnp.logical_or, [e_iota == sel for sel in idxs])
    mh = jnp.where(chosen, 1.0, 0.0)
    ri = lax.broadcasted_iota(jnp.int32, (tm, tm), 0)
    ci = lax.broadcasted_iota(jnp.int32, (tm, tm), 1)
    upper = jnp.where(ri < ci, 1.0, 0.0).astype(BF16)
    rank = jnp.dot(mh.astype(BF16), upper, preferred_element_type=F32) + cnt_sc[:, 0:1]
    cnt_new = cnt_sc[...] + jnp.sum(mh, axis=1, keepdims=True)
    cnt_sc[...] = cnt_new
    cnt_ref[...] = cnt_new

    zero_i = jnp.zeros((SUBLANES - TOP_K, tm), jnp.int32)
    pos = [jnp.sum(jnp.where(e_iota == sel, rank, 0.0), axis=0, keepdims=True).astype(jnp.int32) for sel in idxs]
    idx_ref[...] = jnp.concatenate(idxs + [zero_i], axis=0)
    pos_ref[...] = jnp.concatenate(pos + [zero_i], axis=0)
    gate_ref[...] = jnp.concatenate(gates + [jnp.zeros((SUBLANES - TOP_K, tm), F32)], axis=0)


def _out_proj(xf, ym, yc, w_out_b, norm_w, w_router_t, b_router, d_mlstm, tm):
    n, d = xf.shape
    n_slabs = d // (2 * LANES)
    row = lambda i: (i, 0)
    colb = lambda i: (0, i)
    const = lambda i: (0, 0)
    return pl.pallas_call(
        functools.partial(_outproj_kernel, tm=tm, d_mlstm=d_mlstm),
        grid=(n // tm,),
        in_specs=[pl.BlockSpec((tm, d), row), pl.BlockSpec((tm, ym.shape[1]), row), pl.BlockSpec((tm, yc.shape[1]), row),
                  pl.BlockSpec(w_out_b.shape, const), pl.BlockSpec(norm_w.shape, const),
                  pl.BlockSpec(w_router_t.shape, const), pl.BlockSpec(b_router.shape, const)],
        out_specs=[pl.BlockSpec((tm, d), row),
                   pl.BlockSpec((tm * n_slabs, LANES), row),
                   pl.BlockSpec((SUBLANES, tm), colb),
                   pl.BlockSpec((SUBLANES, tm), colb),
                   pl.BlockSpec((SUBLANES, tm), colb),
                   pl.BlockSpec((N_EXPERTS, LANES), const)],
        out_shape=[jax.ShapeDtypeStruct((n, d), F32),
                   jax.ShapeDtypeStruct((n * n_slabs, LANES), jnp.uint32),
                   jax.ShapeDtypeStruct((SUBLANES, n), jnp.int32),
                   jax.ShapeDtypeStruct((SUBLANES, n), jnp.int32),
                   jax.ShapeDtypeStruct((SUBLANES, n), F32),
                   jax.ShapeDtypeStruct((N_EXPERTS, LANES), F32)],
        scratch_shapes=[pltpu.VMEM((N_EXPERTS, LANES), F32)],
        compiler_params=_cparams("arbitrary"),
        name="out_proj_router",
    )(xf, ym, yc, w_out_b, norm_w, w_router_t, b_router)


def _experts_kernel(blk_ref, exp_ref, valid_ref, lo_ref, hi_ref,
                    xs_ref, w1_ref, b1_ref, w2_ref, b2_ref, ys_ref, w1c_ref, w2c_ref, acc_ref, *, rb, d_ff, n_sub):
    i = pl.program_id(0)
    n_slabs = xs_ref.shape[0] // rb
    prev = jnp.maximum(i - 1, 0)
    first_visit = jnp.logical_or(i == 0, blk_ref[prev] != blk_ref[i])
    new_expert = jnp.logical_or(i == 0, exp_ref[prev] != exp_ref[i])

    @pl.when(new_expert)
    def _():
        w1c_ref[...] = w1_ref[0].astype(BF16)
        w2c_ref[...] = w2_ref[0].astype(BF16)

    @pl.when(first_visit)
    def _():
        acc_ref[...] = jnp.zeros_like(acc_ref)

    rs = rb // n_sub
    lo, hi = lo_ref[i], hi_ref[i]

    def sub_block(c):
        xa, xb = _unpack_bf16_pairs(_load_row_slabs(xs_ref, c * rs, rs, n_slabs))
        x = jnp.concatenate([xa.astype(BF16), xb.astype(BF16)], axis=-1)
        hid = jnp.dot(x, w1c_ref[...], preferred_element_type=F32) + b1_ref[0]
        x_glu = jnp.minimum(hid[:, :d_ff], SWIGLU_LIMIT)
        x_lin = jnp.clip(hid[:, d_ff:], -SWIGLU_LIMIT, SWIGLU_LIMIT)
        act = x_glu * _sigmoid(SWIGLU_ALPHA * x_glu) * (x_lin + 1.0)
        y = jnp.dot(act.astype(BF16), w2c_ref[...], preferred_element_type=F32) + b2_ref[0]
        r = c * rs + lax.broadcasted_iota(jnp.int32, (rs, 1), 0)
        y = jnp.where(jnp.logical_and(r >= lo, r < hi), y, 0.0)
        y = acc_ref[c * rs:(c + 1) * rs, :] + y
        acc_ref[c * rs:(c + 1) * rs, :] = y
        _store_row_slabs(ys_ref, c * rs, _pack_bf16_pairs(y), n_slabs)

    for c in range(n_sub):
        has_rows = jnp.logical_and(valid_ref[i] == 1, jnp.logical_and(lo < (c + 1) * rs, hi > c * rs))
        pl.when(has_rows)(functools.partial(sub_block, c))


def _experts(xs, w1, b1, w2, b2, item_blk, item_exp, item_valid, item_lo, item_hi, rb, n_slabs):
    n_items = item_blk.shape[0]
    d = w1.shape[1]
    d_ff = w2.shape[1]
    by_blk = lambda i, blk, ex, va, lo, hi: (blk[i], 0)
    by_exp = lambda i, blk, ex, va, lo, hi: (ex[i], 0, 0)
    grid_spec = pltpu.PrefetchScalarGridSpec(
        num_scalar_prefetch=5,
        grid=(n_items,),
        in_specs=[pl.BlockSpec((rb * n_slabs, LANES), by_blk),
                  pl.BlockSpec((1,) + w1.shape[1:], by_exp),
                  pl.BlockSpec((1,) + b1.shape[1:], by_exp),
                  pl.BlockSpec((1,) + w2.shape[1:], by_exp),
                  pl.BlockSpec((1,) + b2.shape[1:], by_exp)],
        out_specs=pl.BlockSpec((rb * n_slabs, LANES), by_blk),
        scratch_shapes=[pltpu.VMEM(w1.shape[1:], BF16), pltpu.VMEM(w2.shape[1:], BF16), pltpu.VMEM((rb, d), F32)],
    )
    return pl.pallas_call(
        functools.partial(_experts_kernel, rb=rb, d_ff=d_ff, n_sub=rb // MXU_DIM),
        grid_spec=grid_spec,
        out_shape=jax.ShapeDtypeStruct(xs.shape, jnp.uint32),
        compiler_params=pltpu.CompilerParams(dimension_semantics=("arbitrary",), vmem_limit_bytes=EXPERTS_VMEM_LIMIT),
        name="experts",
    )(item_blk, item_exp, item_valid, item_lo, item_hi, xs, w1, b1, w2, b2)


SC_CORES = 2
SC_SUBCORES = 16
SC_GATHER_WINDOW = 64


def _sc_worker_base(per_worker):
    wid = lax.axis_index("s") * SC_CORES + lax.axis_index("c")
    return wid * per_worker


def _sc_gather_rows(table, idx):
    m = idx.shape[0]
    n_workers = SC_CORES * SC_SUBCORES
    window = SC_GATHER_WINDOW
    per_worker = m // n_workers
    n_win = per_worker // window
    assert per_worker * n_workers == m and n_win * window == per_worker and n_win % 2 == 0
    mesh = plsc.VectorSubcoreMesh(core_axis_name="c", subcore_axis_name="s")
    slab = table.shape[1:]

    @functools.partial(
        pl.kernel, mesh=mesh,
        out_type=jax.ShapeDtypeStruct((m,) + slab, table.dtype),
        scratch_types=[pltpu.VMEM((per_worker,), jnp.int32),
                       pltpu.VMEM((window,) + slab, table.dtype), pltpu.VMEM((window,) + slab, table.dtype)]
        + [pltpu.SemaphoreType.DMA] * 4,
        name="sc_gather_rows",
    )
    def gather(table_hbm, idx_hbm, out_hbm, idx_v, rows0, rows1, gsem0, gsem1, wsem0, wsem1):
        rows_v, gsem, wsem = (rows0, rows1), (gsem0, gsem1), (wsem0, wsem1)
        base = _sc_worker_base(per_worker)
        pltpu.sync_copy(idx_hbm.at[pl.ds(pl.multiple_of(base, window), per_worker)], idx_v)

        def gather_copy(w, b):
            ids = idx_v.at[pl.ds(pl.multiple_of(w * window, window), window)]
            return pltpu.make_async_copy(table_hbm.at[ids], rows_v[b], gsem[b])

        def write_copy(w, b):
            dst = out_hbm.at[pl.ds(pl.multiple_of(base + w * window, window), window)]
            return pltpu.make_async_copy(rows_v[b], dst, wsem[b])

        gather_copy(0, 0).start()

        @pl.loop(0, n_win, step=2)
        def _(w):
            @pl.when(w >= 1)
            def _():
                write_copy(w - 1, 1).wait()

            gather_copy(w + 1, 1).start()
            gather_copy(w, 0).wait()
            write_copy(w, 0).start()

            @pl.when(w + 2 < n_win)
            def _():
                write_copy(w, 0).wait()
                gather_copy(w + 2, 0).start()

            gather_copy(w + 1, 1).wait()
            write_copy(w + 1, 1).start()

        write_copy(n_win - 2, 0).wait()
        write_copy(n_win - 1, 1).wait()

    return gather(table, idx)


def _sc_scatter_rows(rows, dest, n_slots):
    n = rows.shape[0]
    n_workers = SC_CORES * SC_SUBCORES
    window = SC_GATHER_WINDOW
    per_worker = n // n_workers
    n_win = per_worker // window
    assert per_worker * n_workers == n and n_win * window == per_worker and n_win % 2 == 0
    mesh = plsc.VectorSubcoreMesh(core_axis_name="c", subcore_axis_name="s")
    slab = rows.shape[1:]

    @functools.partial(
        pl.kernel, mesh=mesh,
        out_type=jax.ShapeDtypeStruct((n_slots * n,) + slab, rows.dtype),
        scratch_types=[pltpu.VMEM((window,), jnp.int32)] * (2 * n_slots)
        + [pltpu.VMEM((window,) + slab, rows.dtype)] * 2 + [pltpu.SemaphoreType.DMA] * 4,
        name="sc_scatter_rows",
    )
    def scatter(rows_hbm, dest_hbm, out_hbm, *scratch):
        idx_v = (scratch[:n_slots], scratch[n_slots:2 * n_slots])
        rows_v = scratch[2 * n_slots:2 * n_slots + 2]
        rsem = scratch[2 * n_slots + 2:2 * n_slots + 4]
        wsem = scratch[2 * n_slots + 4:2 * n_slots + 6]
        base = _sc_worker_base(per_worker)

        def rows_at(w, k=0):
            return pl.ds(pl.multiple_of(k * n + base + w * window, window), window)

        def start_read(w, b):
            for k in range(n_slots):
                pltpu.sync_copy(dest_hbm.at[rows_at(w, k)], idx_v[b][k])
            pltpu.async_copy(rows_hbm.at[rows_at(w)], rows_v[b], rsem[b])

        def scatter_window(w, b):
            pltpu.make_async_copy(rows_hbm.at[rows_at(w)], rows_v[b], rsem[b]).wait()
            for k in range(n_slots):
                pltpu.async_copy(rows_v[b], out_hbm.at[idx_v[b][k]], wsem[b])
            for k in range(n_slots):
                pltpu.make_async_copy(rows_v[b], out_hbm.at[idx_v[b][k]], wsem[b]).wait()

        start_read(0, 0)

        @pl.loop(0, n_win, step=2)
        def _(w):
            start_read(w + 1, 1)
            scatter_window(w, 0)

            @pl.when(w + 2 < n_win)
            def _():
                start_read(w + 2, 0)

            scatter_window(w + 1, 1)

    return scatter(rows, dest)


def _final_kernel(h1_ref, gate_ref, yg_ref, nw_ref, out_ref, *, tm, n_slabs):
    gpad = jnp.concatenate([gate_ref[...], jnp.zeros((LANES - SUBLANES, tm), F32)], axis=0)
    gcol = gpad.T
    h2 = h1_ref[...]
    for k in range(TOP_K):
        ya, yb = _unpack_bf16_pairs(_load_row_slabs(yg_ref, 0, tm, n_slabs, lead=(k,)))
        h2 = h2 + gcol[:, k:k + 1] * jnp.concatenate([ya, yb], axis=-1)
    out_ref[...] = h2 * lax.rsqrt(jnp.mean(h2 * h2, axis=-1, keepdims=True) + EPS) * nw_ref[...]


def _final(h, gates_t, yg, final_w, n_slabs, tm, chunk, n_chunks):
    n, d = h.shape
    tiles = n // tm // n_chunks
    first = chunk * tiles
    return pl.pallas_call(
        functools.partial(_final_kernel, tm=tm, n_slabs=n_slabs),
        grid=(tiles,),
        in_specs=[pl.BlockSpec((tm, d), lambda i: (first + i, 0)),
                  pl.BlockSpec((SUBLANES, tm), lambda i: (0, first + i)),
                  pl.BlockSpec((TOP_K, tm * n_slabs, LANES), lambda i: (0, i, 0)),
                  pl.BlockSpec((1, d), lambda i: (0, 0))],
        out_specs=pl.BlockSpec((tm, d), lambda i: (first + i, 0)),
        out_shape=jax.ShapeDtypeStruct((n, d), F32),
        input_output_aliases={0: 0},
        compiler_params=_cparams("parallel"),
        name="final",
    )(h, gates_t, yg, final_w)


def _block_diag_tiles(w):
    nb, bs, _ = w.shape
    rows = jnp.tile(w.reshape(nb * bs // MXU_DIM, MXU_DIM, bs), (1, 1, MXU_DIM // bs))
    r_blk = lax.broadcasted_iota(jnp.int32, (MXU_DIM, MXU_DIM), 0) // bs
    c_blk = lax.broadcasted_iota(jnp.int32, (MXU_DIM, MXU_DIM), 1) // bs
    return jnp.where(r_blk == c_blk, rows, 0.0).astype(BF16)


def _layer(xf, bsz, seq, norm_mix_w, w_in, mlstm_conv_w, mlstm_conv_b, w_q, w_k, w_v, w_igate, b_igate,
           w_fgate, b_fgate, mlstm_norm_w, mlstm_skip, conv_dw_w, conv_dw_b, conv_norm_w, conv_norm_b,
           w_out, norm_ffn_w, w_router, b_router, w1, b1, w2, b2, final_norm_w):
    n, d = xf.shape
    d_mlstm = mlstm_norm_w.shape[0]
    d_conv = conv_norm_w.shape[0]
    n_slabs = d // (2 * LANES)
    r2 = lambda v: v.reshape(1, -1)

    xm, xc, sz, u = _in_proj(xf, r2(norm_mix_w), w_in.astype(BF16), mlstm_conv_w, r2(mlstm_conv_b),
                             d_mlstm, d_conv, seq, tm=IN_PROJ_ROWS)

    q_scale = (d_mlstm // N_HEADS) ** -0.5
    inv_q_scale = round(1.0 / q_scale)
    assert inv_q_scale & (inv_q_scale - 1) == 0 and inv_q_scale * q_scale == 1.0, "head dim must be a power of 4"
    wg = jnp.concatenate([w_igate, w_fgate], axis=1)
    wg = jnp.concatenate([wg[:d_mlstm] * float(inv_q_scale), wg[d_mlstm:]], axis=0)
    wg = jnp.pad(wg, ((0, 0), (0, LANES - wg.shape[1]))).astype(BF16)
    bg = jnp.pad(jnp.concatenate([b_igate, b_fgate]), (0, LANES - 2 * N_HEADS)).reshape(1, LANES)
    ym = _mlstm(xm, xc, sz, _block_diag_tiles(w_q * q_scale), _block_diag_tiles(w_k), _block_diag_tiles(w_v),
                wg, bg, r2(mlstm_norm_w), r2(mlstm_skip), bsz, seq)
    yc = _conv_group(u, conv_dw_w, r2(conv_dw_b), r2(conv_norm_w), r2(conv_norm_b), bsz, seq, tile=CONV_ROWS)

    h1, a2s, idx_t, pos_t, gates_t, cnt = _out_proj(
        xf, ym, yc, w_out.astype(BF16), r2(norm_ffn_w), w_router.T.astype(BF16), b_router.reshape(-1, 1), d_mlstm,
        tm=OUT_PROJ_ROWS)

    counts = cnt[:, 0].astype(jnp.int32)
    ends = jnp.cumsum(counts)
    starts = ends - counts
    e_ids = jnp.arange(N_EXPERTS, dtype=jnp.int32)
    idx4 = idx_t[:TOP_K]
    dest = pos_t[:TOP_K] + jnp.sum(
        jnp.where(idx4[None] == e_ids[:, None, None], starts[:, None, None], 0), axis=0)

    n_rows = n * TOP_K
    rb = EXPERT_ROWS
    n_blocks = n_rows // rb
    n_items = n_blocks + N_EXPERTS - 1
    first_blk = starts // rb
    last_blk = jnp.where(counts > 0, (ends - 1) // rb, first_blk - 1)
    per_e = last_blk - first_blk + 1
    item_end = jnp.cumsum(per_e)
    item_start = item_end - per_e
    ids = jnp.arange(n_items, dtype=jnp.int32)
    total = item_end[-1]
    item_valid = (ids < total).astype(jnp.int32)
    item_exp = jnp.minimum(jnp.sum((ids[:, None] >= item_end[None, :]).astype(jnp.int32), axis=1), N_EXPERTS - 1)
    last_valid_exp = jnp.max(jnp.where(per_e > 0, e_ids, 0))
    item_exp = jnp.where(item_valid == 1, item_exp, last_valid_exp).astype(jnp.int32)
    is_exp = item_exp[:, None] == e_ids[None, :]
    of_item = lambda table: jnp.sum(jnp.where(is_exp, table[None, :], 0), axis=1)
    item_blk = jnp.where(item_valid == 1, of_item(first_blk) + ids - of_item(item_start), n_blocks - 1)
    item_blk = item_blk.astype(jnp.int32)
    item_lo = (jnp.maximum(of_item(starts), item_blk * rb) - item_blk * rb).astype(jnp.int32)
    item_hi = (jnp.minimum(of_item(ends), (item_blk + 1) * rb) - item_blk * rb).astype(jnp.int32)

    xs = _sc_scatter_rows(a2s.reshape(n, n_slabs, LANES), dest.reshape(-1), TOP_K)
    ys = _experts(xs.reshape(n_rows * n_slabs, LANES), w1, b1[:, None, :], w2, b2[:, None, :],
                  item_blk, item_exp, item_valid, item_lo, item_hi, rb, n_slabs)
    ys = ys.reshape(n_rows, n_slabs, LANES)

    n_chunks = FINAL_CHUNKS
    nc = n // n_chunks
    out = h1
    for c in range(n_chunks):
        yg = _sc_gather_rows(ys, dest[:, c * nc:(c + 1) * nc].reshape(-1))
        out = _final(out, gates_t, yg.reshape(TOP_K, nc * n_slabs, LANES), r2(final_norm_w), n_slabs,
                     tm=FINAL_ROWS, chunk=c, n_chunks=n_chunks)
    return out


def kernel(x, norm_mix_w, w_in, mlstm_conv_w, mlstm_conv_b, w_q, w_k, w_v, w_igate, b_igate, w_fgate, b_fgate,
           mlstm_norm_w, mlstm_skip, conv_dw_w, conv_dw_b, conv_norm_w, conv_norm_b, w_out, norm_ffn_w,
           w_router, b_router, w1, b1, w2, b2, final_norm_w):
    bsz, seq, d = x.shape
    assert norm_mix_w.shape[0] == 1, "single-layer block"
    out = _layer(x.reshape(bsz * seq, d), bsz, seq, norm_mix_w[0], w_in[0], mlstm_conv_w[0], mlstm_conv_b[0],
                 w_q[0], w_k[0], w_v[0], w_igate[0], b_igate[0], w_fgate[0], b_fgate[0], mlstm_norm_w[0],
                 mlstm_skip[0], conv_dw_w[0], conv_dw_b[0], conv_norm_w[0], conv_norm_b[0], w_out[0],
                 norm_ffn_w[0], w_router[0], b_router[0], w1[0], b1[0], w2[0], b2[0], final_norm_w)
    return out.reshape(bsz, seq, d)
```

```python
import functools

import jax
import jax.numpy as jnp
from jax import lax
from jax.experimental import pallas as pl
from jax.experimental.pallas import tpu as pltpu
from jax.experimental.pallas import tpu_sc as plsc

F32 = jnp.float32
BF16 = jnp.bfloat16

EPS = 1e-5
N_HEADS = 4
MLSTM_CONV_WIDTH = 4
CONV_WIDTH = 31
N_EXPERTS = 32
TOP_K = 4
SWIGLU_ALPHA = 1.702
SWIGLU_LIMIT = 7.0

LANES = 128
SUBLANES = 8
MXU_DIM = 256
VMEM_LIMIT = 52 * 1024 * 1024
EXPERTS_VMEM_LIMIT = 58 * 1024 * 1024

IN_PROJ_ROWS = 512
CONV_ROWS = 512
CONV_ACC_ROWS = 64
OUT_PROJ_ROWS = 512
EXPERT_ROWS = 1024
FINAL_ROWS = 256
FINAL_CHUNKS = 8

MLSTM_CHUNK = 256
CONV_HALO = 32
NEG_INF = float("-inf")
LOG2_E = 1.4426950408889634


def _sigmoid(x):
    return 0.5 * jnp.tanh(0.5 * x) + 0.5


def _silu(x):
    return x * _sigmoid(x)


def _pack_bf16_pairs(v):
    half = v.shape[1] // 2
    hi = lax.bitcast_convert_type(v[:, :half].astype(BF16).astype(F32), jnp.uint32)
    lo = lax.bitcast_convert_type(v[:, half:].astype(BF16).astype(F32), jnp.uint32)
    return hi | (lo >> 16)


def _unpack_bf16_pairs(w):
    hi = lax.bitcast_convert_type(w & jnp.uint32(0xFFFF0000), F32)
    lo = lax.bitcast_convert_type(w << 16, F32)
    return hi, lo


def _load_row_slabs(ref, first, rows, n_slabs, lead=()):
    return jnp.concatenate([ref[lead + (pl.ds(first * n_slabs + s, rows, stride=n_slabs), slice(None))]
                            for s in range(n_slabs)], axis=-1)


def _store_row_slabs(ref, first, v, n_slabs):
    rows = v.shape[0]
    for s in range(n_slabs):
        ref[pl.ds(first * n_slabs + s, rows, stride=n_slabs), :] = v[:, s * LANES:(s + 1) * LANES]


def _cparams(*sem):
    return pltpu.CompilerParams(dimension_semantics=sem, vmem_limit_bytes=VMEM_LIMIT)


def _inproj_kernel(x_ref, nw_ref, w_ref, cw_ref, cb_ref, xm_ref, xc_ref, sz_ref, u_ref, xbuf,
                   *, d_mlstm, d_conv, tiles_per_seq):
    tm = x_ref.shape[0]
    hist = SUBLANES
    i = pl.program_id(0)

    @pl.when(i == 0)
    def _():
        xbuf[...] = jnp.zeros_like(xbuf)

    x = x_ref[...]
    a = x * lax.rsqrt(jnp.mean(x * x, axis=-1, keepdims=True) + EPS) * nw_ref[...]
    ab = a.astype(BF16)
    same_seq = i % tiles_per_seq != 0
    cols = MXU_DIM

    for c in range(d_mlstm // cols):
        sl = slice(c * cols, (c + 1) * cols)
        prev = xbuf[hist:hist + tm, sl]
        acc = cb_ref[:, sl] + cw_ref[MLSTM_CONV_WIDTH - 1:MLSTM_CONV_WIDTH, sl] * prev
        for k in range(MLSTM_CONV_WIDTH - 1):
            off = hist - (MLSTM_CONV_WIDTH - 1) + k
            acc = acc + cw_ref[k:k + 1, sl] * xbuf[off:off + tm, sl]
        xc_ref[:, sl] = _silu(acc).astype(xc_ref.dtype)
        xbuf[0:hist, sl] = jnp.where(same_seq, prev[tm - hist:, :], 0.0)

        xm = jnp.dot(ab, w_ref[:, sl], preferred_element_type=F32)
        xm_ref[:, sl] = xm.astype(xm_ref.dtype)
        xbuf[hist:hist + tm, sl] = xm
        z = jnp.dot(ab, w_ref[:, d_mlstm + c * cols:d_mlstm + (c + 1) * cols], preferred_element_type=F32)
        sz_ref[:, sl] = _silu(z)

    for c in range(d_conv // cols):
        sl = slice(c * cols, (c + 1) * cols)
        ga = jnp.dot(ab, w_ref[:, 2 * d_mlstm + c * cols:2 * d_mlstm + (c + 1) * cols], preferred_element_type=F32)
        gb = jnp.dot(ab, w_ref[:, 2 * d_mlstm + d_conv + c * cols:2 * d_mlstm + d_conv + (c + 1) * cols],
                     preferred_element_type=F32)
        u_ref[:, sl] = ga * _sigmoid(gb)


def _in_proj(xf, norm_w, w_in_b, conv_w, conv_b, d_mlstm, d_conv, seq, tm):
    n, d = xf.shape
    n_tiles = n // tm
    cur = lambda i: (jnp.minimum(i, n_tiles - 1), 0)
    prv = lambda i: (jnp.maximum(i - 1, 0), 0)
    const = lambda i: (0, 0)
    return pl.pallas_call(
        functools.partial(_inproj_kernel, d_mlstm=d_mlstm, d_conv=d_conv, tiles_per_seq=seq // tm),
        grid=(n_tiles + 1,),
        in_specs=[pl.BlockSpec((tm, d), cur),
                  pl.BlockSpec((1, d), const),
                  pl.BlockSpec(w_in_b.shape, const),
                  pl.BlockSpec(conv_w.shape, const),
                  pl.BlockSpec(conv_b.shape, const)],
        out_specs=[pl.BlockSpec((tm, d_mlstm), cur),
                   pl.BlockSpec((tm, d_mlstm), prv),
                   pl.BlockSpec((tm, d_mlstm), cur),
                   pl.BlockSpec((tm, d_conv), cur)],
        out_shape=[jax.ShapeDtypeStruct((n, d_mlstm), BF16),
                   jax.ShapeDtypeStruct((n, d_mlstm), BF16),
                   jax.ShapeDtypeStruct((n, d_mlstm), F32),
                   jax.ShapeDtypeStruct((n, d_conv), F32)],
        scratch_shapes=[pltpu.VMEM((tm + SUBLANES, d_mlstm), F32)],
        compiler_params=_cparams("arbitrary"),
        name="in_proj",
    )(xf, norm_w, w_in_b, conv_w, conv_b)


def _split3(v):
    hi = v.astype(BF16)
    r1 = v - hi.astype(F32)
    mid = r1.astype(BF16)
    lo = (r1 - mid.astype(F32)).astype(BF16)
    return hi, mid, lo


def _mlstm_kernel(xm_ref, xc_ref, sz_ref, wq_ref, wk_ref, wv_ref, wg_ref, bg_ref, nw_ref, sk_ref,
                  ym_ref, c_sc, n_sc, m_sc, *, chunk, dh, n_seq):
    L = chunk
    nh = N_HEADS
    j = pl.program_id(1)

    @pl.when(j == 0)
    def _():
        c_sc[...] = jnp.zeros_like(c_sc)
        n_sc[...] = jnp.zeros_like(n_sc)
        m_sc[...] = jnp.zeros_like(m_sc)

    ri = lax.broadcasted_iota(jnp.int32, (L, L), 0)
    ci = lax.broadcasted_iota(jnp.int32, (L, L), 1)
    causal = ci <= ri
    tri = jnp.where(causal, 1.0, 0.0).astype(BF16)

    for sq in range(n_seq):
        xmb = xm_ref[sq]
        xcb = xc_ref[sq]
        d = xmb.shape[1]
        nb = d // MXU_DIM

        def bd(xb, w_ref):
            return jnp.concatenate(
                [jnp.dot(xb[:, g * MXU_DIM:(g + 1) * MXU_DIM], w_ref[g], preferred_element_type=F32)
                 for g in range(nb)], axis=-1)

        q = bd(xcb, wq_ref)
        k_ = bd(xcb, wk_ref)
        v = bd(xmb, wv_ref)
        qb, kb, vb = q.astype(BF16), k_.astype(BF16), v.astype(BF16)

        g = (jnp.dot(qb, wg_ref[0:d, :], preferred_element_type=F32)
             + jnp.dot(kb, wg_ref[d:2 * d, :], preferred_element_type=F32)
             + jnp.dot(vb, wg_ref[2 * d:3 * d, :], preferred_element_type=F32)
             + bg_ref[...])
        col = lax.broadcasted_iota(jnp.int32, g.shape, 1)
        log_f = jnp.minimum(g, 0.0) - jnp.log(1.0 + jnp.exp(-jnp.abs(g)))
        gates = jnp.where(col < nh, g, jnp.where(col < 2 * nh, log_f, 0.0))
        cum = sum(jnp.dot(tri, part, preferred_element_type=F32) for part in _split3(gates))
        colform = jnp.where(col < nh, gates, cum) * LOG2_E
        rowform = colform.T

        for h in range(nh):
            sl = slice(h * dh, (h + 1) * dh)
            b_row = rowform[nh + h:nh + h + 1, :]
            r_row = rowform[h:h + 1, :] - b_row
            li_col = colform[:, h:h + 1]
            b_col = colform[:, nh + h:nh + h + 1]
            m_prev = m_sc[sq, h:h + 1, 0:1]
            g_tot = b_row[:, L - 1:L]

            dmat = jnp.where(causal, b_col + r_row, NEG_INF)
            inter = b_col + m_prev
            m_i = jnp.maximum(inter, jnp.max(dmat, axis=-1, keepdims=True))
            w_intra = jnp.exp2(dmat - m_i)
            w_inter = jnp.exp2(inter - m_i)

            qh = qb[:, sl]
            nt = (((1,), (1,)), ((), ()))
            s = lax.dot_general(qh, kb[:, sl], nt, preferred_element_type=F32) * w_intra
            c_prev = c_sc[sq, h]
            n_prev = n_sc[sq, h:h + 1, :]
            num = w_inter * jnp.dot(qh, c_prev.astype(BF16), preferred_element_type=F32) \
                + jnp.dot(s.astype(BF16), vb[:, sl], preferred_element_type=F32)
            n_rep = jnp.broadcast_to(n_prev, (LANES, dh)).astype(BF16)
            qn = lax.dot_general(qh, n_rep, nt, preferred_element_type=F32)[:, 0:1]
            den = w_inter * qn + jnp.sum(s, axis=-1, keepdims=True)
            hh = num * (1.0 / jnp.maximum(jnp.abs(den), jnp.exp2(-m_i)))

            mu = jnp.mean(hh, axis=-1, keepdims=True)
            dev = hh - mu
            var = jnp.mean(dev * dev, axis=-1, keepdims=True)
            hn = dev * lax.rsqrt(var + EPS) * nw_ref[:, sl]
            ym_ref[sq, :, sl] = ((hn + sk_ref[:, sl] * xcb[:, sl].astype(F32)) * sz_ref[sq, :, sl]).astype(ym_ref.dtype)

            a_row = g_tot + r_row
            m_new = jnp.maximum(g_tot + m_prev, jnp.max(a_row, axis=-1, keepdims=True))
            a_col = g_tot - b_col + li_col
            w_state = jnp.exp2(a_col - m_new)
            decay = jnp.exp2(g_tot + m_prev - m_new)
            kw = k_[:, sl] * w_state
            c_sc[sq, h] = decay * c_prev + lax.dot_general(kw.astype(BF16), vb[:, sl], (((0,), (0,)), ((), ())),
                                                           preferred_element_type=F32)
            w_state_rows = jnp.broadcast_to(jnp.exp2(a_row - m_new), (SUBLANES, L)).astype(BF16)
            n_sc[sq, h:h + 1, :] = decay * n_prev + jnp.dot(w_state_rows, kb[:, sl],
                                                            preferred_element_type=F32)[0:1, :]
            m_sc[sq, h:h + 1, :] = jnp.broadcast_to(m_new, (1, m_sc.shape[2]))


def _mlstm(xm, xc, sz, wq, wk, wv, wg, bg, norm_w, skip, bsz, seq):
    n, d = xm.shape
    L = MLSTM_CHUNK
    nc = seq // L
    dh = d // N_HEADS
    n_seq = 2 if bsz % 2 == 0 else 1
    blk = lambda b, j: (b, j, 0)
    c2 = lambda b, j: (0, 0)
    c3 = lambda b, j: (0, 0, 0)
    ym = pl.pallas_call(
        functools.partial(_mlstm_kernel, chunk=L, dh=dh, n_seq=n_seq),
        grid=(bsz // n_seq, nc),
        in_specs=[pl.BlockSpec((n_seq, L, d), blk), pl.BlockSpec((n_seq, L, d), blk), pl.BlockSpec((n_seq, L, d), blk),
                  pl.BlockSpec(wq.shape, c3), pl.BlockSpec(wk.shape, c3), pl.BlockSpec(wv.shape, c3),
                  pl.BlockSpec(wg.shape, c2), pl.BlockSpec(bg.shape, c2),
                  pl.BlockSpec(norm_w.shape, c2), pl.BlockSpec(skip.shape, c2)],
        out_specs=pl.BlockSpec((n_seq, L, d), blk),
        out_shape=jax.ShapeDtypeStruct((bsz, seq, d), BF16),
        scratch_shapes=[pltpu.VMEM((n_seq, N_HEADS, dh, dh), F32),
                        pltpu.VMEM((n_seq, SUBLANES, dh), F32),
                        pltpu.VMEM((n_seq, SUBLANES, LANES), F32)],
        compiler_params=_cparams("arbitrary", "arbitrary"),
        name="mlstm",
    )(xm.reshape(bsz, seq, d), xc.reshape(bsz, seq, d), sz.reshape(bsz, seq, d), wq, wk, wv, wg, bg, norm_w, skip)
    return ym.reshape(n, d)


def _conv_kernel(u_ref, w_ref, b_ref, nw_ref, nb_ref, yc_ref, ubuf, pbuf, cbuf, *, tile, rows):
    T = tile
    j = pl.program_id(1)

    @pl.when(j == 0)
    def _():
        ubuf[0:CONV_HALO, :] = jnp.zeros((CONV_HALO, ubuf.shape[1]), F32)

    ubuf[CONV_HALO:CONV_HALO + T, :] = u_ref[...]
    base = CONV_HALO - (CONV_WIDTH - 1)
    span = T + CONV_HALO - SUBLANES
    n_lane_blocks = ubuf.shape[1] // LANES

    def lane_block(c, carry):
        lanes = pl.ds(pl.multiple_of(c * LANES, LANES), LANES)
        for r in range(1, SUBLANES):
            pbuf[r - 1, :, :] = ubuf[r:r + span, lanes]
        for r0 in range(0, T, rows):
            acc = jnp.broadcast_to(b_ref[:, lanes], (rows, LANES))
            for k in range(CONV_WIDTH):
                q, r = divmod(base + k, SUBLANES)
                lo = r0 + q * SUBLANES
                src = ubuf[lo:lo + rows, lanes] if r == 0 else pbuf[r - 1, lo:lo + rows, :]
                acc = acc + w_ref[k:k + 1, lanes] * src
            cbuf[r0:r0 + rows, lanes] = acc
        return carry

    lax.fori_loop(0, n_lane_blocks, lane_block, 0)
    ubuf[0:CONV_HALO, :] = ubuf[T:T + CONV_HALO, :]

    y = cbuf[...]
    mu = jnp.mean(y, axis=-1, keepdims=True)
    dev = y - mu
    var = jnp.mean(dev * dev, axis=-1, keepdims=True)
    yn = dev * lax.rsqrt(var + EPS) * nw_ref[...] + nb_ref[...]
    yc_ref[...] = _silu(yn).astype(yc_ref.dtype)


def _conv_group(u, w, b, norm_w, norm_b, bsz, seq, tile):
    n, d = u.shape
    nt = seq // tile
    row = lambda bi, j: (bi * nt + j, 0)
    c2 = lambda bi, j: (0, 0)
    return pl.pallas_call(
        functools.partial(_conv_kernel, tile=tile, rows=CONV_ACC_ROWS),
        grid=(bsz, nt),
        in_specs=[pl.BlockSpec((tile, d), row), pl.BlockSpec(w.shape, c2), pl.BlockSpec(b.shape, c2),
                  pl.BlockSpec(norm_w.shape, c2), pl.BlockSpec(norm_b.shape, c2)],
        out_specs=pl.BlockSpec((tile, d), row),
        out_shape=jax.ShapeDtypeStruct((n, d), BF16),
        scratch_shapes=[pltpu.VMEM((tile + CONV_HALO, d), F32),
                        pltpu.VMEM((SUBLANES - 1, tile + CONV_HALO - SUBLANES, LANES), F32),
                        pltpu.VMEM((tile, d), F32)],
        compiler_params=_cparams("arbitrary", "arbitrary"),
        name="conv_group",
    )(u, w, b, norm_w, norm_b)


def _outproj_kernel(x_ref, ym_ref, yc_ref, wo_ref, nw_ref, wr_ref, br_ref,
                    h1_ref, a2_ref, idx_ref, pos_ref, gate_ref, cnt_ref, cnt_sc, *, tm, d_mlstm):
    i = pl.program_id(0)

    @pl.when(i == 0)
    def _():
        cnt_sc[...] = jnp.zeros_like(cnt_sc)

    h1 = (x_ref[...]
          + jnp.dot(ym_ref[...], wo_ref[0:d_mlstm, :], preferred_element_type=F32)
          + jnp.dot(yc_ref[...], wo_ref[d_mlstm:, :], preferred_element_type=F32))
    h1_ref[...] = h1
    a2 = h1 * lax.rsqrt(jnp.mean(h1 * h1, axis=-1, keepdims=True) + EPS) * nw_ref[...]
    _store_row_slabs(a2_ref, 0, _pack_bf16_pairs(a2), a2.shape[1] // (2 * LANES))

    logits = lax.dot_general(wr_ref[...], a2.astype(BF16), (((1,), (1,)), ((), ())),
                             preferred_element_type=F32) + br_ref[...]
    e_iota = lax.broadcasted_iota(jnp.int32, logits.shape, 0)
    work = logits
    vals, idxs = [], []
    for _ in range(TOP_K):
        mx = jnp.max(work, axis=0, keepdims=True)
        sel = jnp.min(jnp.where(work == mx, e_iota, N_EXPERTS), axis=0, keepdims=True)
        vals.append(mx)
        idxs.append(sel)
        work = jnp.where(e_iota == sel, NEG_INF, work)
    exps = [jnp.exp(vv - vals[0]) for vv in vals]
    tot = exps[0] + exps[1] + exps[2] + exps[3]
    gates = [ev / tot for ev in exps]

    chosen = functools.reduce(jnp.logical_or, [e_iota == sel for sel in idxs])
    mh = jnp.where(chosen, 1.0, 0.0)
    ri = lax.broadcasted_iota(jnp.int32, (tm, tm), 0)
    ci = lax.broadcasted_iota(jnp.int32, (tm, tm), 1)
    upper = jnp.where(ri < ci, 1.0, 0.0).astype(BF16)
    rank = jnp.dot(mh.astype(BF16), upper, preferred_element_type=F32) + cnt_sc[:, 0:1]
    cnt_new = cnt_sc[...] + jnp.sum(mh, axis=1, keepdims=True)
    cnt_sc[...] = cnt_new
    cnt_ref[...] = cnt_new

    zero_i = jnp.zeros((SUBLANES - TOP_K, tm), jnp.int32)
    pos = [jnp.sum(jnp.where(e_iota == sel, rank, 0.0), axis=0, keepdims=True).astype(jnp.int32) for sel in idxs]
    idx_ref[...] = jnp.concatenate(idxs + [zero_i], axis=0)
    pos_ref[...] = jnp.concatenate(pos + [zero_i], axis=0)
    gate_ref[...] = jnp.concatenate(gates + [jnp.zeros((SUBLANES - TOP_K, tm), F32)], axis=0)


def _out_proj(xf, ym, yc, w_out_b, norm_w, w_router_t, b_router, d_mlstm, tm):
    n, d = xf.shape
    n_slabs = d // (2 * LANES)
    row = lambda i: (i, 0)
    colb = lambda i: (0, i)
    const = lambda i: (0, 0)
    return pl.pallas_call(
        functools.partial(_outproj_kernel, tm=tm, d_mlstm=d_mlstm),
        grid=(n // tm,),
        in_specs=[pl.BlockSpec((tm, d), row), pl.BlockSpec((tm, ym.shape[1]), row), pl.BlockSpec((tm, yc.shape[1]), row),
                  pl.BlockSpec(w_out_b.shape, const), pl.BlockSpec(norm_w.shape, const),
                  pl.BlockSpec(w_router_t.shape, const), pl.BlockSpec(b_router.shape, const)],
        out_specs=[pl.BlockSpec((tm, d), row),
                   pl.BlockSpec((tm * n_slabs, LANES), row),
                   pl.BlockSpec((SUBLANES, tm), colb),
                   pl.BlockSpec((SUBLANES, tm), colb),
                   pl.BlockSpec((SUBLANES, tm), colb),
                   pl.BlockSpec((N_EXPERTS, LANES), const)],
        out_shape=[jax.ShapeDtypeStruct((n, d), F32),
                   jax.ShapeDtypeStruct((n * n_slabs, LANES), jnp.uint32),
                   jax.ShapeDtypeStruct((SUBLANES, n), jnp.int32),
                   jax.ShapeDtypeStruct((SUBLANES, n), jnp.int32),
                   jax.ShapeDtypeStruct((SUBLANES, n), F32),
                   jax.ShapeDtypeStruct((N_EXPERTS, LANES), F32)],
        scratch_shapes=[pltpu.VMEM((N_EXPERTS, LANES), F32)],
        compiler_params=_cparams("arbitrary"),
        name="out_proj_router",
    )(xf, ym, yc, w_out_b, norm_w, w_router_t, b_router)


def _experts_kernel(blk_ref, exp_ref, valid_ref, lo_ref, hi_ref,
                    xs_ref, w1_ref, b1_ref, w2_ref, b2_ref, ys_ref, w1c_ref, w2c_ref, acc_ref, *, rb, d_ff, n_sub):
    i = pl.program_id(0)
    n_slabs = xs_ref.shape[0] // rb
    prev = jnp.maximum(i - 1, 0)
    first_visit = jnp.logical_or(i == 0, blk_ref[prev] != blk_ref[i])
    new_expert = jnp.logical_or(i == 0, exp_ref[prev] != exp_ref[i])

    @pl.when(new_expert)
    def _():
        w1c_ref[...] = w1_ref[0].astype(BF16)
        w2c_ref[...] = w2_ref[0].astype(BF16)

    @pl.when(first_visit)
    def _():
        acc_ref[...] = jnp.zeros_like(acc_ref)

    rs = rb // n_sub
    lo, hi = lo_ref[i], hi_ref[i]

    def sub_block(c):
        xa, xb = _unpack_bf16_pairs(_load_row_slabs(xs_ref, c * rs, rs, n_slabs))
        x = jnp.concatenate([xa.astype(BF16), xb.astype(BF16)], axis=-1)
        hid = jnp.dot(x, w1c_ref[...], preferred_element_type=F32) + b1_ref[0]
        x_glu = jnp.minimum(hid[:, :d_ff], SWIGLU_LIMIT)
        x_lin = jnp.clip(hid[:, d_ff:], -SWIGLU_LIMIT, SWIGLU_LIMIT)
        act = x_glu * _sigmoid(SWIGLU_ALPHA * x_glu) * (x_lin + 1.0)
        y = jnp.dot(act.astype(BF16), w2c_ref[...], preferred_element_type=F32) + b2_ref[0]
        r = c * rs + lax.broadcasted_iota(jnp.int32, (rs, 1), 0)
        y = jnp.where(jnp.logical_and(r >= lo, r < hi), y, 0.0)
        y = acc_ref[c * rs:(c + 1) * rs, :] + y
        acc_ref[c * rs:(c + 1) * rs, :] = y
        _store_row_slabs(ys_ref, c * rs, _pack_bf16_pairs(y), n_slabs)

    for c in range(n_sub):
        has_rows = jnp.logical_and(valid_ref[i] == 1, jnp.logical_and(lo < (c + 1) * rs, hi > c * rs))
        pl.when(has_rows)(functools.partial(sub_block, c))


def _experts(xs, w1, b1, w2, b2, item_blk, item_exp, item_valid, item_lo, item_hi, rb, n_slabs):
    n_items = item_blk.shape[0]
    d = w1.shape[1]
    d_ff = w2.shape[1]
    by_blk = lambda i, blk, ex, va, lo, hi: (blk[i], 0)
    by_exp = lambda i, blk, ex, va, lo, hi: (ex[i], 0, 0)
    grid_spec = pltpu.PrefetchScalarGridSpec(
        num_scalar_prefetch=5,
        grid=(n_items,),
        in_specs=[pl.BlockSpec((rb * n_slabs, LANES), by_blk),
                  pl.BlockSpec((1,) + w1.shape[1:], by_exp),
                  pl.BlockSpec((1,) + b1.shape[1:], by_exp),
                  pl.BlockSpec((1,) + w2.shape[1:], by_exp),
                  pl.BlockSpec((1,) + b2.shape[1:], by_exp)],
        out_specs=pl.BlockSpec((rb * n_slabs, LANES), by_blk),
        scratch_shapes=[pltpu.VMEM(w1.shape[1:], BF16), pltpu.VMEM(w2.shape[1:], BF16), pltpu.VMEM((rb, d), F32)],
    )
    return pl.pallas_call(
        functools.partial(_experts_kernel, rb=rb, d_ff=d_ff, n_sub=rb // MXU_DIM),
        grid_spec=grid_spec,
        out_shape=jax.ShapeDtypeStruct(xs.shape, jnp.uint32),
        compiler_params=pltpu.CompilerParams(dimension_semantics=("arbitrary",), vmem_limit_bytes=EXPERTS_VMEM_LIMIT),
        name="experts",
    )(item_blk, item_exp, item_valid, item_lo, item_hi, xs, w1, b1, w2, b2)


SC_CORES = 2
SC_SUBCORES = 16
SC_GATHER_WINDOW = 64


def _sc_worker_base(per_worker):
    wid = lax.axis_index("s") * SC_CORES + lax.axis_index("c")
    return wid * per_worker


def _sc_gather_rows(table, idx):
    m = idx.shape[0]
    n_workers = SC_CORES * SC_SUBCORES
    window = SC_GATHER_WINDOW
    per_worker = m // n_workers
    n_win = per_worker // window
    assert per_worker * n_workers == m and n_win * window == per_worker and n_win % 2 == 0
    mesh = plsc.VectorSubcoreMesh(core_axis_name="c", subcore_axis_name="s")
    slab = table.shape[1:]

    @functools.partial(
        pl.kernel, mesh=mesh,
        out_type=jax.ShapeDtypeStruct((m,) + slab, table.dtype),
        scratch_types=[pltpu.VMEM((per_worker,), jnp.int32),
                       pltpu.VMEM((window,) + slab, table.dtype), pltpu.VMEM((window,) + slab, table.dtype)]
        + [pltpu.SemaphoreType.DMA] * 4,
        name="sc_gather_rows",
    )
    def gather(table_hbm, idx_hbm, out_hbm, idx_v, rows0, rows1, gsem0, gsem1, wsem0, wsem1):
        rows_v, gsem, wsem = (rows0, rows1), (gsem0, gsem1), (wsem0, wsem1)
        base = _sc_worker_base(per_worker)
        pltpu.sync_copy(idx_hbm.at[pl.ds(pl.multiple_of(base, window), per_worker)], idx_v)

        def gather_copy(w, b):
            ids = idx_v.at[pl.ds(pl.multiple_of(w * window, window), window)]
            return pltpu.make_async_copy(table_hbm.at[ids], rows_v[b], gsem[b])

        def write_copy(w, b):
            dst = out_hbm.at[pl.ds(pl.multiple_of(base + w * window, window), window)]
            return pltpu.make_async_copy(rows_v[b], dst, wsem[b])

        gather_copy(0, 0).start()

        @pl.loop(0, n_win, step=2)
        def _(w):
            @pl.when(w >= 1)
            def _():
                write_copy(w - 1, 1).wait()

            gather_copy(w + 1, 1).start()
            gather_copy(w, 0).wait()
            write_copy(w, 0).start()

            @pl.when(w + 2 < n_win)
            def _():
                write_copy(w, 0).wait()
                gather_copy(w + 2, 0).start()

            gather_copy(w + 1, 1).wait()
            write_copy(w + 1, 1).start()

        write_copy(n_win - 2, 0).wait()
        write_copy(n_win - 1, 1).wait()

    return gather(table, idx)


def _sc_scatter_rows(rows, dest, n_slots):
    n = rows.shape[0]
    n_workers = SC_CORES * SC_SUBCORES
    window = SC_GATHER_WINDOW
    per_worker = n // n_workers
    n_win = per_worker // window
    assert per_worker * n_workers == n and n_win * window == per_worker
    mesh = plsc.VectorSubcoreMesh(core_axis_name="c", subcore_axis_name="s")
    slab = rows.shape[1:]
    n_idx = n_win * n_slots

    @functools.partial(
        pl.kernel, mesh=mesh,
        out_type=jax.ShapeDtypeStruct((n_slots * n,) + slab, rows.dtype),
        scratch_types=[pltpu.VMEM((window,), jnp.int32)] * n_idx
        + [pltpu.VMEM((window,) + slab, rows.dtype)] * 2 + [pltpu.SemaphoreType.DMA] * 5,
        name="sc_scatter_rows",
    )
    def scatter(rows_hbm, dest_hbm, out_hbm, *scratch):
        idx_v = scratch[:n_idx]
        rows_v = scratch[n_idx:n_idx + 2]
        rsem = scratch[n_idx + 2:n_idx + 4]
        wsem = scratch[n_idx + 4:n_idx + 6]
        isem = scratch[n_idx + 6]
        base = _sc_worker_base(per_worker)

        def rows_at(w, k=0):
            return pl.ds(pl.multiple_of(k * n + base + w * window, window), window)

        def index_copy(w, k):
            return pltpu.make_async_copy(dest_hbm.at[rows_at(w, k)], idx_v[w * n_slots + k], isem)

        def read_copy(w):
            return pltpu.make_async_copy(rows_hbm.at[rows_at(w)], rows_v[w % 2], rsem[w % 2])

        def scatter_copy(w, k):
            return pltpu.make_async_copy(rows_v[w % 2], out_hbm.at[idx_v[w * n_slots + k]], wsem[w % 2])

        read_copy(0).start()
        for w in range(n_win):
            for k in range(n_slots):
                index_copy(w, k).start()
        for w in range(n_win):
            for k in range(n_slots):
                index_copy(w, k).wait()

        for w in range(n_win):
            if w >= 1:
                for k in range(n_slots):
                    scatter_copy(w - 1, k).wait()
            if w + 1 < n_win:
                read_copy(w + 1).start()
            read_copy(w).wait()
            for k in range(n_slots):
                scatter_copy(w, k).start()
        for k in range(n_slots):
            scatter_copy(n_win - 1, k).wait()

    return scatter(rows, dest)


def _final_kernel(h1_ref, gate_ref, yg_ref, nw_ref, out_ref, *, tm, n_slabs):
    gpad = jnp.concatenate([gate_ref[...], jnp.zeros((LANES - SUBLANES, tm), F32)], axis=0)
    gcol = gpad.T
    h2 = h1_ref[...]
    for k in range(TOP_K):
        ya, yb = _unpack_bf16_pairs(_load_row_slabs(yg_ref, 0, tm, n_slabs, lead=(k,)))
        h2 = h2 + gcol[:, k:k + 1] * jnp.concatenate([ya, yb], axis=-1)
    out_ref[...] = h2 * lax.rsqrt(jnp.mean(h2 * h2, axis=-1, keepdims=True) + EPS) * nw_ref[...]


def _final(h, gates_t, yg, final_w, n_slabs, tm, chunk, n_chunks):
    n, d = h.shape
    tiles = n // tm // n_chunks
    first = chunk * tiles
    return pl.pallas_call(
        functools.partial(_final_kernel, tm=tm, n_slabs=n_slabs),
        grid=(tiles,),
        in_specs=[pl.BlockSpec((tm, d), lambda i: (first + i, 0)),
                  pl.BlockSpec((SUBLANES, tm), lambda i: (0, first + i)),
                  pl.BlockSpec((TOP_K, tm * n_slabs, LANES), lambda i: (0, i, 0)),
                  pl.BlockSpec((1, d), lambda i: (0, 0))],
        out_specs=pl.BlockSpec((tm, d), lambda i: (first + i, 0)),
        out_shape=jax.ShapeDtypeStruct((n, d), F32),
        input_output_aliases={0: 0},
        compiler_params=_cparams("parallel"),
        name="final",
    )(h, gates_t, yg, final_w)


def _block_diag_tiles(w):
    nb, bs, _ = w.shape
    rows = jnp.tile(w.reshape(nb * bs // MXU_DIM, MXU_DIM, bs), (1, 1, MXU_DIM // bs))
    r_blk = lax.broadcasted_iota(jnp.int32, (MXU_DIM, MXU_DIM), 0) // bs
    c_blk = lax.broadcasted_iota(jnp.int32, (MXU_DIM, MXU_DIM), 1) // bs
    return jnp.where(r_blk == c_blk, rows, 0.0).astype(BF16)


def _layer(xf, bsz, seq, norm_mix_w, w_in, mlstm_conv_w, mlstm_conv_b, w_q, w_k, w_v, w_igate, b_igate,
           w_fgate, b_fgate, mlstm_norm_w, mlstm_skip, conv_dw_w, conv_dw_b, conv_norm_w, conv_norm_b,
           w_out, norm_ffn_w, w_router, b_router, w1, b1, w2, b2, final_norm_w):
    n, d = xf.shape
    d_mlstm = mlstm_norm_w.shape[0]
    d_conv = conv_norm_w.shape[0]
    n_slabs = d // (2 * LANES)
    r2 = lambda v: v.reshape(1, -1)

    xm, xc, sz, u = _in_proj(xf, r2(norm_mix_w), w_in.astype(BF16), mlstm_conv_w, r2(mlstm_conv_b),
                             d_mlstm, d_conv, seq, tm=IN_PROJ_ROWS)

    q_scale = (d_mlstm // N_HEADS) ** -0.5
    inv_q_scale = round(1.0 / q_scale)
    assert inv_q_scale & (inv_q_scale - 1) == 0 and inv_q_scale * q_scale == 1.0, "head dim must be a power of 4"
    wg = jnp.concatenate([w_igate, w_fgate], axis=1)
    wg = jnp.concatenate([wg[:d_mlstm] * float(inv_q_scale), wg[d_mlstm:]], axis=0)
    wg = jnp.pad(wg, ((0, 0), (0, LANES - wg.shape[1]))).astype(BF16)
    bg = jnp.pad(jnp.concatenate([b_igate, b_fgate]), (0, LANES - 2 * N_HEADS)).reshape(1, LANES)
    ym = _mlstm(xm, xc, sz, _block_diag_tiles(w_q * q_scale), _block_diag_tiles(w_k), _block_diag_tiles(w_v),
                wg, bg, r2(mlstm_norm_w), r2(mlstm_skip), bsz, seq)
    yc = _conv_group(u, conv_dw_w, r2(conv_dw_b), r2(conv_norm_w), r2(conv_norm_b), bsz, seq, tile=CONV_ROWS)

    h1, a2s, idx_t, pos_t, gates_t, cnt = _out_proj(
        xf, ym, yc, w_out.astype(BF16), r2(norm_ffn_w), w_router.T.astype(BF16), b_router.reshape(-1, 1), d_mlstm,
        tm=OUT_PROJ_ROWS)

    counts = cnt[:, 0].astype(jnp.int32)
    ends = jnp.cumsum(counts)
    starts = ends - counts
    e_ids = jnp.arange(N_EXPERTS, dtype=jnp.int32)
    idx4 = idx_t[:TOP_K]
    dest = pos_t[:TOP_K] + jnp.sum(
        jnp.where(idx4[None] == e_ids[:, None, None], starts[:, None, None], 0), axis=0)

    n_rows = n * TOP_K
    rb = EXPERT_ROWS
    n_blocks = n_rows // rb
    n_items = n_blocks + N_EXPERTS - 1
    first_blk = starts // rb
    last_blk = jnp.where(counts > 0, (ends - 1) // rb, first_blk - 1)
    per_e = last_blk - first_blk + 1
    item_end = jnp.cumsum(per_e)
    item_start = item_end - per_e
    ids = jnp.arange(n_items, dtype=jnp.int32)
    total = item_end[-1]
    item_valid = (ids < total).astype(jnp.int32)
    item_exp = jnp.minimum(jnp.sum((ids[:, None] >= item_end[None, :]).astype(jnp.int32), axis=1), N_EXPERTS - 1)
    last_valid_exp = jnp.max(jnp.where(per_e > 0, e_ids, 0))
    item_exp = jnp.where(item_valid == 1, item_exp, last_valid_exp).astype(jnp.int32)
    is_exp = item_exp[:, None] == e_ids[None, :]
    of_item = lambda table: jnp.sum(jnp.where(is_exp, table[None, :], 0), axis=1)
    item_blk = jnp.where(item_valid == 1, of_item(first_blk) + ids - of_item(item_start), n_blocks - 1)
    item_blk = item_blk.astype(jnp.int32)
    item_lo = (jnp.maximum(of_item(starts), item_blk * rb) - item_blk * rb).astype(jnp.int32)
    item_hi = (jnp.minimum(of_item(ends), (item_blk + 1) * rb) - item_blk * rb).astype(jnp.int32)

    xs = _sc_scatter_rows(a2s.reshape(n, n_slabs, LANES), dest.reshape(-1), TOP_K)
    ys = _experts(xs.reshape(n_rows * n_slabs, LANES), w1, b1[:, None, :], w2, b2[:, None, :],
                  item_blk, item_exp, item_valid, item_lo, item_hi, rb, n_slabs)
    ys = ys.reshape(n_rows, n_slabs, LANES)

    n_chunks = FINAL_CHUNKS
    nc = n // n_chunks
    out = h1
    for c in range(n_chunks):
        yg = _sc_gather_rows(ys, dest[:, c * nc:(c + 1) * nc].reshape(-1))
        out = _final(out, gates_t, yg.reshape(TOP_K, nc * n_slabs, LANES), r2(final_norm_w), n_slabs,
                     tm=FINAL_ROWS, chunk=c, n_chunks=n_chunks)
    return out


def kernel(x, norm_mix_w, w_in, mlstm_conv_w, mlstm_conv_b, w_q, w_k, w_v, w_igate, b_igate, w_fgate, b_fgate,
           mlstm_norm_w, mlstm_skip, conv_dw_w, conv_dw_b, conv_norm_w, conv_norm_b, w_out, norm_ffn_w,
           w_router, b_router, w1, b1, w2, b2, final_norm_w):
    bsz, seq, d = x.shape
    assert norm_mix_w.shape[0] == 1, "single-layer block"
    out = _layer(x.reshape(bsz * seq, d), bsz, seq, norm_mix_w[0], w_in[0], mlstm_conv_w[0], mlstm_conv_b[0],
                 w_q[0], w_k[0], w_v[0], w_igate[0], b_igate[0], w_fgate[0], b_fgate[0], mlstm_norm_w[0],
                 mlstm_skip[0], conv_dw_w[0], conv_dw_b[0], conv_norm_w[0], conv_norm_b[0], w_out[0],
                 norm_ffn_w[0], w_router[0], b_router[0], w1[0], b1[0], w2[0], b2[0], final_norm_w)
    return out.reshape(bsz, seq, d)
```

```python
import functools

import jax
import jax.numpy as jnp
from jax import lax
from jax.experimental import pallas as pl
from jax.experimental.pallas import tpu as pltpu
from jax.experimental.pallas import tpu_sc as plsc

F32 = jnp.float32
BF16 = jnp.bfloat16

EPS = 1e-5
N_HEADS = 4
MLSTM_CONV_WIDTH = 4
CONV_WIDTH = 31
N_EXPERTS = 32
TOP_K = 4
SWIGLU_ALPHA = 1.702
SWIGLU_LIMIT = 7.0

LANES = 128
SUBLANES = 8
MXU_DIM = 256
VMEM_LIMIT = 52 * 1024 * 1024
EXPERTS_VMEM_LIMIT = 58 * 1024 * 1024

IN_PROJ_ROWS = 512
CONV_ROWS = 512
CONV_ACC_ROWS = 64
OUT_PROJ_ROWS = 512
EXPERT_ROWS = 1024
FINAL_ROWS = 256
FINAL_CHUNKS = 8

MLSTM_CHUNK = 256
CONV_HALO = 32
NEG_INF = float("-inf")
LOG2_E = 1.4426950408889634


def _sigmoid(x):
    return 0.5 * jnp.tanh(0.5 * x) + 0.5


def _silu(x):
    return x * _sigmoid(x)


def _pack_bf16_pairs(v):
    half = v.shape[1] // 2
    hi = lax.bitcast_convert_type(v[:, :half].astype(BF16).astype(F32), jnp.uint32)
    lo = lax.bitcast_convert_type(v[:, half:].astype(BF16).astype(F32), jnp.uint32)
    return hi | (lo >> 16)


def _unpack_bf16_pairs(w):
    hi = lax.bitcast_convert_type(w & jnp.uint32(0xFFFF0000), F32)
    lo = lax.bitcast_convert_type(w << 16, F32)
    return hi, lo


def _load_row_slabs(ref, first, rows, n_slabs, lead=()):
    return jnp.concatenate([ref[lead + (pl.ds(first * n_slabs + s, rows, stride=n_slabs), slice(None))]
                            for s in range(n_slabs)], axis=-1)


def _store_row_slabs(ref, first, v, n_slabs):
    rows = v.shape[0]
    for s in range(n_slabs):
        ref[pl.ds(first * n_slabs + s, rows, stride=n_slabs), :] = v[:, s * LANES:(s + 1) * LANES]


def _cparams(*sem):
    return pltpu.CompilerParams(dimension_semantics=sem, vmem_limit_bytes=VMEM_LIMIT)


def _inproj_kernel(x_ref, nw_ref, w_ref, cw_ref, cb_ref, xm_ref, xc_ref, sz_ref, u_ref, xbuf,
                   *, d_mlstm, d_conv, tiles_per_seq):
    tm = x_ref.shape[0]
    hist = SUBLANES
    i = pl.program_id(0)

    @pl.when(i == 0)
    def _():
        xbuf[...] = jnp.zeros_like(xbuf)

    x = x_ref[...]
    a = x * lax.rsqrt(jnp.mean(x * x, axis=-1, keepdims=True) + EPS) * nw_ref[...]
    ab = a.astype(BF16)
    same_seq = i % tiles_per_seq != 0
    cols = MXU_DIM

    for c in range(d_mlstm // cols):
        sl = slice(c * cols, (c + 1) * cols)
        prev = xbuf[hist:hist + tm, sl]
        acc = cb_ref[:, sl] + cw_ref[MLSTM_CONV_WIDTH - 1:MLSTM_CONV_WIDTH, sl] * prev
        for k in range(MLSTM_CONV_WIDTH - 1):
            off = hist - (MLSTM_CONV_WIDTH - 1) + k
            acc = acc + cw_ref[k:k + 1, sl] * xbuf[off:off + tm, sl]
        xc_ref[:, sl] = _silu(acc).astype(xc_ref.dtype)
        xbuf[0:hist, sl] = jnp.where(same_seq, prev[tm - hist:, :], 0.0)

        xm = jnp.dot(ab, w_ref[:, sl], preferred_element_type=F32)
        xm_ref[:, sl] = xm.astype(xm_ref.dtype)
        xbuf[hist:hist + tm, sl] = xm
        z = jnp.dot(ab, w_ref[:, d_mlstm + c * cols:d_mlstm + (c + 1) * cols], preferred_element_type=F32)
        sz_ref[:, sl] = _silu(z)

    for c in range(d_conv // cols):
        sl = slice(c * cols, (c + 1) * cols)
        ga = jnp.dot(ab, w_ref[:, 2 * d_mlstm + c * cols:2 * d_mlstm + (c + 1) * cols], preferred_element_type=F32)
        gb = jnp.dot(ab, w_ref[:, 2 * d_mlstm + d_conv + c * cols:2 * d_mlstm + d_conv + (c + 1) * cols],
                     preferred_element_type=F32)
        u_ref[:, sl] = ga * _sigmoid(gb)


def _in_proj(xf, norm_w, w_in_b, conv_w, conv_b, d_mlstm, d_conv, seq, tm):
    n, d = xf.shape
    n_tiles = n // tm
    cur = lambda i: (jnp.minimum(i, n_tiles - 1), 0)
    prv = lambda i: (jnp.maximum(i - 1, 0), 0)
    const = lambda i: (0, 0)
    return pl.pallas_call(
        functools.partial(_inproj_kernel, d_mlstm=d_mlstm, d_conv=d_conv, tiles_per_seq=seq // tm),
        grid=(n_tiles + 1,),
        in_specs=[pl.BlockSpec((tm, d), cur),
                  pl.BlockSpec((1, d), const),
                  pl.BlockSpec(w_in_b.shape, const),
                  pl.BlockSpec(conv_w.shape, const),
                  pl.BlockSpec(conv_b.shape, const)],
        out_specs=[pl.BlockSpec((tm, d_mlstm), cur),
                   pl.BlockSpec((tm, d_mlstm), prv),
                   pl.BlockSpec((tm, d_mlstm), cur),
                   pl.BlockSpec((tm, d_conv), cur)],
        out_shape=[jax.ShapeDtypeStruct((n, d_mlstm), BF16),
                   jax.ShapeDtypeStruct((n, d_mlstm), BF16),
                   jax.ShapeDtypeStruct((n, d_mlstm), F32),
                   jax.ShapeDtypeStruct((n, d_conv), F32)],
        scratch_shapes=[pltpu.VMEM((tm + SUBLANES, d_mlstm), F32)],
        compiler_params=_cparams("arbitrary"),
        name="in_proj",
    )(xf, norm_w, w_in_b, conv_w, conv_b)


def _split3(v):
    hi = v.astype(BF16)
    r1 = v - hi.astype(F32)
    mid = r1.astype(BF16)
    lo = (r1 - mid.astype(F32)).astype(BF16)
    return hi, mid, lo


def _mlstm_kernel(xm_ref, xc_ref, sz_ref, wq_ref, wk_ref, wv_ref, wg_ref, bg_ref, nw_ref, sk_ref,
                  ym_ref, c_sc, n_sc, m_sc, *, chunk, dh, n_seq):
    L = chunk
    nh = N_HEADS
    j = pl.program_id(1)

    @pl.when(j == 0)
    def _():
        c_sc[...] = jnp.zeros_like(c_sc)
        n_sc[...] = jnp.zeros_like(n_sc)
        m_sc[...] = jnp.zeros_like(m_sc)

    ri = lax.broadcasted_iota(jnp.int32, (L, L), 0)
    ci = lax.broadcasted_iota(jnp.int32, (L, L), 1)
    causal = ci <= ri
    tri = jnp.where(causal, 1.0, 0.0).astype(BF16)

    for sq in range(n_seq):
        xmb = xm_ref[sq]
        xcb = xc_ref[sq]
        d = xmb.shape[1]
        nb = d // MXU_DIM

        def bd(xb, w_ref):
            return jnp.concatenate(
                [jnp.dot(xb[:, g * MXU_DIM:(g + 1) * MXU_DIM], w_ref[g], preferred_element_type=F32)
                 for g in range(nb)], axis=-1)

        q = bd(xcb, wq_ref)
        k_ = bd(xcb, wk_ref)
        v = bd(xmb, wv_ref)
        qb, kb, vb = q.astype(BF16), k_.astype(BF16), v.astype(BF16)

        g = (jnp.dot(qb, wg_ref[0:d, :], preferred_element_type=F32)
             + jnp.dot(kb, wg_ref[d:2 * d, :], preferred_element_type=F32)
             + jnp.dot(vb, wg_ref[2 * d:3 * d, :], preferred_element_type=F32)
             + bg_ref[...])
        col = lax.broadcasted_iota(jnp.int32, g.shape, 1)
        log_f = jnp.minimum(g, 0.0) - jnp.log(1.0 + jnp.exp(-jnp.abs(g)))
        gates = jnp.where(col < nh, g, jnp.where(col < 2 * nh, log_f, 0.0))
        cum = sum(jnp.dot(tri, part, preferred_element_type=F32) for part in _split3(gates))
        colform = jnp.where(col < nh, gates, cum) * LOG2_E
        rowform = colform.T

        for h in range(nh):
            sl = slice(h * dh, (h + 1) * dh)
            b_row = rowform[nh + h:nh + h + 1, :]
            r_row = rowform[h:h + 1, :] - b_row
            li_col = colform[:, h:h + 1]
            b_col = colform[:, nh + h:nh + h + 1]
            m_prev = m_sc[sq, h:h + 1, 0:1]
            g_tot = b_row[:, L - 1:L]

            dmat = jnp.where(causal, b_col + r_row, NEG_INF)
            inter = b_col + m_prev
            m_i = jnp.maximum(inter, jnp.max(dmat, axis=-1, keepdims=True))
            w_intra = jnp.exp2(dmat - m_i)
            w_inter = jnp.exp2(inter - m_i)

            qh = qb[:, sl]
            nt = (((1,), (1,)), ((), ()))
            s = lax.dot_general(qh, kb[:, sl], nt, preferred_element_type=F32) * w_intra
            c_prev = c_sc[sq, h]
            n_prev = n_sc[sq, h:h + 1, :]
            num = w_inter * jnp.dot(qh, c_prev.astype(BF16), preferred_element_type=F32) \
                + jnp.dot(s.astype(BF16), vb[:, sl], preferred_element_type=F32)
            n_rep = jnp.broadcast_to(n_prev, (LANES, dh)).astype(BF16)
            qn = lax.dot_general(qh, n_rep, nt, preferred_element_type=F32)[:, 0:1]
            den = w_inter * qn + jnp.sum(s, axis=-1, keepdims=True)
            hh = num * (1.0 / jnp.maximum(jnp.abs(den), jnp.exp2(-m_i)))

            mu = jnp.mean(hh, axis=-1, keepdims=True)
            dev = hh - mu
            var = jnp.mean(dev * dev, axis=-1, keepdims=True)
            hn = dev * lax.rsqrt(var + EPS) * nw_ref[:, sl]
            ym_ref[sq, :, sl] = ((hn + sk_ref[:, sl] * xcb[:, sl].astype(F32)) * sz_ref[sq, :, sl]).astype(ym_ref.dtype)

            a_row = g_tot + r_row
            m_new = jnp.maximum(g_tot + m_prev, jnp.max(a_row, axis=-1, keepdims=True))
            a_col = g_tot - b_col + li_col
            w_state = jnp.exp2(a_col - m_new)
            decay = jnp.exp2(g_tot + m_prev - m_new)
            kw = k_[:, sl] * w_state
            c_sc[sq, h] = decay * c_prev + lax.dot_general(kw.astype(BF16), vb[:, sl], (((0,), (0,)), ((), ())),
                                                           preferred_element_type=F32)
            w_state_rows = jnp.broadcast_to(jnp.exp2(a_row - m_new), (SUBLANES, L)).astype(BF16)
            n_sc[sq, h:h + 1, :] = decay * n_prev + jnp.dot(w_state_rows, kb[:, sl],
                                                            preferred_element_type=F32)[0:1, :]
            m_sc[sq, h:h + 1, :] = jnp.broadcast_to(m_new, (1, m_sc.shape[2]))


def _mlstm(xm, xc, sz, wq, wk, wv, wg, bg, norm_w, skip, bsz, seq):
    n, d = xm.shape
    L = MLSTM_CHUNK
    nc = seq // L
    dh = d // N_HEADS
    n_seq = 2 if bsz % 2 == 0 else 1
    blk = lambda b, j: (b, j, 0)
    c2 = lambda b, j: (0, 0)
    c3 = lambda b, j: (0, 0, 0)
    ym = pl.pallas_call(
        functools.partial(_mlstm_kernel, chunk=L, dh=dh, n_seq=n_seq),
        grid=(bsz // n_seq, nc),
        in_specs=[pl.BlockSpec((n_seq, L, d), blk), pl.BlockSpec((n_seq, L, d), blk), pl.BlockSpec((n_seq, L, d), blk),
                  pl.BlockSpec(wq.shape, c3), pl.BlockSpec(wk.shape, c3), pl.BlockSpec(wv.shape, c3),
                  pl.BlockSpec(wg.shape, c2), pl.BlockSpec(bg.shape, c2),
                  pl.BlockSpec(norm_w.shape, c2), pl.BlockSpec(skip.shape, c2)],
        out_specs=pl.BlockSpec((n_seq, L, d), blk),
        out_shape=jax.ShapeDtypeStruct((bsz, seq, d), BF16),
        scratch_shapes=[pltpu.VMEM((n_seq, N_HEADS, dh, dh), F32),
                        pltpu.VMEM((n_seq, SUBLANES, dh), F32),
                        pltpu.VMEM((n_seq, SUBLANES, LANES), F32)],
        compiler_params=_cparams("arbitrary", "arbitrary"),
        name="mlstm",
    )(xm.reshape(bsz, seq, d), xc.reshape(bsz, seq, d), sz.reshape(bsz, seq, d), wq, wk, wv, wg, bg, norm_w, skip)
    return ym.reshape(n, d)


def _conv_kernel(u_ref, w_ref, b_ref, nw_ref, nb_ref, yc_ref, ubuf, pbuf, cbuf, *, tile, rows):
    T = tile
    j = pl.program_id(1)

    @pl.when(j == 0)
    def _():
        ubuf[0:CONV_HALO, :] = jnp.zeros((CONV_HALO, ubuf.shape[1]), F32)

    ubuf[CONV_HALO:CONV_HALO + T, :] = u_ref[...]
    base = CONV_HALO - (CONV_WIDTH - 1)
    span = T + CONV_HALO - SUBLANES
    n_lane_blocks = ubuf.shape[1] // LANES

    def lane_block(c, carry):
        lanes = pl.ds(pl.multiple_of(c * LANES, LANES), LANES)
        for r in range(1, SUBLANES):
            pbuf[r - 1, :, :] = ubuf[r:r + span, lanes]
        for r0 in range(0, T, rows):
            acc = jnp.broadcast_to(b_ref[:, lanes], (rows, LANES))
            for k in range(CONV_WIDTH):
                q, r = divmod(base + k, SUBLANES)
                lo = r0 + q * SUBLANES
                src = ubuf[lo:lo + rows, lanes] if r == 0 else pbuf[r - 1, lo:lo + rows, :]
                acc = acc + w_ref[k:k + 1, lanes] * src
            cbuf[r0:r0 + rows, lanes] = acc
        return carry

    lax.fori_loop(0, n_lane_blocks, lane_block, 0)
    ubuf[0:CONV_HALO, :] = ubuf[T:T + CONV_HALO, :]

    y = cbuf[...]
    mu = jnp.mean(y, axis=-1, keepdims=True)
    dev = y - mu
    var = jnp.mean(dev * dev, axis=-1, keepdims=True)
    yn = dev * lax.rsqrt(var + EPS) * nw_ref[...] + nb_ref[...]
    yc_ref[...] = _silu(yn).astype(yc_ref.dtype)


def _conv_group(u, w, b, norm_w, norm_b, bsz, seq, tile):
    n, d = u.shape
    nt = seq // tile
    row = lambda bi, j: (bi * nt + j, 0)
    c2 = lambda bi, j: (0, 0)
    return pl.pallas_call(
        functools.partial(_conv_kernel, tile=tile, rows=CONV_ACC_ROWS),
        grid=(bsz, nt),
        in_specs=[pl.BlockSpec((tile, d), row), pl.BlockSpec(w.shape, c2), pl.BlockSpec(b.shape, c2),
                  pl.BlockSpec(norm_w.shape, c2), pl.BlockSpec(norm_b.shape, c2)],
        out_specs=pl.BlockSpec((tile, d), row),
        out_shape=jax.ShapeDtypeStruct((n, d), BF16),
        scratch_shapes=[pltpu.VMEM((tile + CONV_HALO, d), F32),
                        pltpu.VMEM((SUBLANES - 1, tile + CONV_HALO - SUBLANES, LANES), F32),
                        pltpu.VMEM((tile, d), F32)],
        compiler_params=_cparams("arbitrary", "arbitrary"),
        name="conv_group",
    )(u, w, b, norm_w, norm_b)


def _outproj_kernel(x_ref, ym_ref, yc_ref, wo_ref, nw_ref, wr_ref, br_ref,
                    h1_ref, a2_ref, idx_ref, pos_ref, gate_ref, cnt_ref, cnt_sc, *, tm, d_mlstm):
    i = pl.program_id(0)

    @pl.when(i == 0)
    def _():
        cnt_sc[...] = jnp.zeros_like(cnt_sc)

    h1 = (x_ref[...]
          + jnp.dot(ym_ref[...], wo_ref[0:d_mlstm, :], preferred_element_type=F32)
          + jnp.dot(yc_ref[...], wo_ref[d_mlstm:, :], preferred_element_type=F32))
    h1_ref[...] = h1
    a2 = h1 * lax.rsqrt(jnp.mean(h1 * h1, axis=-1, keepdims=True) + EPS) * nw_ref[...]
    _store_row_slabs(a2_ref, 0, _pack_bf16_pairs(a2), a2.shape[1] // (2 * LANES))

    logits = lax.dot_general(wr_ref[...], a2.astype(BF16), (((1,), (1,)), ((), ())),
                             preferred_element_type=F32) + br_ref[...]
    e_iota = lax.broadcasted_iota(jnp.int32, logits.shape, 0)
    work = logits
    vals, idxs = [], []
    for _ in range(TOP_K):
        mx = jnp.max(work, axis=0, keepdims=True)
        sel = jnp.min(jnp.where(work == mx, e_iota, N_EXPERTS), axis=0, keepdims=True)
        vals.append(mx)
        idxs.append(sel)
        work = jnp.where(e_iota == sel, NEG_INF, work)
    exps = [jnp.exp(vv - vals[0]) for vv in vals]
    tot = exps[0] + exps[1] + exps[2] + exps[3]
    gates = [ev / tot for ev in exps]

    chosen = functools.reduce(jnp.logical_or, [e_iota == sel for sel in idxs])
    mh = jnp.where(chosen, 1.0, 0.0)
    ri = lax.broadcasted_iota(jnp.int32, (tm, tm), 0)
    ci = lax.broadcasted_iota(jnp.int32, (tm, tm), 1)
    upper = jnp.where(ri < ci, 1.0, 0.0).astype(BF16)
    rank = jnp.dot(mh.astype(BF16), upper, preferred_element_type=F32) + cnt_sc[:, 0:1]
    cnt_new = cnt_sc[...] + jnp.sum(mh, axis=1, keepdims=True)
    cnt_sc[...] = cnt_new
    cnt_ref[...] = cnt_new

    zero_i = jnp.zeros((SUBLANES - TOP_K, tm), jnp.int32)
    pos = [jnp.sum(jnp.where(e_iota == sel, rank, 0.0), axis=0, keepdims=True).astype(jnp.int32) for sel in idxs]
    idx_ref[...] = jnp.concatenate(idxs + [zero_i], axis=0)
    pos_ref[...] = jnp.concatenate(pos + [zero_i], axis=0)
    gate_ref[...] = jnp.concatenate(gates + [jnp.zeros((SUBLANES - TOP_K, tm), F32)], axis=0)


def _out_proj(xf, ym, yc, w_out_b, norm_w, w_router_t, b_router, d_mlstm, tm):
    n, d = xf.shape
    n_slabs = d // (2 * LANES)
    row = lambda i: (i, 0)
    colb = lambda i: (0, i)
    const = lambda i: (0, 0)
    return pl.pallas_call(
        functools.partial(_outproj_kernel, tm=tm, d_mlstm=d_mlstm),
        grid=(n // tm,),
        in_specs=[pl.BlockSpec((tm, d), row), pl.BlockSpec((tm, ym.shape[1]), row), pl.BlockSpec((tm, yc.shape[1]), row),
                  pl.BlockSpec(w_out_b.shape, const), pl.BlockSpec(norm_w.shape, const),
                  pl.BlockSpec(w_router_t.shape, const), pl.BlockSpec(b_router.shape, const)],
        out_specs=[pl.BlockSpec((tm, d), row),
                   pl.BlockSpec((tm * n_slabs, LANES), row),
                   pl.BlockSpec((SUBLANES, tm), colb),
                   pl.BlockSpec((SUBLANES, tm), colb),
                   pl.BlockSpec((SUBLANES, tm), colb),
                   pl.BlockSpec((N_EXPERTS, LANES), const)],
        out_shape=[jax.ShapeDtypeStruct((n, d), F32),
                   jax.ShapeDtypeStruct((n * n_slabs, LANES), jnp.uint32),
                   jax.ShapeDtypeStruct((SUBLANES, n), jnp.int32),
                   jax.ShapeDtypeStruct((SUBLANES, n), jnp.int32),
                   jax.ShapeDtypeStruct((SUBLANES, n), F32),
                   jax.ShapeDtypeStruct((N_EXPERTS, LANES), F32)],
        scratch_shapes=[pltpu.VMEM((N_EXPERTS, LANES), F32)],
        compiler_params=_cparams("arbitrary"),
        name="out_proj_router",
    )(xf, ym, yc, w_out_b, norm_w, w_router_t, b_router)


def _experts_kernel(blk_ref, exp_ref, valid_ref, lo_ref, hi_ref,
                    xs_ref, w1_ref, b1_ref, w2_ref, b2_ref, ys_ref, w1c_ref, w2c_ref, acc_ref, *, rb, d_ff, n_sub):
    i = pl.program_id(0)
    n_slabs = xs_ref.shape[0] // rb
    prev = jnp.maximum(i - 1, 0)
    first_visit = jnp.logical_or(i == 0, blk_ref[prev] != blk_ref[i])
    new_expert = jnp.logical_or(i == 0, exp_ref[prev] != exp_ref[i])

    @pl.when(new_expert)
    def _():
        w1c_ref[...] = w1_ref[0].astype(BF16)
        w2c_ref[...] = w2_ref[0].astype(BF16)

    @pl.when(first_visit)
    def _():
        acc_ref[...] = jnp.zeros_like(acc_ref)

    rs = rb // n_sub
    lo, hi = lo_ref[i], hi_ref[i]

    def sub_block(c, owns_all_rows):
        xa, xb = _unpack_bf16_pairs(_load_row_slabs(xs_ref, c * rs, rs, n_slabs))
        x = jnp.concatenate([xa.astype(BF16), xb.astype(BF16)], axis=-1)
        hid = jnp.dot(x, w1c_ref[...], preferred_element_type=F32) + b1_ref[0]
        x_glu = jnp.minimum(hid[:, :d_ff], SWIGLU_LIMIT)
        x_lin = jnp.clip(hid[:, d_ff:], -SWIGLU_LIMIT, SWIGLU_LIMIT)
        act = x_glu * _sigmoid(SWIGLU_ALPHA * x_glu) * (x_lin + 1.0)
        y = jnp.dot(act.astype(BF16), w2c_ref[...], preferred_element_type=F32) + b2_ref[0]
        if not owns_all_rows:
            r = c * rs + lax.broadcasted_iota(jnp.int32, (rs, 1), 0)
            y = jnp.where(jnp.logical_and(r >= lo, r < hi), y, 0.0)
            y = acc_ref[c * rs:(c + 1) * rs, :] + y
            acc_ref[c * rs:(c + 1) * rs, :] = y
        _store_row_slabs(ys_ref, c * rs, _pack_bf16_pairs(y), n_slabs)

    for c in range(n_sub):
        valid = valid_ref[i] == 1
        owns_all = jnp.logical_and(lo <= c * rs, hi >= (c + 1) * rs)
        has_rows = jnp.logical_and(lo < (c + 1) * rs, hi > c * rs)
        pl.when(jnp.logical_and(valid, owns_all))(functools.partial(sub_block, c, True))
        pl.when(jnp.logical_and(valid, jnp.logical_and(has_rows, jnp.logical_not(owns_all))))(
            functools.partial(sub_block, c, False))


def _experts(xs, w1, b1, w2, b2, item_blk, item_exp, item_valid, item_lo, item_hi, rb, n_slabs):
    n_items = item_blk.shape[0]
    d = w1.shape[1]
    d_ff = w2.shape[1]
    by_blk = lambda i, blk, ex, va, lo, hi: (blk[i], 0)
    by_exp = lambda i, blk, ex, va, lo, hi: (ex[i], 0, 0)
    grid_spec = pltpu.PrefetchScalarGridSpec(
        num_scalar_prefetch=5,
        grid=(n_items,),
        in_specs=[pl.BlockSpec((rb * n_slabs, LANES), by_blk),
                  pl.BlockSpec((1,) + w1.shape[1:], by_exp),
                  pl.BlockSpec((1,) + b1.shape[1:], by_exp),
                  pl.BlockSpec((1,) + w2.shape[1:], by_exp),
                  pl.BlockSpec((1,) + b2.shape[1:], by_exp)],
        out_specs=pl.BlockSpec((rb * n_slabs, LANES), by_blk),
        scratch_shapes=[pltpu.VMEM(w1.shape[1:], BF16), pltpu.VMEM(w2.shape[1:], BF16), pltpu.VMEM((rb, d), F32)],
    )
    return pl.pallas_call(
        functools.partial(_experts_kernel, rb=rb, d_ff=d_ff, n_sub=rb // MXU_DIM),
        grid_spec=grid_spec,
        out_shape=jax.ShapeDtypeStruct(xs.shape, jnp.uint32),
        compiler_params=pltpu.CompilerParams(dimension_semantics=("arbitrary",), vmem_limit_bytes=EXPERTS_VMEM_LIMIT),
        name="experts",
    )(item_blk, item_exp, item_valid, item_lo, item_hi, xs, w1, b1, w2, b2)


SC_CORES = 2
SC_SUBCORES = 16
SC_GATHER_WINDOW = 64


def _sc_worker_base(per_worker):
    wid = lax.axis_index("s") * SC_CORES + lax.axis_index("c")
    return wid * per_worker


def _sc_gather_rows(table, idx):
    m = idx.shape[0]
    n_workers = SC_CORES * SC_SUBCORES
    window = SC_GATHER_WINDOW
    per_worker = m // n_workers
    n_win = per_worker // window
    assert per_worker * n_workers == m and n_win * window == per_worker and n_win % 2 == 0
    mesh = plsc.VectorSubcoreMesh(core_axis_name="c", subcore_axis_name="s")
    slab = table.shape[1:]

    @functools.partial(
        pl.kernel, mesh=mesh,
        out_type=jax.ShapeDtypeStruct((m,) + slab, table.dtype),
        scratch_types=[pltpu.VMEM((per_worker,), jnp.int32),
                       pltpu.VMEM((window,) + slab, table.dtype), pltpu.VMEM((window,) + slab, table.dtype)]
        + [pltpu.SemaphoreType.DMA] * 4,
        name="sc_gather_rows",
    )
    def gather(table_hbm, idx_hbm, out_hbm, idx_v, rows0, rows1, gsem0, gsem1, wsem0, wsem1):
        rows_v, gsem, wsem = (rows0, rows1), (gsem0, gsem1), (wsem0, wsem1)
        base = _sc_worker_base(per_worker)
        pltpu.sync_copy(idx_hbm.at[pl.ds(pl.multiple_of(base, window), per_worker)], idx_v)

        def gather_copy(w, b):
            ids = idx_v.at[pl.ds(pl.multiple_of(w * window, window), window)]
            return pltpu.make_async_copy(table_hbm.at[ids], rows_v[b], gsem[b])

        def write_copy(w, b):
            dst = out_hbm.at[pl.ds(pl.multiple_of(base + w * window, window), window)]
            return pltpu.make_async_copy(rows_v[b], dst, wsem[b])

        gather_copy(0, 0).start()

        @pl.loop(0, n_win, step=2)
        def _(w):
            @pl.when(w >= 1)
            def _():
                write_copy(w - 1, 1).wait()

            gather_copy(w + 1, 1).start()
            gather_copy(w, 0).wait()
            write_copy(w, 0).start()

            @pl.when(w + 2 < n_win)
            def _():
                write_copy(w, 0).wait()
                gather_copy(w + 2, 0).start()

            gather_copy(w + 1, 1).wait()
            write_copy(w + 1, 1).start()

        write_copy(n_win - 2, 0).wait()
        write_copy(n_win - 1, 1).wait()

    return gather(table, idx)


def _sc_scatter_rows(rows, dest, n_slots):
    n = rows.shape[0]
    n_workers = SC_CORES * SC_SUBCORES
    window = SC_GATHER_WINDOW
    per_worker = n // n_workers
    n_win = per_worker // window
    assert per_worker * n_workers == n and n_win * window == per_worker and n_win % 2 == 0
    mesh = plsc.VectorSubcoreMesh(core_axis_name="c", subcore_axis_name="s")
    slab = rows.shape[1:]

    @functools.partial(
        pl.kernel, mesh=mesh,
        out_type=jax.ShapeDtypeStruct((n_slots * n,) + slab, rows.dtype),
        scratch_types=[pltpu.VMEM((window,), jnp.int32)] * (2 * n_slots)
        + [pltpu.VMEM((window,) + slab, rows.dtype)] * 2 + [pltpu.SemaphoreType.DMA] * 4,
        name="sc_scatter_rows",
    )
    def scatter(rows_hbm, dest_hbm, out_hbm, *scratch):
        idx_v = (scratch[:n_slots], scratch[n_slots:2 * n_slots])
        rows_v = scratch[2 * n_slots:2 * n_slots + 2]
        rsem = scratch[2 * n_slots + 2:2 * n_slots + 4]
        wsem = scratch[2 * n_slots + 4:2 * n_slots + 6]
        base = _sc_worker_base(per_worker)

        def rows_at(w, k=0):
            return pl.ds(pl.multiple_of(k * n + base + w * window, window), window)

        def start_read(w, b):
            for k in range(n_slots):
                pltpu.sync_copy(dest_hbm.at[rows_at(w, k)], idx_v[b][k])
            pltpu.async_copy(rows_hbm.at[rows_at(w)], rows_v[b], rsem[b])

        def scatter_window(w, b):
            pltpu.make_async_copy(rows_hbm.at[rows_at(w)], rows_v[b], rsem[b]).wait()
            for k in range(n_slots):
                pltpu.async_copy(rows_v[b], out_hbm.at[idx_v[b][k]], wsem[b])
            for k in range(n_slots):
                pltpu.make_async_copy(rows_v[b], out_hbm.at[idx_v[b][k]], wsem[b]).wait()

        start_read(0, 0)

        @pl.loop(0, n_win, step=2)
        def _(w):
            start_read(w + 1, 1)
            scatter_window(w, 0)

            @pl.when(w + 2 < n_win)
            def _():
                start_read(w + 2, 0)

            scatter_window(w + 1, 1)

    return scatter(rows, dest)


def _final_kernel(h1_ref, gate_ref, yg_ref, nw_ref, out_ref, *, tm, n_slabs):
    gpad = jnp.concatenate([gate_ref[...], jnp.zeros((LANES - SUBLANES, tm), F32)], axis=0)
    gcol = gpad.T
    h2 = h1_ref[...]
    for k in range(TOP_K):
        ya, yb = _unpack_bf16_pairs(_load_row_slabs(yg_ref, 0, tm, n_slabs, lead=(k,)))
        h2 = h2 + gcol[:, k:k + 1] * jnp.concatenate([ya, yb], axis=-1)
    out_ref[...] = h2 * lax.rsqrt(jnp.mean(h2 * h2, axis=-1, keepdims=True) + EPS) * nw_ref[...]


def _final(h, gates_t, yg, final_w, n_slabs, tm, chunk, n_chunks):
    n, d = h.shape
    tiles = n // tm // n_chunks
    first = chunk * tiles
    return pl.pallas_call(
        functools.partial(_final_kernel, tm=tm, n_slabs=n_slabs),
        grid=(tiles,),
        in_specs=[pl.BlockSpec((tm, d), lambda i: (first + i, 0)),
                  pl.BlockSpec((SUBLANES, tm), lambda i: (0, first + i)),
                  pl.BlockSpec((TOP_K, tm * n_slabs, LANES), lambda i: (0, i, 0)),
                  pl.BlockSpec((1, d), lambda i: (0, 0))],
        out_specs=pl.BlockSpec((tm, d), lambda i: (first + i, 0)),
        out_shape=jax.ShapeDtypeStruct((n, d), F32),
        input_output_aliases={0: 0},
        compiler_params=_cparams("parallel"),
        name="final",
    )(h, gates_t, yg, final_w)


def _block_diag_tiles(w):
    nb, bs, _ = w.shape
    rows = jnp.tile(w.reshape(nb * bs // MXU_DIM, MXU_DIM, bs), (1, 1, MXU_DIM // bs))
    r_blk = lax.broadcasted_iota(jnp.int32, (MXU_DIM, MXU_DIM), 0) // bs
    c_blk = lax.broadcasted_iota(jnp.int32, (MXU_DIM, MXU_DIM), 1) // bs
    return jnp.where(r_blk == c_blk, rows, 0.0).astype(BF16)


def _layer(xf, bsz, seq, norm_mix_w, w_in, mlstm_conv_w, mlstm_conv_b, w_q, w_k, w_v, w_igate, b_igate,
           w_fgate, b_fgate, mlstm_norm_w, mlstm_skip, conv_dw_w, conv_dw_b, conv_norm_w, conv_norm_b,
           w_out, norm_ffn_w, w_router, b_router, w1, b1, w2, b2, final_norm_w):
    n, d = xf.shape
    d_mlstm = mlstm_norm_w.shape[0]
    d_conv = conv_norm_w.shape[0]
    n_slabs = d // (2 * LANES)
    r2 = lambda v: v.reshape(1, -1)

    xm, xc, sz, u = _in_proj(xf, r2(norm_mix_w), w_in.astype(BF16), mlstm_conv_w, r2(mlstm_conv_b),
                             d_mlstm, d_conv, seq, tm=IN_PROJ_ROWS)

    q_scale = (d_mlstm // N_HEADS) ** -0.5
    inv_q_scale = round(1.0 / q_scale)
    assert inv_q_scale & (inv_q_scale - 1) == 0 and inv_q_scale * q_scale == 1.0, "head dim must be a power of 4"
    wg = jnp.concatenate([w_igate, w_fgate], axis=1)
    wg = jnp.concatenate([wg[:d_mlstm] * float(inv_q_scale), wg[d_mlstm:]], axis=0)
    wg = jnp.pad(wg, ((0, 0), (0, LANES - wg.shape[1]))).astype(BF16)
    bg = jnp.pad(jnp.concatenate([b_igate, b_fgate]), (0, LANES - 2 * N_HEADS)).reshape(1, LANES)
    ym = _mlstm(xm, xc, sz, _block_diag_tiles(w_q * q_scale), _block_diag_tiles(w_k), _block_diag_tiles(w_v),
                wg, bg, r2(mlstm_norm_w), r2(mlstm_skip), bsz, seq)
    yc = _conv_group(u, conv_dw_w, r2(conv_dw_b), r2(conv_norm_w), r2(conv_norm_b), bsz, seq, tile=CONV_ROWS)

    h1, a2s, idx_t, pos_t, gates_t, cnt = _out_proj(
        xf, ym, yc, w_out.astype(BF16), r2(norm_ffn_w), w_router.T.astype(BF16), b_router.reshape(-1, 1), d_mlstm,
        tm=OUT_PROJ_ROWS)

    counts = cnt[:, 0].astype(jnp.int32)
    ends = jnp.cumsum(counts)
    starts = ends - counts
    e_ids = jnp.arange(N_EXPERTS, dtype=jnp.int32)
    idx4 = idx_t[:TOP_K]
    dest = pos_t[:TOP_K] + jnp.sum(
        jnp.where(idx4[None] == e_ids[:, None, None], starts[:, None, None], 0), axis=0)

    n_rows = n * TOP_K
    rb = EXPERT_ROWS
    n_blocks = n_rows // rb
    n_items = n_blocks + N_EXPERTS - 1
    first_blk = starts // rb
    last_blk = jnp.where(counts > 0, (ends - 1) // rb, first_blk - 1)
    per_e = last_blk - first_blk + 1
    item_end = jnp.cumsum(per_e)
    item_start = item_end - per_e
    ids = jnp.arange(n_items, dtype=jnp.int32)
    total = item_end[-1]
    item_valid = (ids < total).astype(jnp.int32)
    item_exp = jnp.minimum(jnp.sum((ids[:, None] >= item_end[None, :]).astype(jnp.int32), axis=1), N_EXPERTS - 1)
    last_valid_exp = jnp.max(jnp.where(per_e > 0, e_ids, 0))
    item_exp = jnp.where(item_valid == 1, item_exp, last_valid_exp).astype(jnp.int32)
    is_exp = item_exp[:, None] == e_ids[None, :]
    of_item = lambda table: jnp.sum(jnp.where(is_exp, table[None, :], 0), axis=1)
    item_blk = jnp.where(item_valid == 1, of_item(first_blk) + ids - of_item(item_start), n_blocks - 1)
    item_blk = item_blk.astype(jnp.int32)
    item_lo = (jnp.maximum(of_item(starts), item_blk * rb) - item_blk * rb).astype(jnp.int32)
    item_hi = (jnp.minimum(of_item(ends), (item_blk + 1) * rb) - item_blk * rb).astype(jnp.int32)

    xs = _sc_scatter_rows(a2s.reshape(n, n_slabs, LANES), dest.reshape(-1), TOP_K)
    ys = _experts(xs.reshape(n_rows * n_slabs, LANES), w1, b1[:, None, :], w2, b2[:, None, :],
                  item_blk, item_exp, item_valid, item_lo, item_hi, rb, n_slabs)
    ys = ys.reshape(n_rows, n_slabs, LANES)

    n_chunks = FINAL_CHUNKS
    nc = n // n_chunks
    out = h1
    for c in range(n_chunks):
        yg = _sc_gather_rows(ys, dest[:, c * nc:(c + 1) * nc].reshape(-1))
        out = _final(out, gates_t, yg.reshape(TOP_K, nc * n_slabs, LANES), r2(final_norm_w), n_slabs,
                     tm=FINAL_ROWS, chunk=c, n_chunks=n_chunks)
    return out


def kernel(x, norm_mix_w, w_in, mlstm_conv_w, mlstm_conv_b, w_q, w_k, w_v, w_igate, b_igate, w_fgate, b_fgate,
           mlstm_norm_w, mlstm_skip, conv_dw_w, conv_dw_b, conv_norm_w, conv_norm_b, w_out, norm_ffn_w,
           w_router, b_router, w1, b1, w2, b2, final_norm_w):
    bsz, seq, d = x.shape
    assert norm_mix_w.shape[0] == 1, "single-layer block"
    out = _layer(x.reshape(bsz * seq, d), bsz, seq, norm_mix_w[0], w_in[0], mlstm_conv_w[0], mlstm_conv_b[0],
                 w_q[0], w_k[0], w_v[0], w_igate[0], b_igate[0], w_fgate[0], b_fgate[0], mlstm_norm_w[0],
                 mlstm_skip[0], conv_dw_w[0], conv_dw_b[0], conv_norm_w[0], conv_norm_b[0], w_out[0],
                 norm_ffn_w[0], w_router[0], b_router[0], w1[0], b1[0], w2[0], b2[0], final_norm_w)
    return out.reshape(bsz, seq, d)
```

```python
import functools

import jax
import jax.numpy as jnp
from jax import lax
from jax.experimental import pallas as pl
from jax.experimental.pallas import tpu as pltpu
from jax.experimental.pallas import tpu_sc as plsc

F32 = jnp.float32
BF16 = jnp.bfloat16

EPS = 1e-5
N_HEADS = 4
MLSTM_CONV_WIDTH = 4
CONV_WIDTH = 31
N_EXPERTS = 32
TOP_K = 4
SWIGLU_ALPHA = 1.702
SWIGLU_LIMIT = 7.0

LANES = 128
SUBLANES = 8
MXU_DIM = 256
VMEM_LIMIT = 52 * 1024 * 1024
EXPERTS_VMEM_LIMIT = 58 * 1024 * 1024

IN_PROJ_ROWS = 512
CONV_ROWS = 512
CONV_ACC_ROWS = 64
OUT_PROJ_ROWS = 512
EXPERT_ROWS = 1024
FINAL_ROWS = 256
FINAL_CHUNKS = 8
MOE_PARTS = 2

MLSTM_CHUNK = 256
CONV_HALO = 32
NEG_INF = float("-inf")
LOG2_E = 1.4426950408889634


def _sigmoid(x):
    return 0.5 * jnp.tanh(0.5 * x) + 0.5


def _silu(x):
    return x * _sigmoid(x)


def _pack_bf16_pairs(v):
    half = v.shape[1] // 2
    hi = lax.bitcast_convert_type(v[:, :half].astype(BF16).astype(F32), jnp.uint32)
    lo = lax.bitcast_convert_type(v[:, half:].astype(BF16).astype(F32), jnp.uint32)
    return hi | (lo >> 16)


def _unpack_bf16_pairs(w):
    hi = lax.bitcast_convert_type(w & jnp.uint32(0xFFFF0000), F32)
    lo = lax.bitcast_convert_type(w << 16, F32)
    return hi, lo


def _load_row_slabs(ref, first, rows, n_slabs, lead=()):
    return jnp.concatenate([ref[lead + (pl.ds(first * n_slabs + s, rows, stride=n_slabs), slice(None))]
                            for s in range(n_slabs)], axis=-1)


def _store_row_slabs(ref, first, v, n_slabs):
    rows = v.shape[0]
    for s in range(n_slabs):
        ref[pl.ds(first * n_slabs + s, rows, stride=n_slabs), :] = v[:, s * LANES:(s + 1) * LANES]


def _cparams(*sem):
    return pltpu.CompilerParams(dimension_semantics=sem, vmem_limit_bytes=VMEM_LIMIT)


def _inproj_kernel(x_ref, nw_ref, w_ref, cw_ref, cb_ref, xm_ref, xc_ref, sz_ref, u_ref, xbuf,
                   *, d_mlstm, d_conv, tiles_per_seq):
    tm = x_ref.shape[0]
    hist = SUBLANES
    i = pl.program_id(0)

    @pl.when(i == 0)
    def _():
        xbuf[...] = jnp.zeros_like(xbuf)

    x = x_ref[...]
    a = x * lax.rsqrt(jnp.mean(x * x, axis=-1, keepdims=True) + EPS) * nw_ref[...]
    ab = a.astype(BF16)
    same_seq = i % tiles_per_seq != 0
    cols = MXU_DIM

    for c in range(d_mlstm // cols):
        sl = slice(c * cols, (c + 1) * cols)
        prev = xbuf[hist:hist + tm, sl]
        acc = cb_ref[:, sl] + cw_ref[MLSTM_CONV_WIDTH - 1:MLSTM_CONV_WIDTH, sl] * prev
        for k in range(MLSTM_CONV_WIDTH - 1):
            off = hist - (MLSTM_CONV_WIDTH - 1) + k
            acc = acc + cw_ref[k:k + 1, sl] * xbuf[off:off + tm, sl]
        xc_ref[:, sl] = _silu(acc).astype(xc_ref.dtype)
        xbuf[0:hist, sl] = jnp.where(same_seq, prev[tm - hist:, :], 0.0)

        xm = jnp.dot(ab, w_ref[:, sl], preferred_element_type=F32)
        xm_ref[:, sl] = xm.astype(xm_ref.dtype)
        xbuf[hist:hist + tm, sl] = xm
        z = jnp.dot(ab, w_ref[:, d_mlstm + c * cols:d_mlstm + (c + 1) * cols], preferred_element_type=F32)
        sz_ref[:, sl] = _silu(z)

    for c in range(d_conv // cols):
        sl = slice(c * cols, (c + 1) * cols)
        ga = jnp.dot(ab, w_ref[:, 2 * d_mlstm + c * cols:2 * d_mlstm + (c + 1) * cols], preferred_element_type=F32)
        gb = jnp.dot(ab, w_ref[:, 2 * d_mlstm + d_conv + c * cols:2 * d_mlstm + d_conv + (c + 1) * cols],
                     preferred_element_type=F32)
        u_ref[:, sl] = ga * _sigmoid(gb)


def _in_proj(xf, norm_w, w_in_b, conv_w, conv_b, d_mlstm, d_conv, seq, tm):
    n, d = xf.shape
    n_tiles = n // tm
    cur = lambda i: (jnp.minimum(i, n_tiles - 1), 0)
    prv = lambda i: (jnp.maximum(i - 1, 0), 0)
    const = lambda i: (0, 0)
    return pl.pallas_call(
        functools.partial(_inproj_kernel, d_mlstm=d_mlstm, d_conv=d_conv, tiles_per_seq=seq // tm),
        grid=(n_tiles + 1,),
        in_specs=[pl.BlockSpec((tm, d), cur),
                  pl.BlockSpec((1, d), const),
                  pl.BlockSpec(w_in_b.shape, const),
                  pl.BlockSpec(conv_w.shape, const),
                  pl.BlockSpec(conv_b.shape, const)],
        out_specs=[pl.BlockSpec((tm, d_mlstm), cur),
                   pl.BlockSpec((tm, d_mlstm), prv),
                   pl.BlockSpec((tm, d_mlstm), cur),
                   pl.BlockSpec((tm, d_conv), cur)],
        out_shape=[jax.ShapeDtypeStruct((n, d_mlstm), BF16),
                   jax.ShapeDtypeStruct((n, d_mlstm), BF16),
                   jax.ShapeDtypeStruct((n, d_mlstm), F32),
                   jax.ShapeDtypeStruct((n, d_conv), F32)],
        scratch_shapes=[pltpu.VMEM((tm + SUBLANES, d_mlstm), F32)],
        compiler_params=_cparams("arbitrary"),
        name="in_proj",
    )(xf, norm_w, w_in_b, conv_w, conv_b)


def _split3(v):
    hi = v.astype(BF16)
    r1 = v - hi.astype(F32)
    mid = r1.astype(BF16)
    lo = (r1 - mid.astype(F32)).astype(BF16)
    return hi, mid, lo


def _mlstm_kernel(xm_ref, xc_ref, sz_ref, wq_ref, wk_ref, wv_ref, wg_ref, bg_ref, nw_ref, sk_ref,
                  ym_ref, c_sc, n_sc, m_sc, *, chunk, dh, n_seq):
    L = chunk
    nh = N_HEADS
    j = pl.program_id(1)

    @pl.when(j == 0)
    def _():
        c_sc[...] = jnp.zeros_like(c_sc)
        n_sc[...] = jnp.zeros_like(n_sc)
        m_sc[...] = jnp.zeros_like(m_sc)

    ri = lax.broadcasted_iota(jnp.int32, (L, L), 0)
    ci = lax.broadcasted_iota(jnp.int32, (L, L), 1)
    causal = ci <= ri
    tri = jnp.where(causal, 1.0, 0.0).astype(BF16)

    for sq in range(n_seq):
        xmb = xm_ref[sq]
        xcb = xc_ref[sq]
        d = xmb.shape[1]
        nb = d // MXU_DIM

        def bd(xb, w_ref):
            return jnp.concatenate(
                [jnp.dot(xb[:, g * MXU_DIM:(g + 1) * MXU_DIM], w_ref[g], preferred_element_type=F32)
                 for g in range(nb)], axis=-1)

        q = bd(xcb, wq_ref)
        k_ = bd(xcb, wk_ref)
        v = bd(xmb, wv_ref)
        qb, kb, vb = q.astype(BF16), k_.astype(BF16), v.astype(BF16)

        g = (jnp.dot(qb, wg_ref[0:d, :], preferred_element_type=F32)
             + jnp.dot(kb, wg_ref[d:2 * d, :], preferred_element_type=F32)
             + jnp.dot(vb, wg_ref[2 * d:3 * d, :], preferred_element_type=F32)
             + bg_ref[...])
        col = lax.broadcasted_iota(jnp.int32, g.shape, 1)
        log_f = jnp.minimum(g, 0.0) - jnp.log(1.0 + jnp.exp(-jnp.abs(g)))
        gates = jnp.where(col < nh, g, jnp.where(col < 2 * nh, log_f, 0.0))
        cum = sum(jnp.dot(tri, part, preferred_element_type=F32) for part in _split3(gates))
        colform = jnp.where(col < nh, gates, cum) * LOG2_E
        rowform = colform.T

        for h in range(nh):
            sl = slice(h * dh, (h + 1) * dh)
            b_row = rowform[nh + h:nh + h + 1, :]
            r_row = rowform[h:h + 1, :] - b_row
            li_col = colform[:, h:h + 1]
            b_col = colform[:, nh + h:nh + h + 1]
            m_prev = m_sc[sq, h:h + 1, 0:1]
            g_tot = b_row[:, L - 1:L]

            dmat = jnp.where(causal, b_col + r_row, NEG_INF)
            inter = b_col + m_prev
            m_i = jnp.maximum(inter, jnp.max(dmat, axis=-1, keepdims=True))
            w_intra = jnp.exp2(dmat - m_i)
            w_inter = jnp.exp2(inter - m_i)

            qh = qb[:, sl]
            nt = (((1,), (1,)), ((), ()))
            s = lax.dot_general(qh, kb[:, sl], nt, preferred_element_type=F32) * w_intra
            c_prev = c_sc[sq, h]
            n_prev = n_sc[sq, h:h + 1, :]
            num = w_inter * jnp.dot(qh, c_prev.astype(BF16), preferred_element_type=F32) \
                + jnp.dot(s.astype(BF16), vb[:, sl], preferred_element_type=F32)
            n_rep = jnp.broadcast_to(n_prev, (LANES, dh)).astype(BF16)
            qn = lax.dot_general(qh, n_rep, nt, preferred_element_type=F32)[:, 0:1]
            den = w_inter * qn + jnp.sum(s, axis=-1, keepdims=True)
            hh = num * (1.0 / jnp.maximum(jnp.abs(den), jnp.exp2(-m_i)))

            mu = jnp.mean(hh, axis=-1, keepdims=True)
            dev = hh - mu
            var = jnp.mean(dev * dev, axis=-1, keepdims=True)
            hn = dev * lax.rsqrt(var + EPS) * nw_ref[:, sl]
            ym_ref[sq, :, sl] = ((hn + sk_ref[:, sl] * xcb[:, sl].astype(F32)) * sz_ref[sq, :, sl]).astype(ym_ref.dtype)

            a_row = g_tot + r_row
            m_new = jnp.maximum(g_tot + m_prev, jnp.max(a_row, axis=-1, keepdims=True))
            a_col = g_tot - b_col + li_col
            w_state = jnp.exp2(a_col - m_new)
            decay = jnp.exp2(g_tot + m_prev - m_new)
            kw = k_[:, sl] * w_state
            c_sc[sq, h] = decay * c_prev + lax.dot_general(kw.astype(BF16), vb[:, sl], (((0,), (0,)), ((), ())),
                                                           preferred_element_type=F32)
            w_state_rows = jnp.broadcast_to(jnp.exp2(a_row - m_new), (SUBLANES, L)).astype(BF16)
            n_sc[sq, h:h + 1, :] = decay * n_prev + jnp.dot(w_state_rows, kb[:, sl],
                                                            preferred_element_type=F32)[0:1, :]
            m_sc[sq, h:h + 1, :] = jnp.broadcast_to(m_new, (1, m_sc.shape[2]))


def _mlstm(xm, xc, sz, wq, wk, wv, wg, bg, norm_w, skip, bsz, seq):
    n, d = xm.shape
    L = MLSTM_CHUNK
    nc = seq // L
    dh = d // N_HEADS
    n_seq = 2 if bsz % 2 == 0 else 1
    blk = lambda b, j: (b, j, 0)
    c2 = lambda b, j: (0, 0)
    c3 = lambda b, j: (0, 0, 0)
    ym = pl.pallas_call(
        functools.partial(_mlstm_kernel, chunk=L, dh=dh, n_seq=n_seq),
        grid=(bsz // n_seq, nc),
        in_specs=[pl.BlockSpec((n_seq, L, d), blk), pl.BlockSpec((n_seq, L, d), blk), pl.BlockSpec((n_seq, L, d), blk),
                  pl.BlockSpec(wq.shape, c3), pl.BlockSpec(wk.shape, c3), pl.BlockSpec(wv.shape, c3),
                  pl.BlockSpec(wg.shape, c2), pl.BlockSpec(bg.shape, c2),
                  pl.BlockSpec(norm_w.shape, c2), pl.BlockSpec(skip.shape, c2)],
        out_specs=pl.BlockSpec((n_seq, L, d), blk),
        out_shape=jax.ShapeDtypeStruct((bsz, seq, d), BF16),
        scratch_shapes=[pltpu.VMEM((n_seq, N_HEADS, dh, dh), F32),
                        pltpu.VMEM((n_seq, SUBLANES, dh), F32),
                        pltpu.VMEM((n_seq, SUBLANES, LANES), F32)],
        compiler_params=_cparams("arbitrary", "arbitrary"),
        name="mlstm",
    )(xm.reshape(bsz, seq, d), xc.reshape(bsz, seq, d), sz.reshape(bsz, seq, d), wq, wk, wv, wg, bg, norm_w, skip)
    return ym.reshape(n, d)


def _conv_kernel(u_ref, w_ref, b_ref, nw_ref, nb_ref, yc_ref, ubuf, pbuf, cbuf, *, tile, rows):
    T = tile
    j = pl.program_id(1)

    @pl.when(j == 0)
    def _():
        ubuf[0:CONV_HALO, :] = jnp.zeros((CONV_HALO, ubuf.shape[1]), F32)

    ubuf[CONV_HALO:CONV_HALO + T, :] = u_ref[...]
    base = CONV_HALO - (CONV_WIDTH - 1)
    span = T + CONV_HALO - SUBLANES
    n_lane_blocks = ubuf.shape[1] // LANES

    def lane_block(c, carry):
        lanes = pl.ds(pl.multiple_of(c * LANES, LANES), LANES)
        for r in range(1, SUBLANES):
            pbuf[r - 1, :, :] = ubuf[r:r + span, lanes]
        for r0 in range(0, T, rows):
            acc = jnp.broadcast_to(b_ref[:, lanes], (rows, LANES))
            for k in range(CONV_WIDTH):
                q, r = divmod(base + k, SUBLANES)
                lo = r0 + q * SUBLANES
                src = ubuf[lo:lo + rows, lanes] if r == 0 else pbuf[r - 1, lo:lo + rows, :]
                acc = acc + w_ref[k:k + 1, lanes] * src
            cbuf[r0:r0 + rows, lanes] = acc
        return carry

    lax.fori_loop(0, n_lane_blocks, lane_block, 0)
    ubuf[0:CONV_HALO, :] = ubuf[T:T + CONV_HALO, :]

    y = cbuf[...]
    mu = jnp.mean(y, axis=-1, keepdims=True)
    dev = y - mu
    var = jnp.mean(dev * dev, axis=-1, keepdims=True)
    yn = dev * lax.rsqrt(var + EPS) * nw_ref[...] + nb_ref[...]
    yc_ref[...] = _silu(yn).astype(yc_ref.dtype)


def _conv_group(u, w, b, norm_w, norm_b, bsz, seq, tile):
    n, d = u.shape
    nt = seq // tile
    row = lambda bi, j: (bi * nt + j, 0)
    c2 = lambda bi, j: (0, 0)
    return pl.pallas_call(
        functools.partial(_conv_kernel, tile=tile, rows=CONV_ACC_ROWS),
        grid=(bsz, nt),
        in_specs=[pl.BlockSpec((tile, d), row), pl.BlockSpec(w.shape, c2), pl.BlockSpec(b.shape, c2),
                  pl.BlockSpec(norm_w.shape, c2), pl.BlockSpec(norm_b.shape, c2)],
        out_specs=pl.BlockSpec((tile, d), row),
        out_shape=jax.ShapeDtypeStruct((n, d), BF16),
        scratch_shapes=[pltpu.VMEM((tile + CONV_HALO, d), F32),
                        pltpu.VMEM((SUBLANES - 1, tile + CONV_HALO - SUBLANES, LANES), F32),
                        pltpu.VMEM((tile, d), F32)],
        compiler_params=_cparams("arbitrary", "arbitrary"),
        name="conv_group",
    )(u, w, b, norm_w, norm_b)


def _outproj_kernel(x_ref, ym_ref, yc_ref, wo_ref, nw_ref, wr_ref, br_ref,
                    h1_ref, a2_ref, idx_ref, pos_ref, gate_ref, cnt_ref, cnt_sc, *, tm, d_mlstm):
    i = pl.program_id(0)

    @pl.when(i == 0)
    def _():
        cnt_sc[...] = jnp.zeros_like(cnt_sc)

    h1 = (x_ref[...]
          + jnp.dot(ym_ref[...], wo_ref[0:d_mlstm, :], preferred_element_type=F32)
          + jnp.dot(yc_ref[...], wo_ref[d_mlstm:, :], preferred_element_type=F32))
    h1_ref[...] = h1
    a2 = h1 * lax.rsqrt(jnp.mean(h1 * h1, axis=-1, keepdims=True) + EPS) * nw_ref[...]
    _store_row_slabs(a2_ref, 0, _pack_bf16_pairs(a2), a2.shape[1] // (2 * LANES))

    logits = lax.dot_general(wr_ref[...], a2.astype(BF16), (((1,), (1,)), ((), ())),
                             preferred_element_type=F32) + br_ref[...]
    e_iota = lax.broadcasted_iota(jnp.int32, logits.shape, 0)
    work = logits
    vals, idxs = [], []
    for _ in range(TOP_K):
        mx = jnp.max(work, axis=0, keepdims=True)
        sel = jnp.min(jnp.where(work == mx, e_iota, N_EXPERTS), axis=0, keepdims=True)
        vals.append(mx)
        idxs.append(sel)
        work = jnp.where(e_iota == sel, NEG_INF, work)
    exps = [jnp.exp(vv - vals[0]) for vv in vals]
    tot = exps[0] + exps[1] + exps[2] + exps[3]
    gates = [ev / tot for ev in exps]

    chosen = functools.reduce(jnp.logical_or, [e_iota == sel for sel in idxs])
    mh = jnp.where(chosen, 1.0, 0.0)
    ri = lax.broadcasted_iota(jnp.int32, (tm, tm), 0)
    ci = lax.broadcasted_iota(jnp.int32, (tm, tm), 1)
    upper = jnp.where(ri < ci, 1.0, 0.0).astype(BF16)
    rank = jnp.dot(mh.astype(BF16), upper, preferred_element_type=F32) + cnt_sc[:, 0:1]
    cnt_new = cnt_sc[...] + jnp.sum(mh, axis=1, keepdims=True)
    cnt_sc[...] = cnt_new
    cnt_ref[...] = cnt_new

    zero_i = jnp.zeros((SUBLANES - TOP_K, tm), jnp.int32)
    pos = [jnp.sum(jnp.where(e_iota == sel, rank, 0.0), axis=0, keepdims=True).astype(jnp.int32) for sel in idxs]
    idx_ref[...] = jnp.concatenate(idxs + [zero_i], axis=0)
    pos_ref[...] = jnp.concatenate(pos + [zero_i], axis=0)
    gate_ref[...] = jnp.concatenate(gates + [jnp.zeros((SUBLANES - TOP_K, tm), F32)], axis=0)


def _out_proj(xf, ym, yc, w_out_b, norm_w, w_router_t, b_router, d_mlstm, tm, part, n_parts, h_prev=None):
    n, d = xf.shape
    n_part = n // n_parts
    tiles = n_part // tm
    first = part * tiles
    n_slabs = d // (2 * LANES)
    row = lambda i: (first + i, 0)
    loc = lambda i: (i, 0)
    colb = lambda i: (0, i)
    const = lambda i: (0, 0)
    body = functools.partial(_outproj_kernel, tm=tm, d_mlstm=d_mlstm)
    in_specs = [pl.BlockSpec((tm, d), row), pl.BlockSpec((tm, ym.shape[1]), row), pl.BlockSpec((tm, yc.shape[1]), row),
                pl.BlockSpec(w_out_b.shape, const), pl.BlockSpec(norm_w.shape, const),
                pl.BlockSpec(w_router_t.shape, const), pl.BlockSpec(b_router.shape, const)]
    args = (xf, ym, yc, w_out_b, norm_w, w_router_t, b_router)
    aliases = {}
    if h_prev is not None:
        in_specs = [pl.BlockSpec(memory_space=pl.ANY)] + in_specs
        args = (h_prev,) + args
        aliases = {0: 0}
        inner = body
        body = lambda h_prev_ref, *refs: inner(*refs)
    return pl.pallas_call(
        body,
        grid=(tiles,),
        in_specs=in_specs,
        out_specs=[pl.BlockSpec((tm, d), row),
                   pl.BlockSpec((tm * n_slabs, LANES), loc),
                   pl.BlockSpec((SUBLANES, tm), colb),
                   pl.BlockSpec((SUBLANES, tm), colb),
                   pl.BlockSpec((SUBLANES, tm), colb),
                   pl.BlockSpec((N_EXPERTS, LANES), const)],
        out_shape=[jax.ShapeDtypeStruct((n, d), F32),
                   jax.ShapeDtypeStruct((n_part * n_slabs, LANES), jnp.uint32),
                   jax.ShapeDtypeStruct((SUBLANES, n_part), jnp.int32),
                   jax.ShapeDtypeStruct((SUBLANES, n_part), jnp.int32),
                   jax.ShapeDtypeStruct((SUBLANES, n_part), F32),
                   jax.ShapeDtypeStruct((N_EXPERTS, LANES), F32)],
        scratch_shapes=[pltpu.VMEM((N_EXPERTS, LANES), F32)],
        input_output_aliases=aliases,
        compiler_params=_cparams("arbitrary"),
        name="out_proj_router",
    )(*args)


def _experts_kernel(blk_ref, exp_ref, valid_ref, lo_ref, hi_ref,
                    xs_ref, w1_ref, b1_ref, w2_ref, b2_ref, ys_ref, w1c_ref, w2c_ref, acc_ref, *, rb, d_ff, n_sub):
    i = pl.program_id(0)
    n_slabs = xs_ref.shape[0] // rb
    prev = jnp.maximum(i - 1, 0)
    first_visit = jnp.logical_or(i == 0, blk_ref[prev] != blk_ref[i])
    new_expert = jnp.logical_or(i == 0, exp_ref[prev] != exp_ref[i])

    @pl.when(new_expert)
    def _():
        w1c_ref[...] = w1_ref[0].astype(BF16)
        w2c_ref[...] = w2_ref[0].astype(BF16)

    @pl.when(first_visit)
    def _():
        acc_ref[...] = jnp.zeros_like(acc_ref)

    rs = rb // n_sub
    lo, hi = lo_ref[i], hi_ref[i]

    def sub_block(c, owns_all_rows):
        xa, xb = _unpack_bf16_pairs(_load_row_slabs(xs_ref, c * rs, rs, n_slabs))
        x = jnp.concatenate([xa.astype(BF16), xb.astype(BF16)], axis=-1)
        hid = jnp.dot(x, w1c_ref[...], preferred_element_type=F32) + b1_ref[0]
        x_glu = jnp.minimum(hid[:, :d_ff], SWIGLU_LIMIT)
        x_lin = jnp.clip(hid[:, d_ff:], -SWIGLU_LIMIT, SWIGLU_LIMIT)
        act = x_glu * _sigmoid(SWIGLU_ALPHA * x_glu) * (x_lin + 1.0)
        y = jnp.dot(act.astype(BF16), w2c_ref[...], preferred_element_type=F32) + b2_ref[0]
        if not owns_all_rows:
            r = c * rs + lax.broadcasted_iota(jnp.int32, (rs, 1), 0)
            y = jnp.where(jnp.logical_and(r >= lo, r < hi), y, 0.0)
            y = acc_ref[c * rs:(c + 1) * rs, :] + y
            acc_ref[c * rs:(c + 1) * rs, :] = y
        _store_row_slabs(ys_ref, c * rs, _pack_bf16_pairs(y), n_slabs)

    for c in range(n_sub):
        valid = valid_ref[i] == 1
        owns_all = jnp.logical_and(lo <= c * rs, hi >= (c + 1) * rs)
        has_rows = jnp.logical_and(lo < (c + 1) * rs, hi > c * rs)
        pl.when(jnp.logical_and(valid, owns_all))(functools.partial(sub_block, c, True))
        pl.when(jnp.logical_and(valid, jnp.logical_and(has_rows, jnp.logical_not(owns_all))))(
            functools.partial(sub_block, c, False))


def _experts(xs, w1, b1, w2, b2, item_blk, item_exp, item_valid, item_lo, item_hi, rb, n_slabs):
    n_items = item_blk.shape[0]
    d = w1.shape[1]
    d_ff = w2.shape[1]
    by_blk = lambda i, blk, ex, va, lo, hi: (blk[i], 0)
    by_exp = lambda i, blk, ex, va, lo, hi: (ex[i], 0, 0)
    grid_spec = pltpu.PrefetchScalarGridSpec(
        num_scalar_prefetch=5,
        grid=(n_items,),
        in_specs=[pl.BlockSpec((rb * n_slabs, LANES), by_blk),
                  pl.BlockSpec((1,) + w1.shape[1:], by_exp),
                  pl.BlockSpec((1,) + b1.shape[1:], by_exp),
                  pl.BlockSpec((1,) + w2.shape[1:], by_exp),
                  pl.BlockSpec((1,) + b2.shape[1:], by_exp)],
        out_specs=pl.BlockSpec((rb * n_slabs, LANES), by_blk),
        scratch_shapes=[pltpu.VMEM(w1.shape[1:], BF16), pltpu.VMEM(w2.shape[1:], BF16), pltpu.VMEM((rb, d), F32)],
    )
    return pl.pallas_call(
        functools.partial(_experts_kernel, rb=rb, d_ff=d_ff, n_sub=rb // MXU_DIM),
        grid_spec=grid_spec,
        out_shape=jax.ShapeDtypeStruct(xs.shape, jnp.uint32),
        compiler_params=pltpu.CompilerParams(dimension_semantics=("arbitrary",), vmem_limit_bytes=EXPERTS_VMEM_LIMIT),
        name="experts",
    )(item_blk, item_exp, item_valid, item_lo, item_hi, xs, w1, b1, w2, b2)


SC_CORES = 2
SC_SUBCORES = 16
SC_GATHER_WINDOW = 64


def _sc_worker_base(per_worker):
    wid = lax.axis_index("s") * SC_CORES + lax.axis_index("c")
    return wid * per_worker


def _sc_gather_rows(table, idx):
    m = idx.shape[0]
    n_workers = SC_CORES * SC_SUBCORES
    window = SC_GATHER_WINDOW
    per_worker = m // n_workers
    n_win = per_worker // window
    assert per_worker * n_workers == m and n_win * window == per_worker and n_win % 2 == 0
    mesh = plsc.VectorSubcoreMesh(core_axis_name="c", subcore_axis_name="s")
    slab = table.shape[1:]

    @functools.partial(
        pl.kernel, mesh=mesh,
        out_type=jax.ShapeDtypeStruct((m,) + slab, table.dtype),
        scratch_types=[pltpu.VMEM((per_worker,), jnp.int32),
                       pltpu.VMEM((window,) + slab, table.dtype), pltpu.VMEM((window,) + slab, table.dtype)]
        + [pltpu.SemaphoreType.DMA] * 4,
        name="sc_gather_rows",
    )
    def gather(table_hbm, idx_hbm, out_hbm, idx_v, rows0, rows1, gsem0, gsem1, wsem0, wsem1):
        rows_v, gsem, wsem = (rows0, rows1), (gsem0, gsem1), (wsem0, wsem1)
        base = _sc_worker_base(per_worker)
        pltpu.sync_copy(idx_hbm.at[pl.ds(pl.multiple_of(base, window), per_worker)], idx_v)

        def gather_copy(w, b):
            ids = idx_v.at[pl.ds(pl.multiple_of(w * window, window), window)]
            return pltpu.make_async_copy(table_hbm.at[ids], rows_v[b], gsem[b])

        def write_copy(w, b):
            dst = out_hbm.at[pl.ds(pl.multiple_of(base + w * window, window), window)]
            return pltpu.make_async_copy(rows_v[b], dst, wsem[b])

        gather_copy(0, 0).start()

        @pl.loop(0, n_win, step=2)
        def _(w):
            @pl.when(w >= 1)
            def _():
                write_copy(w - 1, 1).wait()

            gather_copy(w + 1, 1).start()
            gather_copy(w, 0).wait()
            write_copy(w, 0).start()

            @pl.when(w + 2 < n_win)
            def _():
                write_copy(w, 0).wait()
                gather_copy(w + 2, 0).start()

            gather_copy(w + 1, 1).wait()
            write_copy(w + 1, 1).start()

        write_copy(n_win - 2, 0).wait()
        write_copy(n_win - 1, 1).wait()

    return gather(table, idx)


def _sc_scatter_rows(rows, dest, n_slots):
    n = rows.shape[0]
    n_workers = SC_CORES * SC_SUBCORES
    window = SC_GATHER_WINDOW
    per_worker = n // n_workers
    n_win = per_worker // window
    assert per_worker * n_workers == n and n_win * window == per_worker and n_win % 2 == 0
    mesh = plsc.VectorSubcoreMesh(core_axis_name="c", subcore_axis_name="s")
    slab = rows.shape[1:]

    @functools.partial(
        pl.kernel, mesh=mesh,
        out_type=jax.ShapeDtypeStruct((n_slots * n,) + slab, rows.dtype),
        scratch_types=[pltpu.VMEM((window,), jnp.int32)] * (2 * n_slots)
        + [pltpu.VMEM((window,) + slab, rows.dtype)] * 2 + [pltpu.SemaphoreType.DMA] * 4,
        name="sc_scatter_rows",
    )
    def scatter(rows_hbm, dest_hbm, out_hbm, *scratch):
        idx_v = (scratch[:n_slots], scratch[n_slots:2 * n_slots])
        rows_v = scratch[2 * n_slots:2 * n_slots + 2]
        rsem = scratch[2 * n_slots + 2:2 * n_slots + 4]
        wsem = scratch[2 * n_slots + 4:2 * n_slots + 6]
        base = _sc_worker_base(per_worker)

        def rows_at(w, k=0):
            return pl.ds(pl.multiple_of(k * n + base + w * window, window), window)

        def start_read(w, b):
            for k in range(n_slots):
                pltpu.sync_copy(dest_hbm.at[rows_at(w, k)], idx_v[b][k])
            pltpu.async_copy(rows_hbm.at[rows_at(w)], rows_v[b], rsem[b])

        def scatter_window(w, b):
            pltpu.make_async_copy(rows_hbm.at[rows_at(w)], rows_v[b], rsem[b]).wait()
            for k in range(n_slots):
                pltpu.async_copy(rows_v[b], out_hbm.at[idx_v[b][k]], wsem[b])
            for k in range(n_slots):
                pltpu.make_async_copy(rows_v[b], out_hbm.at[idx_v[b][k]], wsem[b]).wait()

        start_read(0, 0)

        @pl.loop(0, n_win, step=2)
        def _(w):
            start_read(w + 1, 1)
            scatter_window(w, 0)

            @pl.when(w + 2 < n_win)
            def _():
                start_read(w + 2, 0)

            scatter_window(w + 1, 1)

    return scatter(rows, dest)


def _final_kernel(h1_ref, gate_ref, yg_ref, nw_ref, out_ref, *, tm, n_slabs):
    gpad = jnp.concatenate([gate_ref[...], jnp.zeros((LANES - SUBLANES, tm), F32)], axis=0)
    gcol = gpad.T
    h2 = h1_ref[...]
    for k in range(TOP_K):
        ya, yb = _unpack_bf16_pairs(_load_row_slabs(yg_ref, 0, tm, n_slabs, lead=(k,)))
        h2 = h2 + gcol[:, k:k + 1] * jnp.concatenate([ya, yb], axis=-1)
    out_ref[...] = h2 * lax.rsqrt(jnp.mean(h2 * h2, axis=-1, keepdims=True) + EPS) * nw_ref[...]


def _final(h, gates_t, yg, final_w, n_slabs, tm, chunk, n_chunks, gate_chunk):
    n, d = h.shape
    tiles = n // tm // n_chunks
    first = chunk * tiles
    gate_first = gate_chunk * tiles
    return pl.pallas_call(
        functools.partial(_final_kernel, tm=tm, n_slabs=n_slabs),
        grid=(tiles,),
        in_specs=[pl.BlockSpec((tm, d), lambda i: (first + i, 0)),
                  pl.BlockSpec((SUBLANES, tm), lambda i: (0, gate_first + i)),
                  pl.BlockSpec((TOP_K, tm * n_slabs, LANES), lambda i: (0, i, 0)),
                  pl.BlockSpec((1, d), lambda i: (0, 0))],
        out_specs=pl.BlockSpec((tm, d), lambda i: (first + i, 0)),
        out_shape=jax.ShapeDtypeStruct((n, d), F32),
        input_output_aliases={0: 0},
        compiler_params=_cparams("parallel"),
        name="final",
    )(h, gates_t, yg, final_w)


def _block_diag_tiles(w):
    nb, bs, _ = w.shape
    rows = jnp.tile(w.reshape(nb * bs // MXU_DIM, MXU_DIM, bs), (1, 1, MXU_DIM // bs))
    r_blk = lax.broadcasted_iota(jnp.int32, (MXU_DIM, MXU_DIM), 0) // bs
    c_blk = lax.broadcasted_iota(jnp.int32, (MXU_DIM, MXU_DIM), 1) // bs
    return jnp.where(r_blk == c_blk, rows, 0.0).astype(BF16)


def _layer(xf, bsz, seq, norm_mix_w, w_in, mlstm_conv_w, mlstm_conv_b, w_q, w_k, w_v, w_igate, b_igate,
           w_fgate, b_fgate, mlstm_norm_w, mlstm_skip, conv_dw_w, conv_dw_b, conv_norm_w, conv_norm_b,
           w_out, norm_ffn_w, w_router, b_router, w1, b1, w2, b2, final_norm_w):
    n, d = xf.shape
    d_mlstm = mlstm_norm_w.shape[0]
    d_conv = conv_norm_w.shape[0]
    n_slabs = d // (2 * LANES)
    r2 = lambda v: v.reshape(1, -1)

    xm, xc, sz, u = _in_proj(xf, r2(norm_mix_w), w_in.astype(BF16), mlstm_conv_w, r2(mlstm_conv_b),
                             d_mlstm, d_conv, seq, tm=IN_PROJ_ROWS)

    q_scale = (d_mlstm // N_HEADS) ** -0.5
    inv_q_scale = round(1.0 / q_scale)
    assert inv_q_scale & (inv_q_scale - 1) == 0 and inv_q_scale * q_scale == 1.0, "head dim must be a power of 4"
    wg = jnp.concatenate([w_igate, w_fgate], axis=1)
    wg = jnp.concatenate([wg[:d_mlstm] * float(inv_q_scale), wg[d_mlstm:]], axis=0)
    wg = jnp.pad(wg, ((0, 0), (0, LANES - wg.shape[1]))).astype(BF16)
    bg = jnp.pad(jnp.concatenate([b_igate, b_fgate]), (0, LANES - 2 * N_HEADS)).reshape(1, LANES)
    ym = _mlstm(xm, xc, sz, _block_diag_tiles(w_q * q_scale), _block_diag_tiles(w_k), _block_diag_tiles(w_v),
                wg, bg, r2(mlstm_norm_w), r2(mlstm_skip), bsz, seq)
    yc = _conv_group(u, conv_dw_w, r2(conv_dw_b), r2(conv_norm_w), r2(conv_norm_b), bsz, seq, tile=CONV_ROWS)

    w_out_b, w_router_b = w_out.astype(BF16), w_router.T.astype(BF16)
    n_parts = MOE_PARTS
    n_part = n // n_parts
    e_ids = jnp.arange(N_EXPERTS, dtype=jnp.int32)
    rb = EXPERT_ROWS
    n_rows = n_part * TOP_K
    n_blocks = n_rows // rb
    n_items = n_blocks + N_EXPERTS - 1
    chunks_per_part = FINAL_CHUNKS // n_parts
    nc = n_part // chunks_per_part

    h1 = None
    routed = []
    for p in range(n_parts):
        h1, a2s, idx_t, pos_t, gates_t, cnt = _out_proj(
            xf, ym, yc, w_out_b, r2(norm_ffn_w), w_router_b, b_router.reshape(-1, 1), d_mlstm,
            tm=OUT_PROJ_ROWS, part=p, n_parts=n_parts, h_prev=h1)

        counts = cnt[:, 0].astype(jnp.int32)
        ends = jnp.cumsum(counts)
        starts = ends - counts
        idx4 = idx_t[:TOP_K]
        dest = pos_t[:TOP_K] + jnp.sum(
            jnp.where(idx4[None] == e_ids[:, None, None], starts[:, None, None], 0), axis=0)

        first_blk = starts // rb
        last_blk = jnp.where(counts > 0, (ends - 1) // rb, first_blk - 1)
        per_e = last_blk - first_blk + 1
        item_end = jnp.cumsum(per_e)
        item_start = item_end - per_e
        ids = jnp.arange(n_items, dtype=jnp.int32)
        total = item_end[-1]
        item_valid = (ids < total).astype(jnp.int32)
        item_exp = jnp.minimum(jnp.sum((ids[:, None] >= item_end[None, :]).astype(jnp.int32), axis=1), N_EXPERTS - 1)
        last_valid_exp = jnp.max(jnp.where(per_e > 0, e_ids, 0))
        item_exp = jnp.where(item_valid == 1, item_exp, last_valid_exp).astype(jnp.int32)
        is_exp = item_exp[:, None] == e_ids[None, :]
        of_item = lambda table, is_exp=is_exp: jnp.sum(jnp.where(is_exp, table[None, :], 0), axis=1)
        item_blk = jnp.where(item_valid == 1, of_item(first_blk) + ids - of_item(item_start), n_blocks - 1)
        item_blk = item_blk.astype(jnp.int32)
        item_lo = (jnp.maximum(of_item(starts), item_blk * rb) - item_blk * rb).astype(jnp.int32)
        item_hi = (jnp.minimum(of_item(ends), (item_blk + 1) * rb) - item_blk * rb).astype(jnp.int32)

        xs = _sc_scatter_rows(a2s.reshape(n_part, n_slabs, LANES), dest.reshape(-1), TOP_K)
        routed.append((xs, dest, gates_t, (item_blk, item_exp, item_valid, item_lo, item_hi)))

    out = h1
    for p, (xs, dest, gates_t, items) in enumerate(routed):
        ys = _experts(xs.reshape(n_rows * n_slabs, LANES), w1, b1[:, None, :], w2, b2[:, None, :], *items, rb, n_slabs)
        ys = ys.reshape(n_rows, n_slabs, LANES)
        for c in range(chunks_per_part):
            yg = _sc_gather_rows(ys, dest[:, c * nc:(c + 1) * nc].reshape(-1))
            out = _final(out, gates_t, yg.reshape(TOP_K, nc * n_slabs, LANES), r2(final_norm_w), n_slabs,
                         tm=FINAL_ROWS, chunk=p * chunks_per_part + c, n_chunks=FINAL_CHUNKS, gate_chunk=c)
    return out


def kernel(x, norm_mix_w, w_in, mlstm_conv_w, mlstm_conv_b, w_q, w_k, w_v, w_igate, b_igate, w_fgate, b_fgate,
           mlstm_norm_w, mlstm_skip, conv_dw_w, conv_dw_b, conv_norm_w, conv_norm_b, w_out, norm_ffn_w,
           w_router, b_router, w1, b1, w2, b2, final_norm_w):
    bsz, seq, d = x.shape
    assert norm_mix_w.shape[0] == 1, "single-layer block"
    out = _layer(x.reshape(bsz * seq, d), bsz, seq, norm_mix_w[0], w_in[0], mlstm_conv_w[0], mlstm_conv_b[0],
                 w_q[0], w_k[0], w_v[0], w_igate[0], b_igate[0], w_fgate[0], b_fgate[0], mlstm_norm_w[0],
                 mlstm_skip[0], conv_dw_w[0], conv_dw_b[0], conv_norm_w[0], conv_norm_b[0], w_out[0],
                 norm_ffn_w[0], w_router[0], b_router[0], w1[0], b1[0], w2[0], b2[0], final_norm_w)
    return out.reshape(bsz, seq, d)
```

```python
import functools

import jax
import jax.numpy as jnp
from jax import lax
from jax.experimental import pallas as pl
from jax.experimental.pallas import tpu as pltpu
from jax.experimental.pallas import tpu_sc as plsc

F32 = jnp.float32
BF16 = jnp.bfloat16

EPS = 1e-5
N_HEADS = 4
MLSTM_CONV_WIDTH = 4
CONV_WIDTH = 31
N_EXPERTS = 32
TOP_K = 4
SWIGLU_ALPHA = 1.702
SWIGLU_LIMIT = 7.0

LANES = 128
SUBLANES = 8
MXU_DIM = 256
VMEM_LIMIT = 52 * 1024 * 1024
EXPERTS_VMEM_LIMIT = 58 * 1024 * 1024

IN_PROJ_ROWS = 512
CONV_ROWS = 512
CONV_ACC_ROWS = 64
OUT_PROJ_ROWS = 512
EXPERT_ROWS = 1024
FINAL_ROWS = 256
FINAL_CHUNKS = 16

MLSTM_CHUNK = 256
CONV_HALO = 32
NEG_INF = float("-inf")
LOG2_E = 1.4426950408889634


def _sigmoid(x):
    return 0.5 * jnp.tanh(0.5 * x) + 0.5


def _silu(x):
    return x * _sigmoid(x)


def _pack_bf16_pairs(v):
    half = v.shape[1] // 2
    hi = lax.bitcast_convert_type(v[:, :half].astype(BF16).astype(F32), jnp.uint32)
    lo = lax.bitcast_convert_type(v[:, half:].astype(BF16).astype(F32), jnp.uint32)
    return hi | (lo >> 16)


def _unpack_bf16_pairs(w):
    hi = lax.bitcast_convert_type(w & jnp.uint32(0xFFFF0000), F32)
    lo = lax.bitcast_convert_type(w << 16, F32)
    return hi, lo


def _load_row_slabs(ref, first, rows, n_slabs, lead=()):
    return jnp.concatenate([ref[lead + (pl.ds(first * n_slabs + s, rows, stride=n_slabs), slice(None))]
                            for s in range(n_slabs)], axis=-1)


def _store_row_slabs(ref, first, v, n_slabs):
    rows = v.shape[0]
    for s in range(n_slabs):
        ref[pl.ds(first * n_slabs + s, rows, stride=n_slabs), :] = v[:, s * LANES:(s + 1) * LANES]


def _cparams(*sem):
    return pltpu.CompilerParams(dimension_semantics=sem, vmem_limit_bytes=VMEM_LIMIT)


def _inproj_kernel(x_ref, nw_ref, w_ref, cw_ref, cb_ref, xm_ref, xc_ref, sz_ref, u_ref, xbuf,
                   *, d_mlstm, d_conv, tiles_per_seq):
    tm = x_ref.shape[0]
    hist = SUBLANES
    i = pl.program_id(0)

    @pl.when(i == 0)
    def _():
        xbuf[...] = jnp.zeros_like(xbuf)

    x = x_ref[...]
    a = x * lax.rsqrt(jnp.mean(x * x, axis=-1, keepdims=True) + EPS) * nw_ref[...]
    ab = a.astype(BF16)
    same_seq = i % tiles_per_seq != 0
    cols = MXU_DIM

    for c in range(d_mlstm // cols):
        sl = slice(c * cols, (c + 1) * cols)
        prev = xbuf[hist:hist + tm, sl]
        acc = cb_ref[:, sl] + cw_ref[MLSTM_CONV_WIDTH - 1:MLSTM_CONV_WIDTH, sl] * prev
        for k in range(MLSTM_CONV_WIDTH - 1):
            off = hist - (MLSTM_CONV_WIDTH - 1) + k
            acc = acc + cw_ref[k:k + 1, sl] * xbuf[off:off + tm, sl]
        xc_ref[:, sl] = _silu(acc).astype(xc_ref.dtype)
        xbuf[0:hist, sl] = jnp.where(same_seq, prev[tm - hist:, :], 0.0)

        xm = jnp.dot(ab, w_ref[:, sl], preferred_element_type=F32)
        xm_ref[:, sl] = xm.astype(xm_ref.dtype)
        xbuf[hist:hist + tm, sl] = xm
        z = jnp.dot(ab, w_ref[:, d_mlstm + c * cols:d_mlstm + (c + 1) * cols], preferred_element_type=F32)
        sz_ref[:, sl] = _silu(z)

    for c in range(d_conv // cols):
        sl = slice(c * cols, (c + 1) * cols)
        ga = jnp.dot(ab, w_ref[:, 2 * d_mlstm + c * cols:2 * d_mlstm + (c + 1) * cols], preferred_element_type=F32)
        gb = jnp.dot(ab, w_ref[:, 2 * d_mlstm + d_conv + c * cols:2 * d_mlstm + d_conv + (c + 1) * cols],
                     preferred_element_type=F32)
        u_ref[:, sl] = ga * _sigmoid(gb)


def _in_proj(xf, norm_w, w_in_b, conv_w, conv_b, d_mlstm, d_conv, seq, tm):
    n, d = xf.shape
    n_tiles = n // tm
    cur = lambda i: (jnp.minimum(i, n_tiles - 1), 0)
    prv = lambda i: (jnp.maximum(i - 1, 0), 0)
    const = lambda i: (0, 0)
    return pl.pallas_call(
        functools.partial(_inproj_kernel, d_mlstm=d_mlstm, d_conv=d_conv, tiles_per_seq=seq // tm),
        grid=(n_tiles + 1,),
        in_specs=[pl.BlockSpec((tm, d), cur),
                  pl.BlockSpec((1, d), const),
                  pl.BlockSpec(w_in_b.shape, const),
                  pl.BlockSpec(conv_w.shape, const),
                  pl.BlockSpec(conv_b.shape, const)],
        out_specs=[pl.BlockSpec((tm, d_mlstm), cur),
                   pl.BlockSpec((tm, d_mlstm), prv),
                   pl.BlockSpec((tm, d_mlstm), cur),
                   pl.BlockSpec((tm, d_conv), cur)],
        out_shape=[jax.ShapeDtypeStruct((n, d_mlstm), BF16),
                   jax.ShapeDtypeStruct((n, d_mlstm), BF16),
                   jax.ShapeDtypeStruct((n, d_mlstm), F32),
                   jax.ShapeDtypeStruct((n, d_conv), F32)],
        scratch_shapes=[pltpu.VMEM((tm + SUBLANES, d_mlstm), F32)],
        compiler_params=_cparams("arbitrary"),
        name="in_proj",
    )(xf, norm_w, w_in_b, conv_w, conv_b)


def _split3(v):
    hi = v.astype(BF16)
    r1 = v - hi.astype(F32)
    mid = r1.astype(BF16)
    lo = (r1 - mid.astype(F32)).astype(BF16)
    return hi, mid, lo


def _mlstm_kernel(xm_ref, xc_ref, sz_ref, wq_ref, wk_ref, wv_ref, wg_ref, bg_ref, nw_ref, sk_ref,
                  ym_ref, c_sc, n_sc, m_sc, *, chunk, dh, n_seq):
    L = chunk
    nh = N_HEADS
    j = pl.program_id(1)

    @pl.when(j == 0)
    def _():
        c_sc[...] = jnp.zeros_like(c_sc)
        n_sc[...] = jnp.zeros_like(n_sc)
        m_sc[...] = jnp.zeros_like(m_sc)

    ri = lax.broadcasted_iota(jnp.int32, (L, L), 0)
    ci = lax.broadcasted_iota(jnp.int32, (L, L), 1)
    causal = ci <= ri
    tri = jnp.where(causal, 1.0, 0.0).astype(BF16)

    for sq in range(n_seq):
        xmb = xm_ref[sq]
        xcb = xc_ref[sq]
        d = xmb.shape[1]
        nb = d // MXU_DIM

        def bd(xb, w_ref):
            return jnp.concatenate(
                [jnp.dot(xb[:, g * MXU_DIM:(g + 1) * MXU_DIM], w_ref[g], preferred_element_type=F32)
                 for g in range(nb)], axis=-1)

        q = bd(xcb, wq_ref)
        k_ = bd(xcb, wk_ref)
        v = bd(xmb, wv_ref)
        qb, kb, vb = q.astype(BF16), k_.astype(BF16), v.astype(BF16)

        g = (jnp.dot(qb, wg_ref[0:d, :], preferred_element_type=F32)
             + jnp.dot(kb, wg_ref[d:2 * d, :], preferred_element_type=F32)
             + jnp.dot(vb, wg_ref[2 * d:3 * d, :], preferred_element_type=F32)
             + bg_ref[...])
        col = lax.broadcasted_iota(jnp.int32, g.shape, 1)
        log_f = jnp.minimum(g, 0.0) - jnp.log(1.0 + jnp.exp(-jnp.abs(g)))
        gates = jnp.where(col < nh, g, jnp.where(col < 2 * nh, log_f, 0.0))
        cum = sum(jnp.dot(tri, part, preferred_element_type=F32) for part in _split3(gates))
        colform = jnp.where(col < nh, gates, cum) * LOG2_E
        rowform = colform.T

        for h in range(nh):
            sl = slice(h * dh, (h + 1) * dh)
            b_row = rowform[nh + h:nh + h + 1, :]
            r_row = rowform[h:h + 1, :] - b_row
            li_col = colform[:, h:h + 1]
            b_col = colform[:, nh + h:nh + h + 1]
            m_prev = m_sc[sq, h:h + 1, 0:1]
            g_tot = b_row[:, L - 1:L]

            dmat = jnp.where(causal, b_col + r_row, NEG_INF)
            inter = b_col + m_prev
            m_i = jnp.maximum(inter, jnp.max(dmat, axis=-1, keepdims=True))
            w_intra = jnp.exp2(dmat - m_i)
            w_inter = jnp.exp2(inter - m_i)

            qh = qb[:, sl]
            nt = (((1,), (1,)), ((), ()))
            s = lax.dot_general(qh, kb[:, sl], nt, preferred_element_type=F32) * w_intra
            c_prev = c_sc[sq, h]
            n_prev = n_sc[sq, h:h + 1, :]
            num = w_inter * jnp.dot(qh, c_prev.astype(BF16), preferred_element_type=F32) \
                + jnp.dot(s.astype(BF16), vb[:, sl], preferred_element_type=F32)
            n_rep = jnp.broadcast_to(n_prev, (LANES, dh)).astype(BF16)
            qn = lax.dot_general(qh, n_rep, nt, preferred_element_type=F32)[:, 0:1]
            den = w_inter * qn + jnp.sum(s, axis=-1, keepdims=True)
            hh = num * (1.0 / jnp.maximum(jnp.abs(den), jnp.exp2(-m_i)))

            mu = jnp.mean(hh, axis=-1, keepdims=True)
            dev = hh - mu
            var = jnp.mean(dev * dev, axis=-1, keepdims=True)
            hn = dev * lax.rsqrt(var + EPS) * nw_ref[:, sl]
            ym_ref[sq, :, sl] = ((hn + sk_ref[:, sl] * xcb[:, sl].astype(F32)) * sz_ref[sq, :, sl]).astype(ym_ref.dtype)

            a_row = g_tot + r_row
            m_new = jnp.maximum(g_tot + m_prev, jnp.max(a_row, axis=-1, keepdims=True))
            a_col = g_tot - b_col + li_col
            w_state = jnp.exp2(a_col - m_new)
            decay = jnp.exp2(g_tot + m_prev - m_new)
            kw = k_[:, sl] * w_state
            c_sc[sq, h] = decay * c_prev + lax.dot_general(kw.astype(BF16), vb[:, sl], (((0,), (0,)), ((), ())),
                                                           preferred_element_type=F32)
            w_state_rows = jnp.broadcast_to(jnp.exp2(a_row - m_new), (SUBLANES, L)).astype(BF16)
            n_sc[sq, h:h + 1, :] = decay * n_prev + jnp.dot(w_state_rows, kb[:, sl],
                                                            preferred_element_type=F32)[0:1, :]
            m_sc[sq, h:h + 1, :] = jnp.broadcast_to(m_new, (1, m_sc.shape[2]))


def _mlstm(xm, xc, sz, wq, wk, wv, wg, bg, norm_w, skip, bsz, seq):
    n, d = xm.shape
    L = MLSTM_CHUNK
    nc = seq // L
    dh = d // N_HEADS
    n_seq = 2 if bsz % 2 == 0 else 1
    blk = lambda b, j: (b, j, 0)
    c2 = lambda b, j: (0, 0)
    c3 = lambda b, j: (0, 0, 0)
    ym = pl.pallas_call(
        functools.partial(_mlstm_kernel, chunk=L, dh=dh, n_seq=n_seq),
        grid=(bsz // n_seq, nc),
        in_specs=[pl.BlockSpec((n_seq, L, d), blk), pl.BlockSpec((n_seq, L, d), blk), pl.BlockSpec((n_seq, L, d), blk),
                  pl.BlockSpec(wq.shape, c3), pl.BlockSpec(wk.shape, c3), pl.BlockSpec(wv.shape, c3),
                  pl.BlockSpec(wg.shape, c2), pl.BlockSpec(bg.shape, c2),
                  pl.BlockSpec(norm_w.shape, c2), pl.BlockSpec(skip.shape, c2)],
        out_specs=pl.BlockSpec((n_seq, L, d), blk),
        out_shape=jax.ShapeDtypeStruct((bsz, seq, d), BF16),
        scratch_shapes=[pltpu.VMEM((n_seq, N_HEADS, dh, dh), F32),
                        pltpu.VMEM((n_seq, SUBLANES, dh), F32),
                        pltpu.VMEM((n_seq, SUBLANES, LANES), F32)],
        compiler_params=_cparams("arbitrary", "arbitrary"),
        name="mlstm",
    )(xm.reshape(bsz, seq, d), xc.reshape(bsz, seq, d), sz.reshape(bsz, seq, d), wq, wk, wv, wg, bg, norm_w, skip)
    return ym.reshape(n, d)


def _conv_kernel(u_ref, w_ref, b_ref, nw_ref, nb_ref, yc_ref, ubuf, pbuf, cbuf, *, tile, rows):
    T = tile
    j = pl.program_id(1)

    @pl.when(j == 0)
    def _():
        ubuf[0:CONV_HALO, :] = jnp.zeros((CONV_HALO, ubuf.shape[1]), F32)

    ubuf[CONV_HALO:CONV_HALO + T, :] = u_ref[...]
    base = CONV_HALO - (CONV_WIDTH - 1)
    span = T + CONV_HALO - SUBLANES
    n_lane_blocks = ubuf.shape[1] // LANES

    def lane_block(c, carry):
        lanes = pl.ds(pl.multiple_of(c * LANES, LANES), LANES)
        for r in range(1, SUBLANES):
            pbuf[r - 1, :, :] = ubuf[r:r + span, lanes]
        for r0 in range(0, T, rows):
            acc = jnp.broadcast_to(b_ref[:, lanes], (rows, LANES))
            for k in range(CONV_WIDTH):
                q, r = divmod(base + k, SUBLANES)
                lo = r0 + q * SUBLANES
                src = ubuf[lo:lo + rows, lanes] if r == 0 else pbuf[r - 1, lo:lo + rows, :]
                acc = acc + w_ref[k:k + 1, lanes] * src
            cbuf[r0:r0 + rows, lanes] = acc
        return carry

    lax.fori_loop(0, n_lane_blocks, lane_block, 0)
    ubuf[0:CONV_HALO, :] = ubuf[T:T + CONV_HALO, :]

    y = cbuf[...]
    mu = jnp.mean(y, axis=-1, keepdims=True)
    dev = y - mu
    var = jnp.mean(dev * dev, axis=-1, keepdims=True)
    yn = dev * lax.rsqrt(var + EPS) * nw_ref[...] + nb_ref[...]
    yc_ref[...] = _silu(yn).astype(yc_ref.dtype)


def _conv_group(u, w, b, norm_w, norm_b, bsz, seq, tile):
    n, d = u.shape
    nt = seq // tile
    row = lambda bi, j: (bi * nt + j, 0)
    c2 = lambda bi, j: (0, 0)
    return pl.pallas_call(
        functools.partial(_conv_kernel, tile=tile, rows=CONV_ACC_ROWS),
        grid=(bsz, nt),
        in_specs=[pl.BlockSpec((tile, d), row), pl.BlockSpec(w.shape, c2), pl.BlockSpec(b.shape, c2),
                  pl.BlockSpec(norm_w.shape, c2), pl.BlockSpec(norm_b.shape, c2)],
        out_specs=pl.BlockSpec((tile, d), row),
        out_shape=jax.ShapeDtypeStruct((n, d), BF16),
        scratch_shapes=[pltpu.VMEM((tile + CONV_HALO, d), F32),
                        pltpu.VMEM((SUBLANES - 1, tile + CONV_HALO - SUBLANES, LANES), F32),
                        pltpu.VMEM((tile, d), F32)],
        compiler_params=_cparams("arbitrary", "arbitrary"),
        name="conv_group",
    )(u, w, b, norm_w, norm_b)


def _outproj_kernel(x_ref, ym_ref, yc_ref, wo_ref, nw_ref, wr_ref, br_ref,
                    h1_ref, a2_ref, idx_ref, pos_ref, gate_ref, cnt_ref, cnt_sc, *, tm, d_mlstm):
    i = pl.program_id(0)

    @pl.when(i == 0)
    def _():
        cnt_sc[...] = jnp.zeros_like(cnt_sc)

    h1 = (x_ref[...]
          + jnp.dot(ym_ref[...], wo_ref[0:d_mlstm, :], preferred_element_type=F32)
          + jnp.dot(yc_ref[...], wo_ref[d_mlstm:, :], preferred_element_type=F32))
    h1_ref[...] = h1
    a2 = h1 * lax.rsqrt(jnp.mean(h1 * h1, axis=-1, keepdims=True) + EPS) * nw_ref[...]
    _store_row_slabs(a2_ref, 0, _pack_bf16_pairs(a2), a2.shape[1] // (2 * LANES))

    logits = lax.dot_general(wr_ref[...], a2.astype(BF16), (((1,), (1,)), ((), ())),
                             preferred_element_type=F32) + br_ref[...]
    e_iota = lax.broadcasted_iota(jnp.int32, logits.shape, 0)
    work = logits
    vals, idxs = [], []
    for _ in range(TOP_K):
        mx = jnp.max(work, axis=0, keepdims=True)
        sel = jnp.min(jnp.where(work == mx, e_iota, N_EXPERTS), axis=0, keepdims=True)
        vals.append(mx)
        idxs.append(sel)
        work = jnp.where(e_iota == sel, NEG_INF, work)
    exps = [jnp.exp(vv - vals[0]) for vv in vals]
    tot = exps[0] + exps[1] + exps[2] + exps[3]
    gates = [ev / tot for ev in exps]

    chosen = functools.reduce(jnp.logical_or, [e_iota == sel for sel in idxs])
    mh = jnp.where(chosen, 1.0, 0.0)
    ri = lax.broadcasted_iota(jnp.int32, (tm, tm), 0)
    ci = lax.broadcasted_iota(jnp.int32, (tm, tm), 1)
    upper = jnp.where(ri < ci, 1.0, 0.0).astype(BF16)
    rank = jnp.dot(mh.astype(BF16), upper, preferred_element_type=F32) + cnt_sc[:, 0:1]
    cnt_new = cnt_sc[...] + jnp.sum(mh, axis=1, keepdims=True)
    cnt_sc[...] = cnt_new
    cnt_ref[...] = cnt_new

    zero_i = jnp.zeros((SUBLANES - TOP_K, tm), jnp.int32)
    pos = [jnp.sum(jnp.where(e_iota == sel, rank, 0.0), axis=0, keepdims=True).astype(jnp.int32) for sel in idxs]
    idx_ref[...] = jnp.concatenate(idxs + [zero_i], axis=0)
    pos_ref[...] = jnp.concatenate(pos + [zero_i], axis=0)
    gate_ref[...] = jnp.concatenate(gates + [jnp.zeros((SUBLANES - TOP_K, tm), F32)], axis=0)


def _out_proj(xf, ym, yc, w_out_b, norm_w, w_router_t, b_router, d_mlstm, tm):
    n, d = xf.shape
    n_slabs = d // (2 * LANES)
    row = lambda i: (i, 0)
    colb = lambda i: (0, i)
    const = lambda i: (0, 0)
    return pl.pallas_call(
        functools.partial(_outproj_kernel, tm=tm, d_mlstm=d_mlstm),
        grid=(n // tm,),
        in_specs=[pl.BlockSpec((tm, d), row), pl.BlockSpec((tm, ym.shape[1]), row), pl.BlockSpec((tm, yc.shape[1]), row),
                  pl.BlockSpec(w_out_b.shape, const), pl.BlockSpec(norm_w.shape, const),
                  pl.BlockSpec(w_router_t.shape, const), pl.BlockSpec(b_router.shape, const)],
        out_specs=[pl.BlockSpec((tm, d), row),
                   pl.BlockSpec((tm * n_slabs, LANES), row),
                   pl.BlockSpec((SUBLANES, tm), colb),
                   pl.BlockSpec((SUBLANES, tm), colb),
                   pl.BlockSpec((SUBLANES, tm), colb),
                   pl.BlockSpec((N_EXPERTS, LANES), const)],
        out_shape=[jax.ShapeDtypeStruct((n, d), F32),
                   jax.ShapeDtypeStruct((n * n_slabs, LANES), jnp.uint32),
                   jax.ShapeDtypeStruct((SUBLANES, n), jnp.int32),
                   jax.ShapeDtypeStruct((SUBLANES, n), jnp.int32),
                   jax.ShapeDtypeStruct((SUBLANES, n), F32),
                   jax.ShapeDtypeStruct((N_EXPERTS, LANES), F32)],
        scratch_shapes=[pltpu.VMEM((N_EXPERTS, LANES), F32)],
        compiler_params=_cparams("arbitrary"),
        name="out_proj_router",
    )(xf, ym, yc, w_out_b, norm_w, w_router_t, b_router)


def _experts_kernel(blk_ref, exp_ref, valid_ref, lo_ref, hi_ref,
                    xs_ref, w1_ref, b1_ref, w2_ref, b2_ref, ys_ref, w1c_ref, w2c_ref, acc_ref, *, rb, d_ff, n_sub):
    i = pl.program_id(0)
    n_slabs = xs_ref.shape[0] // rb
    prev = jnp.maximum(i - 1, 0)
    first_visit = jnp.logical_or(i == 0, blk_ref[prev] != blk_ref[i])
    new_expert = jnp.logical_or(i == 0, exp_ref[prev] != exp_ref[i])

    @pl.when(new_expert)
    def _():
        w1c_ref[...] = w1_ref[0].astype(BF16)
        w2c_ref[...] = w2_ref[0].astype(BF16)

    @pl.when(first_visit)
    def _():
        acc_ref[...] = jnp.zeros_like(acc_ref)

    rs = rb // n_sub
    lo, hi = lo_ref[i], hi_ref[i]

    def sub_block(c, owns_all_rows):
        xa, xb = _unpack_bf16_pairs(_load_row_slabs(xs_ref, c * rs, rs, n_slabs))
        x = jnp.concatenate([xa.astype(BF16), xb.astype(BF16)], axis=-1)
        hid = jnp.dot(x, w1c_ref[...], preferred_element_type=F32) + b1_ref[0]
        x_glu = jnp.minimum(hid[:, :d_ff], SWIGLU_LIMIT)
        x_lin = jnp.clip(hid[:, d_ff:], -SWIGLU_LIMIT, SWIGLU_LIMIT)
        act = x_glu * _sigmoid(SWIGLU_ALPHA * x_glu) * (x_lin + 1.0)
        y = jnp.dot(act.astype(BF16), w2c_ref[...], preferred_element_type=F32) + b2_ref[0]
        if not owns_all_rows:
            r = c * rs + lax.broadcasted_iota(jnp.int32, (rs, 1), 0)
            y = jnp.where(jnp.logical_and(r >= lo, r < hi), y, 0.0)
            y = acc_ref[c * rs:(c + 1) * rs, :] + y
            acc_ref[c * rs:(c + 1) * rs, :] = y
        _store_row_slabs(ys_ref, c * rs, _pack_bf16_pairs(y), n_slabs)

    for c in range(n_sub):
        valid = valid_ref[i] == 1
        owns_all = jnp.logical_and(lo <= c * rs, hi >= (c + 1) * rs)
        has_rows = jnp.logical_and(lo < (c + 1) * rs, hi > c * rs)
        pl.when(jnp.logical_and(valid, owns_all))(functools.partial(sub_block, c, True))
        pl.when(jnp.logical_and(valid, jnp.logical_and(has_rows, jnp.logical_not(owns_all))))(
            functools.partial(sub_block, c, False))


def _experts(xs, w1, b1, w2, b2, item_blk, item_exp, item_valid, item_lo, item_hi, rb, n_slabs):
    n_items = item_blk.shape[0]
    d = w1.shape[1]
    d_ff = w2.shape[1]
    by_blk = lambda i, blk, ex, va, lo, hi: (blk[i], 0)
    by_exp = lambda i, blk, ex, va, lo, hi: (ex[i], 0, 0)
    grid_spec = pltpu.PrefetchScalarGridSpec(
        num_scalar_prefetch=5,
        grid=(n_items,),
        in_specs=[pl.BlockSpec((rb * n_slabs, LANES), by_blk),
                  pl.BlockSpec((1,) + w1.shape[1:], by_exp),
                  pl.BlockSpec((1,) + b1.shape[1:], by_exp),
                  pl.BlockSpec((1,) + w2.shape[1:], by_exp),
                  pl.BlockSpec((1,) + b2.shape[1:], by_exp)],
        out_specs=pl.BlockSpec((rb * n_slabs, LANES), by_blk),
        scratch_shapes=[pltpu.VMEM(w1.shape[1:], BF16), pltpu.VMEM(w2.shape[1:], BF16), pltpu.VMEM((rb, d), F32)],
    )
    return pl.pallas_call(
        functools.partial(_experts_kernel, rb=rb, d_ff=d_ff, n_sub=rb // MXU_DIM),
        grid_spec=grid_spec,
        out_shape=jax.ShapeDtypeStruct(xs.shape, jnp.uint32),
        compiler_params=pltpu.CompilerParams(dimension_semantics=("arbitrary",), vmem_limit_bytes=EXPERTS_VMEM_LIMIT),
        name="experts",
    )(item_blk, item_exp, item_valid, item_lo, item_hi, xs, w1, b1, w2, b2)


SC_CORES = 2
SC_SUBCORES = 16
SC_GATHER_WINDOW = 64


def _sc_worker_base(per_worker):
    wid = lax.axis_index("s") * SC_CORES + lax.axis_index("c")
    return wid * per_worker


def _sc_gather_rows(table, idx):
    m = idx.shape[0]
    n_workers = SC_CORES * SC_SUBCORES
    window = SC_GATHER_WINDOW
    per_worker = m // n_workers
    n_win = per_worker // window
    assert per_worker * n_workers == m and n_win * window == per_worker and n_win % 2 == 0
    mesh = plsc.VectorSubcoreMesh(core_axis_name="c", subcore_axis_name="s")
    slab = table.shape[1:]

    @functools.partial(
        pl.kernel, mesh=mesh,
        out_type=jax.ShapeDtypeStruct((m,) + slab, table.dtype),
        scratch_types=[pltpu.VMEM((per_worker,), jnp.int32),
                       pltpu.VMEM((window,) + slab, table.dtype), pltpu.VMEM((window,) + slab, table.dtype)]
        + [pltpu.SemaphoreType.DMA] * 4,
        name="sc_gather_rows",
    )
    def gather(table_hbm, idx_hbm, out_hbm, idx_v, rows0, rows1, gsem0, gsem1, wsem0, wsem1):
        rows_v, gsem, wsem = (rows0, rows1), (gsem0, gsem1), (wsem0, wsem1)
        base = _sc_worker_base(per_worker)
        pltpu.sync_copy(idx_hbm.at[pl.ds(pl.multiple_of(base, window), per_worker)], idx_v)

        def gather_copy(w, b):
            ids = idx_v.at[pl.ds(pl.multiple_of(w * window, window), window)]
            return pltpu.make_async_copy(table_hbm.at[ids], rows_v[b], gsem[b])

        def write_copy(w, b):
            dst = out_hbm.at[pl.ds(pl.multiple_of(base + w * window, window), window)]
            return pltpu.make_async_copy(rows_v[b], dst, wsem[b])

        gather_copy(0, 0).start()

        @pl.loop(0, n_win, step=2)
        def _(w):
            @pl.when(w >= 1)
            def _():
                write_copy(w - 1, 1).wait()

            gather_copy(w + 1, 1).start()
            gather_copy(w, 0).wait()
            write_copy(w, 0).start()

            @pl.when(w + 2 < n_win)
            def _():
                write_copy(w, 0).wait()
                gather_copy(w + 2, 0).start()

            gather_copy(w + 1, 1).wait()
            write_copy(w + 1, 1).start()

        write_copy(n_win - 2, 0).wait()
        write_copy(n_win - 1, 1).wait()

    return gather(table, idx)


def _sc_scatter_rows(rows, dest, n_slots):
    n = rows.shape[0]
    n_workers = SC_CORES * SC_SUBCORES
    window = SC_GATHER_WINDOW
    per_worker = n // n_workers
    n_win = per_worker // window
    assert per_worker * n_workers == n and n_win * window == per_worker and n_win % 2 == 0
    mesh = plsc.VectorSubcoreMesh(core_axis_name="c", subcore_axis_name="s")
    slab = rows.shape[1:]

    @functools.partial(
        pl.kernel, mesh=mesh,
        out_type=jax.ShapeDtypeStruct((n_slots * n,) + slab, rows.dtype),
        scratch_types=[pltpu.VMEM((window,), jnp.int32)] * (2 * n_slots)
        + [pltpu.VMEM((window,) + slab, rows.dtype)] * 2 + [pltpu.SemaphoreType.DMA] * 4,
        name="sc_scatter_rows",
    )
    def scatter(rows_hbm, dest_hbm, out_hbm, *scratch):
        idx_v = (scratch[:n_slots], scratch[n_slots:2 * n_slots])
        rows_v = scratch[2 * n_slots:2 * n_slots + 2]
        rsem = scratch[2 * n_slots + 2:2 * n_slots + 4]
        wsem = scratch[2 * n_slots + 4:2 * n_slots + 6]
        base = _sc_worker_base(per_worker)

        def rows_at(w, k=0):
            return pl.ds(pl.multiple_of(k * n + base + w * window, window), window)

        def start_read(w, b):
            for k in range(n_slots):
                pltpu.sync_copy(dest_hbm.at[rows_at(w, k)], idx_v[b][k])
            pltpu.async_copy(rows_hbm.at[rows_at(w)], rows_v[b], rsem[b])

        def scatter_window(w, b):
            pltpu.make_async_copy(rows_hbm.at[rows_at(w)], rows_v[b], rsem[b]).wait()
            for k in range(n_slots):
                pltpu.async_copy(rows_v[b], out_hbm.at[idx_v[b][k]], wsem[b])
            for k in range(n_slots):
                pltpu.make_async_copy(rows_v[b], out_hbm.at[idx_v[b][k]], wsem[b]).wait()

        start_read(0, 0)

        @pl.loop(0, n_win, step=2)
        def _(w):
            start_read(w + 1, 1)
            scatter_window(w, 0)

            @pl.when(w + 2 < n_win)
            def _():
                start_read(w + 2, 0)

            scatter_window(w + 1, 1)

    return scatter(rows, dest)


def _final_kernel(h1_ref, gate_ref, yg_ref, nw_ref, out_ref, *, tm, n_slabs):
    gpad = jnp.concatenate([gate_ref[...], jnp.zeros((LANES - SUBLANES, tm), F32)], axis=0)
    gcol = gpad.T
    h2 = h1_ref[...]
    for k in range(TOP_K):
        ya, yb = _unpack_bf16_pairs(_load_row_slabs(yg_ref, 0, tm, n_slabs, lead=(k,)))
        h2 = h2 + gcol[:, k:k + 1] * jnp.concatenate([ya, yb], axis=-1)
    out_ref[...] = h2 * lax.rsqrt(jnp.mean(h2 * h2, axis=-1, keepdims=True) + EPS) * nw_ref[...]


def _final(h, gates_t, yg, final_w, n_slabs, tm, chunk, n_chunks):
    n, d = h.shape
    tiles = n // tm // n_chunks
    first = chunk * tiles
    return pl.pallas_call(
        functools.partial(_final_kernel, tm=tm, n_slabs=n_slabs),
        grid=(tiles,),
        in_specs=[pl.BlockSpec((tm, d), lambda i: (first + i, 0)),
                  pl.BlockSpec((SUBLANES, tm), lambda i: (0, first + i)),
                  pl.BlockSpec((TOP_K, tm * n_slabs, LANES), lambda i: (0, i, 0)),
                  pl.BlockSpec((1, d), lambda i: (0, 0))],
        out_specs=pl.BlockSpec((tm, d), lambda i: (first + i, 0)),
        out_shape=jax.ShapeDtypeStruct((n, d), F32),
        input_output_aliases={0: 0},
        compiler_params=_cparams("parallel"),
        name="final",
    )(h, gates_t, yg, final_w)


def _block_diag_tiles(w):
    nb, bs, _ = w.shape
    rows = jnp.tile(w.reshape(nb * bs // MXU_DIM, MXU_DIM, bs), (1, 1, MXU_DIM // bs))
    r_blk = lax.broadcasted_iota(jnp.int32, (MXU_DIM, MXU_DIM), 0) // bs
    c_blk = lax.broadcasted_iota(jnp.int32, (MXU_DIM, MXU_DIM), 1) // bs
    return jnp.where(r_blk == c_blk, rows, 0.0).astype(BF16)


def _layer(xf, bsz, seq, norm_mix_w, w_in, mlstm_conv_w, mlstm_conv_b, w_q, w_k, w_v, w_igate, b_igate,
           w_fgate, b_fgate, mlstm_norm_w, mlstm_skip, conv_dw_w, conv_dw_b, conv_norm_w, conv_norm_b,
           w_out, norm_ffn_w, w_router, b_router, w1, b1, w2, b2, final_norm_w):
    n, d = xf.shape
    d_mlstm = mlstm_norm_w.shape[0]
    d_conv = conv_norm_w.shape[0]
    n_slabs = d // (2 * LANES)
    r2 = lambda v: v.reshape(1, -1)

    xm, xc, sz, u = _in_proj(xf, r2(norm_mix_w), w_in.astype(BF16), mlstm_conv_w, r2(mlstm_conv_b),
                             d_mlstm, d_conv, seq, tm=IN_PROJ_ROWS)

    q_scale = (d_mlstm // N_HEADS) ** -0.5
    inv_q_scale = round(1.0 / q_scale)
    assert inv_q_scale & (inv_q_scale - 1) == 0 and inv_q_scale * q_scale == 1.0, "head dim must be a power of 4"
    wg = jnp.concatenate([w_igate, w_fgate], axis=1)
    wg = jnp.concatenate([wg[:d_mlstm] * float(inv_q_scale), wg[d_mlstm:]], axis=0)
    wg = jnp.pad(wg, ((0, 0), (0, LANES - wg.shape[1]))).astype(BF16)
    bg = jnp.pad(jnp.concatenate([b_igate, b_fgate]), (0, LANES - 2 * N_HEADS)).reshape(1, LANES)
    ym = _mlstm(xm, xc, sz, _block_diag_tiles(w_q * q_scale), _block_diag_tiles(w_k), _block_diag_tiles(w_v),
                wg, bg, r2(mlstm_norm_w), r2(mlstm_skip), bsz, seq)
    yc = _conv_group(u, conv_dw_w, r2(conv_dw_b), r2(conv_norm_w), r2(conv_norm_b), bsz, seq, tile=CONV_ROWS)

    h1, a2s, idx_t, pos_t, gates_t, cnt = _out_proj(
        xf, ym, yc, w_out.astype(BF16), r2(norm_ffn_w), w_router.T.astype(BF16), b_router.reshape(-1, 1), d_mlstm,
        tm=OUT_PROJ_ROWS)

    counts = cnt[:, 0].astype(jnp.int32)
    ends = jnp.cumsum(counts)
    starts = ends - counts
    e_ids = jnp.arange(N_EXPERTS, dtype=jnp.int32)
    idx4 = idx_t[:TOP_K]
    dest = pos_t[:TOP_K] + jnp.sum(
        jnp.where(idx4[None] == e_ids[:, None, None], starts[:, None, None], 0), axis=0)

    n_rows = n * TOP_K
    rb = EXPERT_ROWS
    n_blocks = n_rows // rb
    n_items = n_blocks + N_EXPERTS - 1
    first_blk = starts // rb
    last_blk = jnp.where(counts > 0, (ends - 1) // rb, first_blk - 1)
    per_e = last_blk - first_blk + 1
    item_end = jnp.cumsum(per_e)
    item_start = item_end - per_e
    ids = jnp.arange(n_items, dtype=jnp.int32)
    total = item_end[-1]
    item_valid = (ids < total).astype(jnp.int32)
    item_exp = jnp.minimum(jnp.sum((ids[:, None] >= item_end[None, :]).astype(jnp.int32), axis=1), N_EXPERTS - 1)
    last_valid_exp = jnp.max(jnp.where(per_e > 0, e_ids, 0))
    item_exp = jnp.where(item_valid == 1, item_exp, last_valid_exp).astype(jnp.int32)
    is_exp = item_exp[:, None] == e_ids[None, :]
    of_item = lambda table: jnp.sum(jnp.where(is_exp, table[None, :], 0), axis=1)
    item_blk = jnp.where(item_valid == 1, of_item(first_blk) + ids - of_item(item_start), n_blocks - 1)
    item_blk = item_blk.astype(jnp.int32)
    item_lo = (jnp.maximum(of_item(starts), item_blk * rb) - item_blk * rb).astype(jnp.int32)
    item_hi = (jnp.minimum(of_item(ends), (item_blk + 1) * rb) - item_blk * rb).astype(jnp.int32)

    xs = _sc_scatter_rows(a2s.reshape(n, n_slabs, LANES), dest.reshape(-1), TOP_K)
    ys = _experts(xs.reshape(n_rows * n_slabs, LANES), w1, b1[:, None, :], w2, b2[:, None, :],
                  item_blk, item_exp, item_valid, item_lo, item_hi, rb, n_slabs)
    ys = ys.reshape(n_rows, n_slabs, LANES)

    n_chunks = FINAL_CHUNKS
    nc = n // n_chunks
    out = h1
    for c in range(n_chunks):
        yg = _sc_gather_rows(ys, dest[:, c * nc:(c + 1) * nc].reshape(-1))
        out = _final(out, gates_t, yg.reshape(TOP_K, nc * n_slabs, LANES), r2(final_norm_w), n_slabs,
                     tm=FINAL_ROWS, chunk=c, n_chunks=n_chunks)
    return out


def kernel(x, norm_mix_w, w_in, mlstm_conv_w, mlstm_conv_b, w_q, w_k, w_v, w_igate, b_igate, w_fgate, b_fgate,
           mlstm_norm_w, mlstm_skip, conv_dw_w, conv_dw_b, conv_norm_w, conv_norm_b, w_out, norm_ffn_w,
           w_router, b_router, w1, b1, w2, b2, final_norm_w):
    bsz, seq, d = x.shape
    assert norm_mix_w.shape[0] == 1, "single-layer block"
    out = _layer(x.reshape(bsz * seq, d), bsz, seq, norm_mix_w[0], w_in[0], mlstm_conv_w[0], mlstm_conv_b[0],
                 w_q[0], w_k[0], w_v[0], w_igate[0], b_igate[0], w_fgate[0], b_fgate[0], mlstm_norm_w[0],
                 mlstm_skip[0], conv_dw_w[0], conv_dw_b[0], conv_norm_w[0], conv_norm_b[0], w_out[0],
                 norm_ffn_w[0], w_router[0], b_router[0], w1[0], b1[0], w2[0], b2[0], final_norm_w)
    return out.reshape(bsz, seq, d)
```

```python
import functools

import jax
import jax.numpy as jnp
from jax import lax
from jax.experimental import pallas as pl
from jax.experimental.pallas import tpu as pltpu
from jax.experimental.pallas import tpu_sc as plsc

F32 = jnp.float32
BF16 = jnp.bfloat16

EPS = 1e-5
N_HEADS = 4
MLSTM_CONV_WIDTH = 4
CONV_WIDTH = 31
N_EXPERTS = 32
TOP_K = 4
SWIGLU_ALPHA = 1.702
SWIGLU_LIMIT = 7.0

LANES = 128
SUBLANES = 8
MXU_DIM = 256
VMEM_LIMIT = 52 * 1024 * 1024
EXPERTS_VMEM_LIMIT = 58 * 1024 * 1024

IN_PROJ_ROWS = 512
CONV_ROWS = 512
CONV_ACC_ROWS = 64
OUT_PROJ_ROWS = 512
EXPERT_ROWS = 1024
FINAL_ROWS = 256
FINAL_CHUNKS = 4

MLSTM_CHUNK = 256
CONV_HALO = 32
NEG_INF = float("-inf")
LOG2_E = 1.4426950408889634


def _sigmoid(x):
    return 0.5 * jnp.tanh(0.5 * x) + 0.5


def _silu(x):
    return x * _sigmoid(x)


def _pack_bf16_pairs(v):
    half = v.shape[1] // 2
    hi = lax.bitcast_convert_type(v[:, :half].astype(BF16).astype(F32), jnp.uint32)
    lo = lax.bitcast_convert_type(v[:, half:].astype(BF16).astype(F32), jnp.uint32)
    return hi | (lo >> 16)


def _unpack_bf16_pairs(w):
    hi = lax.bitcast_convert_type(w & jnp.uint32(0xFFFF0000), F32)
    lo = lax.bitcast_convert_type(w << 16, F32)
    return hi, lo


def _load_row_slabs(ref, first, rows, n_slabs, lead=()):
    return jnp.concatenate([ref[lead + (pl.ds(first * n_slabs + s, rows, stride=n_slabs), slice(None))]
                            for s in range(n_slabs)], axis=-1)


def _store_row_slabs(ref, first, v, n_slabs):
    rows = v.shape[0]
    for s in range(n_slabs):
        ref[pl.ds(first * n_slabs + s, rows, stride=n_slabs), :] = v[:, s * LANES:(s + 1) * LANES]


def _cparams(*sem):
    return pltpu.CompilerParams(dimension_semantics=sem, vmem_limit_bytes=VMEM_LIMIT)


def _inproj_kernel(x_ref, nw_ref, w_ref, cw_ref, cb_ref, xm_ref, xc_ref, sz_ref, u_ref, xbuf,
                   *, d_mlstm, d_conv, tiles_per_seq):
    tm = x_ref.shape[0]
    hist = SUBLANES
    i = pl.program_id(0)

    @pl.when(i == 0)
    def _():
        xbuf[...] = jnp.zeros_like(xbuf)

    x = x_ref[...]
    a = x * lax.rsqrt(jnp.mean(x * x, axis=-1, keepdims=True) + EPS) * nw_ref[...]
    ab = a.astype(BF16)
    same_seq = i % tiles_per_seq != 0
    cols = MXU_DIM

    for c in range(d_mlstm // cols):
        sl = slice(c * cols, (c + 1) * cols)
        prev = xbuf[hist:hist + tm, sl]
        acc = cb_ref[:, sl] + cw_ref[MLSTM_CONV_WIDTH - 1:MLSTM_CONV_WIDTH, sl] * prev
        for k in range(MLSTM_CONV_WIDTH - 1):
            off = hist - (MLSTM_CONV_WIDTH - 1) + k
            acc = acc + cw_ref[k:k + 1, sl] * xbuf[off:off + tm, sl]
        xc_ref[:, sl] = _silu(acc).astype(xc_ref.dtype)
        xbuf[0:hist, sl] = jnp.where(same_seq, prev[tm - hist:, :], 0.0)

        xm = jnp.dot(ab, w_ref[:, sl], preferred_element_type=F32)
        xm_ref[:, sl] = xm.astype(xm_ref.dtype)
        xbuf[hist:hist + tm, sl] = xm
        z = jnp.dot(ab, w_ref[:, d_mlstm + c * cols:d_mlstm + (c + 1) * cols], preferred_element_type=F32)
        sz_ref[:, sl] = _silu(z)

    for c in range(d_conv // cols):
        sl = slice(c * cols, (c + 1) * cols)
        ga = jnp.dot(ab, w_ref[:, 2 * d_mlstm + c * cols:2 * d_mlstm + (c + 1) * cols], preferred_element_type=F32)
        gb = jnp.dot(ab, w_ref[:, 2 * d_mlstm + d_conv + c * cols:2 * d_mlstm + d_conv + (c + 1) * cols],
                     preferred_element_type=F32)
        u_ref[:, sl] = ga * _sigmoid(gb)


def _in_proj(xf, norm_w, w_in_b, conv_w, conv_b, d_mlstm, d_conv, seq, tm):
    n, d = xf.shape
    n_tiles = n // tm
    cur = lambda i: (jnp.minimum(i, n_tiles - 1), 0)
    prv = lambda i: (jnp.maximum(i - 1, 0), 0)
    const = lambda i: (0, 0)
    return pl.pallas_call(
        functools.partial(_inproj_kernel, d_mlstm=d_mlstm, d_conv=d_conv, tiles_per_seq=seq // tm),
        grid=(n_tiles + 1,),
        in_specs=[pl.BlockSpec((tm, d), cur),
                  pl.BlockSpec((1, d), const),
                  pl.BlockSpec(w_in_b.shape, const),
                  pl.BlockSpec(conv_w.shape, const),
                  pl.BlockSpec(conv_b.shape, const)],
        out_specs=[pl.BlockSpec((tm, d_mlstm), cur),
                   pl.BlockSpec((tm, d_mlstm), prv),
                   pl.BlockSpec((tm, d_mlstm), cur),
                   pl.BlockSpec((tm, d_conv), cur)],
        out_shape=[jax.ShapeDtypeStruct((n, d_mlstm), BF16),
                   jax.ShapeDtypeStruct((n, d_mlstm), BF16),
                   jax.ShapeDtypeStruct((n, d_mlstm), F32),
                   jax.ShapeDtypeStruct((n, d_conv), F32)],
        scratch_shapes=[pltpu.VMEM((tm + SUBLANES, d_mlstm), F32)],
        compiler_params=_cparams("arbitrary"),
        name="in_proj",
    )(xf, norm_w, w_in_b, conv_w, conv_b)


def _split3(v):
    hi = v.astype(BF16)
    r1 = v - hi.astype(F32)
    mid = r1.astype(BF16)
    lo = (r1 - mid.astype(F32)).astype(BF16)
    return hi, mid, lo


def _mlstm_kernel(xm_ref, xc_ref, sz_ref, wq_ref, wk_ref, wv_ref, wg_ref, bg_ref, nw_ref, sk_ref,
                  ym_ref, c_sc, n_sc, m_sc, *, chunk, dh, n_seq):
    L = chunk
    nh = N_HEADS
    j = pl.program_id(1)

    @pl.when(j == 0)
    def _():
        c_sc[...] = jnp.zeros_like(c_sc)
        n_sc[...] = jnp.zeros_like(n_sc)
        m_sc[...] = jnp.zeros_like(m_sc)

    ri = lax.broadcasted_iota(jnp.int32, (L, L), 0)
    ci = lax.broadcasted_iota(jnp.int32, (L, L), 1)
    causal = ci <= ri
    tri = jnp.where(causal, 1.0, 0.0).astype(BF16)

    for sq in range(n_seq):
        xmb = xm_ref[sq]
        xcb = xc_ref[sq]
        d = xmb.shape[1]
        nb = d // MXU_DIM

        def bd(xb, w_ref):
            return jnp.concatenate(
                [jnp.dot(xb[:, g * MXU_DIM:(g + 1) * MXU_DIM], w_ref[g], preferred_element_type=F32)
                 for g in range(nb)], axis=-1)

        q = bd(xcb, wq_ref)
        k_ = bd(xcb, wk_ref)
        v = bd(xmb, wv_ref)
        qb, kb, vb = q.astype(BF16), k_.astype(BF16), v.astype(BF16)

        g = (jnp.dot(qb, wg_ref[0:d, :], preferred_element_type=F32)
             + jnp.dot(kb, wg_ref[d:2 * d, :], preferred_element_type=F32)
             + jnp.dot(vb, wg_ref[2 * d:3 * d, :], preferred_element_type=F32)
             + bg_ref[...])
        col = lax.broadcasted_iota(jnp.int32, g.shape, 1)
        log_f = jnp.minimum(g, 0.0) - jnp.log(1.0 + jnp.exp(-jnp.abs(g)))
        gates = jnp.where(col < nh, g, jnp.where(col < 2 * nh, log_f, 0.0))
        cum = sum(jnp.dot(tri, part, preferred_element_type=F32) for part in _split3(gates))
        colform = jnp.where(col < nh, gates, cum) * LOG2_E
        rowform = colform.T

        for h in range(nh):
            sl = slice(h * dh, (h + 1) * dh)
            b_row = rowform[nh + h:nh + h + 1, :]
            r_row = rowform[h:h + 1, :] - b_row
            li_col = colform[:, h:h + 1]
            b_col = colform[:, nh + h:nh + h + 1]
            m_prev = m_sc[sq, h:h + 1, 0:1]
            g_tot = b_row[:, L - 1:L]

            dmat = jnp.where(causal, b_col + r_row, NEG_INF)
            inter = b_col + m_prev
            m_i = jnp.maximum(inter, jnp.max(dmat, axis=-1, keepdims=True))
            w_intra = jnp.exp2(dmat - m_i)
            w_inter = jnp.exp2(inter - m_i)

            qh = qb[:, sl]
            nt = (((1,), (1,)), ((), ()))
            s = lax.dot_general(qh, kb[:, sl], nt, preferred_element_type=F32) * w_intra
            c_prev = c_sc[sq, h]
            n_prev = n_sc[sq, h:h + 1, :]
            num = w_inter * jnp.dot(qh, c_prev.astype(BF16), preferred_element_type=F32) \
                + jnp.dot(s.astype(BF16), vb[:, sl], preferred_element_type=F32)
            n_rep = jnp.broadcast_to(n_prev, (LANES, dh)).astype(BF16)
            qn = lax.dot_general(qh, n_rep, nt, preferred_element_type=F32)[:, 0:1]
            den = w_inter * qn + jnp.sum(s, axis=-1, keepdims=True)
            hh = num * (1.0 / jnp.maximum(jnp.abs(den), jnp.exp2(-m_i)))

            mu = jnp.mean(hh, axis=-1, keepdims=True)
            dev = hh - mu
            var = jnp.mean(dev * dev, axis=-1, keepdims=True)
            hn = dev * lax.rsqrt(var + EPS) * nw_ref[:, sl]
            ym_ref[sq, :, sl] = ((hn + sk_ref[:, sl] * xcb[:, sl].astype(F32)) * sz_ref[sq, :, sl]).astype(ym_ref.dtype)

            a_row = g_tot + r_row
            m_new = jnp.maximum(g_tot + m_prev, jnp.max(a_row, axis=-1, keepdims=True))
            a_col = g_tot - b_col + li_col
            w_state = jnp.exp2(a_col - m_new)
            decay = jnp.exp2(g_tot + m_prev - m_new)
            kw = k_[:, sl] * w_state
            c_sc[sq, h] = decay * c_prev + lax.dot_general(kw.astype(BF16), vb[:, sl], (((0,), (0,)), ((), ())),
                                                           preferred_element_type=F32)
            w_state_rows = jnp.broadcast_to(jnp.exp2(a_row - m_new), (SUBLANES, L)).astype(BF16)
            n_sc[sq, h:h + 1, :] = decay * n_prev + jnp.dot(w_state_rows, kb[:, sl],
                                                            preferred_element_type=F32)[0:1, :]
            m_sc[sq, h:h + 1, :] = jnp.broadcast_to(m_new, (1, m_sc.shape[2]))


def _mlstm(xm, xc, sz, wq, wk, wv, wg, bg, norm_w, skip, bsz, seq):
    n, d = xm.shape
    L = MLSTM_CHUNK
    nc = seq // L
    dh = d // N_HEADS
    n_seq = 2 if bsz % 2 == 0 else 1
    blk = lambda b, j: (b, j, 0)
    c2 = lambda b, j: (0, 0)
    c3 = lambda b, j: (0, 0, 0)
    ym = pl.pallas_call(
        functools.partial(_mlstm_kernel, chunk=L, dh=dh, n_seq=n_seq),
        grid=(bsz // n_seq, nc),
        in_specs=[pl.BlockSpec((n_seq, L, d), blk), pl.BlockSpec((n_seq, L, d), blk), pl.BlockSpec((n_seq, L, d), blk),
                  pl.BlockSpec(wq.shape, c3), pl.BlockSpec(wk.shape, c3), pl.BlockSpec(wv.shape, c3),
                  pl.BlockSpec(wg.shape, c2), pl.BlockSpec(bg.shape, c2),
                  pl.BlockSpec(norm_w.shape, c2), pl.BlockSpec(skip.shape, c2)],
        out_specs=pl.BlockSpec((n_seq, L, d), blk),
        out_shape=jax.ShapeDtypeStruct((bsz, seq, d), BF16),
        scratch_shapes=[pltpu.VMEM((n_seq, N_HEADS, dh, dh), F32),
                        pltpu.VMEM((n_seq, SUBLANES, dh), F32),
                        pltpu.VMEM((n_seq, SUBLANES, LANES), F32)],
        compiler_params=_cparams("arbitrary", "arbitrary"),
        name="mlstm",
    )(xm.reshape(bsz, seq, d), xc.reshape(bsz, seq, d), sz.reshape(bsz, seq, d), wq, wk, wv, wg, bg, norm_w, skip)
    return ym.reshape(n, d)


def _conv_kernel(u_ref, w_ref, b_ref, nw_ref, nb_ref, yc_ref, ubuf, pbuf, cbuf, *, tile, rows):
    T = tile
    j = pl.program_id(1)

    @pl.when(j == 0)
    def _():
        ubuf[0:CONV_HALO, :] = jnp.zeros((CONV_HALO, ubuf.shape[1]), F32)

    ubuf[CONV_HALO:CONV_HALO + T, :] = u_ref[...]
    base = CONV_HALO - (CONV_WIDTH - 1)
    span = T + CONV_HALO - SUBLANES
    n_lane_blocks = ubuf.shape[1] // LANES

    def lane_block(c, carry):
        lanes = pl.ds(pl.multiple_of(c * LANES, LANES), LANES)
        for r in range(1, SUBLANES):
            pbuf[r - 1, :, :] = ubuf[r:r + span, lanes]
        for r0 in range(0, T, rows):
            acc = jnp.broadcast_to(b_ref[:, lanes], (rows, LANES))
            for k in range(CONV_WIDTH):
                q, r = divmod(base + k, SUBLANES)
                lo = r0 + q * SUBLANES
                src = ubuf[lo:lo + rows, lanes] if r == 0 else pbuf[r - 1, lo:lo + rows, :]
                acc = acc + w_ref[k:k + 1, lanes] * src
            cbuf[r0:r0 + rows, lanes] = acc
        return carry

    lax.fori_loop(0, n_lane_blocks, lane_block, 0)
    ubuf[0:CONV_HALO, :] = ubuf[T:T + CONV_HALO, :]

    y = cbuf[...]
    mu = jnp.mean(y, axis=-1, keepdims=True)
    dev = y - mu
    var = jnp.mean(dev * dev, axis=-1, keepdims=True)
    yn = dev * lax.rsqrt(var + EPS) * nw_ref[...] + nb_ref[...]
    yc_ref[...] = _silu(yn).astype(yc_ref.dtype)


def _conv_group(u, w, b, norm_w, norm_b, bsz, seq, tile):
    n, d = u.shape
    nt = seq // tile
    row = lambda bi, j: (bi * nt + j, 0)
    c2 = lambda bi, j: (0, 0)
    return pl.pallas_call(
        functools.partial(_conv_kernel, tile=tile, rows=CONV_ACC_ROWS),
        grid=(bsz, nt),
        in_specs=[pl.BlockSpec((tile, d), row), pl.BlockSpec(w.shape, c2), pl.BlockSpec(b.shape, c2),
                  pl.BlockSpec(norm_w.shape, c2), pl.BlockSpec(norm_b.shape, c2)],
        out_specs=pl.BlockSpec((tile, d), row),
        out_shape=jax.ShapeDtypeStruct((n, d), BF16),
        scratch_shapes=[pltpu.VMEM((tile + CONV_HALO, d), F32),
                        pltpu.VMEM((SUBLANES - 1, tile + CONV_HALO - SUBLANES, LANES), F32),
                        pltpu.VMEM((tile, d), F32)],
        compiler_params=_cparams("arbitrary", "arbitrary"),
        name="conv_group",
    )(u, w, b, norm_w, norm_b)


def _outproj_kernel(x_ref, ym_ref, yc_ref, wo_ref, nw_ref, wr_ref, br_ref,
                    h1_ref, a2_ref, idx_ref, pos_ref, gate_ref, cnt_ref, cnt_sc, *, tm, d_mlstm):
    i = pl.program_id(0)

    @pl.when(i == 0)
    def _():
        cnt_sc[...] = jnp.zeros_like(cnt_sc)

    h1 = (x_ref[...]
          + jnp.dot(ym_ref[...], wo_ref[0:d_mlstm, :], preferred_element_type=F32)
          + jnp.dot(yc_ref[...], wo_ref[d_mlstm:, :], preferred_element_type=F32))
    h1_ref[...] = h1
    a2 = h1 * lax.rsqrt(jnp.mean(h1 * h1, axis=-1, keepdims=True) + EPS) * nw_ref[...]
    _store_row_slabs(a2_ref, 0, _pack_bf16_pairs(a2), a2.shape[1] // (2 * LANES))

    logits = lax.dot_general(wr_ref[...], a2.astype(BF16), (((1,), (1,)), ((), ())),
                             preferred_element_type=F32) + br_ref[...]
    e_iota = lax.broadcasted_iota(jnp.int32, logits.shape, 0)
    work = logits
    vals, idxs = [], []
    for _ in range(TOP_K):
        mx = jnp.max(work, axis=0, keepdims=True)
        sel = jnp.min(jnp.where(work == mx, e_iota, N_EXPERTS), axis=0, keepdims=True)
        vals.append(mx)
        idxs.append(sel)
        work = jnp.where(e_iota == sel, NEG_INF, work)
    exps = [jnp.exp(vv - vals[0]) for vv in vals]
    tot = exps[0] + exps[1] + exps[2] + exps[3]
    gates = [ev / tot for ev in exps]

    chosen = functools.reduce(jnp.logical_or, [e_iota == sel for sel in idxs])
    mh = jnp.where(chosen, 1.0, 0.0)
    ri = lax.broadcasted_iota(jnp.int32, (tm, tm), 0)
    ci = lax.broadcasted_iota(jnp.int32, (tm, tm), 1)
    upper = jnp.where(ri < ci, 1.0, 0.0).astype(BF16)
    rank = jnp.dot(mh.astype(BF16), upper, preferred_element_type=F32) + cnt_sc[:, 0:1]
    cnt_new = cnt_sc[...] + jnp.sum(mh, axis=1, keepdims=True)
    cnt_sc[...] = cnt_new
    cnt_ref[...] = cnt_new

    zero_i = jnp.zeros((SUBLANES - TOP_K, tm), jnp.int32)
    pos = [jnp.sum(jnp.where(e_iota == sel, rank, 0.0), axis=0, keepdims=True).astype(jnp.int32) for sel in idxs]
    idx_ref[...] = jnp.concatenate(idxs + [zero_i], axis=0)
    pos_ref[...] = jnp.concatenate(pos + [zero_i], axis=0)
    gate_ref[...] = jnp.concatenate(gates + [jnp.zeros((SUBLANES - TOP_K, tm), F32)], axis=0)


def _out_proj(xf, ym, yc, w_out_b, norm_w, w_router_t, b_router, d_mlstm, tm):
    n, d = xf.shape
    n_slabs = d // (2 * LANES)
    row = lambda i: (i, 0)
    colb = lambda i: (0, i)
    const = lambda i: (0, 0)
    return pl.pallas_call(
        functools.partial(_outproj_kernel, tm=tm, d_mlstm=d_mlstm),
        grid=(n // tm,),
        in_specs=[pl.BlockSpec((tm, d), row), pl.BlockSpec((tm, ym.shape[1]), row), pl.BlockSpec((tm, yc.shape[1]), row),
                  pl.BlockSpec(w_out_b.shape, const), pl.BlockSpec(norm_w.shape, const),
                  pl.BlockSpec(w_router_t.shape, const), pl.BlockSpec(b_router.shape, const)],
        out_specs=[pl.BlockSpec((tm, d), row),
                   pl.BlockSpec((tm * n_slabs, LANES), row),
                   pl.BlockSpec((SUBLANES, tm), colb),
                   pl.BlockSpec((SUBLANES, tm), colb),
                   pl.BlockSpec((SUBLANES, tm), colb),
                   pl.BlockSpec((N_EXPERTS, LANES), const)],
        out_shape=[jax.ShapeDtypeStruct((n, d), F32),
                   jax.ShapeDtypeStruct((n * n_slabs, LANES), jnp.uint32),
                   jax.ShapeDtypeStruct((SUBLANES, n), jnp.int32),
                   jax.ShapeDtypeStruct((SUBLANES, n), jnp.int32),
                   jax.ShapeDtypeStruct((SUBLANES, n), F32),
                   jax.ShapeDtypeStruct((N_EXPERTS, LANES), F32)],
        scratch_shapes=[pltpu.VMEM((N_EXPERTS, LANES), F32)],
        compiler_params=_cparams("arbitrary"),
        name="out_proj_router",
    )(xf, ym, yc, w_out_b, norm_w, w_router_t, b_router)


def _experts_kernel(blk_ref, exp_ref, valid_ref, lo_ref, hi_ref,
                    xs_ref, w1_ref, b1_ref, w2_ref, b2_ref, ys_ref, w1c_ref, w2c_ref, acc_ref, *, rb, d_ff, n_sub):
    i = pl.program_id(0)
    n_slabs = xs_ref.shape[0] // rb
    prev = jnp.maximum(i - 1, 0)
    first_visit = jnp.logical_or(i == 0, blk_ref[prev] != blk_ref[i])
    new_expert = jnp.logical_or(i == 0, exp_ref[prev] != exp_ref[i])

    @pl.when(new_expert)
    def _():
        w1c_ref[...] = w1_ref[0].astype(BF16)
        w2c_ref[...] = w2_ref[0].astype(BF16)

    @pl.when(first_visit)
    def _():
        acc_ref[...] = jnp.zeros_like(acc_ref)

    rs = rb // n_sub
    lo, hi = lo_ref[i], hi_ref[i]

    def sub_block(c, owns_all_rows):
        xa, xb = _unpack_bf16_pairs(_load_row_slabs(xs_ref, c * rs, rs, n_slabs))
        x = jnp.concatenate([xa.astype(BF16), xb.astype(BF16)], axis=-1)
        hid = jnp.dot(x, w1c_ref[...], preferred_element_type=F32) + b1_ref[0]
        x_glu = jnp.minimum(hid[:, :d_ff], SWIGLU_LIMIT)
        x_lin = jnp.clip(hid[:, d_ff:], -SWIGLU_LIMIT, SWIGLU_LIMIT)
        act = x_glu * _sigmoid(SWIGLU_ALPHA * x_glu) * (x_lin + 1.0)
        y = jnp.dot(act.astype(BF16), w2c_ref[...], preferred_element_type=F32) + b2_ref[0]
        if not owns_all_rows:
            r = c * rs + lax.broadcasted_iota(jnp.int32, (rs, 1), 0)
            y = jnp.where(jnp.logical_and(r >= lo, r < hi), y, 0.0)
            y = acc_ref[c * rs:(c + 1) * rs, :] + y
            acc_ref[c * rs:(c + 1) * rs, :] = y
        _store_row_slabs(ys_ref, c * rs, _pack_bf16_pairs(y), n_slabs)

    for c in range(n_sub):
        valid = valid_ref[i] == 1
        owns_all = jnp.logical_and(lo <= c * rs, hi >= (c + 1) * rs)
        has_rows = jnp.logical_and(lo < (c + 1) * rs, hi > c * rs)
        pl.when(jnp.logical_and(valid, owns_all))(functools.partial(sub_block, c, True))
        pl.when(jnp.logical_and(valid, jnp.logical_and(has_rows, jnp.logical_not(owns_all))))(
            functools.partial(sub_block, c, False))


def _experts(xs, w1, b1, w2, b2, item_blk, item_exp, item_valid, item_lo, item_hi, rb, n_slabs):
    n_items = item_blk.shape[0]
    d = w1.shape[1]
    d_ff = w2.shape[1]
    by_blk = lambda i, blk, ex, va, lo, hi: (blk[i], 0)
    by_exp = lambda i, blk, ex, va, lo, hi: (ex[i], 0, 0)
    grid_spec = pltpu.PrefetchScalarGridSpec(
        num_scalar_prefetch=5,
        grid=(n_items,),
        in_specs=[pl.BlockSpec((rb * n_slabs, LANES), by_blk),
                  pl.BlockSpec((1,) + w1.shape[1:], by_exp),
                  pl.BlockSpec((1,) + b1.shape[1:], by_exp),
                  pl.BlockSpec((1,) + w2.shape[1:], by_exp),
                  pl.BlockSpec((1,) + b2.shape[1:], by_exp)],
        out_specs=pl.BlockSpec((rb * n_slabs, LANES), by_blk),
        scratch_shapes=[pltpu.VMEM(w1.shape[1:], BF16), pltpu.VMEM(w2.shape[1:], BF16), pltpu.VMEM((rb, d), F32)],
    )
    return pl.pallas_call(
        functools.partial(_experts_kernel, rb=rb, d_ff=d_ff, n_sub=rb // MXU_DIM),
        grid_spec=grid_spec,
        out_shape=jax.ShapeDtypeStruct(xs.shape, jnp.uint32),
        compiler_params=pltpu.CompilerParams(dimension_semantics=("arbitrary",), vmem_limit_bytes=EXPERTS_VMEM_LIMIT),
        name="experts",
    )(item_blk, item_exp, item_valid, item_lo, item_hi, xs, w1, b1, w2, b2)


SC_CORES = 2
SC_SUBCORES = 16
SC_GATHER_WINDOW = 64


def _sc_worker_base(per_worker):
    wid = lax.axis_index("s") * SC_CORES + lax.axis_index("c")
    return wid * per_worker


def _sc_gather_rows(table, idx):
    m = idx.shape[0]
    n_workers = SC_CORES * SC_SUBCORES
    window = SC_GATHER_WINDOW
    per_worker = m // n_workers
    n_win = per_worker // window
    assert per_worker * n_workers == m and n_win * window == per_worker and n_win % 2 == 0
    mesh = plsc.VectorSubcoreMesh(core_axis_name="c", subcore_axis_name="s")
    slab = table.shape[1:]

    @functools.partial(
        pl.kernel, mesh=mesh,
        out_type=jax.ShapeDtypeStruct((m,) + slab, table.dtype),
        scratch_types=[pltpu.VMEM((per_worker,), jnp.int32),
                       pltpu.VMEM((window,) + slab, table.dtype), pltpu.VMEM((window,) + slab, table.dtype)]
        + [pltpu.SemaphoreType.DMA] * 4,
        name="sc_gather_rows",
    )
    def gather(table_hbm, idx_hbm, out_hbm, idx_v, rows0, rows1, gsem0, gsem1, wsem0, wsem1):
        rows_v, gsem, wsem = (rows0, rows1), (gsem0, gsem1), (wsem0, wsem1)
        base = _sc_worker_base(per_worker)
        pltpu.sync_copy(idx_hbm.at[pl.ds(pl.multiple_of(base, window), per_worker)], idx_v)

        def gather_copy(w, b):
            ids = idx_v.at[pl.ds(pl.multiple_of(w * window, window), window)]
            return pltpu.make_async_copy(table_hbm.at[ids], rows_v[b], gsem[b])

        def write_copy(w, b):
            dst = out_hbm.at[pl.ds(pl.multiple_of(base + w * window, window), window)]
            return pltpu.make_async_copy(rows_v[b], dst, wsem[b])

        gather_copy(0, 0).start()

        @pl.loop(0, n_win, step=2)
        def _(w):
            @pl.when(w >= 1)
            def _():
                write_copy(w - 1, 1).wait()

            gather_copy(w + 1, 1).start()
            gather_copy(w, 0).wait()
            write_copy(w, 0).start()

            @pl.when(w + 2 < n_win)
            def _():
                write_copy(w, 0).wait()
                gather_copy(w + 2, 0).start()

            gather_copy(w + 1, 1).wait()
            write_copy(w + 1, 1).start()

        write_copy(n_win - 2, 0).wait()
        write_copy(n_win - 1, 1).wait()

    return gather(table, idx)


def _sc_scatter_rows(rows, dest, n_slots):
    n = rows.shape[0]
    n_workers = SC_CORES * SC_SUBCORES
    window = SC_GATHER_WINDOW
    per_worker = n // n_workers
    n_win = per_worker // window
    assert per_worker * n_workers == n and n_win * window == per_worker and n_win % 2 == 0
    mesh = plsc.VectorSubcoreMesh(core_axis_name="c", subcore_axis_name="s")
    slab = rows.shape[1:]

    @functools.partial(
        pl.kernel, mesh=mesh,
        out_type=jax.ShapeDtypeStruct((n_slots * n,) + slab, rows.dtype),
        scratch_types=[pltpu.VMEM((window,), jnp.int32)] * (2 * n_slots)
        + [pltpu.VMEM((window,) + slab, rows.dtype)] * 2 + [pltpu.SemaphoreType.DMA] * 4,
        name="sc_scatter_rows",
    )
    def scatter(rows_hbm, dest_hbm, out_hbm, *scratch):
        idx_v = (scratch[:n_slots], scratch[n_slots:2 * n_slots])
        rows_v = scratch[2 * n_slots:2 * n_slots + 2]
        rsem = scratch[2 * n_slots + 2:2 * n_slots + 4]
        wsem = scratch[2 * n_slots + 4:2 * n_slots + 6]
        base = _sc_worker_base(per_worker)

        def rows_at(w, k=0):
            return pl.ds(pl.multiple_of(k * n + base + w * window, window), window)

        def start_read(w, b):
            for k in range(n_slots):
                pltpu.sync_copy(dest_hbm.at[rows_at(w, k)], idx_v[b][k])
            pltpu.async_copy(rows_hbm.at[rows_at(w)], rows_v[b], rsem[b])

        def scatter_window(w, b):
            pltpu.make_async_copy(rows_hbm.at[rows_at(w)], rows_v[b], rsem[b]).wait()
            for k in range(n_slots):
                pltpu.async_copy(rows_v[b], out_hbm.at[idx_v[b][k]], wsem[b])
            for k in range(n_slots):
                pltpu.make_async_copy(rows_v[b], out_hbm.at[idx_v[b][k]], wsem[b]).wait()

        start_read(0, 0)

        @pl.loop(0, n_win, step=2)
        def _(w):
            start_read(w + 1, 1)
            scatter_window(w, 0)

            @pl.when(w + 2 < n_win)
            def _():
                start_read(w + 2, 0)

            scatter_window(w + 1, 1)

    return scatter(rows, dest)


def _final_kernel(h1_ref, gate_ref, yg_ref, nw_ref, out_ref, *, tm, n_slabs):
    gpad = jnp.concatenate([gate_ref[...], jnp.zeros((LANES - SUBLANES, tm), F32)], axis=0)
    gcol = gpad.T
    h2 = h1_ref[...]
    for k in range(TOP_K):
        ya, yb = _unpack_bf16_pairs(_load_row_slabs(yg_ref, 0, tm, n_slabs, lead=(k,)))
        h2 = h2 + gcol[:, k:k + 1] * jnp.concatenate([ya, yb], axis=-1)
    out_ref[...] = h2 * lax.rsqrt(jnp.mean(h2 * h2, axis=-1, keepdims=True) + EPS) * nw_ref[...]


def _final(h, gates_t, yg, final_w, n_slabs, tm, chunk, n_chunks):
    n, d = h.shape
    tiles = n // tm // n_chunks
    first = chunk * tiles
    return pl.pallas_call(
        functools.partial(_final_kernel, tm=tm, n_slabs=n_slabs),
        grid=(tiles,),
        in_specs=[pl.BlockSpec((tm, d), lambda i: (first + i, 0)),
                  pl.BlockSpec((SUBLANES, tm), lambda i: (0, first + i)),
                  pl.BlockSpec((TOP_K, tm * n_slabs, LANES), lambda i: (0, i, 0)),
                  pl.BlockSpec((1, d), lambda i: (0, 0))],
        out_specs=pl.BlockSpec((tm, d), lambda i: (first + i, 0)),
        out_shape=jax.ShapeDtypeStruct((n, d), F32),
        input_output_aliases={0: 0},
        compiler_params=_cparams("parallel"),
        name="final",
    )(h, gates_t, yg, final_w)


def _block_diag_tiles(w):
    nb, bs, _ = w.shape
    rows = jnp.tile(w.reshape(nb * bs // MXU_DIM, MXU_DIM, bs), (1, 1, MXU_DIM // bs))
    r_blk = lax.broadcasted_iota(jnp.int32, (MXU_DIM, MXU_DIM), 0) // bs
    c_blk = lax.broadcasted_iota(jnp.int32, (MXU_DIM, MXU_DIM), 1) // bs
    return jnp.where(r_blk == c_blk, rows, 0.0).astype(BF16)


def _layer(xf, bsz, seq, norm_mix_w, w_in, mlstm_conv_w, mlstm_conv_b, w_q, w_k, w_v, w_igate, b_igate,
           w_fgate, b_fgate, mlstm_norm_w, mlstm_skip, conv_dw_w, conv_dw_b, conv_norm_w, conv_norm_b,
           w_out, norm_ffn_w, w_router, b_router, w1, b1, w2, b2, final_norm_w):
    n, d = xf.shape
    d_mlstm = mlstm_norm_w.shape[0]
    d_conv = conv_norm_w.shape[0]
    n_slabs = d // (2 * LANES)
    r2 = lambda v: v.reshape(1, -1)

    xm, xc, sz, u = _in_proj(xf, r2(norm_mix_w), w_in.astype(BF16), mlstm_conv_w, r2(mlstm_conv_b),
                             d_mlstm, d_conv, seq, tm=IN_PROJ_ROWS)

    q_scale = (d_mlstm // N_HEADS) ** -0.5
    inv_q_scale = round(1.0 / q_scale)
    assert inv_q_scale & (inv_q_scale - 1) == 0 and inv_q_scale * q_scale == 1.0, "head dim must be a power of 4"
    wg = jnp.concatenate([w_igate, w_fgate], axis=1)
    wg = jnp.concatenate([wg[:d_mlstm] * float(inv_q_scale), wg[d_mlstm:]], axis=0)
    wg = jnp.pad(wg, ((0, 0), (0, LANES - wg.shape[1]))).astype(BF16)
    bg = jnp.pad(jnp.concatenate([b_igate, b_fgate]), (0, LANES - 2 * N_HEADS)).reshape(1, LANES)
    ym = _mlstm(xm, xc, sz, _block_diag_tiles(w_q * q_scale), _block_diag_tiles(w_k), _block_diag_tiles(w_v),
                wg, bg, r2(mlstm_norm_w), r2(mlstm_skip), bsz, seq)
    yc = _conv_group(u, conv_dw_w, r2(conv_dw_b), r2(conv_norm_w), r2(conv_norm_b), bsz, seq, tile=CONV_ROWS)

    h1, a2s, idx_t, pos_t, gates_t, cnt = _out_proj(
        xf, ym, yc, w_out.astype(BF16), r2(norm_ffn_w), w_router.T.astype(BF16), b_router.reshape(-1, 1), d_mlstm,
        tm=OUT_PROJ_ROWS)

    counts = cnt[:, 0].astype(jnp.int32)
    ends = jnp.cumsum(counts)
    starts = ends - counts
    e_ids = jnp.arange(N_EXPERTS, dtype=jnp.int32)
    idx4 = idx_t[:TOP_K]
    dest = pos_t[:TOP_K] + jnp.sum(
        jnp.where(idx4[None] == e_ids[:, None, None], starts[:, None, None], 0), axis=0)

    n_rows = n * TOP_K
    rb = EXPERT_ROWS
    n_blocks = n_rows // rb
    n_items = n_blocks + N_EXPERTS - 1
    first_blk = starts // rb
    last_blk = jnp.where(counts > 0, (ends - 1) // rb, first_blk - 1)
    per_e = last_blk - first_blk + 1
    item_end = jnp.cumsum(per_e)
    item_start = item_end - per_e
    ids = jnp.arange(n_items, dtype=jnp.int32)
    total = item_end[-1]
    item_valid = (ids < total).astype(jnp.int32)
    item_exp = jnp.minimum(jnp.sum((ids[:, None] >= item_end[None, :]).astype(jnp.int32), axis=1), N_EXPERTS - 1)
    last_valid_exp = jnp.max(jnp.where(per_e > 0, e_ids, 0))
    item_exp = jnp.where(item_valid == 1, item_exp, last_valid_exp).astype(jnp.int32)
    is_exp = item_exp[:, None] == e_ids[None, :]
    of_item = lambda table: jnp.sum(jnp.where(is_exp, table[None, :], 0), axis=1)
    item_blk = jnp.where(item_valid == 1, of_item(first_blk) + ids - of_item(item_start), n_blocks - 1)
    item_blk = item_blk.astype(jnp.int32)
    item_lo = (jnp.maximum(of_item(starts), item_blk * rb) - item_blk * rb).astype(jnp.int32)
    item_hi = (jnp.minimum(of_item(ends), (item_blk + 1) * rb) - item_blk * rb).astype(jnp.int32)

    xs = _sc_scatter_rows(a2s.reshape(n, n_slabs, LANES), dest.reshape(-1), TOP_K)
    ys = _experts(xs.reshape(n_rows * n_slabs, LANES), w1, b1[:, None, :], w2, b2[:, None, :],
                  item_blk, item_exp, item_valid, item_lo, item_hi, rb, n_slabs)
    ys = ys.reshape(n_rows, n_slabs, LANES)

    n_chunks = FINAL_CHUNKS
    nc = n // n_chunks
    out = h1
    for c in range(n_chunks):
        yg = _sc_gather_rows(ys, dest[:, c * nc:(c + 1) * nc].reshape(-1))
        out = _final(out, gates_t, yg.reshape(TOP_K, nc * n_slabs, LANES), r2(final_norm_w), n_slabs,
                     tm=FINAL_ROWS, chunk=c, n_chunks=n_chunks)
    return out


def kernel(x, norm_mix_w, w_in, mlstm_conv_w, mlstm_conv_b, w_q, w_k, w_v, w_igate, b_igate, w_fgate, b_fgate,
           mlstm_norm_w, mlstm_skip, conv_dw_w, conv_dw_b, conv_norm_w, conv_norm_b, w_out, norm_ffn_w,
           w_router, b_router, w1, b1, w2, b2, final_norm_w):
    bsz, seq, d = x.shape
    assert norm_mix_w.shape[0] == 1, "single-layer block"
    out = _layer(x.reshape(bsz * seq, d), bsz, seq, norm_mix_w[0], w_in[0], mlstm_conv_w[0], mlstm_conv_b[0],
                 w_q[0], w_k[0], w_v[0], w_igate[0], b_igate[0], w_fgate[0], b_fgate[0], mlstm_norm_w[0],
                 mlstm_skip[0], conv_dw_w[0], conv_dw_b[0], conv_norm_w[0], conv_norm_b[0], w_out[0],
                 norm_ffn_w[0], w_router[0], b_router[0], w1[0], b1[0], w2[0], b2[0], final_norm_w)
    return out.reshape(bsz, seq, d)
```
